```python
import jax, jax.numpy as jnp
from jax import lax
import numpy as np

D_MODEL = 1024
BATCH = 1
SEQ = 16384
DEPTH = 2

N_A = (DEPTH + 1) // 2
N_B = DEPTH // 2

D_RNN = D_MODEL
N_RNN_BLOCKS = 8
RNN_BLOCK = D_RNN // N_RNN_BLOCKS
CONV_WIDTH = 4
LRU_C = 8.0

N_HEADS = 16
N_KV_HEADS = 4
HEAD_DIM = 64
GROUP = N_HEADS // N_KV_HEADS
Q_DIM = N_HEADS * HEAD_DIM
KV_DIM = N_KV_HEADS * HEAD_DIM
WINDOW = 128
ATTN_BLOCK = 128
ROPE_THETA = 500000.0
ROT_DIM = HEAD_DIM // 4

D_FF = 3584
N_EXPERTS = 8
TOP_K = 2

EPS = 1e-6
NEG_INF = -1e30

kernel_name = "hybrid_rglru_swa_sink_moe_trunk"


def rmsnorm(t, g):
    tf = t.astype(jnp.float32)
    tf = tf * lax.rsqrt(jnp.mean(tf * tf, axis=-1, keepdims=True) + EPS)
    return (tf * g.astype(jnp.float32)).astype(t.dtype)


def rope_tables(positions):
    inv_freq = ROPE_THETA ** (-jnp.arange(0, ROT_DIM, 2, dtype=jnp.float32) / ROT_DIM)
    ang = positions.astype(jnp.float32)[..., None] * inv_freq
    return jnp.cos(ang)[:, :, None, :], jnp.sin(ang)[:, :, None, :]


def apply_partial_rope(t, cos, sin):
    rot, keep = t[..., :ROT_DIM].astype(jnp.float32), t[..., ROT_DIM:]
    t1, t2 = rot[..., :ROT_DIM // 2], rot[..., ROT_DIM // 2:]
    rot = jnp.concatenate([t1 * cos - t2 * sin, t2 * cos + t1 * sin], axis=-1).astype(t.dtype)
    return jnp.concatenate([rot, keep], axis=-1)


def rglru_mixer(h, w_in, conv_w, conv_b, w_a, b_a, w_x, b_x, lam, w_out):
    B, S, _ = h.shape
    proj = h @ w_in
    y_branch, x_branch = jnp.split(proj, 2, axis=-1)
    y_branch = jax.nn.gelu(y_branch, approximate=True)
    xp = jnp.pad(x_branch, ((0, 0), (CONV_WIDTH - 1, 0), (0, 0)))
    xc = conv_b + sum(xp[:, k:k + S] * conv_w[k] for k in range(CONV_WIDTH))
    xb = xc.reshape(B, S, N_RNN_BLOCKS, RNN_BLOCK)
    r = jax.nn.sigmoid(jnp.einsum('bsnc,ncd->bsnd', xb, w_a).reshape(B, S, D_RNN) + b_a)
    i = jax.nn.sigmoid(jnp.einsum('bsnc,ncd->bsnd', xb, w_x).reshape(B, S, D_RNN) + b_x)
    log_a = -LRU_C * r.astype(jnp.float32) * jax.nn.softplus(-lam.astype(jnp.float32))
    a = jnp.exp(log_a)
    gated_x = jnp.sqrt(-jnp.expm1(2.0 * log_a)) * (i * xc).astype(jnp.float32)

    def combine(left, right):
        a1, b1 = left
        a2, b2 = right
        return a1 * a2, a2 * b1 + b2

    _, hs = lax.associative_scan(combine, (a, gated_x), axis=1)
    return (hs.astype(h.dtype) * y_branch) @ w_out


def swa_sink_mixer(h, cos, sin, w_qkv, b_qkv, sinks, w_o, b_o):
    B, S, _ = h.shape
    nb = S // ATTN_BLOCK
    qkv = h @ w_qkv + b_qkv
    q, k, v = jnp.split(qkv, [Q_DIM, Q_DIM + KV_DIM], axis=-1)
    q = apply_partial_rope(q.reshape(B, S, N_HEADS, HEAD_DIM), cos, sin)
    k = apply_partial_rope(k.reshape(B, S, N_KV_HEADS, HEAD_DIM), cos, sin)
    v = v.reshape(B, S, N_KV_HEADS, HEAD_DIM)
    qb = q.reshape(B, nb, ATTN_BLOCK, N_KV_HEADS, GROUP, HEAD_DIM)
    kb = k.reshape(B, nb, ATTN_BLOCK, N_KV_HEADS, HEAD_DIM)
    vb = v.reshape(B, nb, ATTN_BLOCK, N_KV_HEADS, HEAD_DIM)

    def with_prev(t):
        prev = jnp.pad(t, ((0, 0), (1, 0), (0, 0), (0, 0), (0, 0)))[:, :-1]
        return jnp.concatenate([prev, t], axis=2)

    kc, vc = with_prev(kb), with_prev(vb)
    scores = jnp.einsum('bnqkgd,bnjkd->bnkgqj', qb, kc).astype(jnp.float32) * (HEAD_DIM ** -0.5)
    blk = jnp.arange(nb)[:, None, None]
    qpos = blk * ATTN_BLOCK + jnp.arange(ATTN_BLOCK)[None, :, None]
    kpos = (blk - 1) * ATTN_BLOCK + jnp.arange(2 * ATTN_BLOCK)[None, None, :]
    allowed = (kpos <= qpos) & (kpos > qpos - WINDOW) & (kpos >= 0)
    scores = jnp.where(allowed[None, :, None, None], scores, NEG_INF)
    sink = jnp.broadcast_to(sinks.astype(jnp.float32).reshape(1, 1, N_KV_HEADS, GROUP, 1, 1),
                            scores.shape[:-1] + (1,))
    probs = jax.nn.softmax(jnp.concatenate([scores, sink], axis=-1), axis=-1)[..., :-1]
    o = jnp.einsum('bnkgqj,bnjkd->bnqkgd', probs.astype(vc.dtype), vc).reshape(B, S, Q_DIM)
    return o @ w_o + b_o


def swiglu(h, w_gate, w_up, w_down):
    return (jax.nn.silu(h @ w_gate) * (h @ w_up)) @ w_down


def moe_swiglu(h, w_router, w_gate, w_up, w_down):
    logits = (h @ w_router).astype(jnp.float32)
    top_vals, top_idx = lax.top_k(logits, TOP_K)
    gates = jax.nn.softmax(top_vals, axis=-1)
    comb = jnp.sum(jax.nn.one_hot(top_idx, N_EXPERTS, dtype=jnp.float32) * gates[..., None], axis=-2)
    out = jnp.zeros_like(h)
    for e in range(N_EXPERTS):
        out = out + comb[..., e:e + 1].astype(h.dtype) * swiglu(h, w_gate[e], w_up[e], w_down[e])
    return out


def _normal(k, shape, scale):
    return jax.random.normal(k, shape, jnp.float32) * scale


def setup_inputs(seed: int = 0) -> dict:
    key = jax.random.key(seed)
    ks = jax.random.split(key, 32)
    u = jax.random.uniform(ks[9], (N_A, D_RNN), jnp.float32, 0.9, 0.999)
    s = u ** (1.0 / LRU_C)
    return {
        "x": _normal(ks[0], (BATCH, SEQ, D_MODEL), 1.0),
        "positions": jnp.broadcast_to(jnp.arange(SEQ, dtype=jnp.int32), (BATCH, SEQ)),
        "norm_mix": 1.0 + _normal(ks[1], (DEPTH, D_MODEL), 0.02),
        "norm_ffn": 1.0 + _normal(ks[2], (DEPTH, D_MODEL), 0.02),
        "norm_final": 1.0 + _normal(ks[3], (D_MODEL,), 0.02),
        "lru_w_in": _normal(ks[4], (N_A, D_MODEL, 2 * D_RNN), D_MODEL ** -0.5),
        "lru_conv_w": _normal(ks[5], (N_A, CONV_WIDTH, D_RNN), CONV_WIDTH ** -0.5),
        "lru_conv_b": _normal(ks[6], (N_A, D_RNN), 0.02),
        "lru_w_a": _normal(ks[7], (N_A, N_RNN_BLOCKS, RNN_BLOCK, RNN_BLOCK), RNN_BLOCK ** -0.5),
        "lru_b_a": _normal(ks[8], (N_A, D_RNN), 0.02),
        "lru_w_x": _normal(ks[10], (N_A, N_RNN_BLOCKS, RNN_BLOCK, RNN_BLOCK), RNN_BLOCK ** -0.5),
        "lru_b_x": _normal(ks[11], (N_A, D_RNN), 0.02),
        "lru_lambda": jnp.log(s) - jnp.log1p(-s),
        "lru_w_out": _normal(ks[12], (N_A, D_RNN, D_MODEL), D_RNN ** -0.5),
        "attn_w_qkv": _normal(ks[13], (N_B, D_MODEL, Q_DIM + 2 * KV_DIM), D_MODEL ** -0.5),
        "attn_b_qkv": _normal(ks[14], (N_B, Q_DIM + 2 * KV_DIM), 0.02),
        "attn_sinks": _normal(ks[15], (N_B, N_HEADS), 1.0),
        "attn_w_o": _normal(ks[16], (N_B, Q_DIM, D_MODEL), Q_DIM ** -0.5),
        "attn_b_o": _normal(ks[17], (N_B, D_MODEL), 0.02),
        "ffn_w_gate": _normal(ks[18], (N_A, D_MODEL, D_FF), D_MODEL ** -0.5),
        "ffn_w_up": _normal(ks[19], (N_A, D_MODEL, D_FF), D_MODEL ** -0.5),
        "ffn_w_down": _normal(ks[20], (N_A, D_FF, D_MODEL), D_FF ** -0.5),
        "moe_w_router": _normal(ks[21], (N_B, D_MODEL, N_EXPERTS), D_MODEL ** -0.5),
        "moe_w_gate": _normal(ks[22], (N_B, N_EXPERTS, D_MODEL, D_FF), D_MODEL ** -0.5),
        "moe_w_up": _normal(ks[23], (N_B, N_EXPERTS, D_MODEL, D_FF), D_MODEL ** -0.5),
        "moe_w_down": _normal(ks[24], (N_B, N_EXPERTS, D_FF, D_MODEL), D_FF ** -0.5),
    }


def reference(x, positions, norm_mix, norm_ffn, norm_final,
              lru_w_in, lru_conv_w, lru_conv_b, lru_w_a, lru_b_a, lru_w_x, lru_b_x,
              lru_lambda, lru_w_out,
              attn_w_qkv, attn_b_qkv, attn_sinks, attn_w_o, attn_b_o,
              ffn_w_gate, ffn_w_up, ffn_w_down,
              moe_w_router, moe_w_gate, moe_w_up, moe_w_down):
    cos, sin = rope_tables(positions)
    h = x
    for i in range(DEPTH):
        j = i // 2
        hn = rmsnorm(h, norm_mix[i])
        if i % 2 == 0:
            h = h + rglru_mixer(hn, lru_w_in[j], lru_conv_w[j], lru_conv_b[j], lru_w_a[j], lru_b_a[j],
                                lru_w_x[j], lru_b_x[j], lru_lambda[j], lru_w_out[j])
        else:
            h = h + swa_sink_mixer(hn, cos, sin, attn_w_qkv[j], attn_b_qkv[j], attn_sinks[j],
                                   attn_w_o[j], attn_b_o[j])
        hn = rmsnorm(h, norm_ffn[i])
        if i % 2 == 0:
            h = h + swiglu(hn, ffn_w_gate[j], ffn_w_up[j], ffn_w_down[j])
        else:
            h = h + moe_swiglu(hn, moe_w_router[j], moe_w_gate[j], moe_w_up[j], moe_w_down[j])
    return rmsnorm(h, norm_final)
```

```python
import functools

import jax
import jax.numpy as jnp
from jax import lax
from jax.experimental import pallas as pl
from jax.experimental.pallas import tpu as pltpu

F32 = jnp.float32
BF16 = jnp.bfloat16

D_MODEL = 1024
N_RNN_BLOCKS = 8
RNN_BLOCK = D_MODEL // N_RNN_BLOCKS
CONV_WIDTH = 4
LRU_C = 8.0
N_HEADS = 16
N_KV_HEADS = 4
HEAD_DIM = 64
GROUP = N_HEADS // N_KV_HEADS
WINDOW = 128
ATTN_BLOCK = 128
ROPE_THETA = 500000.0
ROT_DIM = HEAD_DIM // 4
D_FF = 3584
N_EXPERTS = 8
TOP_K = 2
EPS = 1e-6
NEG_INF = -1e30

LANES = 128
SUBLANES = 8
VMEM_LIMIT = 56 * 1024 * 1024

SEQ_TILE = 512
FFN_TILE = 1024
FF_CHUNK = 512
N_FF_CHUNKS = D_FF // FF_CHUNK
MOE_TILE = 512
OUT_TILE = 1024


def _rmsnorm(x, g):
    return x * lax.rsqrt(jnp.mean(x * x, axis=-1, keepdims=True) + EPS) * g


ROWS_PER_TOKEN = D_MODEL // LANES


def _store_token_major(ref, val):
    T = val.shape[0]
    for c in range(ROWS_PER_TOKEN):
        ref[pl.ds(c, T, stride=ROWS_PER_TOKEN), :] = val[:, LANES * c:LANES * (c + 1)]


def _load_token_major(ref, T):
    return jnp.concatenate([ref[pl.ds(c, T, stride=ROWS_PER_TOKEN), :] for c in range(ROWS_PER_TOKEN)], axis=1)


def _const_spec(shape):
    n = len(shape)
    return pl.BlockSpec(shape, lambda *_: (0,) * n, pipeline_mode=pl.Buffered(1))


def _lru_kernel(x_ref, g_ref, win_ref, cw_ref, cb_ref, wax_ref, ba_ref, bx_ref, lam_ref, wout_ref,
                o_ref, xb_scr, carry_scr):
    T = x_ref.shape[0]
    D = D_MODEL

    @pl.when(pl.program_id(0) == 0)
    def _():
        xb_scr[0:SUBLANES, :] = jnp.zeros((SUBLANES, D), F32)
        carry_scr[...] = jnp.zeros((1, D), F32)

    x = x_ref[...]
    hn = _rmsnorm(x, g_ref[...]).astype(BF16)
    proj = jnp.dot(hn, win_ref[...], preferred_element_type=F32)
    y = jax.nn.gelu(proj[:, :D], approximate=True)
    xb = proj[:, D:]

    xb_scr[SUBLANES:SUBLANES + T, :] = xb
    cw = cw_ref[...]
    xc = (cb_ref[...] + xb * cw[3:4, :]
          + xb_scr[SUBLANES - 1:SUBLANES - 1 + T, :] * cw[2:3, :]
          + xb_scr[SUBLANES - 2:SUBLANES - 2 + T, :] * cw[1:2, :]
          + xb_scr[SUBLANES - 3:SUBLANES - 3 + T, :] * cw[0:1, :])
    xb_scr[0:SUBLANES, :] = xb[T - SUBLANES:T, :]

    xcb = xc.astype(BF16)
    r_parts, i_parts = [], []
    for n in range(N_RNN_BLOCKS):
        gn = jnp.dot(xcb[:, RNN_BLOCK * n:RNN_BLOCK * (n + 1)], wax_ref[n], preferred_element_type=F32)
        r_parts.append(gn[:, :RNN_BLOCK])
        i_parts.append(gn[:, RNN_BLOCK:])
    r = jax.nn.sigmoid(jnp.concatenate(r_parts, axis=1) + ba_ref[...])
    ig = jax.nn.sigmoid(jnp.concatenate(i_parts, axis=1) + bx_ref[...])
    log_a = -LRU_C * r * jax.nn.softplus(-lam_ref[...])
    a = jnp.exp(log_a)
    b = jnp.sqrt(jnp.tanh(-log_a) * (1.0 + a * a)) * (ig * xc)

    G = T // SUBLANES
    a3 = a.reshape(G, SUBLANES, D)
    b3 = b.reshape(G, SUBLANES, D)
    sub = lax.broadcasted_iota(jnp.int32, (G, SUBLANES, D), 1)
    for k in (1, 2, 4):
        m = sub >= k
        a_sh = jnp.where(m, pltpu.roll(a3, k, 1), 1.0)
        b_sh = jnp.where(m, pltpu.roll(b3, k, 1), 0.0)
        b3 = b3 + a3 * b_sh
        a3 = a3 * a_sh
    hprev = carry_scr[...]
    hs = []
    for gi in range(G):
        hg = b3[gi] + a3[gi] * hprev
        hs.append(hg)
        hprev = hg[SUBLANES - 1:SUBLANES, :]
    carry_scr[...] = hprev
    h = jnp.concatenate(hs, axis=0)

    o_ref[...] = jnp.dot((h * y).astype(BF16), wout_ref[...], preferred_element_type=F32) + x


def _lru_mixer(x, g, w_in, conv_w, conv_b, w_ax, b_a, b_x, lam, w_out):
    S, D = x.shape
    T = SEQ_TILE
    row = lambda v: v.reshape(1, D)
    return pl.pallas_call(
        _lru_kernel,
        grid=(S // T,),
        in_specs=[
            pl.BlockSpec((T, D), lambda i: (i, 0)),
            _const_spec((1, D)),
            _const_spec((D, 2 * D)),
            _const_spec((CONV_WIDTH, D)),
            _const_spec((1, D)),
            _const_spec((N_RNN_BLOCKS, RNN_BLOCK, 2 * RNN_BLOCK)),
            _const_spec((1, D)),
            _const_spec((1, D)),
            _const_spec((1, D)),
            _const_spec((D, D)),
        ],
        out_specs=pl.BlockSpec((T, D), lambda i: (i, 0)),
        out_shape=jax.ShapeDtypeStruct((S, D), F32),
        scratch_shapes=[pltpu.VMEM((T + SUBLANES, D), F32), pltpu.VMEM((1, D), F32)],
        compiler_params=pltpu.CompilerParams(dimension_semantics=("arbitrary",), vmem_limit_bytes=VMEM_LIMIT),
        name="lru_mixer",
    )(x, row(g), w_in, conv_w, row(conv_b), w_ax, row(b_a), row(b_x), row(lam), w_out)


def _ffn_kernel(x_ref, g_ref, wg_ref, wu_ref, wd_ref, o_ref, xn_scr, acc_scr):
    c = pl.program_id(1)

    @pl.when(c == 0)
    def _():
        xn_scr[...] = _rmsnorm(x_ref[...], g_ref[...]).astype(BF16)
        acc_scr[...] = jnp.zeros_like(acc_scr)

    xn = xn_scr[...]
    gate = jnp.dot(xn, wg_ref[...], preferred_element_type=F32)
    up = jnp.dot(xn, wu_ref[...], preferred_element_type=F32)
    mid = (jax.nn.silu(gate) * up).astype(BF16)
    acc_scr[...] += jnp.dot(mid, wd_ref[...], preferred_element_type=F32)

    @pl.when(c == pl.num_programs(1) - 1)
    def _():
        o_ref[...] = x_ref[...] + acc_scr[...]


def _dense_ffn(x, g, w_gate, w_up, w_down):
    S, D = x.shape
    T = FFN_TILE
    return pl.pallas_call(
        _ffn_kernel,
        grid=(S // T, N_FF_CHUNKS),
        in_specs=[
            pl.BlockSpec((T, D), lambda i, c: (i, 0)),
            pl.BlockSpec((1, D), lambda i, c: (0, 0)),
            pl.BlockSpec((D, FF_CHUNK), lambda i, c: (0, c)),
            pl.BlockSpec((D, FF_CHUNK), lambda i, c: (0, c)),
            pl.BlockSpec((FF_CHUNK, D), lambda i, c: (c, 0)),
        ],
        out_specs=pl.BlockSpec((T, D), lambda i, c: (i, 0)),
        out_shape=jax.ShapeDtypeStruct((S, D), F32),
        scratch_shapes=[pltpu.VMEM((T, D), BF16), pltpu.VMEM((T, D), F32)],
        compiler_params=pltpu.CompilerParams(dimension_semantics=("arbitrary", "arbitrary"),
                                             vmem_limit_bytes=VMEM_LIMIT),
        name="dense_ffn",
    )(x, g.reshape(1, D), w_gate, w_up, w_down)


KV_EXT = N_KV_HEADS * LANES
QKV_EXT = N_HEADS * HEAD_DIM + 2 * KV_EXT


def _attn_kernel(sink_ref, x_ref, pos_ref, g_ref, freq_ref, sign_ref, wqkv_ref, bqkv_ref, wo_ref, bo_ref,
                 g2_ref, wr_ref, o_ref, route_ref,
                 qlo_scr, qhi_scr, k_scr, vlo_scr, vhi_scr, o_scr):
    T = x_ref.shape[0]
    D = D_MODEL
    B = ATTN_BLOCK
    first_tile = pl.program_id(0) == 0

    @pl.when(first_tile)
    def _():
        k_scr[0:B, :] = jnp.zeros((B, KV_EXT), BF16)
        vlo_scr[0:B, :] = jnp.zeros((B, KV_EXT), BF16)
        vhi_scr[0:B, :] = jnp.zeros((B, KV_EXT), BF16)

    x = x_ref[...]
    hn = _rmsnorm(x, g_ref[...]).astype(BF16)
    qkv = jnp.dot(hn, wqkv_ref[...], preferred_element_type=F32) + bqkv_ref[...]

    ang = pos_ref[...] * freq_ref[...]
    cos_t = jnp.cos(ang)
    sin_t = jnp.sin(ang) * sign_ref[...]
    lane = lax.broadcasted_iota(jnp.int32, (T, LANES), 1)
    first_half = (lane % HEAD_DIM) < (ROT_DIM // 2)
    lo_half = lane < HEAD_DIM

    def rope(col):
        partner = jnp.where(first_half, pltpu.roll(col, LANES - ROT_DIM // 2, 1), pltpu.roll(col, ROT_DIM // 2, 1))
        return col * cos_t + partner * sin_t

    scale = HEAD_DIM ** -0.5
    for c in range(D // LANES):
        qc = rope(qkv[:, LANES * c:LANES * (c + 1)]) * scale
        qlo_scr[:, LANES * c:LANES * (c + 1)] = jnp.where(lo_half, qc, 0.0).astype(BF16)
        qhi_scr[:, LANES * c:LANES * (c + 1)] = jnp.where(lo_half, 0.0, qc).astype(BF16)
    for g in range(N_KV_HEADS):
        kc = rope(qkv[:, D + LANES * g:D + LANES * (g + 1)])
        k_scr[B:B + T, LANES * g:LANES * (g + 1)] = kc.astype(BF16)
        vc = qkv[:, D + KV_EXT + LANES * g:D + KV_EXT + LANES * (g + 1)]
        vlo_scr[B:B + T, LANES * g:LANES * (g + 1)] = jnp.where(lo_half, vc, 0.0).astype(BF16)
        vhi_scr[B:B + T, LANES * g:LANES * (g + 1)] = jnp.where(lo_half, 0.0, vc).astype(BF16)

    qi = lax.broadcasted_iota(jnp.int32, (B, 2 * B), 0)
    kj = lax.broadcasted_iota(jnp.int32, (B, 2 * B), 1)
    band_prev = (kj < B) & (kj > qi)
    band_own = (kj >= B) & (kj - B <= qi)

    def block_body(blk, _):
        r0 = pl.multiple_of(blk * B, B)
        k_lo = jnp.where(first_tile & (blk == 0), B, 0)
        allowed = band_own | (band_prev & (kj >= k_lo))
        for g in range(N_KV_HEADS):
            kk = k_scr[pl.ds(r0, 2 * B), LANES * g:LANES * (g + 1)]
            v_lo = vlo_scr[pl.ds(r0, 2 * B), LANES * g:LANES * (g + 1)]
            v_hi = vhi_scr[pl.ds(r0, 2 * B), LANES * g:LANES * (g + 1)]
            for p in range(GROUP // 2):
                c = (GROUP // 2) * g + p
                parts = []
                for half, (q_scr, vv) in enumerate(((qlo_scr, v_lo), (qhi_scr, v_hi))):
                    qh = q_scr[pl.ds(r0, B), LANES * c:LANES * (c + 1)]
                    s = lax.dot_general(qh, kk, (((1,), (1,)), ((), ())), preferred_element_type=F32)
                    s = jnp.where(allowed, s, NEG_INF)
                    sink = sink_ref[2 * c + half]
                    m = jnp.maximum(jnp.max(s, axis=-1, keepdims=True), sink)
                    pexp = jnp.exp(s - m)
                    denom = jnp.sum(pexp, axis=-1, keepdims=True) + jnp.exp(sink - m)
                    parts.append(jnp.dot(pexp.astype(BF16), vv, preferred_element_type=F32) / denom)
                o_scr[pl.ds(r0, B), LANES * c:LANES * (c + 1)] = (parts[0] + parts[1]).astype(BF16)
        return 0

    lax.fori_loop(0, T // B, block_body, 0)

    k_scr[0:B, :] = k_scr[T:T + B, :]
    vlo_scr[0:B, :] = vlo_scr[T:T + B, :]
    vhi_scr[0:B, :] = vhi_scr[T:T + B, :]

    h1 = jnp.dot(o_scr[...], wo_ref[...], preferred_element_type=F32) + bo_ref[...] + x
    _store_token_major(o_ref, h1)

    hn2 = _rmsnorm(h1, g2_ref[...]).astype(BF16)
    logits = jnp.dot(hn2, wr_ref[...], preferred_element_type=F32)
    lane_f = lane.astype(F32)
    lg = jnp.where(lane < N_EXPERTS, logits, -jnp.inf)
    v1 = jnp.max(lg, axis=-1, keepdims=True)
    i1 = jnp.min(jnp.where(lg == v1, lane_f, float(LANES)), axis=-1, keepdims=True)
    lg2 = jnp.where(lane_f == i1, -jnp.inf, lg)
    v2 = jnp.max(lg2, axis=-1, keepdims=True)
    i2 = jnp.min(jnp.where(lg2 == v2, lane_f, float(LANES)), axis=-1, keepdims=True)
    e2 = jnp.exp(v2 - v1)
    gate1 = 1.0 / (1.0 + e2)
    gate2 = e2 / (1.0 + e2)
    route_ref[...] = jnp.where(lane == 0, i1, jnp.where(lane == 1, i2,
                                                        jnp.where(lane == 2, gate1, jnp.where(lane == 3, gate2, 0.0))))


def _attn_mixer(x, pos_f, g, freq_lane, sign_lane, w_qkv_ext, b_qkv_ext, sinks, w_o, b_o, g2, w_router_pad):
    S, D = x.shape
    T = SEQ_TILE
    B = ATTN_BLOCK
    grid_spec = pltpu.PrefetchScalarGridSpec(
        num_scalar_prefetch=1,
        grid=(S // T,),
        in_specs=[
            pl.BlockSpec((T, D), lambda i, s: (i, 0)),
            pl.BlockSpec((T, 1), lambda i, s: (i, 0)),
            _const_spec((1, D)),
            _const_spec((1, LANES)),
            _const_spec((1, LANES)),
            _const_spec((D, QKV_EXT)),
            _const_spec((1, QKV_EXT)),
            _const_spec((D, D)),
            _const_spec((1, D)),
            _const_spec((1, D)),
            _const_spec((D, LANES)),
        ],
        out_specs=[pl.BlockSpec((T * ROWS_PER_TOKEN, LANES), lambda i, s: (i, 0)),
                   pl.BlockSpec((T, LANES), lambda i, s: (i, 0))],
        scratch_shapes=[
            pltpu.VMEM((T, D), BF16), pltpu.VMEM((T, D), BF16),
            pltpu.VMEM((T + B, KV_EXT), BF16), pltpu.VMEM((T + B, KV_EXT), BF16), pltpu.VMEM((T + B, KV_EXT), BF16),
            pltpu.VMEM((T, D), BF16),
        ],
    )
    return pl.pallas_call(
        _attn_kernel,
        grid_spec=grid_spec,
        out_shape=[jax.ShapeDtypeStruct((S * ROWS_PER_TOKEN, LANES), F32), jax.ShapeDtypeStruct((S, LANES), F32)],
        compiler_params=pltpu.CompilerParams(dimension_semantics=("arbitrary",), vmem_limit_bytes=VMEM_LIMIT),
        name="attn_mixer",
    )(sinks, x, pos_f, g.reshape(1, D), freq_lane, sign_lane, w_qkv_ext, b_qkv_ext, w_o, b_o.reshape(1, D),
      g2.reshape(1, D), w_router_pad)


def _moe_kernel(texp_ref, nvt_ref, src_ref, dst_ref, h_hbm, g_ref, wg_ref, wu_ref, wd_ref, y_hbm,
                x_scr, xn_scr, acc_scr, y_scr, sems):
    i = pl.program_id(0)
    c = pl.program_id(1)
    T = xn_scr.shape[0]
    R = ROWS_PER_TOKEN
    valid = i < nvt_ref[0]

    def row_copy(src, src_row, dst, dst_row, sem):
        return pltpu.make_async_copy(src.at[pl.ds(pl.multiple_of(src_row, R), R), :],
                                     dst.at[pl.ds(pl.multiple_of(dst_row, R), R), :], sem)

    @pl.when((i == 0) & (c == 0))
    def _():
        y_scr[...] = jnp.zeros_like(y_scr)
        pad = y_hbm.at[pl.ds(y_hbm.shape[0] - T * R, T * R), :]
        zero_copy = pltpu.make_async_copy(y_scr, pad, sems.at[1])
        zero_copy.start()
        zero_copy.wait()

    @pl.when(valid & (c == 0))
    def _():
        def issue(r, carry):
            row_copy(h_hbm, src_ref[0, 0, r], x_scr, r * R, sems.at[0]).start()
            return carry
        lax.fori_loop(0, T, issue, 0)
        pltpu.make_async_copy(h_hbm.at[pl.ds(0, T * R), :], x_scr, sems.at[0]).wait()
        xn_scr[...] = _rmsnorm(_load_token_major(x_scr, T), g_ref[...]).astype(BF16)
        acc_scr[...] = jnp.zeros_like(acc_scr)

    @pl.when(valid)
    def _():
        xn = xn_scr[...]
        gate = jnp.dot(xn, wg_ref[0], preferred_element_type=F32)
        up = jnp.dot(xn, wu_ref[0], preferred_element_type=F32)
        mid = (jax.nn.silu(gate) * up).astype(BF16)
        acc_scr[...] += jnp.dot(mid, wd_ref[0], preferred_element_type=F32)

    @pl.when(valid & (c == pl.num_programs(1) - 1))
    def _():
        _store_token_major(y_scr, acc_scr[...])

        def issue(r, carry):
            row_copy(y_scr, r * R, y_hbm, dst_ref[0, 0, r], sems.at[1]).start()
            return carry
        lax.fori_loop(0, T, issue, 0)
        pltpu.make_async_copy(y_scr, y_hbm.at[pl.ds(0, T * R), :], sems.at[1]).wait()


def _moe_experts(h1_tm, g, tile_expert, n_valid_tiles, src_rows, dst_rows, w_gate, w_up, w_down, n_out_tokens):
    D = D_MODEL
    T = MOE_TILE
    NT = src_rows.shape[0]
    NC = N_FF_CHUNKS

    def chunk(i, c, nvt):
        return jnp.where(i < nvt[0], c, NC - 1)

    grid_spec = pltpu.PrefetchScalarGridSpec(
        num_scalar_prefetch=2,
        grid=(NT, NC),
        in_specs=[
            pl.BlockSpec((1, 1, T), lambda i, c, te, nvt: (i, 0, 0), memory_space=pltpu.SMEM),
            pl.BlockSpec((1, 1, T), lambda i, c, te, nvt: (i, 0, 0), memory_space=pltpu.SMEM),
            pl.BlockSpec(memory_space=pl.ANY),
            pl.BlockSpec((1, D), lambda i, c, te, nvt: (0, 0)),
            pl.BlockSpec((1, D, FF_CHUNK), lambda i, c, te, nvt: (te[i], 0, chunk(i, c, nvt))),
            pl.BlockSpec((1, D, FF_CHUNK), lambda i, c, te, nvt: (te[i], 0, chunk(i, c, nvt))),
            pl.BlockSpec((1, FF_CHUNK, D), lambda i, c, te, nvt: (te[i], chunk(i, c, nvt), 0)),
        ],
        out_specs=pl.BlockSpec(memory_space=pl.ANY),
        scratch_shapes=[pltpu.VMEM((T * ROWS_PER_TOKEN, LANES), F32), pltpu.VMEM((T, D), BF16),
                        pltpu.VMEM((T, D), F32), pltpu.VMEM((T * ROWS_PER_TOKEN, LANES), F32),
                        pltpu.SemaphoreType.DMA((2,))],
    )
    return pl.pallas_call(
        _moe_kernel,
        grid_spec=grid_spec,
        out_shape=jax.ShapeDtypeStruct((n_out_tokens * ROWS_PER_TOKEN, LANES), F32),
        compiler_params=pltpu.CompilerParams(dimension_semantics=("arbitrary", "arbitrary"),
                                             vmem_limit_bytes=VMEM_LIMIT),
        name="moe_experts",
    )(tile_expert, n_valid_tiles, src_rows, dst_rows, h1_tm, g.reshape(1, D), w_gate, w_up, w_down)


def _moe_plan(route, n_tokens):
    T = MOE_TILE
    n_assign = TOP_K * n_tokens
    n_tiles = n_assign // T + N_EXPERTS
    e_flat = route[:, 0:TOP_K].astype(jnp.int32).reshape(-1)
    order = jnp.argsort(e_flat, stable=True).astype(jnp.int32)
    counts = jnp.sum((e_flat[:, None] == jnp.arange(N_EXPERTS, dtype=jnp.int32)[None, :]).astype(jnp.int32), axis=0)
    tiles_per = (counts + T - 1) // T
    tile_end = jnp.cumsum(tiles_per)
    n_valid = tile_end[-1]
    tile_ids = jnp.arange(n_tiles, dtype=jnp.int32)
    last_valid = jnp.maximum(n_valid - 1, 0)
    texp = jnp.searchsorted(tile_end, jnp.minimum(tile_ids, last_valid), side="right").astype(jnp.int32)
    texp = jnp.minimum(texp, N_EXPERTS - 1)
    first_tile = tile_end - tiles_per
    group_start = jnp.cumsum(counts) - counts
    j = tile_ids - first_tile[texp]
    start = group_start[texp] + j * T
    n_rows = jnp.where(tile_ids < n_valid, jnp.clip(counts[texp] - j * T, 0, T), 0)
    order_pad = jnp.concatenate([order, jnp.zeros((T,), jnp.int32)])
    start = jnp.clip(start, 0, n_assign)
    assign = jax.vmap(lambda s: lax.dynamic_slice(order_pad, (s,), (T,)))(start)
    r = jnp.arange(T, dtype=jnp.int32)[None, :]
    token = assign // TOP_K
    slot = assign % TOP_K
    row_valid = r < n_rows[:, None]
    dst = jnp.where(row_valid, slot * n_tokens + token, TOP_K * n_tokens + r)
    src_rows = (token * ROWS_PER_TOKEN).reshape(n_tiles, 1, T)
    dst_rows = (dst * ROWS_PER_TOKEN).reshape(n_tiles, 1, T)
    return texp, n_valid.reshape(1).astype(jnp.int32), src_rows, dst_rows


def _combine_kernel(h_ref, y0_ref, y1_ref, route_ref, g_ref, o_ref):
    T = o_ref.shape[0]
    route = route_ref[...]
    moe = route[:, 2:3] * _load_token_major(y0_ref, T) + route[:, 3:4] * _load_token_major(y1_ref, T)
    o_ref[...] = _rmsnorm(_load_token_major(h_ref, T) + moe, g_ref[...])


def _combine(h1_tm, y2_tm, route, g):
    D = D_MODEL
    S = route.shape[0]
    T = OUT_TILE
    nb = S // T
    tm_block = (T * ROWS_PER_TOKEN, LANES)
    return pl.pallas_call(
        _combine_kernel,
        grid=(nb,),
        in_specs=[
            pl.BlockSpec(tm_block, lambda i: (i, 0)),
            pl.BlockSpec(tm_block, lambda i: (i, 0)),
            pl.BlockSpec(tm_block, lambda i: (i + nb, 0)),
            pl.BlockSpec((T, LANES), lambda i: (i, 0)),
            pl.BlockSpec((1, D), lambda i: (0, 0)),
        ],
        out_specs=pl.BlockSpec((T, D), lambda i: (i, 0)),
        out_shape=jax.ShapeDtypeStruct((S, D), F32),
        compiler_params=pltpu.CompilerParams(dimension_semantics=("arbitrary",), vmem_limit_bytes=VMEM_LIMIT),
        name="moe_combine",
    )(h1_tm, y2_tm, y2_tm, route, g.reshape(1, D))


def kernel(x, positions, norm_mix, norm_ffn, norm_final, lru_w_in, lru_conv_w, lru_conv_b, lru_w_a, lru_b_a, lru_w_x, lru_b_x, lru_lambda, lru_w_out, attn_w_qkv, attn_b_qkv, attn_sinks, attn_w_o, attn_b_o, ffn_w_gate, ffn_w_up, ffn_w_down, moe_w_router, moe_w_gate, moe_w_up, moe_w_down):
    B, S, D = x.shape
    assert B == 1 and D == D_MODEL and S % FFN_TILE == 0
    h = x.reshape(S, D)

    w_ax = jnp.concatenate([lru_w_a[0], lru_w_x[0]], axis=-1).astype(BF16)
    h = _lru_mixer(h, norm_mix[0], lru_w_in[0].astype(BF16), lru_conv_w[0], lru_conv_b[0], w_ax,
                   lru_b_a[0], lru_b_x[0], lru_lambda[0], lru_w_out[0].astype(BF16))
    h = _dense_ffn(h, norm_ffn[0], ffn_w_gate[0].astype(BF16), ffn_w_up[0].astype(BF16), ffn_w_down[0].astype(BF16))

    q_dim = N_HEADS * HEAD_DIM
    kv_dim = N_KV_HEADS * HEAD_DIM

    def dup_heads(w):
        w4 = w.reshape(w.shape[:-1] + (N_KV_HEADS, 1, HEAD_DIM))
        return jnp.broadcast_to(w4, w.shape[:-1] + (N_KV_HEADS, 2, HEAD_DIM)).reshape(w.shape[:-1] + (KV_EXT,))

    wqkv, bqkv = attn_w_qkv[0], attn_b_qkv[0]
    w_qkv_ext = jnp.concatenate([wqkv[:, :q_dim], dup_heads(wqkv[:, q_dim:q_dim + kv_dim]),
                                 dup_heads(wqkv[:, q_dim + kv_dim:])], axis=1).astype(BF16)
    b_qkv_ext = jnp.concatenate([bqkv[:q_dim], dup_heads(bqkv[q_dim:q_dim + kv_dim]),
                                 dup_heads(bqkv[q_dim + kv_dim:])]).reshape(1, QKV_EXT)
    inv_freq = ROPE_THETA ** (-jnp.arange(0, ROT_DIM, 2, dtype=F32) / ROT_DIM)
    d_in_head = jnp.arange(LANES) % HEAD_DIM
    freq_lane = jnp.where(d_in_head < ROT_DIM, inv_freq[d_in_head % (ROT_DIM // 2)], 0.0).reshape(1, LANES)
    sign_lane = jnp.where(d_in_head < ROT_DIM // 2, -1.0, jnp.where(d_in_head < ROT_DIM, 1.0, 0.0))
    sign_lane = sign_lane.astype(F32).reshape(1, LANES)
    w_router_pad = jnp.pad(moe_w_router[0], ((0, 0), (0, LANES - N_EXPERTS))).astype(BF16)
    pos_f = positions.reshape(S, 1).astype(F32)
    h1_tm, route = _attn_mixer(h, pos_f, norm_mix[1], freq_lane, sign_lane, w_qkv_ext, b_qkv_ext,
                               attn_sinks[0], attn_w_o[0].astype(BF16), attn_b_o[0], norm_ffn[1], w_router_pad)

    texp, n_valid, src_rows, dst_rows = _moe_plan(route, S)
    y2_tm = _moe_experts(h1_tm, norm_ffn[1], texp, n_valid, src_rows, dst_rows,
                         moe_w_gate[0].astype(BF16), moe_w_up[0].astype(BF16), moe_w_down[0].astype(BF16),
                         TOP_K * S + MOE_TILE)
    out = _combine(h1_tm, y2_tm, route, norm_final)
    return out.reshape(B, S, D)
```

```python
import functools

import jax
import jax.numpy as jnp
from jax import lax
from jax.experimental import pallas as pl
from jax.experimental.pallas import tpu as pltpu

F32 = jnp.float32
BF16 = jnp.bfloat16

D_MODEL = 1024
N_RNN_BLOCKS = 8
RNN_BLOCK = D_MODEL // N_RNN_BLOCKS
CONV_WIDTH = 4
LRU_C = 8.0
N_HEADS = 16
N_KV_HEADS = 4
HEAD_DIM = 64
GROUP = N_HEADS // N_KV_HEADS
WINDOW = 128
ATTN_BLOCK = 128
ROPE_THETA = 500000.0
ROT_DIM = HEAD_DIM // 4
D_FF = 3584
N_EXPERTS = 8
TOP_K = 2
EPS = 1e-6
NEG_INF = -1e30

LANES = 128
SUBLANES = 8
VMEM_LIMIT = 56 * 1024 * 1024

SEQ_TILE = 512
FFN_TILE = 1024
FF_CHUNK = 512
N_FF_CHUNKS = D_FF // FF_CHUNK
MOE_TILE = 512
N_MOE_CHUNKS = 2
OUT_TILE = 1024


def _rmsnorm(x, g):
    return x * lax.rsqrt(jnp.mean(x * x, axis=-1, keepdims=True) + EPS) * g


ROWS_PER_TOKEN = D_MODEL // LANES


def _store_token_major(ref, val):
    T = val.shape[0]
    for c in range(ROWS_PER_TOKEN):
        ref[pl.ds(c, T, stride=ROWS_PER_TOKEN), :] = val[:, LANES * c:LANES * (c + 1)]


def _load_token_major(ref, T):
    return jnp.concatenate([ref[pl.ds(c, T, stride=ROWS_PER_TOKEN), :] for c in range(ROWS_PER_TOKEN)], axis=1)


def _const_spec(shape):
    n = len(shape)
    return pl.BlockSpec(shape, lambda *_: (0,) * n, pipeline_mode=pl.Buffered(1))


def _lru_kernel(x_ref, g_ref, win_ref, cw_ref, cb_ref, wax_ref, ba_ref, bx_ref, lam_ref, wout_ref,
                o_ref, xb_scr, carry_scr):
    T = x_ref.shape[0]
    D = D_MODEL

    @pl.when(pl.program_id(0) == 0)
    def _():
        xb_scr[0:SUBLANES, :] = jnp.zeros((SUBLANES, D), F32)
        carry_scr[...] = jnp.zeros((1, D), F32)

    x = x_ref[...]
    hn = _rmsnorm(x, g_ref[...]).astype(BF16)
    proj = jnp.dot(hn, win_ref[...], preferred_element_type=F32)
    y = jax.nn.gelu(proj[:, :D], approximate=True)
    xb = proj[:, D:]

    xb_scr[SUBLANES:SUBLANES + T, :] = xb
    cw = cw_ref[...]
    xc = (cb_ref[...] + xb * cw[3:4, :]
          + xb_scr[SUBLANES - 1:SUBLANES - 1 + T, :] * cw[2:3, :]
          + xb_scr[SUBLANES - 2:SUBLANES - 2 + T, :] * cw[1:2, :]
          + xb_scr[SUBLANES - 3:SUBLANES - 3 + T, :] * cw[0:1, :])
    xb_scr[0:SUBLANES, :] = xb[T - SUBLANES:T, :]

    xcb = xc.astype(BF16)
    r_parts, i_parts = [], []
    for n in range(N_RNN_BLOCKS):
        gn = jnp.dot(xcb[:, RNN_BLOCK * n:RNN_BLOCK * (n + 1)], wax_ref[n], preferred_element_type=F32)
        r_parts.append(gn[:, :RNN_BLOCK])
        i_parts.append(gn[:, RNN_BLOCK:])
    r = jax.nn.sigmoid(jnp.concatenate(r_parts, axis=1) + ba_ref[...])
    ig = jax.nn.sigmoid(jnp.concatenate(i_parts, axis=1) + bx_ref[...])
    log_a = -LRU_C * r * jax.nn.softplus(-lam_ref[...])
    a = jnp.exp(log_a)
    b = jnp.sqrt(jnp.tanh(-log_a) * (1.0 + a * a)) * (ig * xc)

    G = T // SUBLANES
    a3 = a.reshape(G, SUBLANES, D)
    b3 = b.reshape(G, SUBLANES, D)
    sub = lax.broadcasted_iota(jnp.int32, (G, SUBLANES, D), 1)
    for k in (1, 2, 4):
        m = sub >= k
        a_sh = jnp.where(m, pltpu.roll(a3, k, 1), 1.0)
        b_sh = jnp.where(m, pltpu.roll(b3, k, 1), 0.0)
        b3 = b3 + a3 * b_sh
        a3 = a3 * a_sh
    hprev = carry_scr[...]
    hs = []
    for gi in range(G):
        hg = b3[gi] + a3[gi] * hprev
        hs.append(hg)
        hprev = hg[SUBLANES - 1:SUBLANES, :]
    carry_scr[...] = hprev
    h = jnp.concatenate(hs, axis=0)

    o_ref[...] = jnp.dot((h * y).astype(BF16), wout_ref[...], preferred_element_type=F32) + x


def _lru_mixer(x, g, w_in, conv_w, conv_b, w_ax, b_a, b_x, lam, w_out):
    S, D = x.shape
    T = SEQ_TILE
    row = lambda v: v.reshape(1, D)
    return pl.pallas_call(
        _lru_kernel,
        grid=(S // T,),
        in_specs=[
            pl.BlockSpec((T, D), lambda i: (i, 0)),
            _const_spec((1, D)),
            _const_spec((D, 2 * D)),
            _const_spec((CONV_WIDTH, D)),
            _const_spec((1, D)),
            _const_spec((N_RNN_BLOCKS, RNN_BLOCK, 2 * RNN_BLOCK)),
            _const_spec((1, D)),
            _const_spec((1, D)),
            _const_spec((1, D)),
            _const_spec((D, D)),
        ],
        out_specs=pl.BlockSpec((T, D), lambda i: (i, 0)),
        out_shape=jax.ShapeDtypeStruct((S, D), F32),
        scratch_shapes=[pltpu.VMEM((T + SUBLANES, D), F32), pltpu.VMEM((1, D), F32)],
        compiler_params=pltpu.CompilerParams(dimension_semantics=("arbitrary",), vmem_limit_bytes=VMEM_LIMIT),
        name="lru_mixer",
    )(x, row(g), w_in, conv_w, row(conv_b), w_ax, row(b_a), row(b_x), row(lam), w_out)


def _ffn_kernel(x_ref, g_ref, wg_ref, wu_ref, wd_ref, o_ref, xn_scr, acc_scr):
    c = pl.program_id(1)

    @pl.when(c == 0)
    def _():
        xn_scr[...] = _rmsnorm(x_ref[...], g_ref[...]).astype(BF16)
        acc_scr[...] = jnp.zeros_like(acc_scr)

    xn = xn_scr[...]
    gate = jnp.dot(xn, wg_ref[...], preferred_element_type=F32)
    up = jnp.dot(xn, wu_ref[...], preferred_element_type=F32)
    mid = (jax.nn.silu(gate) * up).astype(BF16)
    acc_scr[...] += jnp.dot(mid, wd_ref[...], preferred_element_type=F32)

    @pl.when(c == pl.num_programs(1) - 1)
    def _():
        o_ref[...] = x_ref[...] + acc_scr[...]


def _dense_ffn(x, g, w_gate, w_up, w_down):
    S, D = x.shape
    T = FFN_TILE
    return pl.pallas_call(
        _ffn_kernel,
        grid=(S // T, N_FF_CHUNKS),
        in_specs=[
            pl.BlockSpec((T, D), lambda i, c: (i, 0)),
            pl.BlockSpec((1, D), lambda i, c: (0, 0)),
            pl.BlockSpec((D, FF_CHUNK), lambda i, c: (0, c)),
            pl.BlockSpec((D, FF_CHUNK), lambda i, c: (0, c)),
            pl.BlockSpec((FF_CHUNK, D), lambda i, c: (c, 0)),
        ],
        out_specs=pl.BlockSpec((T, D), lambda i, c: (i, 0)),
        out_shape=jax.ShapeDtypeStruct((S, D), F32),
        scratch_shapes=[pltpu.VMEM((T, D), BF16), pltpu.VMEM((T, D), F32)],
        compiler_params=pltpu.CompilerParams(dimension_semantics=("arbitrary", "arbitrary"),
                                             vmem_limit_bytes=VMEM_LIMIT),
        name="dense_ffn",
    )(x, g.reshape(1, D), w_gate, w_up, w_down)


KV_EXT = N_KV_HEADS * LANES
QKV_EXT = N_HEADS * HEAD_DIM + 2 * KV_EXT


def _attn_kernel(sink_ref, x_ref, pos_ref, g_ref, freq_ref, sign_ref, wqkv_ref, bqkv_ref, wo_ref, bo_ref,
                 g2_ref, wr_ref, o_ref, route_ref,
                 qlo_scr, qhi_scr, k_scr, vlo_scr, vhi_scr, o_scr):
    T = x_ref.shape[0]
    D = D_MODEL
    B = ATTN_BLOCK
    first_tile = pl.program_id(0) == 0

    @pl.when(first_tile)
    def _():
        k_scr[0:B, :] = jnp.zeros((B, KV_EXT), BF16)
        vlo_scr[0:B, :] = jnp.zeros((B, KV_EXT), BF16)
        vhi_scr[0:B, :] = jnp.zeros((B, KV_EXT), BF16)

    x = x_ref[...]
    hn = _rmsnorm(x, g_ref[...]).astype(BF16)
    qkv = jnp.dot(hn, wqkv_ref[...], preferred_element_type=F32) + bqkv_ref[...]

    ang = pos_ref[...] * freq_ref[...]
    cos_t = jnp.cos(ang)
    sin_t = jnp.sin(ang) * sign_ref[...]
    lane = lax.broadcasted_iota(jnp.int32, (T, LANES), 1)
    first_half = (lane % HEAD_DIM) < (ROT_DIM // 2)
    lo_half = lane < HEAD_DIM

    def rope(col):
        partner = jnp.where(first_half, pltpu.roll(col, LANES - ROT_DIM // 2, 1), pltpu.roll(col, ROT_DIM // 2, 1))
        return col * cos_t + partner * sin_t

    scale = HEAD_DIM ** -0.5
    for c in range(D // LANES):
        qc = rope(qkv[:, LANES * c:LANES * (c + 1)]) * scale
        qlo_scr[:, LANES * c:LANES * (c + 1)] = jnp.where(lo_half, qc, 0.0).astype(BF16)
        qhi_scr[:, LANES * c:LANES * (c + 1)] = jnp.where(lo_half, 0.0, qc).astype(BF16)
    for g in range(N_KV_HEADS):
        kc = rope(qkv[:, D + LANES * g:D + LANES * (g + 1)])
        k_scr[B:B + T, LANES * g:LANES * (g + 1)] = kc.astype(BF16)
        vc = qkv[:, D + KV_EXT + LANES * g:D + KV_EXT + LANES * (g + 1)]
        vlo_scr[B:B + T, LANES * g:LANES * (g + 1)] = jnp.where(lo_half, vc, 0.0).astype(BF16)
        vhi_scr[B:B + T, LANES * g:LANES * (g + 1)] = jnp.where(lo_half, 0.0, vc).astype(BF16)

    qi = lax.broadcasted_iota(jnp.int32, (B, 2 * B), 0)
    kj = lax.broadcasted_iota(jnp.int32, (B, 2 * B), 1)
    band_prev = (kj < B) & (kj > qi)
    band_own = (kj >= B) & (kj - B <= qi)

    def block_body(blk, _):
        r0 = pl.multiple_of(blk * B, B)
        k_lo = jnp.where(first_tile & (blk == 0), B, 0)
        allowed = band_own | (band_prev & (kj >= k_lo))
        for g in range(N_KV_HEADS):
            kk = k_scr[pl.ds(r0, 2 * B), LANES * g:LANES * (g + 1)]
            v_lo = vlo_scr[pl.ds(r0, 2 * B), LANES * g:LANES * (g + 1)]
            v_hi = vhi_scr[pl.ds(r0, 2 * B), LANES * g:LANES * (g + 1)]
            for p in range(GROUP // 2):
                c = (GROUP // 2) * g + p
                parts = []
                for half, (q_scr, vv) in enumerate(((qlo_scr, v_lo), (qhi_scr, v_hi))):
                    qh = q_scr[pl.ds(r0, B), LANES * c:LANES * (c + 1)]
                    s = lax.dot_general(qh, kk, (((1,), (1,)), ((), ())), preferred_element_type=F32)
                    s = jnp.where(allowed, s, NEG_INF)
                    sink = sink_ref[2 * c + half]
                    m = jnp.maximum(jnp.max(s, axis=-1, keepdims=True), sink)
                    pexp = jnp.exp(s - m)
                    denom = jnp.sum(pexp, axis=-1, keepdims=True) + jnp.exp(sink - m)
                    parts.append(jnp.dot(pexp.astype(BF16), vv, preferred_element_type=F32) / denom)
                o_scr[pl.ds(r0, B), LANES * c:LANES * (c + 1)] = (parts[0] + parts[1]).astype(BF16)
        return 0

    lax.fori_loop(0, T // B, block_body, 0)

    k_scr[0:B, :] = k_scr[T:T + B, :]
    vlo_scr[0:B, :] = vlo_scr[T:T + B, :]
    vhi_scr[0:B, :] = vhi_scr[T:T + B, :]

    h1 = jnp.dot(o_scr[...], wo_ref[...], preferred_element_type=F32) + bo_ref[...] + x
    _store_token_major(o_ref, h1)

    hn2 = _rmsnorm(h1, g2_ref[...]).astype(BF16)
    logits = jnp.dot(hn2, wr_ref[...], preferred_element_type=F32)
    lane_f = lane.astype(F32)
    lg = jnp.where(lane < N_EXPERTS, logits, -jnp.inf)
    v1 = jnp.max(lg, axis=-1, keepdims=True)
    i1 = jnp.min(jnp.where(lg == v1, lane_f, float(LANES)), axis=-1, keepdims=True)
    lg2 = jnp.where(lane_f == i1, -jnp.inf, lg)
    v2 = jnp.max(lg2, axis=-1, keepdims=True)
    i2 = jnp.min(jnp.where(lg2 == v2, lane_f, float(LANES)), axis=-1, keepdims=True)
    e2 = jnp.exp(v2 - v1)
    gate1 = 1.0 / (1.0 + e2)
    gate2 = e2 / (1.0 + e2)
    route_ref[...] = jnp.where(lane == 0, i1, jnp.where(lane == 1, i2,
                                                        jnp.where(lane == 2, gate1, jnp.where(lane == 3, gate2, 0.0))))


def _attn_mixer(x, pos_f, g, freq_lane, sign_lane, w_qkv_ext, b_qkv_ext, sinks, w_o, b_o, g2, w_router_pad):
    S, D = x.shape
    T = SEQ_TILE
    B = ATTN_BLOCK
    grid_spec = pltpu.PrefetchScalarGridSpec(
        num_scalar_prefetch=1,
        grid=(S // T,),
        in_specs=[
            pl.BlockSpec((T, D), lambda i, s: (i, 0)),
            pl.BlockSpec((T, 1), lambda i, s: (i, 0)),
            _const_spec((1, D)),
            _const_spec((1, LANES)),
            _const_spec((1, LANES)),
            _const_spec((D, QKV_EXT)),
            _const_spec((1, QKV_EXT)),
            _const_spec((D, D)),
            _const_spec((1, D)),
            _const_spec((1, D)),
            _const_spec((D, LANES)),
        ],
        out_specs=[pl.BlockSpec((T * ROWS_PER_TOKEN, LANES), lambda i, s: (i, 0)),
                   pl.BlockSpec((T, LANES), lambda i, s: (i, 0))],
        scratch_shapes=[
            pltpu.VMEM((T, D), BF16), pltpu.VMEM((T, D), BF16),
            pltpu.VMEM((T + B, KV_EXT), BF16), pltpu.VMEM((T + B, KV_EXT), BF16), pltpu.VMEM((T + B, KV_EXT), BF16),
            pltpu.VMEM((T, D), BF16),
        ],
    )
    return pl.pallas_call(
        _attn_kernel,
        grid_spec=grid_spec,
        out_shape=[jax.ShapeDtypeStruct((S * ROWS_PER_TOKEN, LANES), F32), jax.ShapeDtypeStruct((S, LANES), F32)],
        compiler_params=pltpu.CompilerParams(dimension_semantics=("arbitrary",), vmem_limit_bytes=VMEM_LIMIT),
        name="attn_mixer",
    )(sinks, x, pos_f, g.reshape(1, D), freq_lane, sign_lane, w_qkv_ext, b_qkv_ext, w_o, b_o.reshape(1, D),
      g2.reshape(1, D), w_router_pad)


def _moe_kernel(texp_ref, nvt_ref, src0_ref, srcn_ref, dstp_ref, dstl_ref, h_hbm, g_ref, wg_ref, wu_ref, wd_ref,
                y_hbm, x_scr, xn_scr, acc_scr, y_scr, sems):
    i = pl.program_id(0)
    c = pl.program_id(1)
    n_tiles = pl.num_programs(0)
    n_chunks = pl.num_programs(1)
    T = xn_scr.shape[0]
    R = ROWS_PER_TOKEN
    rows_per_step = T // N_MOE_CHUNKS
    valid = i < nvt_ref[0]
    slot = i % 2
    other = 1 - slot
    SCATTER_SEM = 2

    def row_copy(src, src_row, dst, dst_row, sem):
        return pltpu.make_async_copy(src.at[pl.ds(pl.multiple_of(src_row, R), R), :],
                                     dst.at[pl.ds(pl.multiple_of(dst_row, R), R), :], sem)

    def wait_gather(s):
        pltpu.make_async_copy(h_hbm.at[pl.ds(0, T * R), :], x_scr.at[s], sems.at[s]).wait()

    def wait_scatter():
        pltpu.make_async_copy(y_scr, y_hbm.at[pl.ds(0, T * R), :], sems.at[SCATTER_SEM]).wait()

    @pl.when((i == 0) & (c == 0))
    def _():
        y_scr[...] = jnp.zeros_like(y_scr)

        def issue(r, carry):
            row_copy(h_hbm, src0_ref[0, 0, r], x_scr.at[0], r * R, sems.at[0]).start()
            return carry
        lax.fori_loop(0, T, issue, 0)

    @pl.when(c == 0)
    def _():
        wait_gather(slot)

    @pl.when(valid & (c == 0))
    def _():
        xn_scr[...] = _rmsnorm(_load_token_major(x_scr.at[slot], T), g_ref[...]).astype(BF16)
        acc_scr[...] = jnp.zeros_like(acc_scr)

    def step_dmas():
        base = c * rows_per_step
        for j in range(rows_per_step):
            r = base + j
            row_copy(h_hbm, srcn_ref[0, 0, r], x_scr.at[other], r * R, sems.at[other]).start()
            row_copy(y_scr, r * R, y_hbm, dstp_ref[0, 0, r], sems.at[SCATTER_SEM]).start()

    @pl.when(valid)
    def _():
        step_dmas()
        xn = xn_scr[...]
        gate = jnp.dot(xn, wg_ref[0], preferred_element_type=F32)
        up = jnp.dot(xn, wu_ref[0], preferred_element_type=F32)
        mid = (jax.nn.silu(gate) * up).astype(BF16)
        acc_scr[...] += jnp.dot(mid, wd_ref[0], preferred_element_type=F32)

    @pl.when(jnp.logical_not(valid))
    def _():
        step_dmas()

    @pl.when(c == n_chunks - 1)
    def _():
        wait_scatter()

    @pl.when(valid & (c == n_chunks - 1))
    def _():
        _store_token_major(y_scr, acc_scr[...])

    @pl.when((i == n_tiles - 1) & (c == n_chunks - 1))
    def _():
        wait_gather(other)

        def issue(r, carry):
            row_copy(y_scr, r * R, y_hbm, dstl_ref[0, 0, r], sems.at[SCATTER_SEM]).start()
            return carry
        lax.fori_loop(0, T, issue, 0)
        wait_scatter()


def _moe_experts(h1_tm, g, tile_expert, n_valid_tiles, src_rows, dst_rows, w_gate, w_up, w_down, n_out_tokens):
    D = D_MODEL
    T = MOE_TILE
    NT = src_rows.shape[0]
    NC = N_MOE_CHUNKS
    FC = D_FF // NC

    def chunk(i, c, nvt):
        return jnp.where(i < nvt[0], c, NC - 1)

    def idx_spec(index_map):
        return pl.BlockSpec((1, 1, T), index_map, memory_space=pltpu.SMEM)

    grid_spec = pltpu.PrefetchScalarGridSpec(
        num_scalar_prefetch=2,
        grid=(NT, NC),
        in_specs=[
            idx_spec(lambda i, c, te, nvt: (0, 0, 0)),
            idx_spec(lambda i, c, te, nvt: (jnp.minimum(i + 1, NT - 1), 0, 0)),
            idx_spec(lambda i, c, te, nvt: (i, 0, 0)),
            idx_spec(lambda i, c, te, nvt: (NT, 0, 0)),
            pl.BlockSpec(memory_space=pl.ANY),
            pl.BlockSpec((1, D), lambda i, c, te, nvt: (0, 0)),
            pl.BlockSpec((1, D, FC), lambda i, c, te, nvt: (te[i], 0, chunk(i, c, nvt))),
            pl.BlockSpec((1, D, FC), lambda i, c, te, nvt: (te[i], 0, chunk(i, c, nvt))),
            pl.BlockSpec((1, FC, D), lambda i, c, te, nvt: (te[i], chunk(i, c, nvt), 0)),
        ],
        out_specs=pl.BlockSpec(memory_space=pl.ANY),
        scratch_shapes=[pltpu.VMEM((2, T * ROWS_PER_TOKEN, LANES), F32), pltpu.VMEM((T, D), BF16),
                        pltpu.VMEM((T, D), F32), pltpu.VMEM((T * ROWS_PER_TOKEN, LANES), F32),
                        pltpu.SemaphoreType.DMA((3,))],
    )
    return pl.pallas_call(
        _moe_kernel,
        grid_spec=grid_spec,
        out_shape=jax.ShapeDtypeStruct((n_out_tokens * ROWS_PER_TOKEN, LANES), F32),
        compiler_params=pltpu.CompilerParams(dimension_semantics=("arbitrary", "arbitrary"),
                                             vmem_limit_bytes=VMEM_LIMIT),
        name="moe_experts",
    )(tile_expert, n_valid_tiles, src_rows, src_rows, dst_rows, dst_rows, h1_tm, g.reshape(1, D),
      w_gate, w_up, w_down)


def _moe_plan(route, n_tokens):
    T = MOE_TILE
    n_assign = TOP_K * n_tokens
    n_tiles = n_assign // T + N_EXPERTS
    e_flat = route[:, 0:TOP_K].astype(jnp.int32).reshape(-1)
    order = jnp.argsort(e_flat, stable=True).astype(jnp.int32)
    counts = jnp.sum((e_flat[:, None] == jnp.arange(N_EXPERTS, dtype=jnp.int32)[None, :]).astype(jnp.int32), axis=0)
    tiles_per = (counts + T - 1) // T
    tile_end = jnp.cumsum(tiles_per)
    n_valid = tile_end[-1]
    tile_ids = jnp.arange(n_tiles, dtype=jnp.int32)
    last_valid = jnp.maximum(n_valid - 1, 0)
    texp = jnp.sum((jnp.minimum(tile_ids, last_valid)[:, None] >= tile_end[None, :]).astype(jnp.int32), axis=1)
    texp = jnp.minimum(texp, N_EXPERTS - 1)
    onehot = (texp[:, None] == jnp.arange(N_EXPERTS, dtype=jnp.int32)[None, :]).astype(jnp.int32)
    pick = lambda v: jnp.sum(onehot * v[None, :], axis=1)
    j = tile_ids - pick(tile_end - tiles_per)
    start = pick(jnp.cumsum(counts) - counts) + j * T
    n_rows = jnp.where(tile_ids < n_valid, jnp.clip(pick(counts) - j * T, 0, T), 0)
    r = jnp.arange(T, dtype=jnp.int32)[None, :]
    assign = order[jnp.clip(start[:, None] + r, 0, n_assign - 1)]
    token = assign // TOP_K
    slot = assign % TOP_K
    row_valid = r < n_rows[:, None]
    pad_rows = jnp.broadcast_to(TOP_K * n_tokens + r, (1, T))
    dst = jnp.where(row_valid, slot * n_tokens + token, pad_rows)
    dst = jnp.concatenate([pad_rows, dst], axis=0)
    src_rows = (token * ROWS_PER_TOKEN).reshape(n_tiles, 1, T)
    dst_rows = (dst * ROWS_PER_TOKEN).reshape(n_tiles + 1, 1, T)
    return texp, n_valid.reshape(1).astype(jnp.int32), src_rows, dst_rows


def _combine_kernel(h_ref, y0_ref, y1_ref, route_ref, g_ref, o_ref):
    T = o_ref.shape[0]
    route = route_ref[...]
    moe = route[:, 2:3] * _load_token_major(y0_ref, T) + route[:, 3:4] * _load_token_major(y1_ref, T)
    o_ref[...] = _rmsnorm(_load_token_major(h_ref, T) + moe, g_ref[...])


def _combine(h1_tm, y2_tm, route, g):
    D = D_MODEL
    S = route.shape[0]
    T = OUT_TILE
    nb = S // T
    tm_block = (T * ROWS_PER_TOKEN, LANES)
    return pl.pallas_call(
        _combine_kernel,
        grid=(nb,),
        in_specs=[
            pl.BlockSpec(tm_block, lambda i: (i, 0)),
            pl.BlockSpec(tm_block, lambda i: (i, 0)),
            pl.BlockSpec(tm_block, lambda i: (i + nb, 0)),
            pl.BlockSpec((T, LANES), lambda i: (i, 0)),
            pl.BlockSpec((1, D), lambda i: (0, 0)),
        ],
        out_specs=pl.BlockSpec((T, D), lambda i: (i, 0)),
        out_shape=jax.ShapeDtypeStruct((S, D), F32),
        compiler_params=pltpu.CompilerParams(dimension_semantics=("arbitrary",), vmem_limit_bytes=VMEM_LIMIT),
        name="moe_combine",
    )(h1_tm, y2_tm, y2_tm, route, g.reshape(1, D))


def kernel(x, positions, norm_mix, norm_ffn, norm_final, lru_w_in, lru_conv_w, lru_conv_b, lru_w_a, lru_b_a, lru_w_x, lru_b_x, lru_lambda, lru_w_out, attn_w_qkv, attn_b_qkv, attn_sinks, attn_w_o, attn_b_o, ffn_w_gate, ffn_w_up, ffn_w_down, moe_w_router, moe_w_gate, moe_w_up, moe_w_down):
    B, S, D = x.shape
    assert B == 1 and D == D_MODEL and S % FFN_TILE == 0
    h = x.reshape(S, D)

    w_ax = jnp.concatenate([lru_w_a[0], lru_w_x[0]], axis=-1).astype(BF16)
    h = _lru_mixer(h, norm_mix[0], lru_w_in[0].astype(BF16), lru_conv_w[0], lru_conv_b[0], w_ax,
                   lru_b_a[0], lru_b_x[0], lru_lambda[0], lru_w_out[0].astype(BF16))
    h = _dense_ffn(h, norm_ffn[0], ffn_w_gate[0].astype(BF16), ffn_w_up[0].astype(BF16), ffn_w_down[0].astype(BF16))

    q_dim = N_HEADS * HEAD_DIM
    kv_dim = N_KV_HEADS * HEAD_DIM

    def dup_heads(w):
        w4 = w.reshape(w.shape[:-1] + (N_KV_HEADS, 1, HEAD_DIM))
        return jnp.broadcast_to(w4, w.shape[:-1] + (N_KV_HEADS, 2, HEAD_DIM)).reshape(w.shape[:-1] + (KV_EXT,))

    wqkv, bqkv = attn_w_qkv[0], attn_b_qkv[0]
    w_qkv_ext = jnp.concatenate([wqkv[:, :q_dim], dup_heads(wqkv[:, q_dim:q_dim + kv_dim]),
                                 dup_heads(wqkv[:, q_dim + kv_dim:])], axis=1).astype(BF16)
    b_qkv_ext = jnp.concatenate([bqkv[:q_dim], dup_heads(bqkv[q_dim:q_dim + kv_dim]),
                                 dup_heads(bqkv[q_dim + kv_dim:])]).reshape(1, QKV_EXT)
    inv_freq = ROPE_THETA ** (-jnp.arange(0, ROT_DIM, 2, dtype=F32) / ROT_DIM)
    d_in_head = jnp.arange(LANES) % HEAD_DIM
    freq_lane = jnp.where(d_in_head < ROT_DIM, inv_freq[d_in_head % (ROT_DIM // 2)], 0.0).reshape(1, LANES)
    sign_lane = jnp.where(d_in_head < ROT_DIM // 2, -1.0, jnp.where(d_in_head < ROT_DIM, 1.0, 0.0))
    sign_lane = sign_lane.astype(F32).reshape(1, LANES)
    w_router_pad = jnp.pad(moe_w_router[0], ((0, 0), (0, LANES - N_EXPERTS))).astype(BF16)
    pos_f = positions.reshape(S, 1).astype(F32)
    h1_tm, route = _attn_mixer(h, pos_f, norm_mix[1], freq_lane, sign_lane, w_qkv_ext, b_qkv_ext,
                               attn_sinks[0], attn_w_o[0].astype(BF16), attn_b_o[0], norm_ffn[1], w_router_pad)

    texp, n_valid, src_rows, dst_rows = _moe_plan(route, S)
    y2_tm = _moe_experts(h1_tm, norm_ffn[1], texp, n_valid, src_rows, dst_rows,
                         moe_w_gate[0].astype(BF16), moe_w_up[0].astype(BF16), moe_w_down[0].astype(BF16),
                         TOP_K * S + MOE_TILE)
    out = _combine(h1_tm, y2_tm, route, norm_final)
    return out.reshape(B, S, D)
```

```python
import functools

import jax
import jax.numpy as jnp
from jax import lax
from jax.experimental import pallas as pl
from jax.experimental.pallas import tpu as pltpu

F32 = jnp.float32
BF16 = jnp.bfloat16

D_MODEL = 1024
N_RNN_BLOCKS = 8
RNN_BLOCK = D_MODEL // N_RNN_BLOCKS
CONV_WIDTH = 4
LRU_C = 8.0
N_HEADS = 16
N_KV_HEADS = 4
HEAD_DIM = 64
GROUP = N_HEADS // N_KV_HEADS
WINDOW = 128
ATTN_BLOCK = 128
ROPE_THETA = 500000.0
ROT_DIM = HEAD_DIM // 4
D_FF = 3584
N_EXPERTS = 8
TOP_K = 2
EPS = 1e-6
NEG_INF = -1e30

LANES = 128
SUBLANES = 8
VMEM_LIMIT = 56 * 1024 * 1024

SEQ_TILE = 512
FFN_TILE = 1024
FF_CHUNK = 512
N_FF_CHUNKS = D_FF // FF_CHUNK
MOE_TILE = 512
N_MOE_CHUNKS = 2
OUT_TILE = 1024


def _rmsnorm(x, g):
    return x * lax.rsqrt(jnp.mean(x * x, axis=-1, keepdims=True) + EPS) * g


ROWS_PER_TOKEN = D_MODEL // LANES


def _store_token_major(ref, val):
    T = val.shape[0]
    for c in range(ROWS_PER_TOKEN):
        ref[pl.ds(c, T, stride=ROWS_PER_TOKEN), :] = val[:, LANES * c:LANES * (c + 1)]


def _load_token_major(ref, T):
    return jnp.concatenate([ref[pl.ds(c, T, stride=ROWS_PER_TOKEN), :] for c in range(ROWS_PER_TOKEN)], axis=1)


def _const_spec(shape):
    n = len(shape)
    return pl.BlockSpec(shape, lambda *_: (0,) * n, pipeline_mode=pl.Buffered(1))


def _lru_kernel(x_ref, g_ref, win_ref, cw_ref, cb_ref, wax_ref, ba_ref, bx_ref, lam_ref, wout_ref,
                o_ref, xb_scr, carry_scr):
    T = x_ref.shape[0]
    D = D_MODEL

    @pl.when(pl.program_id(0) == 0)
    def _():
        xb_scr[0:SUBLANES, :] = jnp.zeros((SUBLANES, D), F32)
        carry_scr[...] = jnp.zeros((1, D), F32)

    x = x_ref[...]
    hn = _rmsnorm(x, g_ref[...]).astype(BF16)
    proj = jnp.dot(hn, win_ref[...], preferred_element_type=F32)
    y = jax.nn.gelu(proj[:, :D], approximate=True)
    xb = proj[:, D:]

    xb_scr[SUBLANES:SUBLANES + T, :] = xb
    cw = cw_ref[...]
    xc = (cb_ref[...] + xb * cw[3:4, :]
          + xb_scr[SUBLANES - 1:SUBLANES - 1 + T, :] * cw[2:3, :]
          + xb_scr[SUBLANES - 2:SUBLANES - 2 + T, :] * cw[1:2, :]
          + xb_scr[SUBLANES - 3:SUBLANES - 3 + T, :] * cw[0:1, :])
    xb_scr[0:SUBLANES, :] = xb[T - SUBLANES:T, :]

    xcb = xc.astype(BF16)
    r_parts, i_parts = [], []
    for n in range(N_RNN_BLOCKS):
        gn = jnp.dot(xcb[:, RNN_BLOCK * n:RNN_BLOCK * (n + 1)], wax_ref[n], preferred_element_type=F32)
        r_parts.append(gn[:, :RNN_BLOCK])
        i_parts.append(gn[:, RNN_BLOCK:])
    r = jax.nn.sigmoid(jnp.concatenate(r_parts, axis=1) + ba_ref[...])
    ig = jax.nn.sigmoid(jnp.concatenate(i_parts, axis=1) + bx_ref[...])
    log_a = -LRU_C * r * jax.nn.softplus(-lam_ref[...])
    a = jnp.exp(log_a)
    b = jnp.sqrt(jnp.tanh(-log_a) * (1.0 + a * a)) * (ig * xc)

    G = T // SUBLANES
    a3 = a.reshape(G, SUBLANES, D)
    b3 = b.reshape(G, SUBLANES, D)
    sub = lax.broadcasted_iota(jnp.int32, (G, SUBLANES, D), 1)
    for k in (1, 2, 4):
        m = sub >= k
        a_sh = jnp.where(m, pltpu.roll(a3, k, 1), 1.0)
        b_sh = jnp.where(m, pltpu.roll(b3, k, 1), 0.0)
        b3 = b3 + a3 * b_sh
        a3 = a3 * a_sh
    hprev = carry_scr[...]
    hs = []
    for gi in range(G):
        hg = b3[gi] + a3[gi] * hprev
        hs.append(hg)
        hprev = hg[SUBLANES - 1:SUBLANES, :]
    carry_scr[...] = hprev
    h = jnp.concatenate(hs, axis=0)

    o_ref[...] = jnp.dot((h * y).astype(BF16), wout_ref[...], preferred_element_type=F32) + x


def _lru_mixer(x, g, w_in, conv_w, conv_b, w_ax, b_a, b_x, lam, w_out):
    S, D = x.shape
    T = SEQ_TILE
    row = lambda v: v.reshape(1, D)
    return pl.pallas_call(
        _lru_kernel,
        grid=(S // T,),
        in_specs=[
            pl.BlockSpec((T, D), lambda i: (i, 0)),
            _const_spec((1, D)),
            _const_spec((D, 2 * D)),
            _const_spec((CONV_WIDTH, D)),
            _const_spec((1, D)),
            _const_spec((N_RNN_BLOCKS, RNN_BLOCK, 2 * RNN_BLOCK)),
            _const_spec((1, D)),
            _const_spec((1, D)),
            _const_spec((1, D)),
            _const_spec((D, D)),
        ],
        out_specs=pl.BlockSpec((T, D), lambda i: (i, 0)),
        out_shape=jax.ShapeDtypeStruct((S, D), F32),
        scratch_shapes=[pltpu.VMEM((T + SUBLANES, D), F32), pltpu.VMEM((1, D), F32)],
        compiler_params=pltpu.CompilerParams(dimension_semantics=("arbitrary",), vmem_limit_bytes=VMEM_LIMIT),
        name="lru_mixer",
    )(x, row(g), w_in, conv_w, row(conv_b), w_ax, row(b_a), row(b_x), row(lam), w_out)


def _ffn_kernel(x_ref, g_ref, wg_ref, wu_ref, wd_ref, o_ref, xn_scr, acc_scr):
    c = pl.program_id(1)

    @pl.when(c == 0)
    def _():
        xn_scr[...] = _rmsnorm(x_ref[...], g_ref[...]).astype(BF16)
        acc_scr[...] = jnp.zeros_like(acc_scr)

    xn = xn_scr[...]
    gate = jnp.dot(xn, wg_ref[...], preferred_element_type=F32)
    up = jnp.dot(xn, wu_ref[...], preferred_element_type=F32)
    mid = (jax.nn.silu(gate) * up).astype(BF16)
    acc_scr[...] += jnp.dot(mid, wd_ref[...], preferred_element_type=F32)

    @pl.when(c == pl.num_programs(1) - 1)
    def _():
        o_ref[...] = x_ref[...] + acc_scr[...]


def _dense_ffn(x, g, w_gate, w_up, w_down):
    S, D = x.shape
    T = FFN_TILE
    return pl.pallas_call(
        _ffn_kernel,
        grid=(S // T, N_FF_CHUNKS),
        in_specs=[
            pl.BlockSpec((T, D), lambda i, c: (i, 0)),
            pl.BlockSpec((1, D), lambda i, c: (0, 0)),
            pl.BlockSpec((D, FF_CHUNK), lambda i, c: (0, c)),
            pl.BlockSpec((D, FF_CHUNK), lambda i, c: (0, c)),
            pl.BlockSpec((FF_CHUNK, D), lambda i, c: (c, 0)),
        ],
        out_specs=pl.BlockSpec((T, D), lambda i, c: (i, 0)),
        out_shape=jax.ShapeDtypeStruct((S, D), F32),
        scratch_shapes=[pltpu.VMEM((T, D), BF16), pltpu.VMEM((T, D), F32)],
        compiler_params=pltpu.CompilerParams(dimension_semantics=("arbitrary", "arbitrary"),
                                             vmem_limit_bytes=VMEM_LIMIT),
        name="dense_ffn",
    )(x, g.reshape(1, D), w_gate, w_up, w_down)


KV_EXT = N_KV_HEADS * LANES
QKV_EXT = N_HEADS * HEAD_DIM + 2 * KV_EXT


def _attn_kernel(sink_ref, x_ref, pos_ref, g_ref, freq_ref, sign_ref, wqkv_ref, bqkv_ref, wo_ref, bo_ref,
                 g2_ref, wr_ref, o_ref, route_ref,
                 qlo_scr, qhi_scr, k_scr, vlo_scr, vhi_scr, o_scr):
    T = x_ref.shape[0]
    D = D_MODEL
    B = ATTN_BLOCK
    first_tile = pl.program_id(0) == 0

    @pl.when(first_tile)
    def _():
        k_scr[0:B, :] = jnp.zeros((B, KV_EXT), BF16)
        vlo_scr[0:B, :] = jnp.zeros((B, KV_EXT), BF16)
        vhi_scr[0:B, :] = jnp.zeros((B, KV_EXT), BF16)

    x = x_ref[...]
    hn = _rmsnorm(x, g_ref[...]).astype(BF16)
    qkv = jnp.dot(hn, wqkv_ref[...], preferred_element_type=F32) + bqkv_ref[...]

    ang = pos_ref[...] * freq_ref[...]
    cos_t = jnp.cos(ang)
    sin_t = jnp.sin(ang) * sign_ref[...]
    lane = lax.broadcasted_iota(jnp.int32, (T, LANES), 1)
    first_half = (lane % HEAD_DIM) < (ROT_DIM // 2)
    lo_half = lane < HEAD_DIM

    def rope(col):
        partner = jnp.where(first_half, pltpu.roll(col, LANES - ROT_DIM // 2, 1), pltpu.roll(col, ROT_DIM // 2, 1))
        return col * cos_t + partner * sin_t

    scale = HEAD_DIM ** -0.5
    for c in range(D // LANES):
        qc = rope(qkv[:, LANES * c:LANES * (c + 1)]) * scale
        qlo_scr[:, LANES * c:LANES * (c + 1)] = jnp.where(lo_half, qc, 0.0).astype(BF16)
        qhi_scr[:, LANES * c:LANES * (c + 1)] = jnp.where(lo_half, 0.0, qc).astype(BF16)
    for g in range(N_KV_HEADS):
        kc = rope(qkv[:, D + LANES * g:D + LANES * (g + 1)])
        k_scr[B:B + T, LANES * g:LANES * (g + 1)] = kc.astype(BF16)
        vc = qkv[:, D + KV_EXT + LANES * g:D + KV_EXT + LANES * (g + 1)]
        vlo_scr[B:B + T, LANES * g:LANES * (g + 1)] = jnp.where(lo_half, vc, 0.0).astype(BF16)
        vhi_scr[B:B + T, LANES * g:LANES * (g + 1)] = jnp.where(lo_half, 0.0, vc).astype(BF16)

    qi = lax.broadcasted_iota(jnp.int32, (B, 2 * B), 0)
    kj = lax.broadcasted_iota(jnp.int32, (B, 2 * B), 1)
    band_prev = (kj < B) & (kj > qi)
    band_own = (kj >= B) & (kj - B <= qi)

    def block_body(blk, _):
        r0 = pl.multiple_of(blk * B, B)
        k_lo = jnp.where(first_tile & (blk == 0), B, 0)
        allowed = band_own | (band_prev & (kj >= k_lo))
        for g in range(N_KV_HEADS):
            kk = k_scr[pl.ds(r0, 2 * B), LANES * g:LANES * (g + 1)]
            v_lo = vlo_scr[pl.ds(r0, 2 * B), LANES * g:LANES * (g + 1)]
            v_hi = vhi_scr[pl.ds(r0, 2 * B), LANES * g:LANES * (g + 1)]
            for p in range(GROUP // 2):
                c = (GROUP // 2) * g + p
                parts = []
                for half, (q_scr, vv) in enumerate(((qlo_scr, v_lo), (qhi_scr, v_hi))):
                    qh = q_scr[pl.ds(r0, B), LANES * c:LANES * (c + 1)]
                    s = lax.dot_general(qh, kk, (((1,), (1,)), ((), ())), preferred_element_type=F32)
                    s = jnp.where(allowed, s, NEG_INF)
                    sink = sink_ref[2 * c + half]
                    m = jnp.maximum(jnp.max(s, axis=-1, keepdims=True), sink)
                    pexp = jnp.exp(s - m)
                    denom = jnp.sum(pexp, axis=-1, keepdims=True) + jnp.exp(sink - m)
                    parts.append(jnp.dot(pexp.astype(BF16), vv, preferred_element_type=F32) / denom)
                o_scr[pl.ds(r0, B), LANES * c:LANES * (c + 1)] = (parts[0] + parts[1]).astype(BF16)
        return 0

    lax.fori_loop(0, T // B, block_body, 0)

    k_scr[0:B, :] = k_scr[T:T + B, :]
    vlo_scr[0:B, :] = vlo_scr[T:T + B, :]
    vhi_scr[0:B, :] = vhi_scr[T:T + B, :]

    h1 = jnp.dot(o_scr[...], wo_ref[...], preferred_element_type=F32) + bo_ref[...] + x
    _store_token_major(o_ref, h1)

    hn2 = _rmsnorm(h1, g2_ref[...]).astype(BF16)
    logits = jnp.dot(hn2, wr_ref[...], preferred_element_type=F32)
    lane_f = lane.astype(F32)
    lg = jnp.where(lane < N_EXPERTS, logits, -jnp.inf)
    v1 = jnp.max(lg, axis=-1, keepdims=True)
    i1 = jnp.min(jnp.where(lg == v1, lane_f, float(LANES)), axis=-1, keepdims=True)
    lg2 = jnp.where(lane_f == i1, -jnp.inf, lg)
    v2 = jnp.max(lg2, axis=-1, keepdims=True)
    i2 = jnp.min(jnp.where(lg2 == v2, lane_f, float(LANES)), axis=-1, keepdims=True)
    e2 = jnp.exp(v2 - v1)
    gate1 = 1.0 / (1.0 + e2)
    gate2 = e2 / (1.0 + e2)
    route_ref[...] = jnp.where(lane == 0, i1, jnp.where(lane == 1, i2,
                                                        jnp.where(lane == 2, gate1, jnp.where(lane == 3, gate2, 0.0))))


def _attn_mixer(x, pos_f, g, freq_lane, sign_lane, w_qkv_ext, b_qkv_ext, sinks, w_o, b_o, g2, w_router_pad):
    S, D = x.shape
    T = SEQ_TILE
    B = ATTN_BLOCK
    grid_spec = pltpu.PrefetchScalarGridSpec(
        num_scalar_prefetch=1,
        grid=(S // T,),
        in_specs=[
            pl.BlockSpec((T, D), lambda i, s: (i, 0)),
            pl.BlockSpec((T, 1), lambda i, s: (i, 0)),
            _const_spec((1, D)),
            _const_spec((1, LANES)),
            _const_spec((1, LANES)),
            _const_spec((D, QKV_EXT)),
            _const_spec((1, QKV_EXT)),
            _const_spec((D, D)),
            _const_spec((1, D)),
            _const_spec((1, D)),
            _const_spec((D, LANES)),
        ],
        out_specs=[pl.BlockSpec((T * ROWS_PER_TOKEN, LANES), lambda i, s: (i, 0)),
                   pl.BlockSpec((T, LANES), lambda i, s: (i, 0))],
        scratch_shapes=[
            pltpu.VMEM((T, D), BF16), pltpu.VMEM((T, D), BF16),
            pltpu.VMEM((T + B, KV_EXT), BF16), pltpu.VMEM((T + B, KV_EXT), BF16), pltpu.VMEM((T + B, KV_EXT), BF16),
            pltpu.VMEM((T, D), BF16),
        ],
    )
    return pl.pallas_call(
        _attn_kernel,
        grid_spec=grid_spec,
        out_shape=[jax.ShapeDtypeStruct((S * ROWS_PER_TOKEN, LANES), F32), jax.ShapeDtypeStruct((S, LANES), F32)],
        compiler_params=pltpu.CompilerParams(dimension_semantics=("arbitrary",), vmem_limit_bytes=VMEM_LIMIT),
        name="attn_mixer",
    )(sinks, x, pos_f, g.reshape(1, D), freq_lane, sign_lane, w_qkv_ext, b_qkv_ext, w_o, b_o.reshape(1, D),
      g2.reshape(1, D), w_router_pad)


def _moe_kernel(texp_ref, nvt_ref, src0_ref, srcn_ref, dstp_ref, dstl_ref, h_hbm, g_ref, wg_ref, wu_ref, wd_ref,
                y_hbm, x_scr, xn_scr, acc_scr, y_scr, sems):
    i = pl.program_id(0)
    c = pl.program_id(1)
    n_tiles = pl.num_programs(0)
    n_chunks = pl.num_programs(1)
    T = xn_scr.shape[0]
    R = ROWS_PER_TOKEN
    valid = i < nvt_ref[0]
    slot = i % 2
    other = 1 - slot
    SCATTER_SEM = 2

    def row_copy(src, src_row, dst, dst_row, sem):
        return pltpu.make_async_copy(src.at[pl.ds(pl.multiple_of(src_row, R), R), :],
                                     dst.at[pl.ds(pl.multiple_of(dst_row, R), R), :], sem)

    def wait_gather(s):
        pltpu.make_async_copy(h_hbm.at[pl.ds(0, T * R), :], x_scr.at[s], sems.at[s]).wait()

    def wait_scatter():
        pltpu.make_async_copy(y_scr, y_hbm.at[pl.ds(0, T * R), :], sems.at[SCATTER_SEM]).wait()

    @pl.when((i == 0) & (c == 0))
    def _():
        y_scr[...] = jnp.zeros_like(y_scr)

        def issue(r, carry):
            row_copy(h_hbm, src0_ref[0, 0, r], x_scr.at[0], r * R, sems.at[0]).start()
            return carry
        lax.fori_loop(0, T, issue, 0)

    @pl.when(c == 0)
    def _():
        wait_gather(slot)

    @pl.when(valid & (c == 0))
    def _():
        xn_scr[...] = _rmsnorm(_load_token_major(x_scr.at[slot], T), g_ref[...]).astype(BF16)
        acc_scr[...] = jnp.zeros_like(acc_scr)

    def tile_dmas():
        for r in range(T):
            row_copy(h_hbm, srcn_ref[0, 0, r], x_scr.at[other], r * R, sems.at[other]).start()
            row_copy(y_scr, r * R, y_hbm, dstp_ref[0, 0, r], sems.at[SCATTER_SEM]).start()

    def expert_chunk():
        xn = xn_scr[...]
        gate = jnp.dot(xn, wg_ref[0], preferred_element_type=F32)
        up = jnp.dot(xn, wu_ref[0], preferred_element_type=F32)
        mid = (jax.nn.silu(gate) * up).astype(BF16)
        acc_scr[...] += jnp.dot(mid, wd_ref[0], preferred_element_type=F32)

    @pl.when(valid & (c == 0))
    def _():
        tile_dmas()
        expert_chunk()

    @pl.when(valid & (c != 0))
    def _():
        expert_chunk()

    @pl.when(jnp.logical_not(valid) & (c == 0))
    def _():
        tile_dmas()

    @pl.when(c == n_chunks - 1)
    def _():
        wait_scatter()

    @pl.when(valid & (c == n_chunks - 1))
    def _():
        _store_token_major(y_scr, acc_scr[...])

    @pl.when((i == n_tiles - 1) & (c == n_chunks - 1))
    def _():
        wait_gather(other)

        def issue(r, carry):
            row_copy(y_scr, r * R, y_hbm, dstl_ref[0, 0, r], sems.at[SCATTER_SEM]).start()
            return carry
        lax.fori_loop(0, T, issue, 0)
        wait_scatter()


def _moe_experts(h1_tm, g, tile_expert, n_valid_tiles, src_rows, dst_rows, w_gate, w_up, w_down, n_out_tokens):
    D = D_MODEL
    T = MOE_TILE
    NT = src_rows.shape[0]
    NC = N_MOE_CHUNKS
    FC = D_FF // NC

    def chunk(i, c, nvt):
        return jnp.where(i < nvt[0], c, NC - 1)

    def idx_spec(index_map):
        return pl.BlockSpec((1, 1, T), index_map, memory_space=pltpu.SMEM)

    grid_spec = pltpu.PrefetchScalarGridSpec(
        num_scalar_prefetch=2,
        grid=(NT, NC),
        in_specs=[
            idx_spec(lambda i, c, te, nvt: (0, 0, 0)),
            idx_spec(lambda i, c, te, nvt: (jnp.minimum(i + 1, NT - 1), 0, 0)),
            idx_spec(lambda i, c, te, nvt: (i, 0, 0)),
            idx_spec(lambda i, c, te, nvt: (NT, 0, 0)),
            pl.BlockSpec(memory_space=pl.ANY),
            pl.BlockSpec((1, D), lambda i, c, te, nvt: (0, 0)),
            pl.BlockSpec((1, D, FC), lambda i, c, te, nvt: (te[i], 0, chunk(i, c, nvt))),
            pl.BlockSpec((1, D, FC), lambda i, c, te, nvt: (te[i], 0, chunk(i, c, nvt))),
            pl.BlockSpec((1, FC, D), lambda i, c, te, nvt: (te[i], chunk(i, c, nvt), 0)),
        ],
        out_specs=pl.BlockSpec(memory_space=pl.ANY),
        scratch_shapes=[pltpu.VMEM((2, T * ROWS_PER_TOKEN, LANES), F32), pltpu.VMEM((T, D), BF16),
                        pltpu.VMEM((T, D), F32), pltpu.VMEM((T * ROWS_PER_TOKEN, LANES), F32),
                        pltpu.SemaphoreType.DMA((3,))],
    )
    return pl.pallas_call(
        _moe_kernel,
        grid_spec=grid_spec,
        out_shape=jax.ShapeDtypeStruct((n_out_tokens * ROWS_PER_TOKEN, LANES), F32),
        compiler_params=pltpu.CompilerParams(dimension_semantics=("arbitrary", "arbitrary"),
                                             vmem_limit_bytes=VMEM_LIMIT),
        name="moe_experts",
    )(tile_expert, n_valid_tiles, src_rows, src_rows, dst_rows, dst_rows, h1_tm, g.reshape(1, D),
      w_gate, w_up, w_down)


def _moe_plan(route, n_tokens):
    T = MOE_TILE
    n_assign = TOP_K * n_tokens
    n_tiles = n_assign // T + N_EXPERTS
    e_flat = route[:, 0:TOP_K].astype(jnp.int32).reshape(-1)
    order = jnp.argsort(e_flat, stable=True).astype(jnp.int32)
    counts = jnp.sum((e_flat[:, None] == jnp.arange(N_EXPERTS, dtype=jnp.int32)[None, :]).astype(jnp.int32), axis=0)
    tiles_per = (counts + T - 1) // T
    tile_end = jnp.cumsum(tiles_per)
    n_valid = tile_end[-1]
    tile_ids = jnp.arange(n_tiles, dtype=jnp.int32)
    last_valid = jnp.maximum(n_valid - 1, 0)
    texp = jnp.sum((jnp.minimum(tile_ids, last_valid)[:, None] >= tile_end[None, :]).astype(jnp.int32), axis=1)
    texp = jnp.minimum(texp, N_EXPERTS - 1)
    onehot = (texp[:, None] == jnp.arange(N_EXPERTS, dtype=jnp.int32)[None, :]).astype(jnp.int32)
    pick = lambda v: jnp.sum(onehot * v[None, :], axis=1)
    j = tile_ids - pick(tile_end - tiles_per)
    start = pick(jnp.cumsum(counts) - counts) + j * T
    n_rows = jnp.where(tile_ids < n_valid, jnp.clip(pick(counts) - j * T, 0, T), 0)
    r = jnp.arange(T, dtype=jnp.int32)[None, :]
    assign = order[jnp.clip(start[:, None] + r, 0, n_assign - 1)]
    token = assign // TOP_K
    slot = assign % TOP_K
    row_valid = r < n_rows[:, None]
    pad_rows = jnp.broadcast_to(TOP_K * n_tokens + r, (1, T))
    dst = jnp.where(row_valid, slot * n_tokens + token, pad_rows)
    dst = jnp.concatenate([pad_rows, dst], axis=0)
    src_rows = (token * ROWS_PER_TOKEN).reshape(n_tiles, 1, T)
    dst_rows = (dst * ROWS_PER_TOKEN).reshape(n_tiles + 1, 1, T)
    return texp, n_valid.reshape(1).astype(jnp.int32), src_rows, dst_rows


def _combine_kernel(h_ref, y0_ref, y1_ref, route_ref, g_ref, o_ref):
    T = o_ref.shape[0]
    route = route_ref[...]
    moe = route[:, 2:3] * _load_token_major(y0_ref, T) + route[:, 3:4] * _load_token_major(y1_ref, T)
    o_ref[...] = _rmsnorm(_load_token_major(h_ref, T) + moe, g_ref[...])


def _combine(h1_tm, y2_tm, route, g):
    D = D_MODEL
    S = route.shape[0]
    T = OUT_TILE
    nb = S // T
    tm_block = (T * ROWS_PER_TOKEN, LANES)
    return pl.pallas_call(
        _combine_kernel,
        grid=(nb,),
        in_specs=[
            pl.BlockSpec(tm_block, lambda i: (i, 0)),
            pl.BlockSpec(tm_block, lambda i: (i, 0)),
            pl.BlockSpec(tm_block, lambda i: (i + nb, 0)),
            pl.BlockSpec((T, LANES), lambda i: (i, 0)),
            pl.BlockSpec((1, D), lambda i: (0, 0)),
        ],
        out_specs=pl.BlockSpec((T, D), lambda i: (i, 0)),
        out_shape=jax.ShapeDtypeStruct((S, D), F32),
        compiler_params=pltpu.CompilerParams(dimension_semantics=("arbitrary",), vmem_limit_bytes=VMEM_LIMIT),
        name="moe_combine",
    )(h1_tm, y2_tm, y2_tm, route, g.reshape(1, D))


def kernel(x, positions, norm_mix, norm_ffn, norm_final, lru_w_in, lru_conv_w, lru_conv_b, lru_w_a, lru_b_a, lru_w_x, lru_b_x, lru_lambda, lru_w_out, attn_w_qkv, attn_b_qkv, attn_sinks, attn_w_o, attn_b_o, ffn_w_gate, ffn_w_up, ffn_w_down, moe_w_router, moe_w_gate, moe_w_up, moe_w_down):
    B, S, D = x.shape
    assert B == 1 and D == D_MODEL and S % FFN_TILE == 0
    h = x.reshape(S, D)

    w_ax = jnp.concatenate([lru_w_a[0], lru_w_x[0]], axis=-1).astype(BF16)
    h = _lru_mixer(h, norm_mix[0], lru_w_in[0].astype(BF16), lru_conv_w[0], lru_conv_b[0], w_ax,
                   lru_b_a[0], lru_b_x[0], lru_lambda[0], lru_w_out[0].astype(BF16))
    h = _dense_ffn(h, norm_ffn[0], ffn_w_gate[0].astype(BF16), ffn_w_up[0].astype(BF16), ffn_w_down[0].astype(BF16))

    q_dim = N_HEADS * HEAD_DIM
    kv_dim = N_KV_HEADS * HEAD_DIM

    def dup_heads(w):
        w4 = w.reshape(w.shape[:-1] + (N_KV_HEADS, 1, HEAD_DIM))
        return jnp.broadcast_to(w4, w.shape[:-1] + (N_KV_HEADS, 2, HEAD_DIM)).reshape(w.shape[:-1] + (KV_EXT,))

    wqkv, bqkv = attn_w_qkv[0], attn_b_qkv[0]
    w_qkv_ext = jnp.concatenate([wqkv[:, :q_dim], dup_heads(wqkv[:, q_dim:q_dim + kv_dim]),
                                 dup_heads(wqkv[:, q_dim + kv_dim:])], axis=1).astype(BF16)
    b_qkv_ext = jnp.concatenate([bqkv[:q_dim], dup_heads(bqkv[q_dim:q_dim + kv_dim]),
                                 dup_heads(bqkv[q_dim + kv_dim:])]).reshape(1, QKV_EXT)
    inv_freq = ROPE_THETA ** (-jnp.arange(0, ROT_DIM, 2, dtype=F32) / ROT_DIM)
    d_in_head = jnp.arange(LANES) % HEAD_DIM
    freq_lane = jnp.where(d_in_head < ROT_DIM, inv_freq[d_in_head % (ROT_DIM // 2)], 0.0).reshape(1, LANES)
    sign_lane = jnp.where(d_in_head < ROT_DIM // 2, -1.0, jnp.where(d_in_head < ROT_DIM, 1.0, 0.0))
    sign_lane = sign_lane.astype(F32).reshape(1, LANES)
    w_router_pad = jnp.pad(moe_w_router[0], ((0, 0), (0, LANES - N_EXPERTS))).astype(BF16)
    pos_f = positions.reshape(S, 1).astype(F32)
    h1_tm, route = _attn_mixer(h, pos_f, norm_mix[1], freq_lane, sign_lane, w_qkv_ext, b_qkv_ext,
                               attn_sinks[0], attn_w_o[0].astype(BF16), attn_b_o[0], norm_ffn[1], w_router_pad)

    texp, n_valid, src_rows, dst_rows = _moe_plan(route, S)
    y2_tm = _moe_experts(h1_tm, norm_ffn[1], texp, n_valid, src_rows, dst_rows,
                         moe_w_gate[0].astype(BF16), moe_w_up[0].astype(BF16), moe_w_down[0].astype(BF16),
                         TOP_K * S + MOE_TILE)
    out = _combine(h1_tm, y2_tm, route, norm_final)
    return out.reshape(B, S, D)
```

```python
import functools

import jax
import jax.numpy as jnp
from jax import lax
from jax.experimental import pallas as pl
from jax.experimental.pallas import tpu as pltpu

F32 = jnp.float32
BF16 = jnp.bfloat16

D_MODEL = 1024
N_RNN_BLOCKS = 8
RNN_BLOCK = D_MODEL // N_RNN_BLOCKS
CONV_WIDTH = 4
LRU_C = 8.0
N_HEADS = 16
N_KV_HEADS = 4
HEAD_DIM = 64
GROUP = N_HEADS // N_KV_HEADS
WINDOW = 128
ATTN_BLOCK = 128
ROPE_THETA = 500000.0
ROT_DIM = HEAD_DIM // 4
D_FF = 3584
N_EXPERTS = 8
TOP_K = 2
EPS = 1e-6
NEG_INF = -1e30

LANES = 128
SUBLANES = 8
VMEM_LIMIT = 56 * 1024 * 1024

SEQ_TILE = 512
LRU_CHUNKS = 4
FFN_TILE = 1024
FF_CHUNK = 512
N_FF_CHUNKS = D_FF // FF_CHUNK
MOE_TILE = 512
SPARE_ROWS = 32
N_MOE_CHUNKS = 2
OUT_TILE = 1024


def _rmsnorm(x, g):
    return x * lax.rsqrt(jnp.mean(x * x, axis=-1, keepdims=True) + EPS) * g


ROWS_PER_TOKEN = D_MODEL // LANES


def _store_token_major(ref, val):
    T = val.shape[0]
    for c in range(ROWS_PER_TOKEN):
        ref[pl.ds(c, T, stride=ROWS_PER_TOKEN), :] = val[:, LANES * c:LANES * (c + 1)]


def _load_token_major(ref, T):
    return jnp.concatenate([ref[pl.ds(c, T, stride=ROWS_PER_TOKEN), :] for c in range(ROWS_PER_TOKEN)], axis=1)


def _const_spec(shape):
    n = len(shape)
    return pl.BlockSpec(shape, lambda *_: (0,) * n, pipeline_mode=pl.Buffered(1))


def _time_permutation(T):
    rho = jnp.arange(T)
    t_of_rho = (rho % SUBLANES) * (T // SUBLANES) + rho // SUBLANES
    perm = (t_of_rho[:, None] == jnp.arange(T)[None, :]).astype(BF16)
    return perm, perm.T


def _lru_kernel(xin_ref, xres_ref, g_ref, perm_ref, unperm_ref, win_ref, cw_ref, cb_ref, wax_ref, ba_ref, bx_ref,
                lam_ref, wout_ref, cast_g_ref, cast_u_ref, cast_d_ref,
                o_ref, bf_g_ref, bf_u_ref, bf_d_ref, proj_scr, tail_scr, carry_scr):
    i = pl.program_id(0)
    n_tiles = pl.num_programs(0) - 1
    T = xin_ref.shape[0]
    D = D_MODEL
    C = T // LRU_CHUNKS
    GC = C // SUBLANES
    G = T // SUBLANES
    HALO = (CONV_WIDTH - 1) * SUBLANES

    def cast_slices():
        bf_g_ref[...] = cast_g_ref[...].astype(BF16)
        bf_u_ref[...] = cast_u_ref[...].astype(BF16)
        bf_d_ref[...] = cast_d_ref[...].astype(BF16)

    @pl.when(i == 0)
    def _():
        tail_scr[...] = jnp.zeros_like(tail_scr)
        carry_scr[...] = jnp.zeros_like(carry_scr)

    PW = 2 * D // LRU_CHUNKS

    def project_chunks():
        hn = _rmsnorm(xin_ref[...], g_ref[...]).astype(BF16)
        hp = jnp.dot(perm_ref[...], hn, preferred_element_type=F32).astype(BF16)
        for k in range(LRU_CHUNKS):
            yield jnp.dot(hp, win_ref[:, PW * k:PW * (k + 1)], preferred_element_type=F32)

    def recur_chunks():
        sub = lax.broadcasted_iota(jnp.int32, (SUBLANES, D), 0)
        cw = cw_ref[...]
        softplus_neg_lam = jax.nn.softplus(-lam_ref[...])
        last = proj_scr[T - HALO:T, D:]
        prev_groups = []
        for k in range(CONV_WIDTH - 1):
            rows = slice(SUBLANES * k, SUBLANES * (k + 1))
            prev_groups.append(pltpu.roll(jnp.where(sub == SUBLANES - 1, tail_scr[rows, :], last[rows, :]), 1, 0))
        tail_scr[...] = last

        ys, hs, decay = [], [], []
        for k in range(LRU_CHUNKS):
            pc = proj_scr[C * k:C * (k + 1), :]
            ys.append(jax.nn.gelu(pc[:, :D], approximate=True))
            xb = pc[:, D:]
            xc = cb_ref[...] + xb * cw[CONV_WIDTH - 1:CONV_WIDTH, :]
            for back in range(1, CONV_WIDTH):
                shifted = jnp.concatenate(prev_groups[CONV_WIDTH - 1 - back:] + [xb[:C - SUBLANES * back, :]], axis=0)
                xc = xc + shifted * cw[CONV_WIDTH - 1 - back:CONV_WIDTH - back, :]
            prev_groups = [xb[C - HALO + SUBLANES * q:C - HALO + SUBLANES * (q + 1), :] for q in range(CONV_WIDTH - 1)]

            xcb = xc.astype(BF16)
            r_parts, i_parts = [], []
            for n in range(N_RNN_BLOCKS):
                gn = jnp.dot(xcb[:, RNN_BLOCK * n:RNN_BLOCK * (n + 1)], wax_ref[n], preferred_element_type=F32)
                r_parts.append(gn[:, :RNN_BLOCK])
                i_parts.append(gn[:, RNN_BLOCK:])
            r = jax.nn.sigmoid(jnp.concatenate(r_parts, axis=1) + ba_ref[...])
            ig = jax.nn.sigmoid(jnp.concatenate(i_parts, axis=1) + bx_ref[...])
            log_a = -LRU_C * r * softplus_neg_lam
            a = jnp.exp(log_a)
            z = jnp.tanh(-log_a) * (1.0 + a * a)
            b = jnp.where(z > 0.0, z * lax.rsqrt(z), 0.0) * (ig * xc)

            for j in range(GC):
                rows = slice(SUBLANES * j, SUBLANES * (j + 1))
                if hs:
                    hs.append(a[rows, :] * hs[-1] + b[rows, :])
                    decay.append(a[rows, :] * decay[-1])
                else:
                    hs.append(b[rows, :])
                    decay.append(a[rows, :])
            yield None

        p_inc, e_inc = decay[-1], hs[-1]
        for k in (1, 2, 4):
            m = sub >= k
            p_sh = jnp.where(m, pltpu.roll(p_inc, k, 0), 1.0)
            e_sh = jnp.where(m, pltpu.roll(e_inc, k, 0), 0.0)
            e_inc = e_inc + p_inc * e_sh
            p_inc = p_inc * p_sh
        h0 = carry_scr[...]
        after = p_inc * h0 + e_inc
        carry_scr[...] = after[SUBLANES - 1:SUBLANES, :]
        enter = jnp.where(sub == 0, h0, pltpu.roll(after, 1, 0))
        h = jnp.concatenate([hs[j] + decay[j] * enter for j in range(G)], axis=0)
        hy = (h * jnp.concatenate(ys, axis=0)).astype(BF16)
        hy = jnp.dot(unperm_ref[...], hy, preferred_element_type=F32).astype(BF16)
        yield jnp.dot(hy, wout_ref[...], preferred_element_type=F32) + xres_ref[...]

    @pl.when(i == 0)
    def _():
        cast_slices()
        for k, pc in enumerate(project_chunks()):
            proj_scr[:, PW * k:PW * (k + 1)] = pc

    @pl.when((i > 0) & (i < n_tiles))
    def _():
        cast_slices()
        rec = recur_chunks()
        new_proj = []
        for pc in project_chunks():
            new_proj.append(pc)
            next(rec)
        o_ref[...] = next(rec)
        for k, pc in enumerate(new_proj):
            proj_scr[:, PW * k:PW * (k + 1)] = pc

    @pl.when(i == n_tiles)
    def _():
        o_ref[...] = list(recur_chunks())[-1]


def _lru_mixer(x, g, w_in, conv_w, conv_b, w_ax, b_a, b_x, lam, w_out, cast_gate, cast_up, cast_down):
    S, D = x.shape
    T = SEQ_TILE
    n_tiles = S // T
    row = lambda v: v.reshape(1, D)
    perm, unperm = _time_permutation(T)

    def cast_spec(w):
        return pl.BlockSpec((w.shape[0] // n_tiles, w.shape[1]), lambda i: (jnp.minimum(i, n_tiles - 1), 0))

    casts = (cast_gate, cast_up, cast_down)
    return pl.pallas_call(
        _lru_kernel,
        grid=(n_tiles + 1,),
        in_specs=[
            pl.BlockSpec((T, D), lambda i: (jnp.minimum(i, n_tiles - 1), 0)),
            pl.BlockSpec((T, D), lambda i: (jnp.maximum(i - 1, 0), 0)),
            _const_spec((1, D)),
            _const_spec((T, T)),
            _const_spec((T, T)),
            _const_spec((D, 2 * D)),
            _const_spec((CONV_WIDTH, D)),
            _const_spec((1, D)),
            _const_spec((N_RNN_BLOCKS, RNN_BLOCK, 2 * RNN_BLOCK)),
            _const_spec((1, D)),
            _const_spec((1, D)),
            _const_spec((1, D)),
            _const_spec((D, D)),
        ] + [cast_spec(w) for w in casts],
        out_specs=[pl.BlockSpec((T, D), lambda i: (jnp.maximum(i - 1, 0), 0))] + [cast_spec(w) for w in casts],
        out_shape=[jax.ShapeDtypeStruct((S, D), F32)] + [jax.ShapeDtypeStruct(w.shape, BF16) for w in casts],
        scratch_shapes=[pltpu.VMEM((T, 2 * D), F32), pltpu.VMEM(((CONV_WIDTH - 1) * SUBLANES, D), F32),
                        pltpu.VMEM((1, D), F32)],
        compiler_params=pltpu.CompilerParams(dimension_semantics=("arbitrary",), vmem_limit_bytes=VMEM_LIMIT),
        name="lru_mixer",
    )(x, x, row(g), perm, unperm, w_in, conv_w, row(conv_b), w_ax, row(b_a), row(b_x), row(lam), w_out, *casts)


def _ffn_kernel(x_ref, g_ref, wg_ref, wu_ref, wd_ref, cast_g_ref, cast_u_ref, cast_d_ref,
                o_ref, bf_g_ref, bf_u_ref, bf_d_ref, xn_scr, acc_scr):
    c = pl.program_id(1)

    @pl.when(c == 0)
    def _():
        xn_scr[...] = _rmsnorm(x_ref[...], g_ref[...]).astype(BF16)
        acc_scr[...] = jnp.zeros_like(acc_scr)

    bf_g_ref[...] = cast_g_ref[...].astype(BF16)
    bf_u_ref[...] = cast_u_ref[...].astype(BF16)
    bf_d_ref[...] = cast_d_ref[...].astype(BF16)

    xn = xn_scr[...]
    gate = jnp.dot(xn, wg_ref[...], preferred_element_type=F32)
    up = jnp.dot(xn, wu_ref[...], preferred_element_type=F32)
    mid = (jax.nn.silu(gate) * up).astype(BF16)
    acc_scr[...] += jnp.dot(mid, wd_ref[...], preferred_element_type=F32)

    @pl.when(c == pl.num_programs(1) - 1)
    def _():
        o_ref[...] = x_ref[...] + acc_scr[...]


def _dense_ffn(x, g, w_gate, w_up, w_down, cast_gate, cast_up, cast_down):
    S, D = x.shape
    T = FFN_TILE
    n_steps = (S // T) * N_FF_CHUNKS
    E = cast_gate.shape[0]
    row_parts = n_steps // (E * N_FF_CHUNKS)
    assert row_parts * E * N_FF_CHUNKS == n_steps
    gu_block = (1, D // row_parts, D_FF // N_FF_CHUNKS)
    d_block = (1, D_FF // (row_parts * N_FF_CHUNKS), D)

    def gu_map(i, c):
        s = i * N_FF_CHUNKS + c
        return (s // (row_parts * N_FF_CHUNKS), (s // N_FF_CHUNKS) % row_parts, c)

    def d_map(i, c):
        s = i * N_FF_CHUNKS + c
        return (s // (row_parts * N_FF_CHUNKS), s % (row_parts * N_FF_CHUNKS), 0)

    return pl.pallas_call(
        _ffn_kernel,
        grid=(S // T, N_FF_CHUNKS),
        in_specs=[
            pl.BlockSpec((T, D), lambda i, c: (i, 0)),
            pl.BlockSpec((1, D), lambda i, c: (0, 0)),
            pl.BlockSpec((D, FF_CHUNK), lambda i, c: (0, c)),
            pl.BlockSpec((D, FF_CHUNK), lambda i, c: (0, c)),
            pl.BlockSpec((FF_CHUNK, D), lambda i, c: (c, 0)),
            pl.BlockSpec(gu_block, gu_map),
            pl.BlockSpec(gu_block, gu_map),
            pl.BlockSpec(d_block, d_map),
        ],
        out_specs=[pl.BlockSpec((T, D), lambda i, c: (i, 0)), pl.BlockSpec(gu_block, gu_map),
                   pl.BlockSpec(gu_block, gu_map), pl.BlockSpec(d_block, d_map)],
        out_shape=[jax.ShapeDtypeStruct((S, D), F32), jax.ShapeDtypeStruct(cast_gate.shape, BF16),
                   jax.ShapeDtypeStruct(cast_up.shape, BF16), jax.ShapeDtypeStruct(cast_down.shape, BF16)],
        scratch_shapes=[pltpu.VMEM((T, D), BF16), pltpu.VMEM((T, D), F32)],
        compiler_params=pltpu.CompilerParams(dimension_semantics=("arbitrary", "arbitrary"),
                                             vmem_limit_bytes=VMEM_LIMIT),
        name="dense_ffn",
    )(x, g.reshape(1, D), w_gate, w_up, w_down, cast_gate, cast_up, cast_down)


KV_EXT = N_KV_HEADS * LANES
QKV_EXT = N_HEADS * HEAD_DIM + 2 * KV_EXT


def _attn_kernel(sink_ref, x_ref, pos_ref, g_ref, freq_ref, sign_ref, wqkv_ref, bqkv_ref, wo_ref, bo_ref,
                 g2_ref, wr_ref, o_ref, route_ref,
                 qlo_scr, qhi_scr, k_scr, v_scr, o_scr):
    T = x_ref.shape[0]
    D = D_MODEL
    B = ATTN_BLOCK
    first_tile = pl.program_id(0) == 0

    @pl.when(first_tile)
    def _():
        k_scr[0:B, :] = jnp.zeros((B, KV_EXT), BF16)
        v_scr[0:B, :] = jnp.zeros((B, KV_EXT), BF16)

    x = x_ref[...]
    hn = _rmsnorm(x, g_ref[...]).astype(BF16)
    qkv = jnp.dot(hn, wqkv_ref[...], preferred_element_type=F32) + bqkv_ref[...]

    ang = pos_ref[...] * freq_ref[...]
    cos_t = jnp.cos(ang)
    sin_t = jnp.sin(ang) * sign_ref[...]
    lane = lax.broadcasted_iota(jnp.int32, (T, LANES), 1)
    first_half = (lane % HEAD_DIM) < (ROT_DIM // 2)
    lo_half = lane < HEAD_DIM

    def rope(col):
        partner = jnp.where(first_half, pltpu.roll(col, LANES - ROT_DIM // 2, 1), pltpu.roll(col, ROT_DIM // 2, 1))
        return col * cos_t + partner * sin_t

    scale = HEAD_DIM ** -0.5
    for c in range(D // LANES):
        qc = rope(qkv[:, LANES * c:LANES * (c + 1)]) * scale
        qlo_scr[:, LANES * c:LANES * (c + 1)] = jnp.where(lo_half, qc, 0.0).astype(BF16)
        qhi_scr[:, LANES * c:LANES * (c + 1)] = jnp.where(lo_half, 0.0, qc).astype(BF16)
    for g in range(N_KV_HEADS):
        kc = rope(qkv[:, D + LANES * g:D + LANES * (g + 1)])
        k_scr[B:B + T, LANES * g:LANES * (g + 1)] = kc.astype(BF16)
        vc = qkv[:, D + KV_EXT + LANES * g:D + KV_EXT + LANES * (g + 1)]
        v_scr[B:B + T, LANES * g:LANES * (g + 1)] = vc.astype(BF16)

    qi = lax.broadcasted_iota(jnp.int32, (B, 2 * B), 0)
    kj = lax.broadcasted_iota(jnp.int32, (B, 2 * B), 1)
    band_prev = (kj < B) & (kj > qi)
    band_own = (kj >= B) & (kj - B <= qi)
    lo_blk = lax.broadcasted_iota(jnp.int32, (B, LANES), 1) < HEAD_DIM

    def block_body(blk, _):
        r0 = pl.multiple_of(blk * B, B)
        k_lo = jnp.where(first_tile & (blk == 0), B, 0)
        allowed = band_own | (band_prev & (kj >= k_lo))
        for g in range(N_KV_HEADS):
            kk = k_scr[pl.ds(r0, 2 * B), LANES * g:LANES * (g + 1)]
            vv = v_scr[pl.ds(r0, 2 * B), LANES * g:LANES * (g + 1)]
            cols = [slice(LANES * c, LANES * (c + 1)) for c in range((GROUP // 2) * g, (GROUP // 2) * (g + 1))]
            q_all = jnp.concatenate([q_scr[pl.ds(r0, B), col] for col in cols for q_scr in (qlo_scr, qhi_scr)], axis=0)
            s_all = lax.dot_general(q_all, kk, (((1,), (1,)), ((), ())), preferred_element_type=F32)
            p_all, denoms = [], []
            for hh in range(GROUP):
                s = jnp.where(allowed, s_all[B * hh:B * (hh + 1), :], NEG_INF)
                sink = sink_ref[GROUP * g + hh]
                m = jnp.maximum(jnp.max(s, axis=-1, keepdims=True), sink)
                pexp = jnp.exp(s - m)
                denoms.append(jnp.sum(pexp, axis=-1, keepdims=True) + jnp.exp(sink - m))
                p_all.append(pexp.astype(BF16))
            o_all = jnp.dot(jnp.concatenate(p_all, axis=0), vv, preferred_element_type=F32)
            o_heads = [o_all[B * hh:B * (hh + 1), :] / denoms[hh] for hh in range(GROUP)]
            for p, col in enumerate(cols):
                o_scr[pl.ds(r0, B), col] = jnp.where(lo_blk, o_heads[2 * p], o_heads[2 * p + 1]).astype(BF16)
        return 0

    lax.fori_loop(0, T // B, block_body, 0)

    k_scr[0:B, :] = k_scr[T:T + B, :]
    v_scr[0:B, :] = v_scr[T:T + B, :]

    h1 = jnp.dot(o_scr[...], wo_ref[...], preferred_element_type=F32) + bo_ref[...] + x
    _store_token_major(o_ref, h1)

    hn2 = _rmsnorm(h1, g2_ref[...]).astype(BF16)
    logits = jnp.dot(hn2, wr_ref[...], preferred_element_type=F32)
    lane_f = lane.astype(F32)
    lg = jnp.where(lane < N_EXPERTS, logits, -jnp.inf)
    v1 = jnp.max(lg, axis=-1, keepdims=True)
    i1 = jnp.min(jnp.where(lg == v1, lane_f, float(LANES)), axis=-1, keepdims=True)
    lg2 = jnp.where(lane_f == i1, -jnp.inf, lg)
    v2 = jnp.max(lg2, axis=-1, keepdims=True)
    i2 = jnp.min(jnp.where(lg2 == v2, lane_f, float(LANES)), axis=-1, keepdims=True)
    e2 = jnp.exp(v2 - v1)
    gate1 = 1.0 / (1.0 + e2)
    gate2 = e2 / (1.0 + e2)
    route_ref[...] = jnp.where(lane == 0, i1, jnp.where(lane == 1, i2,
                                                        jnp.where(lane == 2, gate1, jnp.where(lane == 3, gate2, 0.0))))


def _attn_mixer(x, pos_f, g, freq_lane, sign_lane, w_qkv_ext, b_qkv_ext, sinks, w_o, b_o, g2, w_router_pad):
    S, D = x.shape
    T = SEQ_TILE
    B = ATTN_BLOCK
    grid_spec = pltpu.PrefetchScalarGridSpec(
        num_scalar_prefetch=1,
        grid=(S // T,),
        in_specs=[
            pl.BlockSpec((T, D), lambda i, s: (i, 0)),
            pl.BlockSpec((T, 1), lambda i, s: (i, 0)),
            _const_spec((1, D)),
            _const_spec((1, LANES)),
            _const_spec((1, LANES)),
            _const_spec((D, QKV_EXT)),
            _const_spec((1, QKV_EXT)),
            _const_spec((D, D)),
            _const_spec((1, D)),
            _const_spec((1, D)),
            _const_spec((D, LANES)),
        ],
        out_specs=[pl.BlockSpec((T * ROWS_PER_TOKEN, LANES), lambda i, s: (i, 0)),
                   pl.BlockSpec((T, LANES), lambda i, s: (i, 0))],
        scratch_shapes=[
            pltpu.VMEM((T, D), BF16), pltpu.VMEM((T, D), BF16),
            pltpu.VMEM((T + B, KV_EXT), BF16), pltpu.VMEM((T + B, KV_EXT), BF16),
            pltpu.VMEM((T, D), BF16),
        ],
    )
    return pl.pallas_call(
        _attn_kernel,
        grid_spec=grid_spec,
        out_shape=[jax.ShapeDtypeStruct((S * ROWS_PER_TOKEN, LANES), F32), jax.ShapeDtypeStruct((S, LANES), F32)],
        compiler_params=pltpu.CompilerParams(dimension_semantics=("arbitrary",), vmem_limit_bytes=VMEM_LIMIT),
        name="attn_mixer",
    )(sinks, x, pos_f, g.reshape(1, D), freq_lane, sign_lane, w_qkv_ext, b_qkv_ext, w_o, b_o.reshape(1, D),
      g2.reshape(1, D), w_router_pad)


def _moe_kernel(texp_ref, nvt_ref, src0_ref, srcn_ref, dstp_ref, dstl_ref, h_hbm, g_ref, wg_ref, wu_ref, wd_ref,
                y_hbm, x_scr, xn_scr, acc_scr, y_scr, sems):
    i = pl.program_id(0)
    c = pl.program_id(1)
    n_tiles = pl.num_programs(0)
    n_chunks = pl.num_programs(1)
    T = xn_scr.shape[0]
    R = ROWS_PER_TOKEN
    y_tile = y_scr.at[pl.ds(0, T * R), :]
    valid = i < nvt_ref[0]
    slot = i % 2
    other = 1 - slot
    SCATTER_SEM = 2

    def row_copy(src, src_row, dst, dst_row, sem):
        return pltpu.make_async_copy(src.at[pl.ds(pl.multiple_of(src_row, R), R), :],
                                     dst.at[pl.ds(pl.multiple_of(dst_row, R), R), :], sem)

    def wait_gather(s):
        pltpu.make_async_copy(h_hbm.at[pl.ds(0, T * R), :], x_scr.at[s, pl.ds(0, T * R), :], sems.at[s]).wait()

    def wait_scatter():
        pltpu.make_async_copy(y_tile, y_hbm.at[pl.ds(0, T * R), :], sems.at[SCATTER_SEM]).wait()

    @pl.when((i == 0) & (c == 0))
    def _():
        y_scr[...] = jnp.zeros_like(y_scr)

        def issue(r, carry):
            row_copy(h_hbm, src0_ref[0, 0, r], x_scr.at[0], r * R, sems.at[0]).start()
            return carry
        lax.fori_loop(0, T, issue, 0)

    @pl.when(c == 0)
    def _():
        wait_gather(slot)

    @pl.when(valid & (c == 0))
    def _():
        xn_scr[...] = _rmsnorm(_load_token_major(x_scr.at[slot], T), g_ref[...]).astype(BF16)
        acc_scr[...] = jnp.zeros_like(acc_scr)

    def row_dmas(rows):
        for r in rows:
            row_copy(h_hbm, srcn_ref[0, 0, r], x_scr.at[other], r * R, sems.at[other]).start(priority=r % 2)
            row_copy(y_scr, r * R, y_hbm, dstp_ref[0, 0, r], sems.at[SCATTER_SEM]).start(priority=(r + 1) % 2)

    def pin(result):
        y_scr[pl.ds(T * R, SUBLANES), :] = result[0:SUBLANES, 0:LANES]
        x_scr[other, pl.ds(T * R, SUBLANES), :] = result[SUBLANES:2 * SUBLANES, 0:LANES]

    def expert_chunk(with_dmas):
        quarter = T // 4
        if with_dmas:
            row_dmas(range(0, quarter))
        xn = xn_scr[...]
        gate = jnp.dot(xn, wg_ref[0], preferred_element_type=F32)
        if with_dmas:
            pin(gate)
            row_dmas(range(quarter, 2 * quarter))
        up = jnp.dot(xn, wu_ref[0], preferred_element_type=F32)
        if with_dmas:
            pin(up)
            row_dmas(range(2 * quarter, 3 * quarter))
        mid = (jax.nn.silu(gate) * up).astype(BF16)
        down = jnp.dot(mid, wd_ref[0], preferred_element_type=F32)
        if with_dmas:
            pin(down)
            row_dmas(range(3 * quarter, T))
        acc_scr[...] += down

    @pl.when(valid & (c == 0))
    def _():
        expert_chunk(True)

    @pl.when(valid & (c != 0))
    def _():
        expert_chunk(False)

    @pl.when(jnp.logical_not(valid) & (c == 0))
    def _():
        row_dmas(range(T))

    @pl.when(c == n_chunks - 1)
    def _():
        wait_scatter()

    @pl.when(valid & (c == n_chunks - 1))
    def _():
        _store_token_major(y_scr, acc_scr[...])

    @pl.when((i == n_tiles - 1) & (c == n_chunks - 1))
    def _():
        wait_gather(other)

        def issue(r, carry):
            row_copy(y_scr, r * R, y_hbm, dstl_ref[0, 0, r], sems.at[SCATTER_SEM]).start()
            return carry
        lax.fori_loop(0, T, issue, 0)
        wait_scatter()


def _moe_experts(h1_tm, g, tile_expert, n_valid_tiles, src_rows, dst_rows, w_gate, w_up, w_down, n_out_tokens):
    D = D_MODEL
    T = MOE_TILE
    NT = src_rows.shape[0]
    NC = N_MOE_CHUNKS
    FC = D_FF // NC

    def chunk(i, c, nvt):
        return jnp.where(i < nvt[0], c, NC - 1)

    def idx_spec(index_map):
        return pl.BlockSpec((1, 1, T), index_map, memory_space=pltpu.SMEM)

    grid_spec = pltpu.PrefetchScalarGridSpec(
        num_scalar_prefetch=2,
        grid=(NT, NC),
        in_specs=[
            idx_spec(lambda i, c, te, nvt: (0, 0, 0)),
            idx_spec(lambda i, c, te, nvt: (jnp.minimum(i + 1, NT - 1), 0, 0)),
            idx_spec(lambda i, c, te, nvt: (i, 0, 0)),
            idx_spec(lambda i, c, te, nvt: (NT, 0, 0)),
            pl.BlockSpec(memory_space=pl.ANY),
            pl.BlockSpec((1, D), lambda i, c, te, nvt: (0, 0)),
            pl.BlockSpec((1, D, FC), lambda i, c, te, nvt: (te[i], 0, chunk(i, c, nvt))),
            pl.BlockSpec((1, D, FC), lambda i, c, te, nvt: (te[i], 0, chunk(i, c, nvt))),
            pl.BlockSpec((1, FC, D), lambda i, c, te, nvt: (te[i], chunk(i, c, nvt), 0)),
        ],
        out_specs=pl.BlockSpec(memory_space=pl.ANY),
        scratch_shapes=[pltpu.VMEM((2, T * ROWS_PER_TOKEN + SPARE_ROWS, LANES), F32), pltpu.VMEM((T, D), BF16),
                        pltpu.VMEM((T, D), F32), pltpu.VMEM((T * ROWS_PER_TOKEN + SPARE_ROWS, LANES), F32),
                        pltpu.SemaphoreType.DMA((3,))],
    )
    return pl.pallas_call(
        _moe_kernel,
        grid_spec=grid_spec,
        out_shape=jax.ShapeDtypeStruct((n_out_tokens * ROWS_PER_TOKEN, LANES), F32),
        compiler_params=pltpu.CompilerParams(dimension_semantics=("arbitrary", "arbitrary"),
                                             vmem_limit_bytes=VMEM_LIMIT),
        name="moe_experts",
    )(tile_expert, n_valid_tiles, src_rows, src_rows, dst_rows, dst_rows, h1_tm, g.reshape(1, D),
      w_gate, w_up, w_down)


def _moe_plan(route, n_tokens):
    T = MOE_TILE
    n_assign = TOP_K * n_tokens
    n_tiles = n_assign // T + N_EXPERTS
    e_flat = route[:, 0:TOP_K].astype(jnp.int32).reshape(-1)
    order = jnp.argsort(e_flat, stable=True).astype(jnp.int32)
    counts = jnp.sum((e_flat[:, None] == jnp.arange(N_EXPERTS, dtype=jnp.int32)[None, :]).astype(jnp.int32), axis=0)
    tiles_per = (counts + T - 1) // T
    tile_end = jnp.cumsum(tiles_per)
    n_valid = tile_end[-1]
    tile_ids = jnp.arange(n_tiles, dtype=jnp.int32)
    last_valid = jnp.maximum(n_valid - 1, 0)
    texp = jnp.sum((jnp.minimum(tile_ids, last_valid)[:, None] >= tile_end[None, :]).astype(jnp.int32), axis=1)
    texp = jnp.minimum(texp, N_EXPERTS - 1)
    onehot = (texp[:, None] == jnp.arange(N_EXPERTS, dtype=jnp.int32)[None, :]).astype(jnp.int32)
    pick = lambda v: jnp.sum(onehot * v[None, :], axis=1)
    j = tile_ids - pick(tile_end - tiles_per)
    start = pick(jnp.cumsum(counts) - counts) + j * T
    n_rows = jnp.where(tile_ids < n_valid, jnp.clip(pick(counts) - j * T, 0, T), 0)
    r = jnp.arange(T, dtype=jnp.int32)[None, :]
    assign = order[jnp.clip(start[:, None] + r, 0, n_assign - 1)]
    token = assign // TOP_K
    slot = assign % TOP_K
    row_valid = r < n_rows[:, None]
    pad_rows = jnp.broadcast_to(TOP_K * n_tokens + r, (1, T))
    dst = jnp.where(row_valid, slot * n_tokens + token, pad_rows)
    dst = jnp.concatenate([pad_rows, dst], axis=0)
    src_rows = (token * ROWS_PER_TOKEN).reshape(n_tiles, 1, T)
    dst_rows = (dst * ROWS_PER_TOKEN).reshape(n_tiles + 1, 1, T)
    return texp, n_valid.reshape(1).astype(jnp.int32), src_rows, dst_rows


def _combine_kernel(h_ref, y0_ref, y1_ref, route_ref, g_ref, o_ref):
    T = o_ref.shape[0]
    route = route_ref[...]
    moe = route[:, 2:3] * _load_token_major(y0_ref, T) + route[:, 3:4] * _load_token_major(y1_ref, T)
    o_ref[...] = _rmsnorm(_load_token_major(h_ref, T) + moe, g_ref[...])


def _combine(h1_tm, y2_tm, route, g):
    D = D_MODEL
    S = route.shape[0]
    T = OUT_TILE
    nb = S // T
    tm_block = (T * ROWS_PER_TOKEN, LANES)
    return pl.pallas_call(
        _combine_kernel,
        grid=(nb,),
        in_specs=[
            pl.BlockSpec(tm_block, lambda i: (i, 0)),
            pl.BlockSpec(tm_block, lambda i: (i, 0)),
            pl.BlockSpec(tm_block, lambda i: (i + nb, 0)),
            pl.BlockSpec((T, LANES), lambda i: (i, 0)),
            pl.BlockSpec((1, D), lambda i: (0, 0)),
        ],
        out_specs=pl.BlockSpec((T, D), lambda i: (i, 0)),
        out_shape=jax.ShapeDtypeStruct((S, D), F32),
        compiler_params=pltpu.CompilerParams(dimension_semantics=("arbitrary",), vmem_limit_bytes=VMEM_LIMIT),
        name="moe_combine",
    )(h1_tm, y2_tm, y2_tm, route, g.reshape(1, D))


def kernel(x, positions, norm_mix, norm_ffn, norm_final, lru_w_in, lru_conv_w, lru_conv_b, lru_w_a, lru_b_a, lru_w_x, lru_b_x, lru_lambda, lru_w_out, attn_w_qkv, attn_b_qkv, attn_sinks, attn_w_o, attn_b_o, ffn_w_gate, ffn_w_up, ffn_w_down, moe_w_router, moe_w_gate, moe_w_up, moe_w_down):
    B, S, D = x.shape
    assert B == 1 and D == D_MODEL and S % FFN_TILE == 0
    h = x.reshape(S, D)

    w_ax = jnp.concatenate([lru_w_a[0], lru_w_x[0]], axis=-1).astype(BF16)
    h, ffn_wg, ffn_wu, ffn_wd = _lru_mixer(h, norm_mix[0], lru_w_in[0].astype(BF16), lru_conv_w[0], lru_conv_b[0],
                                           w_ax, lru_b_a[0], lru_b_x[0], lru_lambda[0], lru_w_out[0].astype(BF16),
                                           ffn_w_gate[0], ffn_w_up[0], ffn_w_down[0])
    h, moe_wg, moe_wu, moe_wd = _dense_ffn(h, norm_ffn[0], ffn_wg, ffn_wu, ffn_wd,
                                           moe_w_gate[0], moe_w_up[0], moe_w_down[0])

    q_dim = N_HEADS * HEAD_DIM
    kv_dim = N_KV_HEADS * HEAD_DIM

    def dup_heads(w):
        w4 = w.reshape(w.shape[:-1] + (N_KV_HEADS, 1, HEAD_DIM))
        return jnp.broadcast_to(w4, w.shape[:-1] + (N_KV_HEADS, 2, HEAD_DIM)).reshape(w.shape[:-1] + (KV_EXT,))

    wqkv, bqkv = attn_w_qkv[0], attn_b_qkv[0]
    w_qkv_ext = jnp.concatenate([wqkv[:, :q_dim], dup_heads(wqkv[:, q_dim:q_dim + kv_dim]),
                                 dup_heads(wqkv[:, q_dim + kv_dim:])], axis=1).astype(BF16)
    b_qkv_ext = jnp.concatenate([bqkv[:q_dim], dup_heads(bqkv[q_dim:q_dim + kv_dim]),
                                 dup_heads(bqkv[q_dim + kv_dim:])]).reshape(1, QKV_EXT)
    inv_freq = ROPE_THETA ** (-jnp.arange(0, ROT_DIM, 2, dtype=F32) / ROT_DIM)
    d_in_head = jnp.arange(LANES) % HEAD_DIM
    freq_lane = jnp.where(d_in_head < ROT_DIM, inv_freq[d_in_head % (ROT_DIM // 2)], 0.0).reshape(1, LANES)
    sign_lane = jnp.where(d_in_head < ROT_DIM // 2, -1.0, jnp.where(d_in_head < ROT_DIM, 1.0, 0.0))
    sign_lane = sign_lane.astype(F32).reshape(1, LANES)
    w_router_pad = jnp.pad(moe_w_router[0], ((0, 0), (0, LANES - N_EXPERTS))).astype(BF16)
    pos_f = positions.reshape(S, 1).astype(F32)
    h1_tm, route = _attn_mixer(h, pos_f, norm_mix[1], freq_lane, sign_lane, w_qkv_ext, b_qkv_ext,
                               attn_sinks[0], attn_w_o[0].astype(BF16), attn_b_o[0], norm_ffn[1], w_router_pad)

    texp, n_valid, src_rows, dst_rows = _moe_plan(route, S)
    y2_tm = _moe_experts(h1_tm, norm_ffn[1], texp, n_valid, src_rows, dst_rows,
                         moe_wg, moe_wu, moe_wd,
                         TOP_K * S + MOE_TILE)
    out = _combine(h1_tm, y2_tm, route, norm_final)
    return out.reshape(B, S, D)
```

```python
import functools

import jax
import jax.numpy as jnp
from jax import lax
from jax.experimental import pallas as pl
from jax.experimental.pallas import tpu as pltpu

F32 = jnp.float32
BF16 = jnp.bfloat16

D_MODEL = 1024
N_RNN_BLOCKS = 8
RNN_BLOCK = D_MODEL // N_RNN_BLOCKS
CONV_WIDTH = 4
LRU_C = 8.0
N_HEADS = 16
N_KV_HEADS = 4
HEAD_DIM = 64
GROUP = N_HEADS // N_KV_HEADS
WINDOW = 128
ATTN_BLOCK = 128
ROPE_THETA = 500000.0
ROT_DIM = HEAD_DIM // 4
D_FF = 3584
N_EXPERTS = 8
TOP_K = 2
EPS = 1e-6
NEG_INF = -1e30
LOG2_E = 1.4426950408889634

LANES = 128
SUBLANES = 8
VMEM_LIMIT = 56 * 1024 * 1024

SEQ_TILE = 512
LRU_CHUNKS = 4
FFN_TILE = 1024
FF_CHUNK = 512
N_FF_CHUNKS = D_FF // FF_CHUNK
MOE_TILE = 512
N_MOE_CHUNKS = 2
OUT_TILE = 1024


def _rmsnorm(x, g):
    return x * lax.rsqrt(jnp.mean(x * x, axis=-1, keepdims=True) + EPS) * g


ROWS_PER_TOKEN = D_MODEL // LANES


def _store_token_major(ref, val):
    T = val.shape[0]
    for c in range(ROWS_PER_TOKEN):
        ref[pl.ds(c, T, stride=ROWS_PER_TOKEN), :] = val[:, LANES * c:LANES * (c + 1)]


def _load_token_major(ref, T):
    return jnp.concatenate([ref[pl.ds(c, T, stride=ROWS_PER_TOKEN), :] for c in range(ROWS_PER_TOKEN)], axis=1)


def _const_spec(shape):
    n = len(shape)
    return pl.BlockSpec(shape, lambda *_: (0,) * n, pipeline_mode=pl.Buffered(1))


def _time_permutation(T):
    rho = jnp.arange(T)
    t_of_rho = (rho % SUBLANES) * (T // SUBLANES) + rho // SUBLANES
    perm = (t_of_rho[:, None] == jnp.arange(T)[None, :]).astype(BF16)
    return perm, perm.T


def _lru_kernel(xin_ref, xres_ref, g_ref, perm_ref, unperm_ref, win_ref, cw_ref, cb_ref, wax_ref, ba_ref, bx_ref,
                lam_ref, wout_ref, cast_g_ref, cast_u_ref, cast_d_ref,
                o_ref, bf_g_ref, bf_u_ref, bf_d_ref, proj_scr, tail_scr, carry_scr):
    i = pl.program_id(0)
    n_tiles = pl.num_programs(0) - 1
    T = xin_ref.shape[0]
    D = D_MODEL
    C = T // LRU_CHUNKS
    GC = C // SUBLANES
    G = T // SUBLANES
    HALO = (CONV_WIDTH - 1) * SUBLANES

    def cast_slices():
        bf_g_ref[...] = cast_g_ref[...].astype(BF16)
        bf_u_ref[...] = cast_u_ref[...].astype(BF16)
        bf_d_ref[...] = cast_d_ref[...].astype(BF16)

    @pl.when(i == 0)
    def _():
        tail_scr[...] = jnp.zeros_like(tail_scr)
        carry_scr[...] = jnp.zeros_like(carry_scr)

    PW = 2 * D // LRU_CHUNKS

    def project_chunks():
        hn = _rmsnorm(xin_ref[...], g_ref[...]).astype(BF16)
        hp = jnp.dot(perm_ref[...], hn, preferred_element_type=F32).astype(BF16)
        for k in range(LRU_CHUNKS):
            yield jnp.dot(hp, win_ref[:, PW * k:PW * (k + 1)], preferred_element_type=F32)

    def recur_chunks():
        sub = lax.broadcasted_iota(jnp.int32, (SUBLANES, D), 0)
        cw = cw_ref[...]
        softplus_neg_lam = jax.nn.softplus(-lam_ref[...])
        last = proj_scr[T - HALO:T, D:]
        prev_groups = []
        for k in range(CONV_WIDTH - 1):
            rows = slice(SUBLANES * k, SUBLANES * (k + 1))
            prev_groups.append(pltpu.roll(jnp.where(sub == SUBLANES - 1, tail_scr[rows, :], last[rows, :]), 1, 0))
        tail_scr[...] = last

        ys, hs, decay = [], [], []
        for k in range(LRU_CHUNKS):
            pc = proj_scr[C * k:C * (k + 1), :]
            ys.append(jax.nn.gelu(pc[:, :D], approximate=True))
            xb = pc[:, D:]
            xc = cb_ref[...] + xb * cw[CONV_WIDTH - 1:CONV_WIDTH, :]
            for back in range(1, CONV_WIDTH):
                shifted = jnp.concatenate(prev_groups[CONV_WIDTH - 1 - back:] + [xb[:C - SUBLANES * back, :]], axis=0)
                xc = xc + shifted * cw[CONV_WIDTH - 1 - back:CONV_WIDTH - back, :]
            prev_groups = [xb[C - HALO + SUBLANES * q:C - HALO + SUBLANES * (q + 1), :] for q in range(CONV_WIDTH - 1)]

            xcb = xc.astype(BF16)
            r_parts, i_parts = [], []
            for n in range(N_RNN_BLOCKS):
                gn = jnp.dot(xcb[:, RNN_BLOCK * n:RNN_BLOCK * (n + 1)], wax_ref[n], preferred_element_type=F32)
                r_parts.append(gn[:, :RNN_BLOCK])
                i_parts.append(gn[:, RNN_BLOCK:])
            r = jax.nn.sigmoid(jnp.concatenate(r_parts, axis=1) + ba_ref[...])
            ig = jax.nn.sigmoid(jnp.concatenate(i_parts, axis=1) + bx_ref[...])
            log_a = -LRU_C * r * softplus_neg_lam
            a = jnp.exp(log_a)
            z = jnp.tanh(-log_a) * (1.0 + a * a)
            b = jnp.where(z > 0.0, z * lax.rsqrt(z), 0.0) * (ig * xc)

            for j in range(GC):
                rows = slice(SUBLANES * j, SUBLANES * (j + 1))
                if hs:
                    hs.append(a[rows, :] * hs[-1] + b[rows, :])
                    decay.append(a[rows, :] * decay[-1])
                else:
                    hs.append(b[rows, :])
                    decay.append(a[rows, :])
            yield None

        p_inc, e_inc = decay[-1], hs[-1]
        for k in (1, 2, 4):
            m = sub >= k
            p_sh = jnp.where(m, pltpu.roll(p_inc, k, 0), 1.0)
            e_sh = jnp.where(m, pltpu.roll(e_inc, k, 0), 0.0)
            e_inc = e_inc + p_inc * e_sh
            p_inc = p_inc * p_sh
        h0 = carry_scr[...]
        after = p_inc * h0 + e_inc
        carry_scr[...] = after[SUBLANES - 1:SUBLANES, :]
        enter = jnp.where(sub == 0, h0, pltpu.roll(after, 1, 0))
        h = jnp.concatenate([hs[j] + decay[j] * enter for j in range(G)], axis=0)
        hy = (h * jnp.concatenate(ys, axis=0)).astype(BF16)
        hy = jnp.dot(unperm_ref[...], hy, preferred_element_type=F32).astype(BF16)
        yield jnp.dot(hy, wout_ref[...], preferred_element_type=F32) + xres_ref[...]

    @pl.when(i == 0)
    def _():
        cast_slices()
        for k, pc in enumerate(project_chunks()):
            proj_scr[:, PW * k:PW * (k + 1)] = pc

    @pl.when((i > 0) & (i < n_tiles))
    def _():
        cast_slices()
        rec = recur_chunks()
        new_proj = []
        for pc in project_chunks():
            new_proj.append(pc)
            next(rec)
        o_ref[...] = next(rec)
        for k, pc in enumerate(new_proj):
            proj_scr[:, PW * k:PW * (k + 1)] = pc

    @pl.when(i == n_tiles)
    def _():
        o_ref[...] = list(recur_chunks())[-1]


def _lru_mixer(x, g, w_in, conv_w, conv_b, w_ax, b_a, b_x, lam, w_out, cast_gate, cast_up, cast_down):
    S, D = x.shape
    T = SEQ_TILE
    n_tiles = S // T
    row = lambda v: v.reshape(1, D)
    perm, unperm = _time_permutation(T)

    def cast_spec(w):
        return pl.BlockSpec((w.shape[0] // n_tiles, w.shape[1]), lambda i: (jnp.minimum(i, n_tiles - 1), 0))

    casts = (cast_gate, cast_up, cast_down)
    return pl.pallas_call(
        _lru_kernel,
        grid=(n_tiles + 1,),
        in_specs=[
            pl.BlockSpec((T, D), lambda i: (jnp.minimum(i, n_tiles - 1), 0)),
            pl.BlockSpec((T, D), lambda i: (jnp.maximum(i - 1, 0), 0)),
            _const_spec((1, D)),
            _const_spec((T, T)),
            _const_spec((T, T)),
            _const_spec((D, 2 * D)),
            _const_spec((CONV_WIDTH, D)),
            _const_spec((1, D)),
            _const_spec((N_RNN_BLOCKS, RNN_BLOCK, 2 * RNN_BLOCK)),
            _const_spec((1, D)),
            _const_spec((1, D)),
            _const_spec((1, D)),
            _const_spec((D, D)),
        ] + [cast_spec(w) for w in casts],
        out_specs=[pl.BlockSpec((T, D), lambda i: (jnp.maximum(i - 1, 0), 0))] + [cast_spec(w) for w in casts],
        out_shape=[jax.ShapeDtypeStruct((S, D), F32)] + [jax.ShapeDtypeStruct(w.shape, BF16) for w in casts],
        scratch_shapes=[pltpu.VMEM((T, 2 * D), F32), pltpu.VMEM(((CONV_WIDTH - 1) * SUBLANES, D), F32),
                        pltpu.VMEM((1, D), F32)],
        compiler_params=pltpu.CompilerParams(dimension_semantics=("arbitrary",), vmem_limit_bytes=VMEM_LIMIT),
        name="lru_mixer",
    )(x, x, row(g), perm, unperm, w_in, conv_w, row(conv_b), w_ax, row(b_a), row(b_x), row(lam), w_out, *casts)


def _ffn_kernel(x_ref, g_ref, wg_ref, wu_ref, wd_ref, cast_g_ref, cast_u_ref, cast_d_ref,
                o_ref, bf_g_ref, bf_u_ref, bf_d_ref, xn_scr, acc_scr):
    c = pl.program_id(1)
    last = pl.num_programs(1) - 1

    def ff_chunk():
        bf_g_ref[...] = cast_g_ref[...].astype(BF16)
        bf_u_ref[...] = cast_u_ref[...].astype(BF16)
        bf_d_ref[...] = cast_d_ref[...].astype(BF16)
        xn = xn_scr[...]
        gate = jnp.dot(xn, wg_ref[...], preferred_element_type=F32)
        up = jnp.dot(xn, wu_ref[...], preferred_element_type=F32)
        mid = (jax.nn.silu(gate) * up).astype(BF16)
        return jnp.dot(mid, wd_ref[...], preferred_element_type=F32)

    @pl.when(c == 0)
    def _():
        xn_scr[...] = _rmsnorm(x_ref[...], g_ref[...]).astype(BF16)
        acc_scr[...] = ff_chunk()

    @pl.when((c > 0) & (c < last))
    def _():
        acc_scr[...] += ff_chunk()

    @pl.when(c == last)
    def _():
        o_ref[...] = x_ref[...] + (acc_scr[...] + ff_chunk())


def _dense_ffn(x, g, w_gate, w_up, w_down, cast_gate, cast_up, cast_down):
    S, D = x.shape
    T = FFN_TILE
    n_steps = (S // T) * N_FF_CHUNKS
    E = cast_gate.shape[0]
    row_parts = n_steps // (E * N_FF_CHUNKS)
    assert row_parts * E * N_FF_CHUNKS == n_steps
    gu_block = (1, D // row_parts, D_FF // N_FF_CHUNKS)
    d_block = (1, D_FF // (row_parts * N_FF_CHUNKS), D)

    def gu_map(i, c):
        s = i * N_FF_CHUNKS + c
        return (s // (row_parts * N_FF_CHUNKS), (s // N_FF_CHUNKS) % row_parts, c)

    def d_map(i, c):
        s = i * N_FF_CHUNKS + c
        return (s // (row_parts * N_FF_CHUNKS), s % (row_parts * N_FF_CHUNKS), 0)

    return pl.pallas_call(
        _ffn_kernel,
        grid=(S // T, N_FF_CHUNKS),
        in_specs=[
            pl.BlockSpec((T, D), lambda i, c: (i, 0)),
            pl.BlockSpec((1, D), lambda i, c: (0, 0)),
            pl.BlockSpec((D, FF_CHUNK), lambda i, c: (0, c)),
            pl.BlockSpec((D, FF_CHUNK), lambda i, c: (0, c)),
            pl.BlockSpec((FF_CHUNK, D), lambda i, c: (c, 0)),
            pl.BlockSpec(gu_block, gu_map),
            pl.BlockSpec(gu_block, gu_map),
            pl.BlockSpec(d_block, d_map),
        ],
        out_specs=[pl.BlockSpec((T, D), lambda i, c: (i, 0)), pl.BlockSpec(gu_block, gu_map),
                   pl.BlockSpec(gu_block, gu_map), pl.BlockSpec(d_block, d_map)],
        out_shape=[jax.ShapeDtypeStruct((S, D), F32), jax.ShapeDtypeStruct(cast_gate.shape, BF16),
                   jax.ShapeDtypeStruct(cast_up.shape, BF16), jax.ShapeDtypeStruct(cast_down.shape, BF16)],
        scratch_shapes=[pltpu.VMEM((T, D), BF16), pltpu.VMEM((T, D), F32)],
        compiler_params=pltpu.CompilerParams(dimension_semantics=("arbitrary", "arbitrary"),
                                             vmem_limit_bytes=VMEM_LIMIT),
        name="dense_ffn",
    )(x, g.reshape(1, D), w_gate, w_up, w_down, cast_gate, cast_up, cast_down)


KV_EXT = N_KV_HEADS * LANES
QKV_EXT = N_HEADS * HEAD_DIM + 2 * KV_EXT


def _attn_kernel(sink_ref, x_ref, pos_ref, g_ref, freq_ref, sign_ref, wqkv_ref, bqkv_ref, wo_ref, bo_ref,
                 g2_ref, wr_ref, o_ref, route_ref,
                 qlo_scr, qhi_scr, k_scr, v_scr, o_scr):
    T = x_ref.shape[0]
    D = D_MODEL
    B = ATTN_BLOCK
    first_tile = pl.program_id(0) == 0

    @pl.when(first_tile)
    def _():
        k_scr[0:B, :] = jnp.zeros((B, KV_EXT), BF16)
        v_scr[0:B, :] = jnp.zeros((B, KV_EXT), BF16)

    x = x_ref[...]
    hn = _rmsnorm(x, g_ref[...]).astype(BF16)
    qkv = jnp.dot(hn, wqkv_ref[...], preferred_element_type=F32) + bqkv_ref[...]

    ang = pos_ref[...] * freq_ref[...]
    cos_t = jnp.cos(ang)
    sin_t = jnp.sin(ang) * sign_ref[...]
    lane = lax.broadcasted_iota(jnp.int32, (T, LANES), 1)
    first_half = (lane % HEAD_DIM) < (ROT_DIM // 2)
    lo_half = lane < HEAD_DIM

    def rope(col):
        partner = jnp.where(first_half, pltpu.roll(col, LANES - ROT_DIM // 2, 1), pltpu.roll(col, ROT_DIM // 2, 1))
        return col * cos_t + partner * sin_t

    scale = HEAD_DIM ** -0.5 * LOG2_E
    for c in range(D // LANES):
        qc = rope(qkv[:, LANES * c:LANES * (c + 1)]) * scale
        qlo_scr[:, LANES * c:LANES * (c + 1)] = jnp.where(lo_half, qc, 0.0).astype(BF16)
        qhi_scr[:, LANES * c:LANES * (c + 1)] = jnp.where(lo_half, 0.0, qc).astype(BF16)
    for g in range(N_KV_HEADS):
        kc = rope(qkv[:, D + LANES * g:D + LANES * (g + 1)])
        k_scr[B:B + T, LANES * g:LANES * (g + 1)] = kc.astype(BF16)
        vc = qkv[:, D + KV_EXT + LANES * g:D + KV_EXT + LANES * (g + 1)]
        v_scr[B:B + T, LANES * g:LANES * (g + 1)] = vc.astype(BF16)

    qi = lax.broadcasted_iota(jnp.int32, (B, B), 0)
    kj = lax.broadcasted_iota(jnp.int32, (B, B), 1)
    causal_own = kj <= qi
    window_prev = kj > qi
    key0 = kj == 0
    key0_row = lax.broadcasted_iota(jnp.int32, (1, B), 1) == 0
    lo_blk = lax.broadcasted_iota(jnp.int32, (B, LANES), 1) < HEAD_DIM

    def block_body(blk, _):
        r0 = pl.multiple_of(blk * B, B)
        k_lo = jnp.where(first_tile & (blk == 0), B, 0)
        allowed_prev = window_prev & (kj >= k_lo)
        for g in range(N_KV_HEADS):
            kk = k_scr[pl.ds(r0, 2 * B), LANES * g:LANES * (g + 1)]
            vv = v_scr[pl.ds(r0, 2 * B), LANES * g:LANES * (g + 1)]
            cols = [slice(LANES * c, LANES * (c + 1)) for c in range((GROUP // 2) * g, (GROUP // 2) * (g + 1))]
            q_all = jnp.concatenate([q_scr[pl.ds(r0, B), col] for col in cols for q_scr in (qlo_scr, qhi_scr)], axis=0)
            s_all = lax.dot_general(q_all, kk, (((1,), (1,)), ((), ())), preferred_element_type=F32)
            p_all, denoms = [], []
            for hh in range(GROUP):
                rows = slice(B * hh, B * (hh + 1))
                sink_fill = jnp.where(key0_row, sink_ref[GROUP * g + hh] * LOG2_E, NEG_INF)
                s_prev = jnp.where(allowed_prev, s_all[rows, :B], sink_fill)
                s_own = jnp.where(causal_own, s_all[rows, B:], NEG_INF)
                m = jnp.max(jnp.maximum(s_prev, s_own), axis=-1, keepdims=True)
                p_prev = jnp.exp2(s_prev - m)
                p_own = jnp.exp2(s_own - m)
                denoms.append(jnp.sum(p_prev + p_own, axis=-1, keepdims=True))
                p_all.append(jnp.concatenate([jnp.where(key0, 0.0, p_prev), p_own], axis=1).astype(BF16))
            o_all = jnp.dot(jnp.concatenate(p_all, axis=0), vv, preferred_element_type=F32)
            o_heads = [o_all[B * hh:B * (hh + 1), :] / denoms[hh] for hh in range(GROUP)]
            for p, col in enumerate(cols):
                o_scr[pl.ds(r0, B), col] = jnp.where(lo_blk, o_heads[2 * p], o_heads[2 * p + 1]).astype(BF16)
        return 0

    lax.fori_loop(0, T // B, block_body, 0)

    k_scr[0:B, :] = k_scr[T:T + B, :]
    v_scr[0:B, :] = v_scr[T:T + B, :]

    h1 = jnp.dot(o_scr[...], wo_ref[...], preferred_element_type=F32) + bo_ref[...] + x
    _store_token_major(o_ref, h1)

    hn2 = _rmsnorm(h1, g2_ref[...]).astype(BF16)
    logits = jnp.dot(hn2, wr_ref[...], preferred_element_type=F32)
    lane_f = lane.astype(F32)
    lg = jnp.where(lane < N_EXPERTS, logits, -jnp.inf)
    v1 = jnp.max(lg, axis=-1, keepdims=True)
    i1 = jnp.min(jnp.where(lg == v1, lane_f, float(LANES)), axis=-1, keepdims=True)
    lg2 = jnp.where(lane_f == i1, -jnp.inf, lg)
    v2 = jnp.max(lg2, axis=-1, keepdims=True)
    i2 = jnp.min(jnp.where(lg2 == v2, lane_f, float(LANES)), axis=-1, keepdims=True)
    e2 = jnp.exp(v2 - v1)
    gate1 = 1.0 / (1.0 + e2)
    gate2 = e2 / (1.0 + e2)
    route_ref[...] = jnp.where(lane == 0, i1, jnp.where(lane == 1, i2,
                                                        jnp.where(lane == 2, gate1, jnp.where(lane == 3, gate2, 0.0))))


def _attn_mixer(x, pos_f, g, freq_lane, sign_lane, w_qkv_ext, b_qkv_ext, sinks, w_o, b_o, g2, w_router_pad):
    S, D = x.shape
    T = SEQ_TILE
    B = ATTN_BLOCK
    grid_spec = pltpu.PrefetchScalarGridSpec(
        num_scalar_prefetch=1,
        grid=(S // T,),
        in_specs=[
            pl.BlockSpec((T, D), lambda i, s: (i, 0)),
            pl.BlockSpec((T, 1), lambda i, s: (i, 0)),
            _const_spec((1, D)),
            _const_spec((1, LANES)),
            _const_spec((1, LANES)),
            _const_spec((D, QKV_EXT)),
            _const_spec((1, QKV_EXT)),
            _const_spec((D, D)),
            _const_spec((1, D)),
            _const_spec((1, D)),
            _const_spec((D, LANES)),
        ],
        out_specs=[pl.BlockSpec((T * ROWS_PER_TOKEN, LANES), lambda i, s: (i, 0)),
                   pl.BlockSpec((T, LANES), lambda i, s: (i, 0))],
        scratch_shapes=[
            pltpu.VMEM((T, D), BF16), pltpu.VMEM((T, D), BF16),
            pltpu.VMEM((T + B, KV_EXT), BF16), pltpu.VMEM((T + B, KV_EXT), BF16),
            pltpu.VMEM((T, D), BF16),
        ],
    )
    return pl.pallas_call(
        _attn_kernel,
        grid_spec=grid_spec,
        out_shape=[jax.ShapeDtypeStruct((S * ROWS_PER_TOKEN, LANES), F32), jax.ShapeDtypeStruct((S, LANES), F32)],
        compiler_params=pltpu.CompilerParams(dimension_semantics=("arbitrary",), vmem_limit_bytes=VMEM_LIMIT),
        name="attn_mixer",
    )(sinks, x, pos_f, g.reshape(1, D), freq_lane, sign_lane, w_qkv_ext, b_qkv_ext, w_o, b_o.reshape(1, D),
      g2.reshape(1, D), w_router_pad)


def _moe_kernel(texp_ref, nvt_ref, src0_ref, srcn_ref, dstp_ref, dstl_ref, h_hbm, g_ref, wg_ref, wu_ref, wd_ref,
                y_hbm, x_scr, xn_scr, acc_scr, y_scr, sems):
    i = pl.program_id(0)
    c = pl.program_id(1)
    n_tiles = pl.num_programs(0)
    n_chunks = pl.num_programs(1)
    T = xn_scr.shape[0]
    R = ROWS_PER_TOKEN
    valid = i < nvt_ref[0]
    slot = i % 2
    other = 1 - slot
    SCATTER_SEM = 2

    def row_copy(src, src_row, dst, dst_row, sem):
        return pltpu.make_async_copy(src.at[pl.ds(pl.multiple_of(src_row, R), R), :],
                                     dst.at[pl.ds(pl.multiple_of(dst_row, R), R), :], sem)

    def wait_gather(s):
        pltpu.make_async_copy(h_hbm.at[pl.ds(0, T * R), :], x_scr.at[s], sems.at[s]).wait()

    def wait_scatter():
        pltpu.make_async_copy(y_scr, y_hbm.at[pl.ds(0, T * R), :], sems.at[SCATTER_SEM]).wait()

    @pl.when((i == 0) & (c == 0))
    def _():
        y_scr[...] = jnp.zeros_like(y_scr)

        def issue(r, carry):
            row_copy(h_hbm, src0_ref[0, 0, r], x_scr.at[0], r * R, sems.at[0]).start()
            return carry
        lax.fori_loop(0, T, issue, 0)

    @pl.when(c == 0)
    def _():
        wait_gather(slot)

    @pl.when(valid & (c == 0))
    def _():
        xn_scr[...] = _rmsnorm(_load_token_major(x_scr.at[slot], T), g_ref[...]).astype(BF16)

    def row_dmas():
        for r in range(T):
            row_copy(h_hbm, srcn_ref[0, 0, r], x_scr.at[other], r * R, sems.at[other]).start()
            row_copy(y_scr, r * R, y_hbm, dstp_ref[0, 0, r], sems.at[SCATTER_SEM]).start()

    def expert_chunk(first):
        xn = xn_scr[...]
        gate = jnp.dot(xn, wg_ref[0], preferred_element_type=F32)
        up = jnp.dot(xn, wu_ref[0], preferred_element_type=F32)
        mid = (jax.nn.silu(gate) * up).astype(BF16)
        down = jnp.dot(mid, wd_ref[0], preferred_element_type=F32)
        if first:
            acc_scr[...] = down
        else:
            acc_scr[...] += down

    @pl.when(valid & (c == 0))
    def _():
        row_dmas()
        expert_chunk(True)

    @pl.when(valid & (c != 0))
    def _():
        expert_chunk(False)

    @pl.when(jnp.logical_not(valid) & (c == 0))
    def _():
        row_dmas()

    @pl.when(c == n_chunks - 1)
    def _():
        wait_scatter()

    @pl.when(valid & (c == n_chunks - 1))
    def _():
        _store_token_major(y_scr, acc_scr[...])

    @pl.when((i == n_tiles - 1) & (c == n_chunks - 1))
    def _():
        wait_gather(other)


        def issue(r, carry):
            row_copy(y_scr, r * R, y_hbm, dstl_ref[0, 0, r], sems.at[SCATTER_SEM]).start()
            return carry
        lax.fori_loop(0, T, issue, 0)
        wait_scatter()


def _moe_experts(h1_tm, g, tile_expert, n_valid_tiles, src_rows, dst_rows, w_gate, w_up, w_down, n_out_tokens):
    D = D_MODEL
    T = MOE_TILE
    NT = src_rows.shape[0]
    NC = N_MOE_CHUNKS
    FC = D_FF // NC

    def chunk(i, c, nvt):
        return jnp.where(i < nvt[0], c, NC - 1)

    def idx_spec(index_map):
        return pl.BlockSpec((1, 1, T), index_map, memory_space=pltpu.SMEM)

    grid_spec = pltpu.PrefetchScalarGridSpec(
        num_scalar_prefetch=2,
        grid=(NT, NC),
        in_specs=[
            idx_spec(lambda i, c, te, nvt: (0, 0, 0)),
            idx_spec(lambda i, c, te, nvt: (jnp.minimum(i + 1, NT - 1), 0, 0)),
            idx_spec(lambda i, c, te, nvt: (i, 0, 0)),
            idx_spec(lambda i, c, te, nvt: (NT, 0, 0)),
            pl.BlockSpec(memory_space=pl.ANY),
            pl.BlockSpec((1, D), lambda i, c, te, nvt: (0, 0)),
            pl.BlockSpec((1, D, FC), lambda i, c, te, nvt: (te[i], 0, chunk(i, c, nvt))),
            pl.BlockSpec((1, D, FC), lambda i, c, te, nvt: (te[i], 0, chunk(i, c, nvt))),
            pl.BlockSpec((1, FC, D), lambda i, c, te, nvt: (te[i], chunk(i, c, nvt), 0)),
        ],
        out_specs=pl.BlockSpec(memory_space=pl.ANY),
        scratch_shapes=[pltpu.VMEM((2, T * ROWS_PER_TOKEN, LANES), F32), pltpu.VMEM((T, D), BF16),
                        pltpu.VMEM((T, D), F32), pltpu.VMEM((T * ROWS_PER_TOKEN, LANES), F32),
                        pltpu.SemaphoreType.DMA((3,))],
    )
    return pl.pallas_call(
        _moe_kernel,
        grid_spec=grid_spec,
        out_shape=jax.ShapeDtypeStruct((n_out_tokens * ROWS_PER_TOKEN, LANES), F32),
        compiler_params=pltpu.CompilerParams(dimension_semantics=("arbitrary", "arbitrary"),
                                             vmem_limit_bytes=VMEM_LIMIT),
        name="moe_experts",
    )(tile_expert, n_valid_tiles, src_rows, src_rows, dst_rows, dst_rows, h1_tm, g.reshape(1, D),
      w_gate, w_up, w_down)


def _moe_plan(route, n_tokens):
    T = MOE_TILE
    n_assign = TOP_K * n_tokens
    n_tiles = n_assign // T + N_EXPERTS
    e_flat = route[:, 0:TOP_K].astype(jnp.int32).reshape(-1)
    order = jnp.argsort(e_flat, stable=True).astype(jnp.int32)
    counts = jnp.sum((e_flat[:, None] == jnp.arange(N_EXPERTS, dtype=jnp.int32)[None, :]).astype(jnp.int32), axis=0)
    tiles_per = (counts + T - 1) // T
    tile_end = jnp.cumsum(tiles_per)
    n_valid = tile_end[-1]
    tile_ids = jnp.arange(n_tiles, dtype=jnp.int32)
    last_valid = jnp.maximum(n_valid - 1, 0)
    texp = jnp.sum((jnp.minimum(tile_ids, last_valid)[:, None] >= tile_end[None, :]).astype(jnp.int32), axis=1)
    texp = jnp.minimum(texp, N_EXPERTS - 1)
    onehot = (texp[:, None] == jnp.arange(N_EXPERTS, dtype=jnp.int32)[None, :]).astype(jnp.int32)
    pick = lambda v: jnp.sum(onehot * v[None, :], axis=1)
    j = tile_ids - pick(tile_end - tiles_per)
    start = pick(jnp.cumsum(counts) - counts) + j * T
    n_rows = jnp.where(tile_ids < n_valid, jnp.clip(pick(counts) - j * T, 0, T), 0)
    r = jnp.arange(T, dtype=jnp.int32)[None, :]
    assign = order[jnp.clip(start[:, None] + r, 0, n_assign - 1)]
    token = assign // TOP_K
    slot = assign % TOP_K
    row_valid = r < n_rows[:, None]
    pad_rows = jnp.broadcast_to(TOP_K * n_tokens + r, (1, T))
    dst = jnp.where(row_valid, slot * n_tokens + token, pad_rows)
    dst = jnp.concatenate([pad_rows, dst], axis=0)
    src_rows = (token * ROWS_PER_TOKEN).reshape(n_tiles, 1, T)
    dst_rows = (dst * ROWS_PER_TOKEN).reshape(n_tiles + 1, 1, T)
    return texp, n_valid.reshape(1).astype(jnp.int32), src_rows, dst_rows


def _combine_kernel(h_ref, y0_ref, y1_ref, route_ref, g_ref, o_ref):
    T = o_ref.shape[0]
    route = route_ref[...]
    moe = route[:, 2:3] * _load_token_major(y0_ref, T) + route[:, 3:4] * _load_token_major(y1_ref, T)
    o_ref[...] = _rmsnorm(_load_token_major(h_ref, T) + moe, g_ref[...])


def _combine(h1_tm, y2_tm, route, g):
    D = D_MODEL
    S = route.shape[0]
    T = OUT_TILE
    nb = S // T
    tm_block = (T * ROWS_PER_TOKEN, LANES)
    return pl.pallas_call(
        _combine_kernel,
        grid=(nb,),
        in_specs=[
            pl.BlockSpec(tm_block, lambda i: (i, 0)),
            pl.BlockSpec(tm_block, lambda i: (i, 0)),
            pl.BlockSpec(tm_block, lambda i: (i + nb, 0)),
            pl.BlockSpec((T, LANES), lambda i: (i, 0)),
            pl.BlockSpec((1, D), lambda i: (0, 0)),
        ],
        out_specs=pl.BlockSpec((T, D), lambda i: (i, 0)),
        out_shape=jax.ShapeDtypeStruct((S, D), F32),
        compiler_params=pltpu.CompilerParams(dimension_semantics=("arbitrary",), vmem_limit_bytes=VMEM_LIMIT),
        name="moe_combine",
    )(h1_tm, y2_tm, y2_tm, route, g.reshape(1, D))


def kernel(x, positions, norm_mix, norm_ffn, norm_final, lru_w_in, lru_conv_w, lru_conv_b, lru_w_a, lru_b_a, lru_w_x, lru_b_x, lru_lambda, lru_w_out, attn_w_qkv, attn_b_qkv, attn_sinks, attn_w_o, attn_b_o, ffn_w_gate, ffn_w_up, ffn_w_down, moe_w_router, moe_w_gate, moe_w_up, moe_w_down):
    B, S, D = x.shape
    assert B == 1 and D == D_MODEL and S % FFN_TILE == 0
    h = x.reshape(S, D)

    w_ax = jnp.concatenate([lru_w_a[0], lru_w_x[0]], axis=-1).astype(BF16)
    h, ffn_wg, ffn_wu, ffn_wd = _lru_mixer(h, norm_mix[0], lru_w_in[0].astype(BF16), lru_conv_w[0], lru_conv_b[0],
                                           w_ax, lru_b_a[0], lru_b_x[0], lru_lambda[0], lru_w_out[0].astype(BF16),
                                           ffn_w_gate[0], ffn_w_up[0], ffn_w_down[0])
    h, moe_wg, moe_wu, moe_wd = _dense_ffn(h, norm_ffn[0], ffn_wg, ffn_wu, ffn_wd,
                                           moe_w_gate[0], moe_w_up[0], moe_w_down[0])

    q_dim = N_HEADS * HEAD_DIM
    kv_dim = N_KV_HEADS * HEAD_DIM

    def dup_heads(w):
        w4 = w.reshape(w.shape[:-1] + (N_KV_HEADS, 1, HEAD_DIM))
        return jnp.broadcast_to(w4, w.shape[:-1] + (N_KV_HEADS, 2, HEAD_DIM)).reshape(w.shape[:-1] + (KV_EXT,))

    wqkv, bqkv = attn_w_qkv[0], attn_b_qkv[0]
    w_qkv_ext = jnp.concatenate([wqkv[:, :q_dim], dup_heads(wqkv[:, q_dim:q_dim + kv_dim]),
                                 dup_heads(wqkv[:, q_dim + kv_dim:])], axis=1).astype(BF16)
    b_qkv_ext = jnp.concatenate([bqkv[:q_dim], dup_heads(bqkv[q_dim:q_dim + kv_dim]),
                                 dup_heads(bqkv[q_dim + kv_dim:])]).reshape(1, QKV_EXT)
    inv_freq = ROPE_THETA ** (-jnp.arange(0, ROT_DIM, 2, dtype=F32) / ROT_DIM)
    d_in_head = jnp.arange(LANES) % HEAD_DIM
    freq_lane = jnp.where(d_in_head < ROT_DIM, inv_freq[d_in_head % (ROT_DIM // 2)], 0.0).reshape(1, LANES)
    sign_lane = jnp.where(d_in_head < ROT_DIM // 2, -1.0, jnp.where(d_in_head < ROT_DIM, 1.0, 0.0))
    sign_lane = sign_lane.astype(F32).reshape(1, LANES)
    w_router_pad = jnp.pad(moe_w_router[0], ((0, 0), (0, LANES - N_EXPERTS))).astype(BF16)
    pos_f = positions.reshape(S, 1).astype(F32)
    h1_tm, route = _attn_mixer(h, pos_f, norm_mix[1], freq_lane, sign_lane, w_qkv_ext, b_qkv_ext,
                               attn_sinks[0], attn_w_o[0].astype(BF16), attn_b_o[0], norm_ffn[1], w_router_pad)

    texp, n_valid, src_rows, dst_rows = _moe_plan(route, S)
    y2_tm = _moe_experts(h1_tm, norm_ffn[1], texp, n_valid, src_rows, dst_rows,
                         moe_wg, moe_wu, moe_wd,
                         TOP_K * S + MOE_TILE)
    out = _combine(h1_tm, y2_tm, route, norm_final)
    return out.reshape(B, S, D)
```

```python
import functools

import jax
import jax.numpy as jnp
from jax import lax
from jax.experimental import pallas as pl
from jax.experimental.pallas import tpu as pltpu

F32 = jnp.float32
BF16 = jnp.bfloat16

D_MODEL = 1024
N_RNN_BLOCKS = 8
RNN_BLOCK = D_MODEL // N_RNN_BLOCKS
CONV_WIDTH = 4
LRU_C = 8.0
N_HEADS = 16
N_KV_HEADS = 4
HEAD_DIM = 64
GROUP = N_HEADS // N_KV_HEADS
WINDOW = 128
ATTN_BLOCK = 128
ROPE_THETA = 500000.0
ROT_DIM = HEAD_DIM // 4
D_FF = 3584
N_EXPERTS = 8
TOP_K = 2
EPS = 1e-6
NEG_INF = -1e30
LOG2_E = 1.4426950408889634

LANES = 128
SUBLANES = 8
VMEM_LIMIT = 56 * 1024 * 1024

SEQ_TILE = 512
LRU_CHUNKS = 4
FFN_TILE = 1024
FF_CHUNK = 512
N_FF_CHUNKS = D_FF // FF_CHUNK
MOE_TILE = 512
N_MOE_CHUNKS = 2
OUT_TILE = 1024


def _rmsnorm(x, g):
    return x * lax.rsqrt(jnp.mean(x * x, axis=-1, keepdims=True) + EPS) * g


ROWS_PER_TOKEN = D_MODEL // LANES


def _store_token_major(ref, val):
    T = val.shape[0]
    for c in range(ROWS_PER_TOKEN):
        ref[pl.ds(c, T, stride=ROWS_PER_TOKEN), :] = val[:, LANES * c:LANES * (c + 1)]


def _load_token_major(ref, T):
    return jnp.concatenate([ref[pl.ds(c, T, stride=ROWS_PER_TOKEN), :] for c in range(ROWS_PER_TOKEN)], axis=1)


def _const_spec(shape):
    n = len(shape)
    return pl.BlockSpec(shape, lambda *_: (0,) * n, pipeline_mode=pl.Buffered(1))


def _time_permutation(T):
    rho = jnp.arange(T)
    t_of_rho = (rho % SUBLANES) * (T // SUBLANES) + rho // SUBLANES
    perm = (t_of_rho[:, None] == jnp.arange(T)[None, :]).astype(BF16)
    return perm, perm.T


def _lru_kernel(xin_ref, xres_ref, g_ref, perm_ref, unperm_ref, win_ref, cw_ref, cb_ref, wax_ref, ba_ref, bx_ref,
                lam_ref, wout_ref, cast_g_ref, cast_u_ref, cast_d_ref,
                o_ref, bf_g_ref, bf_u_ref, bf_d_ref, proj_scr, tail_scr, carry_scr):
    i = pl.program_id(0)
    n_tiles = pl.num_programs(0) - 1
    T = xin_ref.shape[0]
    D = D_MODEL
    C = T // LRU_CHUNKS
    GC = C // SUBLANES
    G = T // SUBLANES
    HALO = (CONV_WIDTH - 1) * SUBLANES

    def cast_slices():
        bf_g_ref[...] = cast_g_ref[...].astype(BF16)
        bf_u_ref[...] = cast_u_ref[...].astype(BF16)
        bf_d_ref[...] = cast_d_ref[...].astype(BF16)

    @pl.when(i == 0)
    def _():
        tail_scr[...] = jnp.zeros_like(tail_scr)
        carry_scr[...] = jnp.zeros_like(carry_scr)

    PW = 2 * D // LRU_CHUNKS

    def project_chunks():
        hn = _rmsnorm(xin_ref[...], g_ref[...]).astype(BF16)
        hp = jnp.dot(perm_ref[...], hn, preferred_element_type=F32).astype(BF16)
        for k in range(LRU_CHUNKS):
            yield jnp.dot(hp, win_ref[:, PW * k:PW * (k + 1)], preferred_element_type=F32)

    def recur_chunks():
        sub = lax.broadcasted_iota(jnp.int32, (SUBLANES, D), 0)
        cw = cw_ref[...]
        softplus_neg_lam = jax.nn.softplus(-lam_ref[...])
        last = proj_scr[T - HALO:T, D:]
        prev_groups = []
        for k in range(CONV_WIDTH - 1):
            rows = slice(SUBLANES * k, SUBLANES * (k + 1))
            prev_groups.append(pltpu.roll(jnp.where(sub == SUBLANES - 1, tail_scr[rows, :], last[rows, :]), 1, 0))
        tail_scr[...] = last

        ys, hs, decay = [], [], []
        for k in range(LRU_CHUNKS):
            pc = proj_scr[C * k:C * (k + 1), :]
            ys.append(jax.nn.gelu(pc[:, :D], approximate=True))
            xb = pc[:, D:]
            xc = cb_ref[...] + xb * cw[CONV_WIDTH - 1:CONV_WIDTH, :]
            for back in range(1, CONV_WIDTH):
                shifted = jnp.concatenate(prev_groups[CONV_WIDTH - 1 - back:] + [xb[:C - SUBLANES * back, :]], axis=0)
                xc = xc + shifted * cw[CONV_WIDTH - 1 - back:CONV_WIDTH - back, :]
            prev_groups = [xb[C - HALO + SUBLANES * q:C - HALO + SUBLANES * (q + 1), :] for q in range(CONV_WIDTH - 1)]

            xcb = xc.astype(BF16)
            r_parts, i_parts = [], []
            for n in range(N_RNN_BLOCKS):
                gn = jnp.dot(xcb[:, RNN_BLOCK * n:RNN_BLOCK * (n + 1)], wax_ref[n], preferred_element_type=F32)
                r_parts.append(gn[:, :RNN_BLOCK])
                i_parts.append(gn[:, RNN_BLOCK:])
            r = jax.nn.sigmoid(jnp.concatenate(r_parts, axis=1) + ba_ref[...])
            ig = jax.nn.sigmoid(jnp.concatenate(i_parts, axis=1) + bx_ref[...])
            log_a = -LRU_C * r * softplus_neg_lam
            a = jnp.exp(log_a)
            z = jnp.tanh(-log_a) * (1.0 + a * a)
            b = jnp.where(z > 0.0, z * lax.rsqrt(z), 0.0) * (ig * xc)

            for j in range(GC):
                rows = slice(SUBLANES * j, SUBLANES * (j + 1))
                if hs:
                    hs.append(a[rows, :] * hs[-1] + b[rows, :])
                    decay.append(a[rows, :] * decay[-1])
                else:
                    hs.append(b[rows, :])
                    decay.append(a[rows, :])
            yield None

        p_inc, e_inc = decay[-1], hs[-1]
        for k in (1, 2, 4):
            m = sub >= k
            p_sh = jnp.where(m, pltpu.roll(p_inc, k, 0), 1.0)
            e_sh = jnp.where(m, pltpu.roll(e_inc, k, 0), 0.0)
            e_inc = e_inc + p_inc * e_sh
            p_inc = p_inc * p_sh
        h0 = carry_scr[...]
        after = p_inc * h0 + e_inc
        carry_scr[...] = after[SUBLANES - 1:SUBLANES, :]
        enter = jnp.where(sub == 0, h0, pltpu.roll(after, 1, 0))
        h = jnp.concatenate([hs[j] + decay[j] * enter for j in range(G)], axis=0)
        hy = (h * jnp.concatenate(ys, axis=0)).astype(BF16)
        hy = jnp.dot(unperm_ref[...], hy, preferred_element_type=F32).astype(BF16)
        yield jnp.dot(hy, wout_ref[...], preferred_element_type=F32) + xres_ref[...]

    @pl.when(i == 0)
    def _():
        cast_slices()
        for k, pc in enumerate(project_chunks()):
            proj_scr[:, PW * k:PW * (k + 1)] = pc

    @pl.when((i > 0) & (i < n_tiles))
    def _():
        cast_slices()
        rec = recur_chunks()
        new_proj = []
        for pc in project_chunks():
            new_proj.append(pc)
            next(rec)
        o_ref[...] = next(rec)
        for k, pc in enumerate(new_proj):
            proj_scr[:, PW * k:PW * (k + 1)] = pc

    @pl.when(i == n_tiles)
    def _():
        o_ref[...] = list(recur_chunks())[-1]


def _lru_mixer(x, g, w_in, conv_w, conv_b, w_ax, b_a, b_x, lam, w_out, cast_gate, cast_up, cast_down):
    S, D = x.shape
    T = SEQ_TILE
    n_tiles = S // T
    row = lambda v: v.reshape(1, D)
    perm, unperm = _time_permutation(T)

    def cast_spec(w):
        return pl.BlockSpec((w.shape[0] // n_tiles, w.shape[1]), lambda i: (jnp.minimum(i, n_tiles - 1), 0))

    casts = (cast_gate, cast_up, cast_down)
    return pl.pallas_call(
        _lru_kernel,
        grid=(n_tiles + 1,),
        in_specs=[
            pl.BlockSpec((T, D), lambda i: (jnp.minimum(i, n_tiles - 1), 0)),
            pl.BlockSpec((T, D), lambda i: (jnp.maximum(i - 1, 0), 0)),
            _const_spec((1, D)),
            _const_spec((T, T)),
            _const_spec((T, T)),
            _const_spec((D, 2 * D)),
            _const_spec((CONV_WIDTH, D)),
            _const_spec((1, D)),
            _const_spec((N_RNN_BLOCKS, RNN_BLOCK, 2 * RNN_BLOCK)),
            _const_spec((1, D)),
            _const_spec((1, D)),
            _const_spec((1, D)),
            _const_spec((D, D)),
        ] + [cast_spec(w) for w in casts],
        out_specs=[pl.BlockSpec((T, D), lambda i: (jnp.maximum(i - 1, 0), 0))] + [cast_spec(w) for w in casts],
        out_shape=[jax.ShapeDtypeStruct((S, D), F32)] + [jax.ShapeDtypeStruct(w.shape, BF16) for w in casts],
        scratch_shapes=[pltpu.VMEM((T, 2 * D), F32), pltpu.VMEM(((CONV_WIDTH - 1) * SUBLANES, D), F32),
                        pltpu.VMEM((1, D), F32)],
        compiler_params=pltpu.CompilerParams(dimension_semantics=("arbitrary",), vmem_limit_bytes=VMEM_LIMIT),
        name="lru_mixer",
    )(x, x, row(g), perm, unperm, w_in, conv_w, row(conv_b), w_ax, row(b_a), row(b_x), row(lam), w_out, *casts)


def _ffn_kernel(x_ref, g_ref, wg_ref, wu_ref, wd_ref, cast_g_ref, cast_u_ref, cast_d_ref,
                o_ref, bf_g_ref, bf_u_ref, bf_d_ref, xn_scr, acc_scr):
    c = pl.program_id(1)
    last = pl.num_programs(1) - 1

    def ff_chunk():
        bf_g_ref[...] = cast_g_ref[...].astype(BF16)
        bf_u_ref[...] = cast_u_ref[...].astype(BF16)
        bf_d_ref[...] = cast_d_ref[...].astype(BF16)
        xn = xn_scr[...]
        gate = jnp.dot(xn, wg_ref[...], preferred_element_type=F32)
        up = jnp.dot(xn, wu_ref[...], preferred_element_type=F32)
        mid = (jax.nn.silu(gate) * up).astype(BF16)
        return jnp.dot(mid, wd_ref[...], preferred_element_type=F32)

    @pl.when(c == 0)
    def _():
        xn_scr[...] = _rmsnorm(x_ref[...], g_ref[...]).astype(BF16)
        acc_scr[...] = ff_chunk()

    @pl.when((c > 0) & (c < last))
    def _():
        acc_scr[...] += ff_chunk()

    @pl.when(c == last)
    def _():
        o_ref[...] = x_ref[...] + (acc_scr[...] + ff_chunk())


def _dense_ffn(x, g, w_gate, w_up, w_down, cast_gate, cast_up, cast_down):
    S, D = x.shape
    T = FFN_TILE
    n_steps = (S // T) * N_FF_CHUNKS
    E = cast_gate.shape[0]
    row_parts = n_steps // (E * N_FF_CHUNKS)
    assert row_parts * E * N_FF_CHUNKS == n_steps
    gu_block = (1, D // row_parts, D_FF // N_FF_CHUNKS)
    d_block = (1, D_FF // (row_parts * N_FF_CHUNKS), D)

    def gu_map(i, c):
        s = i * N_FF_CHUNKS + c
        return (s // (row_parts * N_FF_CHUNKS), (s // N_FF_CHUNKS) % row_parts, c)

    def d_map(i, c):
        s = i * N_FF_CHUNKS + c
        return (s // (row_parts * N_FF_CHUNKS), s % (row_parts * N_FF_CHUNKS), 0)

    return pl.pallas_call(
        _ffn_kernel,
        grid=(S // T, N_FF_CHUNKS),
        in_specs=[
            pl.BlockSpec((T, D), lambda i, c: (i, 0)),
            pl.BlockSpec((1, D), lambda i, c: (0, 0)),
            pl.BlockSpec((D, FF_CHUNK), lambda i, c: (0, c)),
            pl.BlockSpec((D, FF_CHUNK), lambda i, c: (0, c)),
            pl.BlockSpec((FF_CHUNK, D), lambda i, c: (c, 0)),
            pl.BlockSpec(gu_block, gu_map),
            pl.BlockSpec(gu_block, gu_map),
            pl.BlockSpec(d_block, d_map),
        ],
        out_specs=[pl.BlockSpec((T, D), lambda i, c: (i, 0)), pl.BlockSpec(gu_block, gu_map),
                   pl.BlockSpec(gu_block, gu_map), pl.BlockSpec(d_block, d_map)],
        out_shape=[jax.ShapeDtypeStruct((S, D), F32), jax.ShapeDtypeStruct(cast_gate.shape, BF16),
                   jax.ShapeDtypeStruct(cast_up.shape, BF16), jax.ShapeDtypeStruct(cast_down.shape, BF16)],
        scratch_shapes=[pltpu.VMEM((T, D), BF16), pltpu.VMEM((T, D), F32)],
        compiler_params=pltpu.CompilerParams(dimension_semantics=("arbitrary", "arbitrary"),
                                             vmem_limit_bytes=VMEM_LIMIT),
        name="dense_ffn",
    )(x, g.reshape(1, D), w_gate, w_up, w_down, cast_gate, cast_up, cast_down)


KV_EXT = N_KV_HEADS * LANES
QKV_EXT = N_HEADS * HEAD_DIM + 2 * KV_EXT


def _attn_kernel(sink_ref, x_ref, pos_ref, g_ref, freq_ref, sign_ref, wqkv_ref, bqkv_ref, wo_ref, bo_ref,
                 g2_ref, wr_ref, o_ref, route_ref,
                 qlo_scr, qhi_scr, k_scr, v_scr, o_scr):
    T = x_ref.shape[0]
    D = D_MODEL
    B = ATTN_BLOCK
    first_tile = pl.program_id(0) == 0

    @pl.when(first_tile)
    def _():
        k_scr[0:B, :] = jnp.zeros((B, KV_EXT), BF16)
        v_scr[0:B, :] = jnp.zeros((B, KV_EXT), BF16)

    x = x_ref[...]
    hn = _rmsnorm(x, g_ref[...]).astype(BF16)
    qkv = jnp.dot(hn, wqkv_ref[...], preferred_element_type=F32) + bqkv_ref[...]

    ang = pos_ref[...] * freq_ref[...]
    cos_t = jnp.cos(ang)
    sin_t = jnp.sin(ang) * sign_ref[...]
    lane = lax.broadcasted_iota(jnp.int32, (T, LANES), 1)
    first_half = (lane % HEAD_DIM) < (ROT_DIM // 2)
    lo_half = lane < HEAD_DIM

    def rope(col):
        partner = jnp.where(first_half, pltpu.roll(col, LANES - ROT_DIM // 2, 1), pltpu.roll(col, ROT_DIM // 2, 1))
        return col * cos_t + partner * sin_t

    scale = HEAD_DIM ** -0.5 * LOG2_E
    for c in range(D // LANES):
        qc = rope(qkv[:, LANES * c:LANES * (c + 1)]) * scale
        qlo_scr[:, LANES * c:LANES * (c + 1)] = jnp.where(lo_half, qc, 0.0).astype(BF16)
        qhi_scr[:, LANES * c:LANES * (c + 1)] = jnp.where(lo_half, 0.0, qc).astype(BF16)
    for g in range(N_KV_HEADS):
        kc = rope(qkv[:, D + LANES * g:D + LANES * (g + 1)])
        k_scr[B:B + T, LANES * g:LANES * (g + 1)] = kc.astype(BF16)
        vc = qkv[:, D + KV_EXT + LANES * g:D + KV_EXT + LANES * (g + 1)]
        v_scr[B:B + T, LANES * g:LANES * (g + 1)] = vc.astype(BF16)

    qi = lax.broadcasted_iota(jnp.int32, (B, B), 0)
    kj = lax.broadcasted_iota(jnp.int32, (B, B), 1)
    causal_own = kj <= qi
    window_prev = kj > qi
    key0 = kj == 0
    key0_row = lax.broadcasted_iota(jnp.int32, (1, B), 1) == 0
    lo_blk = lax.broadcasted_iota(jnp.int32, (B, LANES), 1) < HEAD_DIM

    def block_body(blk, _):
        r0 = pl.multiple_of(blk * B, B)
        k_lo = jnp.where(first_tile & (blk == 0), B, 0)
        allowed_prev = window_prev & (kj >= k_lo)
        for g in range(N_KV_HEADS):
            kk = k_scr[pl.ds(r0, 2 * B), LANES * g:LANES * (g + 1)]
            vv = v_scr[pl.ds(r0, 2 * B), LANES * g:LANES * (g + 1)]
            cols = [slice(LANES * c, LANES * (c + 1)) for c in range((GROUP // 2) * g, (GROUP // 2) * (g + 1))]
            q_all = jnp.concatenate([q_scr[pl.ds(r0, B), col] for col in cols for q_scr in (qlo_scr, qhi_scr)], axis=0)
            s_all = lax.dot_general(q_all, kk, (((1,), (1,)), ((), ())), preferred_element_type=F32)
            p_all, denoms = [], []
            for hh in range(GROUP):
                rows = slice(B * hh, B * (hh + 1))
                sink_fill = jnp.where(key0_row, sink_ref[GROUP * g + hh] * LOG2_E, NEG_INF)
                s_prev = jnp.where(allowed_prev, s_all[rows, :B], sink_fill)
                s_own = jnp.where(causal_own, s_all[rows, B:], NEG_INF)
                m = jnp.max(jnp.maximum(s_prev, s_own), axis=-1, keepdims=True)
                p_prev = jnp.exp2(s_prev - m)
                p_own = jnp.exp2(s_own - m)
                denoms.append(jnp.sum(p_prev + p_own, axis=-1, keepdims=True))
                p_all.append(jnp.concatenate([jnp.where(key0, 0.0, p_prev), p_own], axis=1).astype(BF16))
            o_all = jnp.dot(jnp.concatenate(p_all, axis=0), vv, preferred_element_type=F32)
            o_heads = [o_all[B * hh:B * (hh + 1), :] / denoms[hh] for hh in range(GROUP)]
            for p, col in enumerate(cols):
                o_scr[pl.ds(r0, B), col] = jnp.where(lo_blk, o_heads[2 * p], o_heads[2 * p + 1]).astype(BF16)
        return 0

    lax.fori_loop(0, T // B, block_body, 0)

    k_scr[0:B, :] = k_scr[T:T + B, :]
    v_scr[0:B, :] = v_scr[T:T + B, :]

    h1 = jnp.dot(o_scr[...], wo_ref[...], preferred_element_type=F32) + bo_ref[...] + x
    _store_token_major(o_ref, h1)

    hn2 = _rmsnorm(h1, g2_ref[...]).astype(BF16)
    logits = jnp.dot(hn2, wr_ref[...], preferred_element_type=F32)
    lane_f = lane.astype(F32)
    lg = jnp.where(lane < N_EXPERTS, logits, -jnp.inf)
    v1 = jnp.max(lg, axis=-1, keepdims=True)
    i1 = jnp.min(jnp.where(lg == v1, lane_f, float(LANES)), axis=-1, keepdims=True)
    lg2 = jnp.where(lane_f == i1, -jnp.inf, lg)
    v2 = jnp.max(lg2, axis=-1, keepdims=True)
    i2 = jnp.min(jnp.where(lg2 == v2, lane_f, float(LANES)), axis=-1, keepdims=True)
    e2 = jnp.exp(v2 - v1)
    gate1 = 1.0 / (1.0 + e2)
    gate2 = e2 / (1.0 + e2)
    route_ref[...] = jnp.where(lane == 0, i1, jnp.where(lane == 1, i2,
                                                        jnp.where(lane == 2, gate1, jnp.where(lane == 3, gate2, 0.0))))


def _attn_mixer(x, pos_f, g, freq_lane, sign_lane, w_qkv_ext, b_qkv_ext, sinks, w_o, b_o, g2, w_router_pad):
    S, D = x.shape
    T = SEQ_TILE
    B = ATTN_BLOCK
    grid_spec = pltpu.PrefetchScalarGridSpec(
        num_scalar_prefetch=1,
        grid=(S // T,),
        in_specs=[
            pl.BlockSpec((T, D), lambda i, s: (i, 0)),
            pl.BlockSpec((T, 1), lambda i, s: (i, 0)),
            _const_spec((1, D)),
            _const_spec((1, LANES)),
            _const_spec((1, LANES)),
            _const_spec((D, QKV_EXT)),
            _const_spec((1, QKV_EXT)),
            _const_spec((D, D)),
            _const_spec((1, D)),
            _const_spec((1, D)),
            _const_spec((D, LANES)),
        ],
        out_specs=[pl.BlockSpec((T * ROWS_PER_TOKEN, LANES), lambda i, s: (i, 0)),
                   pl.BlockSpec((T, LANES), lambda i, s: (i, 0))],
        scratch_shapes=[
            pltpu.VMEM((T, D), BF16), pltpu.VMEM((T, D), BF16),
            pltpu.VMEM((T + B, KV_EXT), BF16), pltpu.VMEM((T + B, KV_EXT), BF16),
            pltpu.VMEM((T, D), BF16),
        ],
    )
    return pl.pallas_call(
        _attn_kernel,
        grid_spec=grid_spec,
        out_shape=[jax.ShapeDtypeStruct((S * ROWS_PER_TOKEN, LANES), F32), jax.ShapeDtypeStruct((S, LANES), F32)],
        compiler_params=pltpu.CompilerParams(dimension_semantics=("arbitrary",), vmem_limit_bytes=VMEM_LIMIT),
        name="attn_mixer",
    )(sinks, x, pos_f, g.reshape(1, D), freq_lane, sign_lane, w_qkv_ext, b_qkv_ext, w_o, b_o.reshape(1, D),
      g2.reshape(1, D), w_router_pad)


def _moe_kernel(texp_ref, nvt_ref, src0_ref, srcn_ref, dstp_ref, dstl_ref, h_hbm, g_ref, wg_ref, wu_ref, wd_ref,
                y_hbm, x_scr, xn_scr, acc_scr, y_scr, sems):
    i = pl.program_id(0)
    c = pl.program_id(1)
    n_tiles = pl.num_programs(0)
    n_chunks = pl.num_programs(1)
    T = xn_scr.shape[0]
    R = ROWS_PER_TOKEN
    valid = i < nvt_ref[0]
    slot = i % 2
    other = 1 - slot
    SCATTER_SEM = 2

    def row_copy(src, src_row, dst, dst_row, sem):
        return pltpu.make_async_copy(src.at[pl.ds(pl.multiple_of(src_row, R), R), :],
                                     dst.at[pl.ds(pl.multiple_of(dst_row, R), R), :], sem)

    def wait_gather(s):
        pltpu.make_async_copy(h_hbm.at[pl.ds(0, T * R), :], x_scr.at[s], sems.at[s]).wait()

    def wait_scatter():
        pltpu.make_async_copy(y_scr, y_hbm.at[pl.ds(0, T * R), :], sems.at[SCATTER_SEM]).wait()

    @pl.when((i == 0) & (c == 0))
    def _():
        y_scr[...] = jnp.zeros_like(y_scr)
        acc_scr[...] = jnp.zeros_like(acc_scr)

        def issue(r, carry):
            row_copy(h_hbm, src0_ref[0, 0, r], x_scr.at[0], r * R, sems.at[0]).start()
            return carry
        lax.fori_loop(0, T, issue, 0)

    @pl.when(c == 0)
    def _():
        wait_gather(slot)

    @pl.when(valid & (c == 0))
    def _():
        xn_scr[...] = _rmsnorm(_load_token_major(x_scr.at[slot], T), g_ref[...]).astype(BF16)

    def row_dmas():
        for r in range(T):
            row_copy(h_hbm, srcn_ref[0, 0, r], x_scr.at[other], r * R, sems.at[other]).start()
            row_copy(y_scr, r * R, y_hbm, dstp_ref[0, 0, r], sems.at[SCATTER_SEM]).start()

    def expert_chunk(first, rows):
        xn = xn_scr[0:rows, :]
        gate = jnp.dot(xn, wg_ref[0], preferred_element_type=F32)
        up = jnp.dot(xn, wu_ref[0], preferred_element_type=F32)
        mid = (jax.nn.silu(gate) * up).astype(BF16)
        down = jnp.dot(mid, wd_ref[0], preferred_element_type=F32)
        if first:
            acc_scr[0:rows, :] = down
        else:
            acc_scr[0:rows, :] += down

    few_rows = nvt_ref[1 + i] <= T // 2

    for few, rows in ((False, T), (True, T // 2)):
        branch = valid & (few_rows if few else jnp.logical_not(few_rows))

        @pl.when(branch & (c == 0))
        def _():
            row_dmas()
            expert_chunk(True, rows)

        @pl.when(branch & (c != 0))
        def _():
            expert_chunk(False, rows)

    @pl.when(jnp.logical_not(valid) & (c == 0))
    def _():
        row_dmas()

    @pl.when(c == n_chunks - 1)
    def _():
        wait_scatter()

    @pl.when(valid & (c == n_chunks - 1))
    def _():
        _store_token_major(y_scr, acc_scr[...])

    @pl.when((i == n_tiles - 1) & (c == n_chunks - 1))
    def _():
        wait_gather(other)


        def issue(r, carry):
            row_copy(y_scr, r * R, y_hbm, dstl_ref[0, 0, r], sems.at[SCATTER_SEM]).start()
            return carry
        lax.fori_loop(0, T, issue, 0)
        wait_scatter()


def _moe_experts(h1_tm, g, tile_expert, n_valid_tiles, src_rows, dst_rows, w_gate, w_up, w_down, n_out_tokens):
    D = D_MODEL
    T = MOE_TILE
    NT = src_rows.shape[0]
    NC = N_MOE_CHUNKS
    FC = D_FF // NC

    def chunk(i, c, nvt):
        return jnp.where(i < nvt[0], c, NC - 1)

    def idx_spec(index_map):
        return pl.BlockSpec((1, 1, T), index_map, memory_space=pltpu.SMEM)

    grid_spec = pltpu.PrefetchScalarGridSpec(
        num_scalar_prefetch=2,
        grid=(NT, NC),
        in_specs=[
            idx_spec(lambda i, c, te, nvt: (0, 0, 0)),
            idx_spec(lambda i, c, te, nvt: (jnp.minimum(i + 1, NT - 1), 0, 0)),
            idx_spec(lambda i, c, te, nvt: (i, 0, 0)),
            idx_spec(lambda i, c, te, nvt: (NT, 0, 0)),
            pl.BlockSpec(memory_space=pl.ANY),
            pl.BlockSpec((1, D), lambda i, c, te, nvt: (0, 0)),
            pl.BlockSpec((1, D, FC), lambda i, c, te, nvt: (te[i], 0, chunk(i, c, nvt))),
            pl.BlockSpec((1, D, FC), lambda i, c, te, nvt: (te[i], 0, chunk(i, c, nvt))),
            pl.BlockSpec((1, FC, D), lambda i, c, te, nvt: (te[i], chunk(i, c, nvt), 0)),
        ],
        out_specs=pl.BlockSpec(memory_space=pl.ANY),
        scratch_shapes=[pltpu.VMEM((2, T * ROWS_PER_TOKEN, LANES), F32), pltpu.VMEM((T, D), BF16),
                        pltpu.VMEM((T, D), F32), pltpu.VMEM((T * ROWS_PER_TOKEN, LANES), F32),
                        pltpu.SemaphoreType.DMA((3,))],
    )
    return pl.pallas_call(
        _moe_kernel,
        grid_spec=grid_spec,
        out_shape=jax.ShapeDtypeStruct((n_out_tokens * ROWS_PER_TOKEN, LANES), F32),
        compiler_params=pltpu.CompilerParams(dimension_semantics=("arbitrary", "arbitrary"),
                                             vmem_limit_bytes=VMEM_LIMIT),
        name="moe_experts",
    )(tile_expert, n_valid_tiles, src_rows, src_rows, dst_rows, dst_rows, h1_tm, g.reshape(1, D),
      w_gate, w_up, w_down)


def _moe_plan(route, n_tokens):
    T = MOE_TILE
    n_assign = TOP_K * n_tokens
    n_tiles = n_assign // T + N_EXPERTS
    e_flat = route[:, 0:TOP_K].astype(jnp.int32).reshape(-1)
    order = jnp.argsort(e_flat, stable=True).astype(jnp.int32)
    counts = jnp.sum((e_flat[:, None] == jnp.arange(N_EXPERTS, dtype=jnp.int32)[None, :]).astype(jnp.int32), axis=0)
    tiles_per = (counts + T - 1) // T
    tile_end = jnp.cumsum(tiles_per)
    n_valid = tile_end[-1]
    tile_ids = jnp.arange(n_tiles, dtype=jnp.int32)
    last_valid = jnp.maximum(n_valid - 1, 0)
    texp = jnp.sum((jnp.minimum(tile_ids, last_valid)[:, None] >= tile_end[None, :]).astype(jnp.int32), axis=1)
    texp = jnp.minimum(texp, N_EXPERTS - 1)
    onehot = (texp[:, None] == jnp.arange(N_EXPERTS, dtype=jnp.int32)[None, :]).astype(jnp.int32)
    pick = lambda v: jnp.sum(onehot * v[None, :], axis=1)
    j = tile_ids - pick(tile_end - tiles_per)
    start = pick(jnp.cumsum(counts) - counts) + j * T
    n_rows = jnp.where(tile_ids < n_valid, jnp.clip(pick(counts) - j * T, 0, T), 0)
    r = jnp.arange(T, dtype=jnp.int32)[None, :]
    assign = order[jnp.clip(start[:, None] + r, 0, n_assign - 1)]
    token = assign // TOP_K
    slot = assign % TOP_K
    row_valid = r < n_rows[:, None]
    pad_rows = jnp.broadcast_to(TOP_K * n_tokens + r, (1, T))
    dst = jnp.where(row_valid, slot * n_tokens + token, pad_rows)
    dst = jnp.concatenate([pad_rows, dst], axis=0)
    src_rows = (token * ROWS_PER_TOKEN).reshape(n_tiles, 1, T)
    dst_rows = (dst * ROWS_PER_TOKEN).reshape(n_tiles + 1, 1, T)
    tile_counts = jnp.concatenate([n_valid.reshape(1), n_rows]).astype(jnp.int32)
    return texp, tile_counts, src_rows, dst_rows


def _combine_kernel(h_ref, y0_ref, y1_ref, route_ref, g_ref, o_ref):
    T = o_ref.shape[0]
    route = route_ref[...]
    moe = route[:, 2:3] * _load_token_major(y0_ref, T) + route[:, 3:4] * _load_token_major(y1_ref, T)
    o_ref[...] = _rmsnorm(_load_token_major(h_ref, T) + moe, g_ref[...])


def _combine(h1_tm, y2_tm, route, g):
    D = D_MODEL
    S = route.shape[0]
    T = OUT_TILE
    nb = S // T
    tm_block = (T * ROWS_PER_TOKEN, LANES)
    return pl.pallas_call(
        _combine_kernel,
        grid=(nb,),
        in_specs=[
            pl.BlockSpec(tm_block, lambda i: (i, 0)),
            pl.BlockSpec(tm_block, lambda i: (i, 0)),
            pl.BlockSpec(tm_block, lambda i: (i + nb, 0)),
            pl.BlockSpec((T, LANES), lambda i: (i, 0)),
            pl.BlockSpec((1, D), lambda i: (0, 0)),
        ],
        out_specs=pl.BlockSpec((T, D), lambda i: (i, 0)),
        out_shape=jax.ShapeDtypeStruct((S, D), F32),
        compiler_params=pltpu.CompilerParams(dimension_semantics=("arbitrary",), vmem_limit_bytes=VMEM_LIMIT),
        name="moe_combine",
    )(h1_tm, y2_tm, y2_tm, route, g.reshape(1, D))


def kernel(x, positions, norm_mix, norm_ffn, norm_final, lru_w_in, lru_conv_w, lru_conv_b, lru_w_a, lru_b_a, lru_w_x, lru_b_x, lru_lambda, lru_w_out, attn_w_qkv, attn_b_qkv, attn_sinks, attn_w_o, attn_b_o, ffn_w_gate, ffn_w_up, ffn_w_down, moe_w_router, moe_w_gate, moe_w_up, moe_w_down):
    B, S, D = x.shape
    assert B == 1 and D == D_MODEL and S % FFN_TILE == 0
    h = x.reshape(S, D)

    w_ax = jnp.concatenate([lru_w_a[0], lru_w_x[0]], axis=-1).astype(BF16)
    h, ffn_wg, ffn_wu, ffn_wd = _lru_mixer(h, norm_mix[0], lru_w_in[0].astype(BF16), lru_conv_w[0], lru_conv_b[0],
                                           w_ax, lru_b_a[0], lru_b_x[0], lru_lambda[0], lru_w_out[0].astype(BF16),
                                           ffn_w_gate[0], ffn_w_up[0], ffn_w_down[0])
    h, moe_wg, moe_wu, moe_wd = _dense_ffn(h, norm_ffn[0], ffn_wg, ffn_wu, ffn_wd,
                                           moe_w_gate[0], moe_w_up[0], moe_w_down[0])

    q_dim = N_HEADS * HEAD_DIM
    kv_dim = N_KV_HEADS * HEAD_DIM

    def dup_heads(w):
        w4 = w.reshape(w.shape[:-1] + (N_KV_HEADS, 1, HEAD_DIM))
        return jnp.broadcast_to(w4, w.shape[:-1] + (N_KV_HEADS, 2, HEAD_DIM)).reshape(w.shape[:-1] + (KV_EXT,))

    wqkv, bqkv = attn_w_qkv[0], attn_b_qkv[0]
    w_qkv_ext = jnp.concatenate([wqkv[:, :q_dim], dup_heads(wqkv[:, q_dim:q_dim + kv_dim]),
                                 dup_heads(wqkv[:, q_dim + kv_dim:])], axis=1).astype(BF16)
    b_qkv_ext = jnp.concatenate([bqkv[:q_dim], dup_heads(bqkv[q_dim:q_dim + kv_dim]),
                                 dup_heads(bqkv[q_dim + kv_dim:])]).reshape(1, QKV_EXT)
    inv_freq = ROPE_THETA ** (-jnp.arange(0, ROT_DIM, 2, dtype=F32) / ROT_DIM)
    d_in_head = jnp.arange(LANES) % HEAD_DIM
    freq_lane = jnp.where(d_in_head < ROT_DIM, inv_freq[d_in_head % (ROT_DIM // 2)], 0.0).reshape(1, LANES)
    sign_lane = jnp.where(d_in_head < ROT_DIM // 2, -1.0, jnp.where(d_in_head < ROT_DIM, 1.0, 0.0))
    sign_lane = sign_lane.astype(F32).reshape(1, LANES)
    w_router_pad = jnp.pad(moe_w_router[0], ((0, 0), (0, LANES - N_EXPERTS))).astype(BF16)
    pos_f = positions.reshape(S, 1).astype(F32)
    h1_tm, route = _attn_mixer(h, pos_f, norm_mix[1], freq_lane, sign_lane, w_qkv_ext, b_qkv_ext,
                               attn_sinks[0], attn_w_o[0].astype(BF16), attn_b_o[0], norm_ffn[1], w_router_pad)

    texp, n_valid, src_rows, dst_rows = _moe_plan(route, S)
    y2_tm = _moe_experts(h1_tm, norm_ffn[1], texp, n_valid, src_rows, dst_rows,
                         moe_wg, moe_wu, moe_wd,
                         TOP_K * S + MOE_TILE)
    out = _combine(h1_tm, y2_tm, route, norm_final)
    return out.reshape(B, S, D)
```

```python
import functools

import jax
import jax.numpy as jnp
from jax import lax
from jax.experimental import pallas as pl
from jax.experimental.pallas import tpu as pltpu

F32 = jnp.float32
BF16 = jnp.bfloat16

D_MODEL = 1024
N_RNN_BLOCKS = 8
RNN_BLOCK = D_MODEL // N_RNN_BLOCKS
CONV_WIDTH = 4
LRU_C = 8.0
N_HEADS = 16
N_KV_HEADS = 4
HEAD_DIM = 64
GROUP = N_HEADS // N_KV_HEADS
WINDOW = 128
ATTN_BLOCK = 128
ROPE_THETA = 500000.0
ROT_DIM = HEAD_DIM // 4
D_FF = 3584
N_EXPERTS = 8
TOP_K = 2
EPS = 1e-6
NEG_INF = -1e30
LOG2_E = 1.4426950408889634

LANES = 128
SUBLANES = 8
VMEM_LIMIT = 56 * 1024 * 1024

SEQ_TILE = 512
LRU_CHUNKS = 4
FFN_TILE = 1024
FF_CHUNK = 512
N_FF_CHUNKS = D_FF // FF_CHUNK
MOE_TILE = 512
N_MOE_CHUNKS = 2
OUT_TILE = 1024


def _rmsnorm(x, g):
    return x * lax.rsqrt(jnp.mean(x * x, axis=-1, keepdims=True) + EPS) * g


ROWS_PER_TOKEN = D_MODEL // LANES


def _store_token_major(ref, val):
    T = val.shape[0]
    for c in range(ROWS_PER_TOKEN):
        ref[pl.ds(c, T, stride=ROWS_PER_TOKEN), :] = val[:, LANES * c:LANES * (c + 1)]


def _load_token_major(ref, T):
    return jnp.concatenate([ref[pl.ds(c, T, stride=ROWS_PER_TOKEN), :] for c in range(ROWS_PER_TOKEN)], axis=1)


def _const_spec(shape):
    n = len(shape)
    return pl.BlockSpec(shape, lambda *_: (0,) * n, pipeline_mode=pl.Buffered(1))


def _time_permutation(T):
    rho = jnp.arange(T)
    t_of_rho = (rho % SUBLANES) * (T // SUBLANES) + rho // SUBLANES
    perm = (t_of_rho[:, None] == jnp.arange(T)[None, :]).astype(BF16)
    return perm, perm.T


def _lru_kernel(xin_ref, xres_ref, g_ref, perm_ref, unperm_ref, win_ref, cw_ref, cb_ref, wax_ref, ba_ref, bx_ref,
                lam_ref, wout_ref, cast_g_ref, cast_u_ref, cast_d_ref,
                o_ref, bf_g_ref, bf_u_ref, bf_d_ref, proj_scr, tail_scr, carry_scr):
    i = pl.program_id(0)
    n_tiles = pl.num_programs(0) - 1
    T = xin_ref.shape[0]
    D = D_MODEL
    C = T // LRU_CHUNKS
    GC = C // SUBLANES
    G = T // SUBLANES
    HALO = (CONV_WIDTH - 1) * SUBLANES

    def cast_slices():
        bf_g_ref[...] = cast_g_ref[...].astype(BF16)
        bf_u_ref[...] = cast_u_ref[...].astype(BF16)
        bf_d_ref[...] = cast_d_ref[...].astype(BF16)

    @pl.when(i == 0)
    def _():
        tail_scr[...] = jnp.zeros_like(tail_scr)
        carry_scr[...] = jnp.zeros_like(carry_scr)

    PW = 2 * D // LRU_CHUNKS

    def project_chunks():
        hn = _rmsnorm(xin_ref[...], g_ref[...]).astype(BF16)
        hp = jnp.dot(perm_ref[...], hn, preferred_element_type=F32).astype(BF16)
        for k in range(LRU_CHUNKS):
            yield jnp.dot(hp, win_ref[:, PW * k:PW * (k + 1)], preferred_element_type=F32)

    def recur_chunks():
        sub = lax.broadcasted_iota(jnp.int32, (SUBLANES, D), 0)
        cw = cw_ref[...]
        softplus_neg_lam = jax.nn.softplus(-lam_ref[...])
        last = proj_scr[T - HALO:T, D:]
        prev_groups = []
        for k in range(CONV_WIDTH - 1):
            rows = slice(SUBLANES * k, SUBLANES * (k + 1))
            prev_groups.append(pltpu.roll(jnp.where(sub == SUBLANES - 1, tail_scr[rows, :], last[rows, :]), 1, 0))
        tail_scr[...] = last

        ys, hs, decay = [], [], []
        for k in range(LRU_CHUNKS):
            pc = proj_scr[C * k:C * (k + 1), :]
            ys.append(jax.nn.gelu(pc[:, :D], approximate=True))
            xb = pc[:, D:]
            xc = cb_ref[...] + xb * cw[CONV_WIDTH - 1:CONV_WIDTH, :]
            for back in range(1, CONV_WIDTH):
                shifted = jnp.concatenate(prev_groups[CONV_WIDTH - 1 - back:] + [xb[:C - SUBLANES * back, :]], axis=0)
                xc = xc + shifted * cw[CONV_WIDTH - 1 - back:CONV_WIDTH - back, :]
            prev_groups = [xb[C - HALO + SUBLANES * q:C - HALO + SUBLANES * (q + 1), :] for q in range(CONV_WIDTH - 1)]

            xcb = xc.astype(BF16)
            r_parts, i_parts = [], []
            for n in range(N_RNN_BLOCKS):
                gn = jnp.dot(xcb[:, RNN_BLOCK * n:RNN_BLOCK * (n + 1)], wax_ref[n], preferred_element_type=F32)
                r_parts.append(gn[:, :RNN_BLOCK])
                i_parts.append(gn[:, RNN_BLOCK:])
            r = jax.nn.sigmoid(jnp.concatenate(r_parts, axis=1) + ba_ref[...])
            ig = jax.nn.sigmoid(jnp.concatenate(i_parts, axis=1) + bx_ref[...])
            log_a = -LRU_C * r * softplus_neg_lam
            a = jnp.exp(log_a)
            z = jnp.tanh(-log_a) * (1.0 + a * a)
            b = jnp.where(z > 0.0, z * lax.rsqrt(z), 0.0) * (ig * xc)

            for j in range(GC):
                rows = slice(SUBLANES * j, SUBLANES * (j + 1))
                if hs:
                    hs.append(a[rows, :] * hs[-1] + b[rows, :])
                    decay.append(a[rows, :] * decay[-1])
                else:
                    hs.append(b[rows, :])
                    decay.append(a[rows, :])
            yield None

        p_inc, e_inc = decay[-1], hs[-1]
        for k in (1, 2, 4):
            m = sub >= k
            p_sh = jnp.where(m, pltpu.roll(p_inc, k, 0), 1.0)
            e_sh = jnp.where(m, pltpu.roll(e_inc, k, 0), 0.0)
            e_inc = e_inc + p_inc * e_sh
            p_inc = p_inc * p_sh
        h0 = carry_scr[...]
        after = p_inc * h0 + e_inc
        carry_scr[...] = after[SUBLANES - 1:SUBLANES, :]
        enter = jnp.where(sub == 0, h0, pltpu.roll(after, 1, 0))
        h = jnp.concatenate([hs[j] + decay[j] * enter for j in range(G)], axis=0)
        hy = (h * jnp.concatenate(ys, axis=0)).astype(BF16)
        hy = jnp.dot(unperm_ref[...], hy, preferred_element_type=F32).astype(BF16)
        yield jnp.dot(hy, wout_ref[...], preferred_element_type=F32) + xres_ref[...]

    @pl.when(i == 0)
    def _():
        cast_slices()
        for k, pc in enumerate(project_chunks()):
            proj_scr[:, PW * k:PW * (k + 1)] = pc

    @pl.when((i > 0) & (i < n_tiles))
    def _():
        cast_slices()
        rec = recur_chunks()
        new_proj = []
        for pc in project_chunks():
            new_proj.append(pc)
            next(rec)
        o_ref[...] = next(rec)
        for k, pc in enumerate(new_proj):
            proj_scr[:, PW * k:PW * (k + 1)] = pc

    @pl.when(i == n_tiles)
    def _():
        o_ref[...] = list(recur_chunks())[-1]


def _lru_mixer(x, g, w_in, conv_w, conv_b, w_ax, b_a, b_x, lam, w_out, cast_gate, cast_up, cast_down):
    S, D = x.shape
    T = SEQ_TILE
    n_tiles = S // T
    row = lambda v: v.reshape(1, D)
    perm, unperm = _time_permutation(T)

    def cast_spec(w):
        return pl.BlockSpec((w.shape[0] // n_tiles, w.shape[1]), lambda i: (jnp.minimum(i, n_tiles - 1), 0))

    casts = (cast_gate, cast_up, cast_down)
    return pl.pallas_call(
        _lru_kernel,
        grid=(n_tiles + 1,),
        in_specs=[
            pl.BlockSpec((T, D), lambda i: (jnp.minimum(i, n_tiles - 1), 0)),
            pl.BlockSpec((T, D), lambda i: (jnp.maximum(i - 1, 0), 0)),
            _const_spec((1, D)),
            _const_spec((T, T)),
            _const_spec((T, T)),
            _const_spec((D, 2 * D)),
            _const_spec((CONV_WIDTH, D)),
            _const_spec((1, D)),
            _const_spec((N_RNN_BLOCKS, RNN_BLOCK, 2 * RNN_BLOCK)),
            _const_spec((1, D)),
            _const_spec((1, D)),
            _const_spec((1, D)),
            _const_spec((D, D)),
        ] + [cast_spec(w) for w in casts],
        out_specs=[pl.BlockSpec((T, D), lambda i: (jnp.maximum(i - 1, 0), 0))] + [cast_spec(w) for w in casts],
        out_shape=[jax.ShapeDtypeStruct((S, D), F32)] + [jax.ShapeDtypeStruct(w.shape, BF16) for w in casts],
        scratch_shapes=[pltpu.VMEM((T, 2 * D), F32), pltpu.VMEM(((CONV_WIDTH - 1) * SUBLANES, D), F32),
                        pltpu.VMEM((1, D), F32)],
        compiler_params=pltpu.CompilerParams(dimension_semantics=("arbitrary",), vmem_limit_bytes=VMEM_LIMIT),
        name="lru_mixer",
    )(x, x, row(g), perm, unperm, w_in, conv_w, row(conv_b), w_ax, row(b_a), row(b_x), row(lam), w_out, *casts)


def _ffn_kernel(x_ref, g_ref, wg_ref, wu_ref, wd_ref, cast_g_ref, cast_u_ref, cast_d_ref,
                o_ref, bf_g_ref, bf_u_ref, bf_d_ref, xn_scr, acc_scr):
    c = pl.program_id(1)
    last = pl.num_programs(1) - 1

    def ff_chunk():
        bf_g_ref[...] = cast_g_ref[...].astype(BF16)
        bf_u_ref[...] = cast_u_ref[...].astype(BF16)
        bf_d_ref[...] = cast_d_ref[...].astype(BF16)
        xn = xn_scr[...]
        gate = jnp.dot(xn, wg_ref[...], preferred_element_type=F32)
        up = jnp.dot(xn, wu_ref[...], preferred_element_type=F32)
        mid = (jax.nn.silu(gate) * up).astype(BF16)
        return jnp.dot(mid, wd_ref[...], preferred_element_type=F32)

    @pl.when(c == 0)
    def _():
        xn_scr[...] = _rmsnorm(x_ref[...], g_ref[...]).astype(BF16)
        acc_scr[...] = ff_chunk()

    @pl.when((c > 0) & (c < last))
    def _():
        acc_scr[...] += ff_chunk()

    @pl.when(c == last)
    def _():
        o_ref[...] = x_ref[...] + (acc_scr[...] + ff_chunk())


def _dense_ffn(x, g, w_gate, w_up, w_down, cast_gate, cast_up, cast_down):
    S, D = x.shape
    T = FFN_TILE
    n_steps = (S // T) * N_FF_CHUNKS
    E = cast_gate.shape[0]
    row_parts = n_steps // (E * N_FF_CHUNKS)
    assert row_parts * E * N_FF_CHUNKS == n_steps
    gu_block = (1, D // row_parts, D_FF // N_FF_CHUNKS)
    d_block = (1, D_FF // (row_parts * N_FF_CHUNKS), D)

    def gu_map(i, c):
        s = i * N_FF_CHUNKS + c
        return (s // (row_parts * N_FF_CHUNKS), (s // N_FF_CHUNKS) % row_parts, c)

    def d_map(i, c):
        s = i * N_FF_CHUNKS + c
        return (s // (row_parts * N_FF_CHUNKS), s % (row_parts * N_FF_CHUNKS), 0)

    return pl.pallas_call(
        _ffn_kernel,
        grid=(S // T, N_FF_CHUNKS),
        in_specs=[
            pl.BlockSpec((T, D), lambda i, c: (i, 0)),
            pl.BlockSpec((1, D), lambda i, c: (0, 0)),
            pl.BlockSpec((D, FF_CHUNK), lambda i, c: (0, c)),
            pl.BlockSpec((D, FF_CHUNK), lambda i, c: (0, c)),
            pl.BlockSpec((FF_CHUNK, D), lambda i, c: (c, 0)),
            pl.BlockSpec(gu_block, gu_map),
            pl.BlockSpec(gu_block, gu_map),
            pl.BlockSpec(d_block, d_map),
        ],
        out_specs=[pl.BlockSpec((T, D), lambda i, c: (i, 0)), pl.BlockSpec(gu_block, gu_map),
                   pl.BlockSpec(gu_block, gu_map), pl.BlockSpec(d_block, d_map)],
        out_shape=[jax.ShapeDtypeStruct((S, D), F32), jax.ShapeDtypeStruct(cast_gate.shape, BF16),
                   jax.ShapeDtypeStruct(cast_up.shape, BF16), jax.ShapeDtypeStruct(cast_down.shape, BF16)],
        scratch_shapes=[pltpu.VMEM((T, D), BF16), pltpu.VMEM((T, D), F32)],
        compiler_params=pltpu.CompilerParams(dimension_semantics=("arbitrary", "arbitrary"),
                                             vmem_limit_bytes=VMEM_LIMIT),
        name="dense_ffn",
    )(x, g.reshape(1, D), w_gate, w_up, w_down, cast_gate, cast_up, cast_down)


KV_EXT = N_KV_HEADS * LANES
QKV_EXT = N_HEADS * HEAD_DIM + 2 * KV_EXT


def _attn_kernel(sink_ref, x_ref, pos_ref, g_ref, freq_ref, sign_ref, wqkv_ref, bqkv_ref, wo_ref, bo_ref,
                 g2_ref, wr_ref, o_ref, route_ref,
                 qlo_scr, qhi_scr, k_scr, v_scr, o_scr):
    T = x_ref.shape[0]
    D = D_MODEL
    B = ATTN_BLOCK
    first_tile = pl.program_id(0) == 0

    @pl.when(first_tile)
    def _():
        k_scr[0:B, :] = jnp.zeros((B, KV_EXT), BF16)
        v_scr[0:B, :] = jnp.zeros((B, KV_EXT), BF16)

    x = x_ref[...]
    hn = _rmsnorm(x, g_ref[...]).astype(BF16)
    qkv = jnp.dot(hn, wqkv_ref[...], preferred_element_type=F32) + bqkv_ref[...]

    ang = pos_ref[...] * freq_ref[...]
    cos_t = jnp.cos(ang)
    sin_t = jnp.sin(ang) * sign_ref[...]
    lane = lax.broadcasted_iota(jnp.int32, (T, LANES), 1)
    first_half = (lane % HEAD_DIM) < (ROT_DIM // 2)
    lo_half = lane < HEAD_DIM

    def rope(col):
        partner = jnp.where(first_half, pltpu.roll(col, LANES - ROT_DIM // 2, 1), pltpu.roll(col, ROT_DIM // 2, 1))
        return col * cos_t + partner * sin_t

    scale = HEAD_DIM ** -0.5 * LOG2_E
    for c in range(D // LANES):
        qc = rope(qkv[:, LANES * c:LANES * (c + 1)]) * scale
        qlo_scr[:, LANES * c:LANES * (c + 1)] = jnp.where(lo_half, qc, 0.0).astype(BF16)
        qhi_scr[:, LANES * c:LANES * (c + 1)] = jnp.where(lo_half, 0.0, qc).astype(BF16)
    for g in range(N_KV_HEADS):
        kc = rope(qkv[:, D + LANES * g:D + LANES * (g + 1)])
        k_scr[B:B + T, LANES * g:LANES * (g + 1)] = kc.astype(BF16)
        vc = qkv[:, D + KV_EXT + LANES * g:D + KV_EXT + LANES * (g + 1)]
        v_scr[B:B + T, LANES * g:LANES * (g + 1)] = vc.astype(BF16)

    qi = lax.broadcasted_iota(jnp.int32, (B, B), 0)
    kj = lax.broadcasted_iota(jnp.int32, (B, B), 1)
    causal_own = kj <= qi
    window_prev = kj > qi
    key0 = kj == 0
    key0_row = lax.broadcasted_iota(jnp.int32, (1, B), 1) == 0
    lo_blk = lax.broadcasted_iota(jnp.int32, (B, LANES), 1) < HEAD_DIM

    def block_body(blk, _):
        r0 = pl.multiple_of(blk * B, B)
        k_lo = jnp.where(first_tile & (blk == 0), B, 0)
        allowed_prev = window_prev & (kj >= k_lo)
        for g in range(N_KV_HEADS):
            kk = k_scr[pl.ds(r0, 2 * B), LANES * g:LANES * (g + 1)]
            vv = v_scr[pl.ds(r0, 2 * B), LANES * g:LANES * (g + 1)]
            cols = [slice(LANES * c, LANES * (c + 1)) for c in range((GROUP // 2) * g, (GROUP // 2) * (g + 1))]
            q_all = jnp.concatenate([q_scr[pl.ds(r0, B), col] for col in cols for q_scr in (qlo_scr, qhi_scr)], axis=0)
            s_all = lax.dot_general(q_all, kk, (((1,), (1,)), ((), ())), preferred_element_type=F32)
            p_all, denoms = [], []
            for hh in range(GROUP):
                rows = slice(B * hh, B * (hh + 1))
                sink_fill = jnp.where(key0_row, sink_ref[GROUP * g + hh] * LOG2_E, NEG_INF)
                s_prev = jnp.where(allowed_prev, s_all[rows, :B], sink_fill)
                s_own = jnp.where(causal_own, s_all[rows, B:], NEG_INF)
                m = jnp.max(jnp.maximum(s_prev, s_own), axis=-1, keepdims=True)
                p_prev = jnp.exp2(s_prev - m)
                p_own = jnp.exp2(s_own - m)
                denoms.append(jnp.sum(p_prev + p_own, axis=-1, keepdims=True))
                p_all.append(jnp.concatenate([jnp.where(key0, 0.0, p_prev), p_own], axis=1).astype(BF16))
            o_all = jnp.dot(jnp.concatenate(p_all, axis=0), vv, preferred_element_type=F32)
            o_heads = [o_all[B * hh:B * (hh + 1), :] / denoms[hh] for hh in range(GROUP)]
            for p, col in enumerate(cols):
                o_scr[pl.ds(r0, B), col] = jnp.where(lo_blk, o_heads[2 * p], o_heads[2 * p + 1]).astype(BF16)
        return 0

    lax.fori_loop(0, T // B, block_body, 0)

    k_scr[0:B, :] = k_scr[T:T + B, :]
    v_scr[0:B, :] = v_scr[T:T + B, :]

    h1 = jnp.dot(o_scr[...], wo_ref[...], preferred_element_type=F32) + bo_ref[...] + x
    _store_token_major(o_ref, h1)

    hn2 = _rmsnorm(h1, g2_ref[...]).astype(BF16)
    logits = jnp.dot(hn2, wr_ref[...], preferred_element_type=F32)
    lane_f = lane.astype(F32)
    lg = jnp.where(lane < N_EXPERTS, logits, -jnp.inf)
    v1 = jnp.max(lg, axis=-1, keepdims=True)
    i1 = jnp.min(jnp.where(lg == v1, lane_f, float(LANES)), axis=-1, keepdims=True)
    lg2 = jnp.where(lane_f == i1, -jnp.inf, lg)
    v2 = jnp.max(lg2, axis=-1, keepdims=True)
    i2 = jnp.min(jnp.where(lg2 == v2, lane_f, float(LANES)), axis=-1, keepdims=True)
    e2 = jnp.exp(v2 - v1)
    gate1 = 1.0 / (1.0 + e2)
    gate2 = e2 / (1.0 + e2)
    route_ref[...] = jnp.where(lane == 0, i1, jnp.where(lane == 1, i2,
                                                        jnp.where(lane == 2, gate1, jnp.where(lane == 3, gate2, 0.0))))


def _attn_mixer(x, pos_f, g, freq_lane, sign_lane, w_qkv_ext, b_qkv_ext, sinks, w_o, b_o, g2, w_router_pad):
    S, D = x.shape
    T = SEQ_TILE
    B = ATTN_BLOCK
    grid_spec = pltpu.PrefetchScalarGridSpec(
        num_scalar_prefetch=1,
        grid=(S // T,),
        in_specs=[
            pl.BlockSpec((T, D), lambda i, s: (i, 0)),
            pl.BlockSpec((T, 1), lambda i, s: (i, 0)),
            _const_spec((1, D)),
            _const_spec((1, LANES)),
            _const_spec((1, LANES)),
            _const_spec((D, QKV_EXT)),
            _const_spec((1, QKV_EXT)),
            _const_spec((D, D)),
            _const_spec((1, D)),
            _const_spec((1, D)),
            _const_spec((D, LANES)),
        ],
        out_specs=[pl.BlockSpec((T * ROWS_PER_TOKEN, LANES), lambda i, s: (i, 0)),
                   pl.BlockSpec((T, LANES), lambda i, s: (i, 0))],
        scratch_shapes=[
            pltpu.VMEM((T, D), BF16), pltpu.VMEM((T, D), BF16),
            pltpu.VMEM((T + B, KV_EXT), BF16), pltpu.VMEM((T + B, KV_EXT), BF16),
            pltpu.VMEM((T, D), BF16),
        ],
    )
    return pl.pallas_call(
        _attn_kernel,
        grid_spec=grid_spec,
        out_shape=[jax.ShapeDtypeStruct((S * ROWS_PER_TOKEN, LANES), F32), jax.ShapeDtypeStruct((S, LANES), F32)],
        compiler_params=pltpu.CompilerParams(dimension_semantics=("arbitrary",), vmem_limit_bytes=VMEM_LIMIT),
        name="attn_mixer",
    )(sinks, x, pos_f, g.reshape(1, D), freq_lane, sign_lane, w_qkv_ext, b_qkv_ext, w_o, b_o.reshape(1, D),
      g2.reshape(1, D), w_router_pad)


def _moe_kernel(texp_ref, nvt_ref, src0_ref, srcn_ref, dstp_ref, dstl_ref, h_hbm, g_ref, wg_ref, wu_ref, wd_ref,
                y_hbm, x_scr, xn_scr, acc_scr, y_scr, sems):
    i = pl.program_id(0)
    c = pl.program_id(1)
    n_tiles = pl.num_programs(0)
    n_chunks = pl.num_programs(1)
    T = xn_scr.shape[0]
    R = ROWS_PER_TOKEN
    valid = i < nvt_ref[0]
    slot = i % 2
    other = 1 - slot
    SCATTER_SEM = 2

    def row_copy(src, src_row, dst, dst_row, sem):
        return pltpu.make_async_copy(src.at[pl.ds(pl.multiple_of(src_row, R), R), :],
                                     dst.at[pl.ds(pl.multiple_of(dst_row, R), R), :], sem)

    rows_prev = jnp.where(i > 0, nvt_ref[i], 0)
    pad_base = y_hbm.shape[0] - T * R

    def dest_row(window_ref, r, n_rows):
        return jnp.where(r < n_rows, window_ref[0, r], pad_base + r * R)

    def wait_gather(s):
        pltpu.make_async_copy(h_hbm.at[pl.ds(0, T * R), :], x_scr.at[s], sems.at[s]).wait()

    def wait_scatter():
        pltpu.make_async_copy(y_scr, y_hbm.at[pl.ds(0, T * R), :], sems.at[SCATTER_SEM]).wait()

    @pl.when((i == 0) & (c == 0))
    def _():
        y_scr[...] = jnp.zeros_like(y_scr)
        acc_scr[...] = jnp.zeros_like(acc_scr)

        def issue(r, carry):
            row_copy(h_hbm, src0_ref[0, r], x_scr.at[0], r * R, sems.at[0]).start()
            return carry
        lax.fori_loop(0, T, issue, 0)

    @pl.when(c == 0)
    def _():
        wait_gather(slot)

    @pl.when(valid & (c == 0))
    def _():
        xn_scr[...] = _rmsnorm(_load_token_major(x_scr.at[slot], T), g_ref[...]).astype(BF16)

    def row_dmas():
        for r in range(T):
            row_copy(h_hbm, srcn_ref[0, r], x_scr.at[other], r * R, sems.at[other]).start()
            row_copy(y_scr, r * R, y_hbm, dest_row(dstp_ref, r, rows_prev), sems.at[SCATTER_SEM]).start()

    def expert_chunk(first, rows):
        xn = xn_scr[0:rows, :]
        gate = jnp.dot(xn, wg_ref[0], preferred_element_type=F32)
        up = jnp.dot(xn, wu_ref[0], preferred_element_type=F32)
        mid = (jax.nn.silu(gate) * up).astype(BF16)
        down = jnp.dot(mid, wd_ref[0], preferred_element_type=F32)
        if first:
            acc_scr[0:rows, :] = down
        else:
            acc_scr[0:rows, :] += down

    few_rows = nvt_ref[1 + i] <= T // 2

    for few, rows in ((False, T), (True, T // 2)):
        branch = valid & (few_rows if few else jnp.logical_not(few_rows))

        @pl.when(branch & (c == 0))
        def _():
            row_dmas()
            expert_chunk(True, rows)

        @pl.when(branch & (c != 0))
        def _():
            expert_chunk(False, rows)

    @pl.when(jnp.logical_not(valid) & (c == 0))
    def _():
        row_dmas()

    @pl.when(c == n_chunks - 1)
    def _():
        wait_scatter()

    @pl.when(valid & (c == n_chunks - 1))
    def _():
        _store_token_major(y_scr, acc_scr[...])

    @pl.when((i == n_tiles - 1) & (c == n_chunks - 1))
    def _():
        wait_gather(other)


        def issue(r, carry):
            row_copy(y_scr, r * R, y_hbm, dest_row(dstl_ref, r, nvt_ref[n_tiles]), sems.at[SCATTER_SEM]).start()
            return carry
        lax.fori_loop(0, T, issue, 0)
        wait_scatter()


def _moe_experts(h1_tm, g, tile_expert, n_valid_tiles, src_rows, dst_rows, w_gate, w_up, w_down, n_out_tokens):
    D = D_MODEL
    T = MOE_TILE
    NT = tile_expert.shape[0]
    NC = N_MOE_CHUNKS
    FC = D_FF // NC

    def chunk(i, c, nvt):
        return jnp.where(i < nvt[0], c, NC - 1)

    def window_spec(tile_of_step):
        return pl.BlockSpec((pl.Element(1), pl.Element(T)),
                            lambda i, c, te, nvt: (0, nvt[1 + NT + tile_of_step(i)] * LANES),
                            memory_space=pltpu.SMEM)

    grid_spec = pltpu.PrefetchScalarGridSpec(
        num_scalar_prefetch=2,
        grid=(NT, NC),
        in_specs=[
            window_spec(lambda i: 0),
            window_spec(lambda i: jnp.minimum(i + 1, NT - 1)),
            window_spec(lambda i: jnp.maximum(i - 1, 0)),
            window_spec(lambda i: NT - 1),
            pl.BlockSpec(memory_space=pl.ANY),
            pl.BlockSpec((1, D), lambda i, c, te, nvt: (0, 0)),
            pl.BlockSpec((1, D, FC), lambda i, c, te, nvt: (te[i], 0, chunk(i, c, nvt))),
            pl.BlockSpec((1, D, FC), lambda i, c, te, nvt: (te[i], 0, chunk(i, c, nvt))),
            pl.BlockSpec((1, FC, D), lambda i, c, te, nvt: (te[i], chunk(i, c, nvt), 0)),
        ],
        out_specs=pl.BlockSpec(memory_space=pl.ANY),
        scratch_shapes=[pltpu.VMEM((2, T * ROWS_PER_TOKEN, LANES), F32), pltpu.VMEM((T, D), BF16),
                        pltpu.VMEM((T, D), F32), pltpu.VMEM((T * ROWS_PER_TOKEN, LANES), F32),
                        pltpu.SemaphoreType.DMA((3,))],
    )
    return pl.pallas_call(
        _moe_kernel,
        grid_spec=grid_spec,
        out_shape=jax.ShapeDtypeStruct((n_out_tokens * ROWS_PER_TOKEN, LANES), F32),
        compiler_params=pltpu.CompilerParams(dimension_semantics=("arbitrary", "arbitrary"),
                                             vmem_limit_bytes=VMEM_LIMIT),
        name="moe_experts",
    )(tile_expert, n_valid_tiles, src_rows, src_rows, dst_rows, dst_rows, h1_tm, g.reshape(1, D),
      w_gate, w_up, w_down)


def _moe_plan(route, n_tokens):
    T = MOE_TILE
    n_assign = TOP_K * n_tokens
    n_tiles = n_assign // T + N_EXPERTS
    e_flat = route[:, 0:TOP_K].astype(jnp.int32).reshape(-1)
    counts = jnp.sum((e_flat[:, None] == jnp.arange(N_EXPERTS, dtype=jnp.int32)[None, :]).astype(jnp.int32), axis=0)
    fill = (-counts) % LANES
    dummy_expert = jnp.repeat(jnp.arange(N_EXPERTS, dtype=jnp.int32), LANES - 1)
    dummy_rank = jnp.tile(jnp.arange(LANES - 1, dtype=jnp.int32), N_EXPERTS)
    dummy_key = jnp.where(dummy_rank < fill[dummy_expert], dummy_expert, N_EXPERTS)
    order = jnp.argsort(jnp.concatenate([e_flat, dummy_key]), stable=True).astype(jnp.int32)
    tiles_per = (counts + T - 1) // T
    tile_end = jnp.cumsum(tiles_per)
    n_valid = tile_end[-1]
    tile_ids = jnp.arange(n_tiles, dtype=jnp.int32)
    last_valid = jnp.maximum(n_valid - 1, 0)
    texp = jnp.sum((jnp.minimum(tile_ids, last_valid)[:, None] >= tile_end[None, :]).astype(jnp.int32), axis=1)
    texp = jnp.minimum(texp, N_EXPERTS - 1)
    onehot = (texp[:, None] == jnp.arange(N_EXPERTS, dtype=jnp.int32)[None, :]).astype(jnp.int32)
    pick = lambda v: jnp.sum(onehot * v[None, :], axis=1)
    j = tile_ids - pick(tile_end - tiles_per)
    group_size = counts + fill
    start = pick(jnp.cumsum(group_size) - group_size) + j * T
    start_tiles = jnp.where(tile_ids < n_valid, start // LANES, 0)
    n_rows = jnp.where(tile_ids < n_valid, jnp.clip(pick(counts) - j * T, 0, T), 0)
    real = order < n_assign
    token = jnp.where(real, order // TOP_K, 0)
    slot = jnp.where(real, order % TOP_K, 0)
    spare = jnp.zeros((T + N_EXPERTS,), jnp.int32)
    src_sorted = jnp.concatenate([token * ROWS_PER_TOKEN, spare]).reshape(1, -1)
    dst_sorted = jnp.concatenate([(slot * n_tokens + token) * ROWS_PER_TOKEN, spare]).reshape(1, -1)
    tile_counts = jnp.concatenate([n_valid.reshape(1), n_rows, start_tiles]).astype(jnp.int32)
    return texp, tile_counts, src_sorted, dst_sorted


def _combine_kernel(h_ref, y0_ref, y1_ref, route_ref, g_ref, o_ref):
    T = o_ref.shape[0]
    route = route_ref[...]
    moe = route[:, 2:3] * _load_token_major(y0_ref, T) + route[:, 3:4] * _load_token_major(y1_ref, T)
    o_ref[...] = _rmsnorm(_load_token_major(h_ref, T) + moe, g_ref[...])


def _combine(h1_tm, y2_tm, route, g):
    D = D_MODEL
    S = route.shape[0]
    T = OUT_TILE
    nb = S // T
    tm_block = (T * ROWS_PER_TOKEN, LANES)
    return pl.pallas_call(
        _combine_kernel,
        grid=(nb,),
        in_specs=[
            pl.BlockSpec(tm_block, lambda i: (i, 0)),
            pl.BlockSpec(tm_block, lambda i: (i, 0)),
            pl.BlockSpec(tm_block, lambda i: (i + nb, 0)),
            pl.BlockSpec((T, LANES), lambda i: (i, 0)),
            pl.BlockSpec((1, D), lambda i: (0, 0)),
        ],
        out_specs=pl.BlockSpec((T, D), lambda i: (i, 0)),
        out_shape=jax.ShapeDtypeStruct((S, D), F32),
        compiler_params=pltpu.CompilerParams(dimension_semantics=("arbitrary",), vmem_limit_bytes=VMEM_LIMIT),
        name="moe_combine",
    )(h1_tm, y2_tm, y2_tm, route, g.reshape(1, D))


def kernel(x, positions, norm_mix, norm_ffn, norm_final, lru_w_in, lru_conv_w, lru_conv_b, lru_w_a, lru_b_a, lru_w_x, lru_b_x, lru_lambda, lru_w_out, attn_w_qkv, attn_b_qkv, attn_sinks, attn_w_o, attn_b_o, ffn_w_gate, ffn_w_up, ffn_w_down, moe_w_router, moe_w_gate, moe_w_up, moe_w_down):
    B, S, D = x.shape
    assert B == 1 and D == D_MODEL and S % FFN_TILE == 0
    h = x.reshape(S, D)

    w_ax = jnp.concatenate([lru_w_a[0], lru_w_x[0]], axis=-1).astype(BF16)
    h, ffn_wg, ffn_wu, ffn_wd = _lru_mixer(h, norm_mix[0], lru_w_in[0].astype(BF16), lru_conv_w[0], lru_conv_b[0],
                                           w_ax, lru_b_a[0], lru_b_x[0], lru_lambda[0], lru_w_out[0].astype(BF16),
                                           ffn_w_gate[0], ffn_w_up[0], ffn_w_down[0])
    h, moe_wg, moe_wu, moe_wd = _dense_ffn(h, norm_ffn[0], ffn_wg, ffn_wu, ffn_wd,
                                           moe_w_gate[0], moe_w_up[0], moe_w_down[0])

    q_dim = N_HEADS * HEAD_DIM
    kv_dim = N_KV_HEADS * HEAD_DIM

    def dup_heads(w):
        w4 = w.reshape(w.shape[:-1] + (N_KV_HEADS, 1, HEAD_DIM))
        return jnp.broadcast_to(w4, w.shape[:-1] + (N_KV_HEADS, 2, HEAD_DIM)).reshape(w.shape[:-1] + (KV_EXT,))

    wqkv, bqkv = attn_w_qkv[0], attn_b_qkv[0]
    w_qkv_ext = jnp.concatenate([wqkv[:, :q_dim], dup_heads(wqkv[:, q_dim:q_dim + kv_dim]),
                                 dup_heads(wqkv[:, q_dim + kv_dim:])], axis=1).astype(BF16)
    b_qkv_ext = jnp.concatenate([bqkv[:q_dim], dup_heads(bqkv[q_dim:q_dim + kv_dim]),
                                 dup_heads(bqkv[q_dim + kv_dim:])]).reshape(1, QKV_EXT)
    inv_freq = ROPE_THETA ** (-jnp.arange(0, ROT_DIM, 2, dtype=F32) / ROT_DIM)
    d_in_head = jnp.arange(LANES) % HEAD_DIM
    freq_lane = jnp.where(d_in_head < ROT_DIM, inv_freq[d_in_head % (ROT_DIM // 2)], 0.0).reshape(1, LANES)
    sign_lane = jnp.where(d_in_head < ROT_DIM // 2, -1.0, jnp.where(d_in_head < ROT_DIM, 1.0, 0.0))
    sign_lane = sign_lane.astype(F32).reshape(1, LANES)
    w_router_pad = jnp.pad(moe_w_router[0], ((0, 0), (0, LANES - N_EXPERTS))).astype(BF16)
    pos_f = positions.reshape(S, 1).astype(F32)
    h1_tm, route = _attn_mixer(h, pos_f, norm_mix[1], freq_lane, sign_lane, w_qkv_ext, b_qkv_ext,
                               attn_sinks[0], attn_w_o[0].astype(BF16), attn_b_o[0], norm_ffn[1], w_router_pad)

    texp, n_valid, src_rows, dst_rows = _moe_plan(route, S)
    y2_tm = _moe_experts(h1_tm, norm_ffn[1], texp, n_valid, src_rows, dst_rows,
                         moe_wg, moe_wu, moe_wd,
                         TOP_K * S + MOE_TILE)
    out = _combine(h1_tm, y2_tm, route, norm_final)
    return out.reshape(B, S, D)
```

```python
import functools

import jax
import jax.numpy as jnp
from jax import lax
from jax.experimental import pallas as pl
from jax.experimental.pallas import tpu as pltpu

F32 = jnp.float32
BF16 = jnp.bfloat16

D_MODEL = 1024
N_RNN_BLOCKS = 8
RNN_BLOCK = D_MODEL // N_RNN_BLOCKS
CONV_WIDTH = 4
LRU_C = 8.0
N_HEADS = 16
N_KV_HEADS = 4
HEAD_DIM = 64
GROUP = N_HEADS // N_KV_HEADS
WINDOW = 128
ATTN_BLOCK = 128
ROPE_THETA = 500000.0
ROT_DIM = HEAD_DIM // 4
D_FF = 3584
N_EXPERTS = 8
TOP_K = 2
EPS = 1e-6
NEG_INF = -1e30
LOG2_E = 1.4426950408889634

LANES = 128
SUBLANES = 8
VMEM_LIMIT = 56 * 1024 * 1024

SEQ_TILE = 512
LRU_CHUNKS = 4
FFN_TILE = 1024
FF_CHUNK = 512
N_FF_CHUNKS = D_FF // FF_CHUNK
MOE_TILE = 512
N_MOE_CHUNKS = 2
OUT_TILE = 1024


def _rmsnorm(x, g):
    return x * lax.rsqrt(jnp.mean(x * x, axis=-1, keepdims=True) + EPS) * g


ROWS_PER_TOKEN = D_MODEL // LANES


def _store_token_major(ref, val):
    T = val.shape[0]
    for c in range(ROWS_PER_TOKEN):
        ref[pl.ds(c, T, stride=ROWS_PER_TOKEN), :] = val[:, LANES * c:LANES * (c + 1)]


def _load_token_major(ref, T):
    return jnp.concatenate([ref[pl.ds(c, T, stride=ROWS_PER_TOKEN), :] for c in range(ROWS_PER_TOKEN)], axis=1)


def _const_spec(shape):
    n = len(shape)
    return pl.BlockSpec(shape, lambda *_: (0,) * n, pipeline_mode=pl.Buffered(1))


def _time_permutation(T):
    rho = jnp.arange(T)
    t_of_rho = (rho % SUBLANES) * (T // SUBLANES) + rho // SUBLANES
    perm = (t_of_rho[:, None] == jnp.arange(T)[None, :]).astype(BF16)
    return perm, perm.T


def _lru_kernel(xin_ref, xres_ref, g_ref, perm_ref, unperm_ref, win_ref, cw_ref, cb_ref, wax_ref, ba_ref, bx_ref,
                lam_ref, wout_ref, cast_g_ref, cast_u_ref, cast_d_ref,
                o_ref, bf_g_ref, bf_u_ref, bf_d_ref, proj_scr, tail_scr, carry_scr):
    i = pl.program_id(0)
    n_tiles = pl.num_programs(0) - 1
    T = xin_ref.shape[0]
    D = D_MODEL
    C = T // LRU_CHUNKS
    GC = C // SUBLANES
    G = T // SUBLANES
    HALO = (CONV_WIDTH - 1) * SUBLANES

    def cast_slices():
        bf_g_ref[...] = cast_g_ref[...].astype(BF16)
        bf_u_ref[...] = cast_u_ref[...].astype(BF16)
        bf_d_ref[...] = cast_d_ref[...].astype(BF16)

    @pl.when(i == 0)
    def _():
        tail_scr[...] = jnp.zeros_like(tail_scr)
        carry_scr[...] = jnp.zeros_like(carry_scr)

    PW = 2 * D // LRU_CHUNKS

    def project_chunks():
        hn = _rmsnorm(xin_ref[...], g_ref[...]).astype(BF16)
        hp = jnp.dot(perm_ref[...], hn, preferred_element_type=F32).astype(BF16)
        for k in range(LRU_CHUNKS):
            yield jnp.dot(hp, win_ref[:, PW * k:PW * (k + 1)], preferred_element_type=F32)

    def recur_chunks():
        sub = lax.broadcasted_iota(jnp.int32, (SUBLANES, D), 0)
        cw = cw_ref[...]
        softplus_neg_lam = jax.nn.softplus(-lam_ref[...])
        last = proj_scr[T - HALO:T, D:]
        prev_groups = []
        for k in range(CONV_WIDTH - 1):
            rows = slice(SUBLANES * k, SUBLANES * (k + 1))
            prev_groups.append(pltpu.roll(jnp.where(sub == SUBLANES - 1, tail_scr[rows, :], last[rows, :]), 1, 0))
        tail_scr[...] = last

        ys, hs, decay = [], [], []
        for k in range(LRU_CHUNKS):
            pc = proj_scr[C * k:C * (k + 1), :]
            ys.append(jax.nn.gelu(pc[:, :D], approximate=True))
            xb = pc[:, D:]
            xc = cb_ref[...] + xb * cw[CONV_WIDTH - 1:CONV_WIDTH, :]
            for back in range(1, CONV_WIDTH):
                shifted = jnp.concatenate(prev_groups[CONV_WIDTH - 1 - back:] + [xb[:C - SUBLANES * back, :]], axis=0)
                xc = xc + shifted * cw[CONV_WIDTH - 1 - back:CONV_WIDTH - back, :]
            prev_groups = [xb[C - HALO + SUBLANES * q:C - HALO + SUBLANES * (q + 1), :] for q in range(CONV_WIDTH - 1)]

            xcb = xc.astype(BF16)
            r_parts, i_parts = [], []
            for n in range(N_RNN_BLOCKS):
                gn = jnp.dot(xcb[:, RNN_BLOCK * n:RNN_BLOCK * (n + 1)], wax_ref[n], preferred_element_type=F32)
                r_parts.append(gn[:, :RNN_BLOCK])
                i_parts.append(gn[:, RNN_BLOCK:])
            r = jax.nn.sigmoid(jnp.concatenate(r_parts, axis=1) + ba_ref[...])
            ig = jax.nn.sigmoid(jnp.concatenate(i_parts, axis=1) + bx_ref[...])
            log_a = -LRU_C * r * softplus_neg_lam
            a = jnp.exp(log_a)
            z = jnp.tanh(-log_a) * (1.0 + a * a)
            b = jnp.where(z > 0.0, z * lax.rsqrt(z), 0.0) * (ig * xc)

            for j in range(GC):
                rows = slice(SUBLANES * j, SUBLANES * (j + 1))
                if hs:
                    hs.append(a[rows, :] * hs[-1] + b[rows, :])
                    decay.append(a[rows, :] * decay[-1])
                else:
                    hs.append(b[rows, :])
                    decay.append(a[rows, :])
            yield None

        p_inc, e_inc = decay[-1], hs[-1]
        for k in (1, 2, 4):
            m = sub >= k
            p_sh = jnp.where(m, pltpu.roll(p_inc, k, 0), 1.0)
            e_sh = jnp.where(m, pltpu.roll(e_inc, k, 0), 0.0)
            e_inc = e_inc + p_inc * e_sh
            p_inc = p_inc * p_sh
        h0 = carry_scr[...]
        after = p_inc * h0 + e_inc
        carry_scr[...] = after[SUBLANES - 1:SUBLANES, :]
        enter = jnp.where(sub == 0, h0, pltpu.roll(after, 1, 0))
        h = jnp.concatenate([hs[j] + decay[j] * enter for j in range(G)], axis=0)
        hy = (h * jnp.concatenate(ys, axis=0)).astype(BF16)
        hy = jnp.dot(unperm_ref[...], hy, preferred_element_type=F32).astype(BF16)
        yield jnp.dot(hy, wout_ref[...], preferred_element_type=F32) + xres_ref[...]

    @pl.when(i == 0)
    def _():
        cast_slices()
        for k, pc in enumerate(project_chunks()):
            proj_scr[:, PW * k:PW * (k + 1)] = pc

    @pl.when((i > 0) & (i < n_tiles))
    def _():
        cast_slices()
        rec = recur_chunks()
        new_proj = []
        for pc in project_chunks():
            new_proj.append(pc)
            next(rec)
        o_ref[...] = next(rec)
        for k, pc in enumerate(new_proj):
            proj_scr[:, PW * k:PW * (k + 1)] = pc

    @pl.when(i == n_tiles)
    def _():
        o_ref[...] = list(recur_chunks())[-1]


def _lru_mixer(x, g, w_in, conv_w, conv_b, w_ax, b_a, b_x, lam, w_out, cast_gate, cast_up, cast_down):
    S, D = x.shape
    T = SEQ_TILE
    n_tiles = S // T
    row = lambda v: v.reshape(1, D)
    perm, unperm = _time_permutation(T)

    def cast_spec(w):
        return pl.BlockSpec((w.shape[0] // n_tiles, w.shape[1]), lambda i: (jnp.minimum(i, n_tiles - 1), 0))

    casts = (cast_gate, cast_up, cast_down)
    return pl.pallas_call(
        _lru_kernel,
        grid=(n_tiles + 1,),
        in_specs=[
            pl.BlockSpec((T, D), lambda i: (jnp.minimum(i, n_tiles - 1), 0)),
            pl.BlockSpec((T, D), lambda i: (jnp.maximum(i - 1, 0), 0)),
            _const_spec((1, D)),
            _const_spec((T, T)),
            _const_spec((T, T)),
            _const_spec((D, 2 * D)),
            _const_spec((CONV_WIDTH, D)),
            _const_spec((1, D)),
            _const_spec((N_RNN_BLOCKS, RNN_BLOCK, 2 * RNN_BLOCK)),
            _const_spec((1, D)),
            _const_spec((1, D)),
            _const_spec((1, D)),
            _const_spec((D, D)),
        ] + [cast_spec(w) for w in casts],
        out_specs=[pl.BlockSpec((T, D), lambda i: (jnp.maximum(i - 1, 0), 0))] + [cast_spec(w) for w in casts],
        out_shape=[jax.ShapeDtypeStruct((S, D), F32)] + [jax.ShapeDtypeStruct(w.shape, BF16) for w in casts],
        scratch_shapes=[pltpu.VMEM((T, 2 * D), F32), pltpu.VMEM(((CONV_WIDTH - 1) * SUBLANES, D), F32),
                        pltpu.VMEM((1, D), F32)],
        compiler_params=pltpu.CompilerParams(dimension_semantics=("arbitrary",), vmem_limit_bytes=VMEM_LIMIT),
        name="lru_mixer",
    )(x, x, row(g), perm, unperm, w_in, conv_w, row(conv_b), w_ax, row(b_a), row(b_x), row(lam), w_out, *casts)


def _ffn_kernel(x_ref, g_ref, wg_ref, wu_ref, wd_ref, cast_g_ref, cast_u_ref, cast_d_ref,
                o_ref, bf_g_ref, bf_u_ref, bf_d_ref, xn_scr, acc_scr):
    c = pl.program_id(1)
    last = pl.num_programs(1) - 1

    def ff_chunk():
        bf_g_ref[...] = cast_g_ref[...].astype(BF16)
        bf_u_ref[...] = cast_u_ref[...].astype(BF16)
        bf_d_ref[...] = cast_d_ref[...].astype(BF16)
        xn = xn_scr[...]
        gate = jnp.dot(xn, wg_ref[...], preferred_element_type=F32)
        up = jnp.dot(xn, wu_ref[...], preferred_element_type=F32)
        mid = (jax.nn.silu(gate) * up).astype(BF16)
        return jnp.dot(mid, wd_ref[...], preferred_element_type=F32)

    @pl.when(c == 0)
    def _():
        xn_scr[...] = _rmsnorm(x_ref[...], g_ref[...]).astype(BF16)
        acc_scr[...] = ff_chunk()

    @pl.when((c > 0) & (c < last))
    def _():
        acc_scr[...] += ff_chunk()

    @pl.when(c == last)
    def _():
        o_ref[...] = x_ref[...] + (acc_scr[...] + ff_chunk())


def _dense_ffn(x, g, w_gate, w_up, w_down, cast_gate, cast_up, cast_down):
    S, D = x.shape
    T = FFN_TILE
    n_steps = (S // T) * N_FF_CHUNKS
    E = cast_gate.shape[0]
    row_parts = n_steps // (E * N_FF_CHUNKS)
    assert row_parts * E * N_FF_CHUNKS == n_steps
    gu_block = (1, D // row_parts, D_FF // N_FF_CHUNKS)
    d_block = (1, D_FF // (row_parts * N_FF_CHUNKS), D)

    def gu_map(i, c):
        s = i * N_FF_CHUNKS + c
        return (s // (row_parts * N_FF_CHUNKS), (s // N_FF_CHUNKS) % row_parts, c)

    def d_map(i, c):
        s = i * N_FF_CHUNKS + c
        return (s // (row_parts * N_FF_CHUNKS), s % (row_parts * N_FF_CHUNKS), 0)

    return pl.pallas_call(
        _ffn_kernel,
        grid=(S // T, N_FF_CHUNKS),
        in_specs=[
            pl.BlockSpec((T, D), lambda i, c: (i, 0)),
            pl.BlockSpec((1, D), lambda i, c: (0, 0)),
            pl.BlockSpec((D, FF_CHUNK), lambda i, c: (0, c)),
            pl.BlockSpec((D, FF_CHUNK), lambda i, c: (0, c)),
            pl.BlockSpec((FF_CHUNK, D), lambda i, c: (c, 0)),
            pl.BlockSpec(gu_block, gu_map),
            pl.BlockSpec(gu_block, gu_map),
            pl.BlockSpec(d_block, d_map),
        ],
        out_specs=[pl.BlockSpec((T, D), lambda i, c: (i, 0)), pl.BlockSpec(gu_block, gu_map),
                   pl.BlockSpec(gu_block, gu_map), pl.BlockSpec(d_block, d_map)],
        out_shape=[jax.ShapeDtypeStruct((S, D), F32), jax.ShapeDtypeStruct(cast_gate.shape, BF16),
                   jax.ShapeDtypeStruct(cast_up.shape, BF16), jax.ShapeDtypeStruct(cast_down.shape, BF16)],
        scratch_shapes=[pltpu.VMEM((T, D), BF16), pltpu.VMEM((T, D), F32)],
        compiler_params=pltpu.CompilerParams(dimension_semantics=("arbitrary", "arbitrary"),
                                             vmem_limit_bytes=VMEM_LIMIT),
        name="dense_ffn",
    )(x, g.reshape(1, D), w_gate, w_up, w_down, cast_gate, cast_up, cast_down)


KV_EXT = N_KV_HEADS * LANES
QKV_EXT = N_HEADS * HEAD_DIM + 2 * KV_EXT


def _attn_kernel(sink_ref, x_ref, pos_ref, g_ref, freq_ref, sign_ref, wqkv_ref, bqkv_ref, wo_ref, bo_ref,
                 g2_ref, wr_ref, o_ref, route_ref,
                 qlo_scr, qhi_scr, k_scr, v_scr, o_scr):
    T = x_ref.shape[0]
    D = D_MODEL
    B = ATTN_BLOCK
    first_tile = pl.program_id(0) == 0

    @pl.when(first_tile)
    def _():
        k_scr[0:B, :] = jnp.zeros((B, KV_EXT), BF16)
        v_scr[0:B, :] = jnp.zeros((B, KV_EXT), BF16)

    x = x_ref[...]
    hn = _rmsnorm(x, g_ref[...]).astype(BF16)
    qkv = jnp.dot(hn, wqkv_ref[...], preferred_element_type=F32) + bqkv_ref[...]

    ang = pos_ref[...] * freq_ref[...]
    cos_t = jnp.cos(ang)
    sin_t = jnp.sin(ang) * sign_ref[...]
    lane = lax.broadcasted_iota(jnp.int32, (T, LANES), 1)
    first_half = (lane % HEAD_DIM) < (ROT_DIM // 2)
    lo_half = lane < HEAD_DIM

    def rope(col):
        partner = jnp.where(first_half, pltpu.roll(col, LANES - ROT_DIM // 2, 1), pltpu.roll(col, ROT_DIM // 2, 1))
        return col * cos_t + partner * sin_t

    scale = HEAD_DIM ** -0.5 * LOG2_E
    for c in range(D // LANES):
        qc = rope(qkv[:, LANES * c:LANES * (c + 1)]) * scale
        qlo_scr[:, LANES * c:LANES * (c + 1)] = jnp.where(lo_half, qc, 0.0).astype(BF16)
        qhi_scr[:, LANES * c:LANES * (c + 1)] = jnp.where(lo_half, 0.0, qc).astype(BF16)
    for g in range(N_KV_HEADS):
        kc = rope(qkv[:, D + LANES * g:D + LANES * (g + 1)])
        k_scr[B:B + T, LANES * g:LANES * (g + 1)] = kc.astype(BF16)
        vc = qkv[:, D + KV_EXT + LANES * g:D + KV_EXT + LANES * (g + 1)]
        v_scr[B:B + T, LANES * g:LANES * (g + 1)] = vc.astype(BF16)

    qi = lax.broadcasted_iota(jnp.int32, (B, B), 0)
    kj = lax.broadcasted_iota(jnp.int32, (B, B), 1)
    causal_own = kj <= qi
    window_prev = kj > qi
    key0 = kj == 0
    key0_row = lax.broadcasted_iota(jnp.int32, (1, B), 1) == 0
    lo_blk = lax.broadcasted_iota(jnp.int32, (B, LANES), 1) < HEAD_DIM

    def block_body(blk, _):
        r0 = pl.multiple_of(blk * B, B)
        k_lo = jnp.where(first_tile & (blk == 0), B, 0)
        allowed_prev = window_prev & (kj >= k_lo)
        for g in range(N_KV_HEADS):
            kk = k_scr[pl.ds(r0, 2 * B), LANES * g:LANES * (g + 1)]
            vv = v_scr[pl.ds(r0, 2 * B), LANES * g:LANES * (g + 1)]
            cols = [slice(LANES * c, LANES * (c + 1)) for c in range((GROUP // 2) * g, (GROUP // 2) * (g + 1))]
            q_all = jnp.concatenate([q_scr[pl.ds(r0, B), col] for col in cols for q_scr in (qlo_scr, qhi_scr)], axis=0)
            s_all = lax.dot_general(q_all, kk, (((1,), (1,)), ((), ())), preferred_element_type=F32)
            p_all, denoms = [], []
            for hh in range(GROUP):
                rows = slice(B * hh, B * (hh + 1))
                sink_fill = jnp.where(key0_row, sink_ref[GROUP * g + hh] * LOG2_E, NEG_INF)
                s_prev = jnp.where(allowed_prev, s_all[rows, :B], sink_fill)
                s_own = jnp.where(causal_own, s_all[rows, B:], NEG_INF)
                m = jnp.max(jnp.maximum(s_prev, s_own), axis=-1, keepdims=True)
                p_prev = jnp.exp2(s_prev - m)
                p_own = jnp.exp2(s_own - m)
                denoms.append(jnp.sum(p_prev + p_own, axis=-1, keepdims=True))
                p_all.append(jnp.concatenate([jnp.where(key0, 0.0, p_prev), p_own], axis=1).astype(BF16))
            o_all = jnp.dot(jnp.concatenate(p_all, axis=0), vv, preferred_element_type=F32)
            o_heads = [o_all[B * hh:B * (hh + 1), :] / denoms[hh] for hh in range(GROUP)]
            for p, col in enumerate(cols):
                o_scr[pl.ds(r0, B), col] = jnp.where(lo_blk, o_heads[2 * p], o_heads[2 * p + 1]).astype(BF16)
        return 0

    lax.fori_loop(0, T // B, block_body, 0)

    k_scr[0:B, :] = k_scr[T:T + B, :]
    v_scr[0:B, :] = v_scr[T:T + B, :]

    h1 = jnp.dot(o_scr[...], wo_ref[...], preferred_element_type=F32) + bo_ref[...] + x
    _store_token_major(o_ref, h1)

    hn2 = _rmsnorm(h1, g2_ref[...]).astype(BF16)
    logits = jnp.dot(hn2, wr_ref[...], preferred_element_type=F32)
    lane_f = lane.astype(F32)
    lg = jnp.where(lane < N_EXPERTS, logits, -jnp.inf)
    v1 = jnp.max(lg, axis=-1, keepdims=True)
    i1 = jnp.min(jnp.where(lg == v1, lane_f, float(LANES)), axis=-1, keepdims=True)
    lg2 = jnp.where(lane_f == i1, -jnp.inf, lg)
    v2 = jnp.max(lg2, axis=-1, keepdims=True)
    i2 = jnp.min(jnp.where(lg2 == v2, lane_f, float(LANES)), axis=-1, keepdims=True)
    e2 = jnp.exp(v2 - v1)
    gate1 = 1.0 / (1.0 + e2)
    gate2 = e2 / (1.0 + e2)
    route_ref[...] = jnp.where(lane == 0, i1, jnp.where(lane == 1, i2,
                                                        jnp.where(lane == 2, gate1, jnp.where(lane == 3, gate2, 0.0))))


def _attn_mixer(x, pos_f, g, freq_lane, sign_lane, w_qkv_ext, b_qkv_ext, sinks, w_o, b_o, g2, w_router_pad):
    S, D = x.shape
    T = SEQ_TILE
    B = ATTN_BLOCK
    grid_spec = pltpu.PrefetchScalarGridSpec(
        num_scalar_prefetch=1,
        grid=(S // T,),
        in_specs=[
            pl.BlockSpec((T, D), lambda i, s: (i, 0)),
            pl.BlockSpec((T, 1), lambda i, s: (i, 0)),
            _const_spec((1, D)),
            _const_spec((1, LANES)),
            _const_spec((1, LANES)),
            _const_spec((D, QKV_EXT)),
            _const_spec((1, QKV_EXT)),
            _const_spec((D, D)),
            _const_spec((1, D)),
            _const_spec((1, D)),
            _const_spec((D, LANES)),
        ],
        out_specs=[pl.BlockSpec((T * ROWS_PER_TOKEN, LANES), lambda i, s: (i, 0)),
                   pl.BlockSpec((T, LANES), lambda i, s: (i, 0))],
        scratch_shapes=[
            pltpu.VMEM((T, D), BF16), pltpu.VMEM((T, D), BF16),
            pltpu.VMEM((T + B, KV_EXT), BF16), pltpu.VMEM((T + B, KV_EXT), BF16),
            pltpu.VMEM((T, D), BF16),
        ],
    )
    return pl.pallas_call(
        _attn_kernel,
        grid_spec=grid_spec,
        out_shape=[jax.ShapeDtypeStruct((S * ROWS_PER_TOKEN, LANES), F32), jax.ShapeDtypeStruct((S, LANES), F32)],
        compiler_params=pltpu.CompilerParams(dimension_semantics=("arbitrary",), vmem_limit_bytes=VMEM_LIMIT),
        name="attn_mixer",
    )(sinks, x, pos_f, g.reshape(1, D), freq_lane, sign_lane, w_qkv_ext, b_qkv_ext, w_o, b_o.reshape(1, D),
      g2.reshape(1, D), w_router_pad)


def _moe_kernel(texp_ref, nvt_ref, src0_ref, srcn_ref, dstp_ref, dstl_ref, h_hbm, g_ref, wg_ref, wu_ref, wd_ref,
                y_hbm, x_scr, xn_scr, acc_scr, y_scr, sems):
    i = pl.program_id(0)
    c = pl.program_id(1)
    n_tiles = pl.num_programs(0)
    n_chunks = pl.num_programs(1)
    T = xn_scr.shape[0]
    R = ROWS_PER_TOKEN
    valid = i < nvt_ref[0]
    slot = i % 2
    other = 1 - slot
    SCATTER_SEM = 2

    def row_copy(src, src_row, dst, dst_row, sem):
        return pltpu.make_async_copy(src.at[pl.ds(pl.multiple_of(src_row, R), R), :],
                                     dst.at[pl.ds(pl.multiple_of(dst_row, R), R), :], sem)

    rows_prev = jnp.where(i > 0, nvt_ref[i], 0)
    pad_base = y_hbm.shape[0] - T * R

    def dest_row(window_ref, r, n_rows):
        return jnp.where(r < n_rows, window_ref[0, r], pad_base + r * R)

    def wait_gather(s):
        pltpu.make_async_copy(h_hbm.at[pl.ds(0, T * R), :], x_scr.at[s], sems.at[s]).wait()

    def wait_scatter():
        pltpu.make_async_copy(y_scr, y_hbm.at[pl.ds(0, T * R), :], sems.at[SCATTER_SEM]).wait()

    @pl.when((i == 0) & (c == 0))
    def _():
        y_scr[...] = jnp.zeros_like(y_scr)
        acc_scr[...] = jnp.zeros_like(acc_scr)

        def issue(r, carry):
            row_copy(h_hbm, src0_ref[0, r], x_scr.at[0], r * R, sems.at[0]).start()
            return carry
        lax.fori_loop(0, T, issue, 0)

    @pl.when(c == 0)
    def _():
        wait_gather(slot)

    @pl.when(valid & (c == 0))
    def _():
        xn_scr[...] = _rmsnorm(_load_token_major(x_scr.at[slot], T), g_ref[...]).astype(BF16)

    def row_dmas():
        for r in range(T):
            row_copy(h_hbm, srcn_ref[0, r], x_scr.at[other], r * R, sems.at[other]).start()
            row_copy(y_scr, r * R, y_hbm, dest_row(dstp_ref, r, rows_prev), sems.at[SCATTER_SEM]).start()

    def expert_chunk(first, rows):
        xn = xn_scr[0:rows, :]
        gate = jnp.dot(xn, wg_ref[0], preferred_element_type=F32)
        up = jnp.dot(xn, wu_ref[0], preferred_element_type=F32)
        mid = (jax.nn.silu(gate) * up).astype(BF16)
        down = jnp.dot(mid, wd_ref[0], preferred_element_type=F32)
        if first:
            acc_scr[0:rows, :] = down
        else:
            acc_scr[0:rows, :] += down

    few_rows = nvt_ref[1 + i] <= T // 2

    for few, rows in ((False, T), (True, T // 2)):
        branch = valid & (few_rows if few else jnp.logical_not(few_rows))

        @pl.when(branch & (c == 0))
        def _():
            row_dmas()
            expert_chunk(True, rows)

        @pl.when(branch & (c != 0))
        def _():
            expert_chunk(False, rows)

    @pl.when(jnp.logical_not(valid) & (c == 0))
    def _():
        row_dmas()

    @pl.when(c == n_chunks - 1)
    def _():
        wait_scatter()

    @pl.when(valid & (c == n_chunks - 1))
    def _():
        _store_token_major(y_scr, acc_scr[...])

    @pl.when((i == n_tiles - 1) & (c == n_chunks - 1))
    def _():
        wait_gather(other)


        def issue(r, carry):
            row_copy(y_scr, r * R, y_hbm, dest_row(dstl_ref, r, nvt_ref[n_tiles]), sems.at[SCATTER_SEM]).start()
            return carry
        lax.fori_loop(0, T, issue, 0)
        wait_scatter()


def _moe_experts(h1_tm, g, tile_expert, n_valid_tiles, src_rows, dst_rows, w_gate, w_up, w_down, n_out_tokens):
    D = D_MODEL
    T = MOE_TILE
    NT = tile_expert.shape[0]
    NC = N_MOE_CHUNKS
    FC = D_FF // NC

    def chunk(i, c, nvt):
        return jnp.where(i < nvt[0], c, NC - 1)

    def window_spec(tile_of_step):
        return pl.BlockSpec((pl.Element(1), pl.Element(T)),
                            lambda i, c, te, nvt: (0, nvt[1 + NT + tile_of_step(i)] * LANES),
                            memory_space=pltpu.SMEM)

    grid_spec = pltpu.PrefetchScalarGridSpec(
        num_scalar_prefetch=2,
        grid=(NT, NC),
        in_specs=[
            window_spec(lambda i: 0),
            window_spec(lambda i: jnp.minimum(i + 1, NT - 1)),
            window_spec(lambda i: jnp.maximum(i - 1, 0)),
            window_spec(lambda i: NT - 1),
            pl.BlockSpec(memory_space=pl.ANY),
            pl.BlockSpec((1, D), lambda i, c, te, nvt: (0, 0)),
            pl.BlockSpec((1, D, FC), lambda i, c, te, nvt: (te[i], 0, chunk(i, c, nvt))),
            pl.BlockSpec((1, D, FC), lambda i, c, te, nvt: (te[i], 0, chunk(i, c, nvt))),
            pl.BlockSpec((1, FC, D), lambda i, c, te, nvt: (te[i], chunk(i, c, nvt), 0)),
        ],
        out_specs=pl.BlockSpec(memory_space=pl.ANY),
        scratch_shapes=[pltpu.VMEM((2, T * ROWS_PER_TOKEN, LANES), F32), pltpu.VMEM((T, D), BF16),
                        pltpu.VMEM((T, D), F32), pltpu.VMEM((T * ROWS_PER_TOKEN, LANES), F32),
                        pltpu.SemaphoreType.DMA((3,))],
    )
    return pl.pallas_call(
        _moe_kernel,
        grid_spec=grid_spec,
        out_shape=jax.ShapeDtypeStruct((n_out_tokens * ROWS_PER_TOKEN, LANES), F32),
        compiler_params=pltpu.CompilerParams(dimension_semantics=("arbitrary", "arbitrary"),
                                             vmem_limit_bytes=VMEM_LIMIT),
        name="moe_experts",
    )(tile_expert, n_valid_tiles, src_rows, src_rows, dst_rows, dst_rows, h1_tm, g.reshape(1, D),
      w_gate, w_up, w_down)


def _moe_plan(route, n_tokens):
    T = MOE_TILE
    n_assign = TOP_K * n_tokens
    n_tiles = n_assign // T + N_EXPERTS
    e_flat = route[:, 0:TOP_K].astype(jnp.int32).reshape(-1)
    counts = jnp.sum((e_flat[:, None] == jnp.arange(N_EXPERTS, dtype=jnp.int32)[None, :]).astype(jnp.int32), axis=0)
    fill = (-counts) % LANES
    dummy_expert = jnp.repeat(jnp.arange(N_EXPERTS, dtype=jnp.int32), LANES - 1)
    dummy_rank = jnp.tile(jnp.arange(LANES - 1, dtype=jnp.int32), N_EXPERTS)
    dummy_key = jnp.where(dummy_rank < fill[dummy_expert], dummy_expert, N_EXPERTS)
    keys = jnp.concatenate([e_flat, dummy_key])
    assert keys.shape[0] <= 1 << 16
    order = jnp.sort(keys * (1 << 16) + jnp.arange(keys.shape[0], dtype=jnp.int32)) & ((1 << 16) - 1)
    tiles_per = (counts + T - 1) // T
    tile_end = jnp.cumsum(tiles_per)
    n_valid = tile_end[-1]
    tile_ids = jnp.arange(n_tiles, dtype=jnp.int32)
    last_valid = jnp.maximum(n_valid - 1, 0)
    texp = jnp.sum((jnp.minimum(tile_ids, last_valid)[:, None] >= tile_end[None, :]).astype(jnp.int32), axis=1)
    texp = jnp.minimum(texp, N_EXPERTS - 1)
    onehot = (texp[:, None] == jnp.arange(N_EXPERTS, dtype=jnp.int32)[None, :]).astype(jnp.int32)
    pick = lambda v: jnp.sum(onehot * v[None, :], axis=1)
    j = tile_ids - pick(tile_end - tiles_per)
    group_size = counts + fill
    start = pick(jnp.cumsum(group_size) - group_size) + j * T
    start_tiles = jnp.where(tile_ids < n_valid, start // LANES, 0)
    n_rows = jnp.where(tile_ids < n_valid, jnp.clip(pick(counts) - j * T, 0, T), 0)
    real = order < n_assign
    token = jnp.where(real, order // TOP_K, 0)
    slot = jnp.where(real, order % TOP_K, 0)
    spare = jnp.zeros((T + N_EXPERTS,), jnp.int32)
    src_sorted = jnp.concatenate([token * ROWS_PER_TOKEN, spare]).reshape(1, -1)
    dst_sorted = jnp.concatenate([(slot * n_tokens + token) * ROWS_PER_TOKEN, spare]).reshape(1, -1)
    tile_counts = jnp.concatenate([n_valid.reshape(1), n_rows, start_tiles]).astype(jnp.int32)
    return texp, tile_counts, src_sorted, dst_sorted


def _combine_kernel(h_ref, y0_ref, y1_ref, route_ref, g_ref, o_ref):
    T = o_ref.shape[0]
    route = route_ref[...]
    moe = route[:, 2:3] * _load_token_major(y0_ref, T) + route[:, 3:4] * _load_token_major(y1_ref, T)
    o_ref[...] = _rmsnorm(_load_token_major(h_ref, T) + moe, g_ref[...])


def _combine(h1_tm, y2_tm, route, g):
    D = D_MODEL
    S = route.shape[0]
    T = OUT_TILE
    nb = S // T
    tm_block = (T * ROWS_PER_TOKEN, LANES)
    return pl.pallas_call(
        _combine_kernel,
        grid=(nb,),
        in_specs=[
            pl.BlockSpec(tm_block, lambda i: (i, 0)),
            pl.BlockSpec(tm_block, lambda i: (i, 0)),
            pl.BlockSpec(tm_block, lambda i: (i + nb, 0)),
            pl.BlockSpec((T, LANES), lambda i: (i, 0)),
            pl.BlockSpec((1, D), lambda i: (0, 0)),
        ],
        out_specs=pl.BlockSpec((T, D), lambda i: (i, 0)),
        out_shape=jax.ShapeDtypeStruct((S, D), F32),
        compiler_params=pltpu.CompilerParams(dimension_semantics=("arbitrary",), vmem_limit_bytes=VMEM_LIMIT),
        name="moe_combine",
    )(h1_tm, y2_tm, y2_tm, route, g.reshape(1, D))


def kernel(x, positions, norm_mix, norm_ffn, norm_final, lru_w_in, lru_conv_w, lru_conv_b, lru_w_a, lru_b_a, lru_w_x, lru_b_x, lru_lambda, lru_w_out, attn_w_qkv, attn_b_qkv, attn_sinks, attn_w_o, attn_b_o, ffn_w_gate, ffn_w_up, ffn_w_down, moe_w_router, moe_w_gate, moe_w_up, moe_w_down):
    B, S, D = x.shape
    assert B == 1 and D == D_MODEL and S % FFN_TILE == 0
    h = x.reshape(S, D)

    w_ax = jnp.concatenate([lru_w_a[0], lru_w_x[0]], axis=-1).astype(BF16)
    h, ffn_wg, ffn_wu, ffn_wd = _lru_mixer(h, norm_mix[0], lru_w_in[0].astype(BF16), lru_conv_w[0], lru_conv_b[0],
                                           w_ax, lru_b_a[0], lru_b_x[0], lru_lambda[0], lru_w_out[0].astype(BF16),
                                           ffn_w_gate[0], ffn_w_up[0], ffn_w_down[0])
    h, moe_wg, moe_wu, moe_wd = _dense_ffn(h, norm_ffn[0], ffn_wg, ffn_wu, ffn_wd,
                                           moe_w_gate[0], moe_w_up[0], moe_w_down[0])

    q_dim = N_HEADS * HEAD_DIM
    kv_dim = N_KV_HEADS * HEAD_DIM

    def dup_heads(w):
        w4 = w.reshape(w.shape[:-1] + (N_KV_HEADS, 1, HEAD_DIM))
        return jnp.broadcast_to(w4, w.shape[:-1] + (N_KV_HEADS, 2, HEAD_DIM)).reshape(w.shape[:-1] + (KV_EXT,))

    wqkv, bqkv = attn_w_qkv[0], attn_b_qkv[0]
    w_qkv_ext = jnp.concatenate([wqkv[:, :q_dim], dup_heads(wqkv[:, q_dim:q_dim + kv_dim]),
                                 dup_heads(wqkv[:, q_dim + kv_dim:])], axis=1).astype(BF16)
    b_qkv_ext = jnp.concatenate([bqkv[:q_dim], dup_heads(bqkv[q_dim:q_dim + kv_dim]),
                                 dup_heads(bqkv[q_dim + kv_dim:])]).reshape(1, QKV_EXT)
    inv_freq = ROPE_THETA ** (-jnp.arange(0, ROT_DIM, 2, dtype=F32) / ROT_DIM)
    d_in_head = jnp.arange(LANES) % HEAD_DIM
    freq_lane = jnp.where(d_in_head < ROT_DIM, inv_freq[d_in_head % (ROT_DIM // 2)], 0.0).reshape(1, LANES)
    sign_lane = jnp.where(d_in_head < ROT_DIM // 2, -1.0, jnp.where(d_in_head < ROT_DIM, 1.0, 0.0))
    sign_lane = sign_lane.astype(F32).reshape(1, LANES)
    w_router_pad = jnp.pad(moe_w_router[0], ((0, 0), (0, LANES - N_EXPERTS))).astype(BF16)
    pos_f = positions.reshape(S, 1).astype(F32)
    h1_tm, route = _attn_mixer(h, pos_f, norm_mix[1], freq_lane, sign_lane, w_qkv_ext, b_qkv_ext,
                               attn_sinks[0], attn_w_o[0].astype(BF16), attn_b_o[0], norm_ffn[1], w_router_pad)

    texp, n_valid, src_rows, dst_rows = _moe_plan(route, S)
    y2_tm = _moe_experts(h1_tm, norm_ffn[1], texp, n_valid, src_rows, dst_rows,
                         moe_wg, moe_wu, moe_wd,
                         TOP_K * S + MOE_TILE)
    out = _combine(h1_tm, y2_tm, route, norm_final)
    return out.reshape(B, S, D)
```

```python
import functools

import jax
import jax.numpy as jnp
from jax import lax
from jax.experimental import pallas as pl
from jax.experimental.pallas import tpu as pltpu

F32 = jnp.float32
BF16 = jnp.bfloat16

D_MODEL = 1024
N_RNN_BLOCKS = 8
RNN_BLOCK = D_MODEL // N_RNN_BLOCKS
CONV_WIDTH = 4
LRU_C = 8.0
N_HEADS = 16
N_KV_HEADS = 4
HEAD_DIM = 64
GROUP = N_HEADS // N_KV_HEADS
WINDOW = 128
ATTN_BLOCK = 128
ROPE_THETA = 500000.0
ROT_DIM = HEAD_DIM // 4
D_FF = 3584
N_EXPERTS = 8
TOP_K = 2
EPS = 1e-6
NEG_INF = -1e30
LOG2_E = 1.4426950408889634

LANES = 128
SUBLANES = 8
VMEM_LIMIT = 56 * 1024 * 1024

SEQ_TILE = 512
LRU_CHUNKS = 4
FFN_TILE = 1024
FF_CHUNK = 512
N_FF_CHUNKS = D_FF // FF_CHUNK
MOE_TILE = 512
N_MOE_CHUNKS = 2
OUT_TILE = 1024


def _rmsnorm(x, g):
    return x * lax.rsqrt(jnp.mean(x * x, axis=-1, keepdims=True) + EPS) * g


ROWS_PER_TOKEN = D_MODEL // LANES


def _store_token_major(ref, val):
    T = val.shape[0]
    for c in range(ROWS_PER_TOKEN):
        ref[pl.ds(c, T, stride=ROWS_PER_TOKEN), :] = val[:, LANES * c:LANES * (c + 1)]


def _load_token_major(ref, T):
    return jnp.concatenate([ref[pl.ds(c, T, stride=ROWS_PER_TOKEN), :] for c in range(ROWS_PER_TOKEN)], axis=1)


def _const_spec(shape):
    n = len(shape)
    return pl.BlockSpec(shape, lambda *_: (0,) * n, pipeline_mode=pl.Buffered(1))


def _time_permutation(T):
    rho = jnp.arange(T)
    t_of_rho = (rho % SUBLANES) * (T // SUBLANES) + rho // SUBLANES
    perm = (t_of_rho[:, None] == jnp.arange(T)[None, :]).astype(BF16)
    return perm, perm.T


def _lru_kernel(xin_ref, xres_ref, g_ref, perm_ref, unperm_ref, win_ref, cw_ref, cb_ref, wax_ref, ba_ref, bx_ref,
                lam_ref, wout_ref, cast_g_ref, cast_u_ref, cast_d_ref,
                o_ref, bf_g_ref, bf_u_ref, bf_d_ref, proj_scr, tail_scr, carry_scr):
    i = pl.program_id(0)
    n_tiles = pl.num_programs(0) - 1
    T = xin_ref.shape[0]
    D = D_MODEL
    C = T // LRU_CHUNKS
    GC = C // SUBLANES
    G = T // SUBLANES
    HALO = (CONV_WIDTH - 1) * SUBLANES

    def cast_slices():
        bf_g_ref[...] = cast_g_ref[...].astype(BF16)
        bf_u_ref[...] = cast_u_ref[...].astype(BF16)
        bf_d_ref[...] = cast_d_ref[...].astype(BF16)

    @pl.when(i == 0)
    def _():
        tail_scr[...] = jnp.zeros_like(tail_scr)
        carry_scr[...] = jnp.zeros_like(carry_scr)

    PW = 2 * D // LRU_CHUNKS

    def project_chunks():
        hn = _rmsnorm(xin_ref[...], g_ref[...]).astype(BF16)
        hp = jnp.dot(perm_ref[...], hn, preferred_element_type=F32).astype(BF16)
        for k in range(LRU_CHUNKS):
            yield jnp.dot(hp, win_ref[:, PW * k:PW * (k + 1)], preferred_element_type=F32)

    def recur_chunks():
        sub = lax.broadcasted_iota(jnp.int32, (SUBLANES, D), 0)
        cw = cw_ref[...]
        softplus_neg_lam = jax.nn.softplus(-lam_ref[...])
        last = proj_scr[T - HALO:T, D:]
        prev_groups = []
        for k in range(CONV_WIDTH - 1):
            rows = slice(SUBLANES * k, SUBLANES * (k + 1))
            prev_groups.append(pltpu.roll(jnp.where(sub == SUBLANES - 1, tail_scr[rows, :], last[rows, :]), 1, 0))
        tail_scr[...] = last

        ys, hs, decay = [], [], []
        for k in range(LRU_CHUNKS):
            pc = proj_scr[C * k:C * (k + 1), :]
            ys.append(jax.nn.gelu(pc[:, :D], approximate=True))
            xb = pc[:, D:]
            xc = cb_ref[...] + xb * cw[CONV_WIDTH - 1:CONV_WIDTH, :]
            for back in range(1, CONV_WIDTH):
                shifted = jnp.concatenate(prev_groups[CONV_WIDTH - 1 - back:] + [xb[:C - SUBLANES * back, :]], axis=0)
                xc = xc + shifted * cw[CONV_WIDTH - 1 - back:CONV_WIDTH - back, :]
            prev_groups = [xb[C - HALO + SUBLANES * q:C - HALO + SUBLANES * (q + 1), :] for q in range(CONV_WIDTH - 1)]

            xcb = xc.astype(BF16)
            r_parts, i_parts = [], []
            for n in range(N_RNN_BLOCKS):
                gn = jnp.dot(xcb[:, RNN_BLOCK * n:RNN_BLOCK * (n + 1)], wax_ref[n], preferred_element_type=F32)
                r_parts.append(gn[:, :RNN_BLOCK])
                i_parts.append(gn[:, RNN_BLOCK:])
            r = jax.nn.sigmoid(jnp.concatenate(r_parts, axis=1) + ba_ref[...])
            ig = jax.nn.sigmoid(jnp.concatenate(i_parts, axis=1) + bx_ref[...])
            log_a = -LRU_C * r * softplus_neg_lam
            a = jnp.exp(log_a)
            z = jnp.tanh(-log_a) * (1.0 + a * a)
            b = jnp.where(z > 0.0, z * lax.rsqrt(z), 0.0) * (ig * xc)

            for j in range(GC):
                rows = slice(SUBLANES * j, SUBLANES * (j + 1))
                if hs:
                    hs.append(a[rows, :] * hs[-1] + b[rows, :])
                    decay.append(a[rows, :] * decay[-1])
                else:
                    hs.append(b[rows, :])
                    decay.append(a[rows, :])
            yield None

        p_inc, e_inc = decay[-1], hs[-1]
        for k in (1, 2, 4):
            m = sub >= k
            p_sh = jnp.where(m, pltpu.roll(p_inc, k, 0), 1.0)
            e_sh = jnp.where(m, pltpu.roll(e_inc, k, 0), 0.0)
            e_inc = e_inc + p_inc * e_sh
            p_inc = p_inc * p_sh
        h0 = carry_scr[...]
        after = p_inc * h0 + e_inc
        carry_scr[...] = after[SUBLANES - 1:SUBLANES, :]
        enter = jnp.where(sub == 0, h0, pltpu.roll(after, 1, 0))
        h = jnp.concatenate([hs[j] + decay[j] * enter for j in range(G)], axis=0)
        hy = (h * jnp.concatenate(ys, axis=0)).astype(BF16)
        hy = jnp.dot(unperm_ref[...], hy, preferred_element_type=F32).astype(BF16)
        yield jnp.dot(hy, wout_ref[...], preferred_element_type=F32) + xres_ref[...]

    @pl.when(i == 0)
    def _():
        cast_slices()
        for k, pc in enumerate(project_chunks()):
            proj_scr[:, PW * k:PW * (k + 1)] = pc

    @pl.when((i > 0) & (i < n_tiles))
    def _():
        cast_slices()
        rec = recur_chunks()
        new_proj = []
        for pc in project_chunks():
            new_proj.append(pc)
            next(rec)
        o_ref[...] = next(rec)
        for k, pc in enumerate(new_proj):
            proj_scr[:, PW * k:PW * (k + 1)] = pc

    @pl.when(i == n_tiles)
    def _():
        o_ref[...] = list(recur_chunks())[-1]


def _lru_mixer(x, g, w_in, conv_w, conv_b, w_ax, b_a, b_x, lam, w_out, cast_gate, cast_up, cast_down):
    S, D = x.shape
    T = SEQ_TILE
    n_tiles = S // T
    row = lambda v: v.reshape(1, D)
    perm, unperm = _time_permutation(T)

    def cast_spec(w):
        return pl.BlockSpec((w.shape[0] // n_tiles, w.shape[1]), lambda i: (jnp.minimum(i, n_tiles - 1), 0))

    casts = (cast_gate, cast_up, cast_down)
    return pl.pallas_call(
        _lru_kernel,
        grid=(n_tiles + 1,),
        in_specs=[
            pl.BlockSpec((T, D), lambda i: (jnp.minimum(i, n_tiles - 1), 0)),
            pl.BlockSpec((T, D), lambda i: (jnp.maximum(i - 1, 0), 0)),
            _const_spec((1, D)),
            _const_spec((T, T)),
            _const_spec((T, T)),
            _const_spec((D, 2 * D)),
            _const_spec((CONV_WIDTH, D)),
            _const_spec((1, D)),
            _const_spec((N_RNN_BLOCKS, RNN_BLOCK, 2 * RNN_BLOCK)),
            _const_spec((1, D)),
            _const_spec((1, D)),
            _const_spec((1, D)),
            _const_spec((D, D)),
        ] + [cast_spec(w) for w in casts],
        out_specs=[pl.BlockSpec((T, D), lambda i: (jnp.maximum(i - 1, 0), 0))] + [cast_spec(w) for w in casts],
        out_shape=[jax.ShapeDtypeStruct((S, D), F32)] + [jax.ShapeDtypeStruct(w.shape, BF16) for w in casts],
        scratch_shapes=[pltpu.VMEM((T, 2 * D), F32), pltpu.VMEM(((CONV_WIDTH - 1) * SUBLANES, D), F32),
                        pltpu.VMEM((1, D), F32)],
        compiler_params=pltpu.CompilerParams(dimension_semantics=("arbitrary",), vmem_limit_bytes=VMEM_LIMIT),
        name="lru_mixer",
    )(x, x, row(g), perm, unperm, w_in, conv_w, row(conv_b), w_ax, row(b_a), row(b_x), row(lam), w_out, *casts)


def _ffn_kernel(x_ref, g_ref, wg_ref, wu_ref, wd_ref, cast_g_ref, cast_u_ref, cast_d_ref,
                o_ref, bf_g_ref, bf_u_ref, bf_d_ref, xn_scr, acc_scr):
    c = pl.program_id(1)
    last = pl.num_programs(1) - 1

    def ff_chunk():
        bf_g_ref[...] = cast_g_ref[...].astype(BF16)
        bf_u_ref[...] = cast_u_ref[...].astype(BF16)
        bf_d_ref[...] = cast_d_ref[...].astype(BF16)
        xn = xn_scr[...]
        gate = jnp.dot(xn, wg_ref[...], preferred_element_type=F32)
        up = jnp.dot(xn, wu_ref[...], preferred_element_type=F32)
        mid = (jax.nn.silu(gate) * up).astype(BF16)
        return jnp.dot(mid, wd_ref[...], preferred_element_type=F32)

    @pl.when(c == 0)
    def _():
        xn_scr[...] = _rmsnorm(x_ref[...], g_ref[...]).astype(BF16)
        acc_scr[...] = ff_chunk()

    @pl.when((c > 0) & (c < last))
    def _():
        acc_scr[...] += ff_chunk()

    @pl.when(c == last)
    def _():
        o_ref[...] = x_ref[...] + (acc_scr[...] + ff_chunk())


def _dense_ffn(x, g, w_gate, w_up, w_down, cast_gate, cast_up, cast_down):
    S, D = x.shape
    T = FFN_TILE
    n_steps = (S // T) * N_FF_CHUNKS
    E = cast_gate.shape[0]
    row_parts = n_steps // (E * N_FF_CHUNKS)
    assert row_parts * E * N_FF_CHUNKS == n_steps
    gu_block = (1, D // row_parts, D_FF // N_FF_CHUNKS)
    d_block = (1, D_FF // (row_parts * N_FF_CHUNKS), D)

    def gu_map(i, c):
        s = i * N_FF_CHUNKS + c
        return (s // (row_parts * N_FF_CHUNKS), (s // N_FF_CHUNKS) % row_parts, c)

    def d_map(i, c):
        s = i * N_FF_CHUNKS + c
        return (s // (row_parts * N_FF_CHUNKS), s % (row_parts * N_FF_CHUNKS), 0)

    return pl.pallas_call(
        _ffn_kernel,
        grid=(S // T, N_FF_CHUNKS),
        in_specs=[
            pl.BlockSpec((T, D), lambda i, c: (i, 0)),
            pl.BlockSpec((1, D), lambda i, c: (0, 0)),
            pl.BlockSpec((D, FF_CHUNK), lambda i, c: (0, c)),
            pl.BlockSpec((D, FF_CHUNK), lambda i, c: (0, c)),
            pl.BlockSpec((FF_CHUNK, D), lambda i, c: (c, 0)),
            pl.BlockSpec(gu_block, gu_map),
            pl.BlockSpec(gu_block, gu_map),
            pl.BlockSpec(d_block, d_map),
        ],
        out_specs=[pl.BlockSpec((T, D), lambda i, c: (i, 0)), pl.BlockSpec(gu_block, gu_map),
                   pl.BlockSpec(gu_block, gu_map), pl.BlockSpec(d_block, d_map)],
        out_shape=[jax.ShapeDtypeStruct((S, D), F32), jax.ShapeDtypeStruct(cast_gate.shape, BF16),
                   jax.ShapeDtypeStruct(cast_up.shape, BF16), jax.ShapeDtypeStruct(cast_down.shape, BF16)],
        scratch_shapes=[pltpu.VMEM((T, D), BF16), pltpu.VMEM((T, D), F32)],
        compiler_params=pltpu.CompilerParams(dimension_semantics=("arbitrary", "arbitrary"),
                                             vmem_limit_bytes=VMEM_LIMIT),
        name="dense_ffn",
    )(x, g.reshape(1, D), w_gate, w_up, w_down, cast_gate, cast_up, cast_down)


KV_EXT = N_KV_HEADS * LANES
QKV_EXT = N_HEADS * HEAD_DIM + 2 * KV_EXT


def _attn_kernel(sink_ref, x_ref, pos_ref, g_ref, freq_ref, sign_ref, wqkv_ref, bqkv_ref, wo_ref, bo_ref,
                 g2_ref, wr_ref, o_ref, route_ref,
                 qlo_scr, qhi_scr, k_scr, v_scr, o_scr):
    T = x_ref.shape[0]
    D = D_MODEL
    B = ATTN_BLOCK
    first_tile = pl.program_id(0) == 0

    @pl.when(first_tile)
    def _():
        k_scr[0:B, :] = jnp.zeros((B, KV_EXT), BF16)
        v_scr[0:B, :] = jnp.zeros((B, KV_EXT), BF16)

    x = x_ref[...]
    hn = _rmsnorm(x, g_ref[...]).astype(BF16)
    qkv = jnp.dot(hn, wqkv_ref[...], preferred_element_type=F32) + bqkv_ref[...]

    ang = pos_ref[...] * freq_ref[...]
    cos_t = jnp.cos(ang)
    sin_t = jnp.sin(ang) * sign_ref[...]
    lane = lax.broadcasted_iota(jnp.int32, (T, LANES), 1)
    first_half = (lane % HEAD_DIM) < (ROT_DIM // 2)
    lo_half = lane < HEAD_DIM

    def rope(col):
        partner = jnp.where(first_half, pltpu.roll(col, LANES - ROT_DIM // 2, 1), pltpu.roll(col, ROT_DIM // 2, 1))
        return col * cos_t + partner * sin_t

    scale = HEAD_DIM ** -0.5 * LOG2_E
    for c in range(D // LANES):
        qc = rope(qkv[:, LANES * c:LANES * (c + 1)]) * scale
        qlo_scr[:, LANES * c:LANES * (c + 1)] = jnp.where(lo_half, qc, 0.0).astype(BF16)
        qhi_scr[:, LANES * c:LANES * (c + 1)] = jnp.where(lo_half, 0.0, qc).astype(BF16)
    for g in range(N_KV_HEADS):
        kc = rope(qkv[:, D + LANES * g:D + LANES * (g + 1)])
        k_scr[B:B + T, LANES * g:LANES * (g + 1)] = kc.astype(BF16)
        vc = qkv[:, D + KV_EXT + LANES * g:D + KV_EXT + LANES * (g + 1)]
        v_scr[B:B + T, LANES * g:LANES * (g + 1)] = vc.astype(BF16)

    qi = lax.broadcasted_iota(jnp.int32, (B, B), 0)
    kj = lax.broadcasted_iota(jnp.int32, (B, B), 1)
    causal_own = kj <= qi
    window_prev = kj > qi
    key0 = kj == 0
    key0_row = lax.broadcasted_iota(jnp.int32, (1, B), 1) == 0
    lo_blk = lax.broadcasted_iota(jnp.int32, (B, LANES), 1) < HEAD_DIM

    def block_body(blk, _):
        r0 = pl.multiple_of(blk * B, B)
        k_lo = jnp.where(first_tile & (blk == 0), B, 0)
        allowed_prev = window_prev & (kj >= k_lo)
        for g in range(N_KV_HEADS):
            kk = k_scr[pl.ds(r0, 2 * B), LANES * g:LANES * (g + 1)]
            vv = v_scr[pl.ds(r0, 2 * B), LANES * g:LANES * (g + 1)]
            cols = [slice(LANES * c, LANES * (c + 1)) for c in range((GROUP // 2) * g, (GROUP // 2) * (g + 1))]
            q_all = jnp.concatenate([q_scr[pl.ds(r0, B), col] for col in cols for q_scr in (qlo_scr, qhi_scr)], axis=0)
            s_all = lax.dot_general(q_all, kk, (((1,), (1,)), ((), ())), preferred_element_type=F32)
            p_all, denoms = [], []
            for hh in range(GROUP):
                rows = slice(B * hh, B * (hh + 1))
                sink_fill = jnp.where(key0_row, sink_ref[GROUP * g + hh] * LOG2_E, NEG_INF)
                s_prev = jnp.where(allowed_prev, s_all[rows, :B], sink_fill)
                s_own = jnp.where(causal_own, s_all[rows, B:], NEG_INF)
                m = jnp.max(jnp.maximum(s_prev, s_own), axis=-1, keepdims=True)
                p_prev = jnp.exp2(s_prev - m)
                p_own = jnp.exp2(s_own - m)
                denoms.append(jnp.sum(p_prev + p_own, axis=-1, keepdims=True))
                p_all.append(jnp.concatenate([jnp.where(key0, 0.0, p_prev), p_own], axis=1).astype(BF16))
            o_all = jnp.dot(jnp.concatenate(p_all, axis=0), vv, preferred_element_type=F32)
            o_heads = [o_all[B * hh:B * (hh + 1), :] / denoms[hh] for hh in range(GROUP)]
            for p, col in enumerate(cols):
                o_scr[pl.ds(r0, B), col] = jnp.where(lo_blk, o_heads[2 * p], o_heads[2 * p + 1]).astype(BF16)
        return 0

    lax.fori_loop(0, T // B, block_body, 0)

    k_scr[0:B, :] = k_scr[T:T + B, :]
    v_scr[0:B, :] = v_scr[T:T + B, :]

    h1 = jnp.dot(o_scr[...], wo_ref[...], preferred_element_type=F32) + bo_ref[...] + x
    _store_token_major(o_ref, h1)

    hn2 = _rmsnorm(h1, g2_ref[...]).astype(BF16)
    logits = jnp.dot(hn2, wr_ref[...], preferred_element_type=F32)
    lane_f = lane.astype(F32)
    lg = jnp.where(lane < N_EXPERTS, logits, -jnp.inf)
    v1 = jnp.max(lg, axis=-1, keepdims=True)
    i1 = jnp.min(jnp.where(lg == v1, lane_f, float(LANES)), axis=-1, keepdims=True)
    lg2 = jnp.where(lane_f == i1, -jnp.inf, lg)
    v2 = jnp.max(lg2, axis=-1, keepdims=True)
    i2 = jnp.min(jnp.where(lg2 == v2, lane_f, float(LANES)), axis=-1, keepdims=True)
    e2 = jnp.exp(v2 - v1)
    gate1 = 1.0 / (1.0 + e2)
    gate2 = e2 / (1.0 + e2)
    route_ref[...] = jnp.where(lane == 0, i1, jnp.where(lane == 1, i2,
                                                        jnp.where(lane == 2, gate1, jnp.where(lane == 3, gate2, 0.0))))


def _attn_mixer(x, pos_f, g, freq_lane, sign_lane, w_qkv_ext, b_qkv_ext, sinks, w_o, b_o, g2, w_router_pad):
    S, D = x.shape
    T = SEQ_TILE
    B = ATTN_BLOCK
    grid_spec = pltpu.PrefetchScalarGridSpec(
        num_scalar_prefetch=1,
        grid=(S // T,),
        in_specs=[
            pl.BlockSpec((T, D), lambda i, s: (i, 0)),
            pl.BlockSpec((T, 1), lambda i, s: (i, 0)),
            _const_spec((1, D)),
            _const_spec((1, LANES)),
            _const_spec((1, LANES)),
            _const_spec((D, QKV_EXT)),
            _const_spec((1, QKV_EXT)),
            _const_spec((D, D)),
            _const_spec((1, D)),
            _const_spec((1, D)),
            _const_spec((D, LANES)),
        ],
        out_specs=[pl.BlockSpec((T * ROWS_PER_TOKEN, LANES), lambda i, s: (i, 0)),
                   pl.BlockSpec((T, LANES), lambda i, s: (i, 0))],
        scratch_shapes=[
            pltpu.VMEM((T, D), BF16), pltpu.VMEM((T, D), BF16),
            pltpu.VMEM((T + B, KV_EXT), BF16), pltpu.VMEM((T + B, KV_EXT), BF16),
            pltpu.VMEM((T, D), BF16),
        ],
    )
    return pl.pallas_call(
        _attn_kernel,
        grid_spec=grid_spec,
        out_shape=[jax.ShapeDtypeStruct((S * ROWS_PER_TOKEN, LANES), F32), jax.ShapeDtypeStruct((S, LANES), F32)],
        compiler_params=pltpu.CompilerParams(dimension_semantics=("arbitrary",), vmem_limit_bytes=VMEM_LIMIT),
        name="attn_mixer",
    )(sinks, x, pos_f, g.reshape(1, D), freq_lane, sign_lane, w_qkv_ext, b_qkv_ext, w_o, b_o.reshape(1, D),
      g2.reshape(1, D), w_router_pad)


def _moe_kernel(texp_ref, nvt_ref, src0_ref, srcn_ref, dstp_ref, dstl_ref, h_hbm, g_ref, wg_ref, wu_ref, wd_ref,
                y_hbm, x_scr, xn_scr, acc_scr, y_scr, sems):
    i = pl.program_id(0)
    c = pl.program_id(1)
    n_tiles = pl.num_programs(0)
    n_chunks = pl.num_programs(1)
    T = xn_scr.shape[0]
    R = ROWS_PER_TOKEN
    valid = i < nvt_ref[0]
    slot = i % 2
    other = 1 - slot
    SCATTER_SEM = 2

    def row_copy(src, src_row, dst, dst_row, sem):
        return pltpu.make_async_copy(src.at[pl.ds(pl.multiple_of(src_row, R), R), :],
                                     dst.at[pl.ds(pl.multiple_of(dst_row, R), R), :], sem)

    def tile_rows(t):
        return nvt_ref[1 + t]

    def window_offset(t):
        return nvt_ref[1 + 2 * n_tiles + t]

    prev_tile = jnp.maximum(i - 1, 0)
    next_tile = jnp.minimum(i + 1, n_tiles - 1)
    rows_prev = jnp.where(i > 0, tile_rows(prev_tile), 0)
    pad_base = y_hbm.shape[0] - T * R

    def dest_row(window_ref, offset, r, n_rows):
        return jnp.where(r < n_rows, window_ref[0, offset + r], pad_base + r * R)

    def wait_gather(s):
        pltpu.make_async_copy(h_hbm.at[pl.ds(0, T * R), :], x_scr.at[s], sems.at[s]).wait()

    def wait_scatter():
        pltpu.make_async_copy(y_scr, y_hbm.at[pl.ds(0, T * R), :], sems.at[SCATTER_SEM]).wait()

    @pl.when((i == 0) & (c == 0))
    def _():
        y_scr[...] = jnp.zeros_like(y_scr)
        acc_scr[...] = jnp.zeros_like(acc_scr)

        def issue(r, carry):
            row_copy(h_hbm, src0_ref[0, window_offset(0) + r], x_scr.at[0], r * R, sems.at[0]).start()
            return carry
        lax.fori_loop(0, T, issue, 0)

    @pl.when(c == 0)
    def _():
        wait_gather(slot)

    @pl.when(valid & (c == 0))
    def _():
        xn_scr[...] = _rmsnorm(_load_token_major(x_scr.at[slot], T), g_ref[...]).astype(BF16)

    def row_dmas():
        off_next = window_offset(next_tile)
        off_prev = window_offset(prev_tile)
        for r in range(T):
            row_copy(h_hbm, srcn_ref[0, off_next + r], x_scr.at[other], r * R, sems.at[other]).start()
            row_copy(y_scr, r * R, y_hbm, dest_row(dstp_ref, off_prev, r, rows_prev),
                     sems.at[SCATTER_SEM]).start()

    def expert_chunk(first, rows):
        xn = xn_scr[0:rows, :]
        gate = jnp.dot(xn, wg_ref[0], preferred_element_type=F32)
        up = jnp.dot(xn, wu_ref[0], preferred_element_type=F32)
        mid = (jax.nn.silu(gate) * up).astype(BF16)
        down = jnp.dot(mid, wd_ref[0], preferred_element_type=F32)
        if first:
            acc_scr[0:rows, :] = down
        else:
            acc_scr[0:rows, :] += down

    few_rows = tile_rows(i) <= T // 2

    for few, rows in ((False, T), (True, T // 2)):
        branch = valid & (few_rows if few else jnp.logical_not(few_rows))

        @pl.when(branch & (c == 0))
        def _():
            row_dmas()
            expert_chunk(True, rows)

        @pl.when(branch & (c != 0))
        def _():
            expert_chunk(False, rows)

    @pl.when(jnp.logical_not(valid) & (c == 0))
    def _():
        row_dmas()

    @pl.when(c == n_chunks - 1)
    def _():
        wait_scatter()

    @pl.when(valid & (c == n_chunks - 1))
    def _():
        _store_token_major(y_scr, acc_scr[...])

    @pl.when((i == n_tiles - 1) & (c == n_chunks - 1))
    def _():
        wait_gather(other)


        def issue(r, carry):
            row_copy(y_scr, r * R, y_hbm, dest_row(dstl_ref, window_offset(n_tiles - 1), r, tile_rows(n_tiles - 1)),
                     sems.at[SCATTER_SEM]).start()
            return carry
        lax.fori_loop(0, T, issue, 0)
        wait_scatter()


def _moe_experts(h1_tm, g, tile_expert, n_valid_tiles, src_rows, dst_rows, w_gate, w_up, w_down, n_out_tokens):
    D = D_MODEL
    T = MOE_TILE
    NT = tile_expert.shape[0]
    NC = N_MOE_CHUNKS
    FC = D_FF // NC

    def chunk(i, c, nvt):
        return jnp.where(i < nvt[0], c, NC - 1)

    def window_spec(tile_of_step):
        return pl.BlockSpec((pl.Element(1), pl.Element(T + LANES)),
                            lambda i, c, te, nvt: (0, nvt[1 + NT + tile_of_step(i)] * LANES),
                            memory_space=pltpu.SMEM)

    grid_spec = pltpu.PrefetchScalarGridSpec(
        num_scalar_prefetch=2,
        grid=(NT, NC),
        in_specs=[
            window_spec(lambda i: 0),
            window_spec(lambda i: jnp.minimum(i + 1, NT - 1)),
            window_spec(lambda i: jnp.maximum(i - 1, 0)),
            window_spec(lambda i: NT - 1),
            pl.BlockSpec(memory_space=pl.ANY),
            pl.BlockSpec((1, D), lambda i, c, te, nvt: (0, 0)),
            pl.BlockSpec((1, D, FC), lambda i, c, te, nvt: (te[i], 0, chunk(i, c, nvt))),
            pl.BlockSpec((1, D, FC), lambda i, c, te, nvt: (te[i], 0, chunk(i, c, nvt))),
            pl.BlockSpec((1, FC, D), lambda i, c, te, nvt: (te[i], chunk(i, c, nvt), 0)),
        ],
        out_specs=pl.BlockSpec(memory_space=pl.ANY),
        scratch_shapes=[pltpu.VMEM((2, T * ROWS_PER_TOKEN, LANES), F32), pltpu.VMEM((T, D), BF16),
                        pltpu.VMEM((T, D), F32), pltpu.VMEM((T * ROWS_PER_TOKEN, LANES), F32),
                        pltpu.SemaphoreType.DMA((3,))],
    )
    return pl.pallas_call(
        _moe_kernel,
        grid_spec=grid_spec,
        out_shape=jax.ShapeDtypeStruct((n_out_tokens * ROWS_PER_TOKEN, LANES), F32),
        compiler_params=pltpu.CompilerParams(dimension_semantics=("arbitrary", "arbitrary"),
                                             vmem_limit_bytes=VMEM_LIMIT),
        name="moe_experts",
    )(tile_expert, n_valid_tiles, src_rows, src_rows, dst_rows, dst_rows, h1_tm, g.reshape(1, D),
      w_gate, w_up, w_down)


def _moe_plan(route, n_tokens):
    T = MOE_TILE
    n_assign = TOP_K * n_tokens
    n_tiles = n_assign // T + N_EXPERTS
    e_flat = route[:, 0:TOP_K].astype(jnp.int32).reshape(-1)
    counts = jnp.sum((e_flat[:, None] == jnp.arange(N_EXPERTS, dtype=jnp.int32)[None, :]).astype(jnp.int32), axis=0)
    order = jnp.argsort(e_flat, stable=True).astype(jnp.int32)
    tiles_per = (counts + T - 1) // T
    tile_end = jnp.cumsum(tiles_per)
    n_valid = tile_end[-1]
    tile_ids = jnp.arange(n_tiles, dtype=jnp.int32)
    last_valid = jnp.maximum(n_valid - 1, 0)
    texp = jnp.sum((jnp.minimum(tile_ids, last_valid)[:, None] >= tile_end[None, :]).astype(jnp.int32), axis=1)
    texp = jnp.minimum(texp, N_EXPERTS - 1)
    onehot = (texp[:, None] == jnp.arange(N_EXPERTS, dtype=jnp.int32)[None, :]).astype(jnp.int32)
    pick = lambda v: jnp.sum(onehot * v[None, :], axis=1)
    j = tile_ids - pick(tile_end - tiles_per)
    start = jnp.where(tile_ids < n_valid, pick(jnp.cumsum(counts) - counts) + j * T, 0)
    n_rows = jnp.where(tile_ids < n_valid, jnp.clip(pick(counts) - j * T, 0, T), 0)
    token = order // TOP_K
    slot = order % TOP_K
    spare = jnp.zeros((T + LANES,), jnp.int32)
    src_sorted = jnp.concatenate([token * ROWS_PER_TOKEN, spare]).reshape(1, -1)
    dst_sorted = jnp.concatenate([(slot * n_tokens + token) * ROWS_PER_TOKEN, spare]).reshape(1, -1)
    tile_counts = jnp.concatenate([n_valid.reshape(1), n_rows, start // LANES, start % LANES]).astype(jnp.int32)
    return texp, tile_counts, src_sorted, dst_sorted


def _combine_kernel(h_ref, y0_ref, y1_ref, route_ref, g_ref, o_ref):
    T = o_ref.shape[0]
    route = route_ref[...]
    moe = route[:, 2:3] * _load_token_major(y0_ref, T) + route[:, 3:4] * _load_token_major(y1_ref, T)
    o_ref[...] = _rmsnorm(_load_token_major(h_ref, T) + moe, g_ref[...])


def _combine(h1_tm, y2_tm, route, g):
    D = D_MODEL
    S = route.shape[0]
    T = OUT_TILE
    nb = S // T
    tm_block = (T * ROWS_PER_TOKEN, LANES)
    return pl.pallas_call(
        _combine_kernel,
        grid=(nb,),
        in_specs=[
            pl.BlockSpec(tm_block, lambda i: (i, 0)),
            pl.BlockSpec(tm_block, lambda i: (i, 0)),
            pl.BlockSpec(tm_block, lambda i: (i + nb, 0)),
            pl.BlockSpec((T, LANES), lambda i: (i, 0)),
            pl.BlockSpec((1, D), lambda i: (0, 0)),
        ],
        out_specs=pl.BlockSpec((T, D), lambda i: (i, 0)),
        out_shape=jax.ShapeDtypeStruct((S, D), F32),
        compiler_params=pltpu.CompilerParams(dimension_semantics=("arbitrary",), vmem_limit_bytes=VMEM_LIMIT),
        name="moe_combine",
    )(h1_tm, y2_tm, y2_tm, route, g.reshape(1, D))


def kernel(x, positions, norm_mix, norm_ffn, norm_final, lru_w_in, lru_conv_w, lru_conv_b, lru_w_a, lru_b_a, lru_w_x, lru_b_x, lru_lambda, lru_w_out, attn_w_qkv, attn_b_qkv, attn_sinks, attn_w_o, attn_b_o, ffn_w_gate, ffn_w_up, ffn_w_down, moe_w_router, moe_w_gate, moe_w_up, moe_w_down):
    B, S, D = x.shape
    assert B == 1 and D == D_MODEL and S % FFN_TILE == 0
    h = x.reshape(S, D)

    w_ax = jnp.concatenate([lru_w_a[0], lru_w_x[0]], axis=-1).astype(BF16)
    h, ffn_wg, ffn_wu, ffn_wd = _lru_mixer(h, norm_mix[0], lru_w_in[0].astype(BF16), lru_conv_w[0], lru_conv_b[0],
                                           w_ax, lru_b_a[0], lru_b_x[0], lru_lambda[0], lru_w_out[0].astype(BF16),
                                           ffn_w_gate[0], ffn_w_up[0], ffn_w_down[0])
    h, moe_wg, moe_wu, moe_wd = _dense_ffn(h, norm_ffn[0], ffn_wg, ffn_wu, ffn_wd,
                                           moe_w_gate[0], moe_w_up[0], moe_w_down[0])

    q_dim = N_HEADS * HEAD_DIM
    kv_dim = N_KV_HEADS * HEAD_DIM

    def dup_heads(w):
        w4 = w.reshape(w.shape[:-1] + (N_KV_HEADS, 1, HEAD_DIM))
        return jnp.broadcast_to(w4, w.shape[:-1] + (N_KV_HEADS, 2, HEAD_DIM)).reshape(w.shape[:-1] + (KV_EXT,))

    wqkv, bqkv = attn_w_qkv[0], attn_b_qkv[0]
    w_qkv_ext = jnp.concatenate([wqkv[:, :q_dim], dup_heads(wqkv[:, q_dim:q_dim + kv_dim]),
                                 dup_heads(wqkv[:, q_dim + kv_dim:])], axis=1).astype(BF16)
    b_qkv_ext = jnp.concatenate([bqkv[:q_dim], dup_heads(bqkv[q_dim:q_dim + kv_dim]),
                                 dup_heads(bqkv[q_dim + kv_dim:])]).reshape(1, QKV_EXT)
    inv_freq = ROPE_THETA ** (-jnp.arange(0, ROT_DIM, 2, dtype=F32) / ROT_DIM)
    d_in_head = jnp.arange(LANES) % HEAD_DIM
    freq_lane = jnp.where(d_in_head < ROT_DIM, inv_freq[d_in_head % (ROT_DIM // 2)], 0.0).reshape(1, LANES)
    sign_lane = jnp.where(d_in_head < ROT_DIM // 2, -1.0, jnp.where(d_in_head < ROT_DIM, 1.0, 0.0))
    sign_lane = sign_lane.astype(F32).reshape(1, LANES)
    w_router_pad = jnp.pad(moe_w_router[0], ((0, 0), (0, LANES - N_EXPERTS))).astype(BF16)
    pos_f = positions.reshape(S, 1).astype(F32)
    h1_tm, route = _attn_mixer(h, pos_f, norm_mix[1], freq_lane, sign_lane, w_qkv_ext, b_qkv_ext,
                               attn_sinks[0], attn_w_o[0].astype(BF16), attn_b_o[0], norm_ffn[1], w_router_pad)

    texp, n_valid, src_rows, dst_rows = _moe_plan(route, S)
    y2_tm = _moe_experts(h1_tm, norm_ffn[1], texp, n_valid, src_rows, dst_rows,
                         moe_wg, moe_wu, moe_wd,
                         TOP_K * S + MOE_TILE)
    out = _combine(h1_tm, y2_tm, route, norm_final)
    return out.reshape(B, S, D)
```

```python
import functools

import jax
import jax.numpy as jnp
from jax import lax
from jax.experimental import pallas as pl
from jax.experimental.pallas import tpu as pltpu

F32 = jnp.float32
BF16 = jnp.bfloat16

D_MODEL = 1024
N_RNN_BLOCKS = 8
RNN_BLOCK = D_MODEL // N_RNN_BLOCKS
CONV_WIDTH = 4
LRU_C = 8.0
N_HEADS = 16
N_KV_HEADS = 4
HEAD_DIM = 64
GROUP = N_HEADS // N_KV_HEADS
WINDOW = 128
ATTN_BLOCK = 128
ROPE_THETA = 500000.0
ROT_DIM = HEAD_DIM // 4
D_FF = 3584
N_EXPERTS = 8
TOP_K = 2
EPS = 1e-6
NEG_INF = -1e30
LOG2_E = 1.4426950408889634

LANES = 128
SUBLANES = 8
VMEM_LIMIT = 56 * 1024 * 1024

SEQ_TILE = 512
LRU_CHUNKS = 4
FFN_TILE = 1024
FF_CHUNK = 512
N_FF_CHUNKS = D_FF // FF_CHUNK
MOE_TILE = 512
N_MOE_CHUNKS = 2
OUT_TILE = 1024


def _rmsnorm(x, g):
    return x * lax.rsqrt(jnp.mean(x * x, axis=-1, keepdims=True) + EPS) * g


ROWS_PER_TOKEN = D_MODEL // LANES


def _store_token_major(ref, val):
    T = val.shape[0]
    for c in range(ROWS_PER_TOKEN):
        ref[pl.ds(c, T, stride=ROWS_PER_TOKEN), :] = val[:, LANES * c:LANES * (c + 1)]


def _load_token_major(ref, T):
    return jnp.concatenate([ref[pl.ds(c, T, stride=ROWS_PER_TOKEN), :] for c in range(ROWS_PER_TOKEN)], axis=1)


def _const_spec(shape):
    n = len(shape)
    return pl.BlockSpec(shape, lambda *_: (0,) * n, pipeline_mode=pl.Buffered(1))


def _time_permutation(T):
    rho = jnp.arange(T)
    t_of_rho = (rho % SUBLANES) * (T // SUBLANES) + rho // SUBLANES
    perm = (t_of_rho[:, None] == jnp.arange(T)[None, :]).astype(BF16)
    return perm, perm.T


def _lru_kernel(xin_ref, xres_ref, g_ref, perm_ref, unperm_ref, win_ref, cw_ref, cb_ref, wax_ref, ba_ref, bx_ref,
                lam_ref, wout_ref, cast_g_ref, cast_u_ref, cast_d_ref,
                o_ref, bf_g_ref, bf_u_ref, bf_d_ref, proj_scr, tail_scr, carry_scr):
    i = pl.program_id(0)
    n_tiles = pl.num_programs(0) - 1
    T = xin_ref.shape[0]
    D = D_MODEL
    C = T // LRU_CHUNKS
    GC = C // SUBLANES
    G = T // SUBLANES
    HALO = (CONV_WIDTH - 1) * SUBLANES

    def cast_slices():
        bf_g_ref[...] = cast_g_ref[...].astype(BF16)
        bf_u_ref[...] = cast_u_ref[...].astype(BF16)
        bf_d_ref[...] = cast_d_ref[...].astype(BF16)

    @pl.when(i == 0)
    def _():
        tail_scr[...] = jnp.zeros_like(tail_scr)
        carry_scr[...] = jnp.zeros_like(carry_scr)

    PW = 2 * D // LRU_CHUNKS

    def project_chunks():
        hn = _rmsnorm(xin_ref[...], g_ref[...]).astype(BF16)
        hp = jnp.dot(perm_ref[...], hn, preferred_element_type=F32).astype(BF16)
        for k in range(LRU_CHUNKS):
            yield jnp.dot(hp, win_ref[:, PW * k:PW * (k + 1)], preferred_element_type=F32)

    def recur_chunks():
        sub = lax.broadcasted_iota(jnp.int32, (SUBLANES, D), 0)
        cw = cw_ref[...]
        softplus_neg_lam = jax.nn.softplus(-lam_ref[...])
        last = proj_scr[T - HALO:T, D:]
        prev_groups = []
        for k in range(CONV_WIDTH - 1):
            rows = slice(SUBLANES * k, SUBLANES * (k + 1))
            prev_groups.append(pltpu.roll(jnp.where(sub == SUBLANES - 1, tail_scr[rows, :], last[rows, :]), 1, 0))
        tail_scr[...] = last

        ys, hs, decay = [], [], []
        for k in range(LRU_CHUNKS):
            pc = proj_scr[C * k:C * (k + 1), :]
            ys.append(jax.nn.gelu(pc[:, :D], approximate=True))
            xb = pc[:, D:]
            xc = cb_ref[...] + xb * cw[CONV_WIDTH - 1:CONV_WIDTH, :]
            for back in range(1, CONV_WIDTH):
                shifted = jnp.concatenate(prev_groups[CONV_WIDTH - 1 - back:] + [xb[:C - SUBLANES * back, :]], axis=0)
                xc = xc + shifted * cw[CONV_WIDTH - 1 - back:CONV_WIDTH - back, :]
            prev_groups = [xb[C - HALO + SUBLANES * q:C - HALO + SUBLANES * (q + 1), :] for q in range(CONV_WIDTH - 1)]

            xcb = xc.astype(BF16)
            r_parts, i_parts = [], []
            for n in range(N_RNN_BLOCKS):
                gn = jnp.dot(xcb[:, RNN_BLOCK * n:RNN_BLOCK * (n + 1)], wax_ref[n], preferred_element_type=F32)
                r_parts.append(gn[:, :RNN_BLOCK])
                i_parts.append(gn[:, RNN_BLOCK:])
            r = jax.nn.sigmoid(jnp.concatenate(r_parts, axis=1) + ba_ref[...])
            ig = jax.nn.sigmoid(jnp.concatenate(i_parts, axis=1) + bx_ref[...])
            log_a = -LRU_C * r * softplus_neg_lam
            a = jnp.exp(log_a)
            z = jnp.tanh(-log_a) * (1.0 + a * a)
            b = jnp.where(z > 0.0, z * lax.rsqrt(z), 0.0) * (ig * xc)

            for j in range(GC):
                rows = slice(SUBLANES * j, SUBLANES * (j + 1))
                if hs:
                    hs.append(a[rows, :] * hs[-1] + b[rows, :])
                    decay.append(a[rows, :] * decay[-1])
                else:
                    hs.append(b[rows, :])
                    decay.append(a[rows, :])
            yield None

        p_inc, e_inc = decay[-1], hs[-1]
        for k in (1, 2, 4):
            m = sub >= k
            p_sh = jnp.where(m, pltpu.roll(p_inc, k, 0), 1.0)
            e_sh = jnp.where(m, pltpu.roll(e_inc, k, 0), 0.0)
            e_inc = e_inc + p_inc * e_sh
            p_inc = p_inc * p_sh
        h0 = carry_scr[...]
        after = p_inc * h0 + e_inc
        carry_scr[...] = after[SUBLANES - 1:SUBLANES, :]
        enter = jnp.where(sub == 0, h0, pltpu.roll(after, 1, 0))
        h = jnp.concatenate([hs[j] + decay[j] * enter for j in range(G)], axis=0)
        hy = (h * jnp.concatenate(ys, axis=0)).astype(BF16)
        hy = jnp.dot(unperm_ref[...], hy, preferred_element_type=F32).astype(BF16)
        yield jnp.dot(hy, wout_ref[...], preferred_element_type=F32) + xres_ref[...]

    @pl.when(i == 0)
    def _():
        cast_slices()
        for k, pc in enumerate(project_chunks()):
            proj_scr[:, PW * k:PW * (k + 1)] = pc

    @pl.when((i > 0) & (i < n_tiles))
    def _():
        cast_slices()
        rec = recur_chunks()
        new_proj = []
        for pc in project_chunks():
            new_proj.append(pc)
            next(rec)
        o_ref[...] = next(rec)
        for k, pc in enumerate(new_proj):
            proj_scr[:, PW * k:PW * (k + 1)] = pc

    @pl.when(i == n_tiles)
    def _():
        o_ref[...] = list(recur_chunks())[-1]


def _lru_mixer(x, g, w_in, conv_w, conv_b, w_ax, b_a, b_x, lam, w_out, cast_gate, cast_up, cast_down):
    S, D = x.shape
    T = SEQ_TILE
    n_tiles = S // T
    row = lambda v: v.reshape(1, D)
    perm, unperm = _time_permutation(T)

    def cast_spec(w):
        return pl.BlockSpec((w.shape[0] // n_tiles, w.shape[1]), lambda i: (jnp.minimum(i, n_tiles - 1), 0))

    casts = (cast_gate, cast_up, cast_down)
    return pl.pallas_call(
        _lru_kernel,
        grid=(n_tiles + 1,),
        in_specs=[
            pl.BlockSpec((T, D), lambda i: (jnp.minimum(i, n_tiles - 1), 0)),
            pl.BlockSpec((T, D), lambda i: (jnp.maximum(i - 1, 0), 0)),
            _const_spec((1, D)),
            _const_spec((T, T)),
            _const_spec((T, T)),
            _const_spec((D, 2 * D)),
            _const_spec((CONV_WIDTH, D)),
            _const_spec((1, D)),
            _const_spec((N_RNN_BLOCKS, RNN_BLOCK, 2 * RNN_BLOCK)),
            _const_spec((1, D)),
            _const_spec((1, D)),
            _const_spec((1, D)),
            _const_spec((D, D)),
        ] + [cast_spec(w) for w in casts],
        out_specs=[pl.BlockSpec((T, D), lambda i: (jnp.maximum(i - 1, 0), 0))] + [cast_spec(w) for w in casts],
        out_shape=[jax.ShapeDtypeStruct((S, D), F32)] + [jax.ShapeDtypeStruct(w.shape, BF16) for w in casts],
        scratch_shapes=[pltpu.VMEM((T, 2 * D), F32), pltpu.VMEM(((CONV_WIDTH - 1) * SUBLANES, D), F32),
                        pltpu.VMEM((1, D), F32)],
        compiler_params=pltpu.CompilerParams(dimension_semantics=("arbitrary",), vmem_limit_bytes=VMEM_LIMIT),
        name="lru_mixer",
    )(x, x, row(g), perm, unperm, w_in, conv_w, row(conv_b), w_ax, row(b_a), row(b_x), row(lam), w_out, *casts)


def _ffn_kernel(x_ref, g_ref, wg_ref, wu_ref, wd_ref, cast_g_ref, cast_u_ref, cast_d_ref,
                o_ref, bf_g_ref, bf_u_ref, bf_d_ref, xn_scr, acc_scr):
    c = pl.program_id(1)
    last = pl.num_programs(1) - 1

    def ff_chunk():
        bf_g_ref[...] = cast_g_ref[...].astype(BF16)
        bf_u_ref[...] = cast_u_ref[...].astype(BF16)
        bf_d_ref[...] = cast_d_ref[...].astype(BF16)
        xn = xn_scr[...]
        gate = jnp.dot(xn, wg_ref[...], preferred_element_type=F32)
        up = jnp.dot(xn, wu_ref[...], preferred_element_type=F32)
        mid = (jax.nn.silu(gate) * up).astype(BF16)
        return jnp.dot(mid, wd_ref[...], preferred_element_type=F32)

    @pl.when(c == 0)
    def _():
        xn_scr[...] = _rmsnorm(x_ref[...], g_ref[...]).astype(BF16)
        acc_scr[...] = ff_chunk()

    @pl.when((c > 0) & (c < last))
    def _():
        acc_scr[...] += ff_chunk()

    @pl.when(c == last)
    def _():
        o_ref[...] = x_ref[...] + (acc_scr[...] + ff_chunk())


def _dense_ffn(x, g, w_gate, w_up, w_down, cast_gate, cast_up, cast_down):
    S, D = x.shape
    T = FFN_TILE
    n_steps = (S // T) * N_FF_CHUNKS
    E = cast_gate.shape[0]
    row_parts = n_steps // (E * N_FF_CHUNKS)
    assert row_parts * E * N_FF_CHUNKS == n_steps
    gu_block = (1, D // row_parts, D_FF // N_FF_CHUNKS)
    d_block = (1, D_FF // (row_parts * N_FF_CHUNKS), D)

    def gu_map(i, c):
        s = i * N_FF_CHUNKS + c
        return (s // (row_parts * N_FF_CHUNKS), (s // N_FF_CHUNKS) % row_parts, c)

    def d_map(i, c):
        s = i * N_FF_CHUNKS + c
        return (s // (row_parts * N_FF_CHUNKS), s % (row_parts * N_FF_CHUNKS), 0)

    return pl.pallas_call(
        _ffn_kernel,
        grid=(S // T, N_FF_CHUNKS),
        in_specs=[
            pl.BlockSpec((T, D), lambda i, c: (i, 0)),
            pl.BlockSpec((1, D), lambda i, c: (0, 0)),
            pl.BlockSpec((D, FF_CHUNK), lambda i, c: (0, c)),
            pl.BlockSpec((D, FF_CHUNK), lambda i, c: (0, c)),
            pl.BlockSpec((FF_CHUNK, D), lambda i, c: (c, 0)),
            pl.BlockSpec(gu_block, gu_map),
            pl.BlockSpec(gu_block, gu_map),
            pl.BlockSpec(d_block, d_map),
        ],
        out_specs=[pl.BlockSpec((T, D), lambda i, c: (i, 0)), pl.BlockSpec(gu_block, gu_map),
                   pl.BlockSpec(gu_block, gu_map), pl.BlockSpec(d_block, d_map)],
        out_shape=[jax.ShapeDtypeStruct((S, D), F32), jax.ShapeDtypeStruct(cast_gate.shape, BF16),
                   jax.ShapeDtypeStruct(cast_up.shape, BF16), jax.ShapeDtypeStruct(cast_down.shape, BF16)],
        scratch_shapes=[pltpu.VMEM((T, D), BF16), pltpu.VMEM((T, D), F32)],
        compiler_params=pltpu.CompilerParams(dimension_semantics=("arbitrary", "arbitrary"),
                                             vmem_limit_bytes=VMEM_LIMIT),
        name="dense_ffn",
    )(x, g.reshape(1, D), w_gate, w_up, w_down, cast_gate, cast_up, cast_down)


KV_EXT = N_KV_HEADS * LANES
QKV_EXT = N_HEADS * HEAD_DIM + 2 * KV_EXT


def _attn_kernel(sink_ref, x_ref, pos_ref, g_ref, freq_ref, sign_ref, wqkv_ref, bqkv_ref, wo_ref, bo_ref,
                 g2_ref, wr_ref, o_ref, route_ref,
                 qlo_scr, qhi_scr, k_scr, v_scr, o_scr):
    T = x_ref.shape[0]
    D = D_MODEL
    B = ATTN_BLOCK
    first_tile = pl.program_id(0) == 0

    @pl.when(first_tile)
    def _():
        k_scr[0:B, :] = jnp.zeros((B, KV_EXT), BF16)
        v_scr[0:B, :] = jnp.zeros((B, KV_EXT), BF16)

    x = x_ref[...]
    hn = _rmsnorm(x, g_ref[...]).astype(BF16)
    qkv = jnp.dot(hn, wqkv_ref[...], preferred_element_type=F32) + bqkv_ref[...]

    ang = pos_ref[...] * freq_ref[...]
    cos_t = jnp.cos(ang)
    sin_t = jnp.sin(ang) * sign_ref[...]
    lane = lax.broadcasted_iota(jnp.int32, (T, LANES), 1)
    first_half = (lane % HEAD_DIM) < (ROT_DIM // 2)
    lo_half = lane < HEAD_DIM

    def rope(col):
        partner = jnp.where(first_half, pltpu.roll(col, LANES - ROT_DIM // 2, 1), pltpu.roll(col, ROT_DIM // 2, 1))
        return col * cos_t + partner * sin_t

    scale = HEAD_DIM ** -0.5 * LOG2_E
    for c in range(D // LANES):
        qc = rope(qkv[:, LANES * c:LANES * (c + 1)]) * scale
        qlo_scr[:, LANES * c:LANES * (c + 1)] = jnp.where(lo_half, qc, 0.0).astype(BF16)
        qhi_scr[:, LANES * c:LANES * (c + 1)] = jnp.where(lo_half, 0.0, qc).astype(BF16)
    for g in range(N_KV_HEADS):
        kc = rope(qkv[:, D + LANES * g:D + LANES * (g + 1)])
        k_scr[B:B + T, LANES * g:LANES * (g + 1)] = kc.astype(BF16)
        vc = qkv[:, D + KV_EXT + LANES * g:D + KV_EXT + LANES * (g + 1)]
        v_scr[B:B + T, LANES * g:LANES * (g + 1)] = vc.astype(BF16)

    qi = lax.broadcasted_iota(jnp.int32, (B, B), 0)
    kj = lax.broadcasted_iota(jnp.int32, (B, B), 1)
    causal_own = kj <= qi
    window_prev = kj > qi
    key0 = kj == 0
    key0_row = lax.broadcasted_iota(jnp.int32, (1, B), 1) == 0
    lo_blk = lax.broadcasted_iota(jnp.int32, (B, LANES), 1) < HEAD_DIM

    def block_body(blk, _):
        r0 = pl.multiple_of(blk * B, B)
        k_lo = jnp.where(first_tile & (blk == 0), B, 0)
        allowed_prev = window_prev & (kj >= k_lo)
        for g in range(N_KV_HEADS):
            kk = k_scr[pl.ds(r0, 2 * B), LANES * g:LANES * (g + 1)]
            vv = v_scr[pl.ds(r0, 2 * B), LANES * g:LANES * (g + 1)]
            cols = [slice(LANES * c, LANES * (c + 1)) for c in range((GROUP // 2) * g, (GROUP // 2) * (g + 1))]
            q_all = jnp.concatenate([q_scr[pl.ds(r0, B), col] for col in cols for q_scr in (qlo_scr, qhi_scr)], axis=0)
            s_all = lax.dot_general(q_all, kk, (((1,), (1,)), ((), ())), preferred_element_type=F32)
            p_all, denoms = [], []
            for hh in range(GROUP):
                rows = slice(B * hh, B * (hh + 1))
                sink_fill = jnp.where(key0_row, sink_ref[GROUP * g + hh] * LOG2_E, NEG_INF)
                s_prev = jnp.where(allowed_prev, s_all[rows, :B], sink_fill)
                s_own = jnp.where(causal_own, s_all[rows, B:], NEG_INF)
                m = jnp.max(jnp.maximum(s_prev, s_own), axis=-1, keepdims=True)
                p_prev = jnp.exp2(s_prev - m)
                p_own = jnp.exp2(s_own - m)
                denoms.append(jnp.sum(p_prev + p_own, axis=-1, keepdims=True))
                p_all.append(jnp.concatenate([jnp.where(key0, 0.0, p_prev), p_own], axis=1).astype(BF16))
            o_all = jnp.dot(jnp.concatenate(p_all, axis=0), vv, preferred_element_type=F32)
            o_heads = [o_all[B * hh:B * (hh + 1), :] / denoms[hh] for hh in range(GROUP)]
            for p, col in enumerate(cols):
                o_scr[pl.ds(r0, B), col] = jnp.where(lo_blk, o_heads[2 * p], o_heads[2 * p + 1]).astype(BF16)
        return 0

    lax.fori_loop(0, T // B, block_body, 0)

    k_scr[0:B, :] = k_scr[T:T + B, :]
    v_scr[0:B, :] = v_scr[T:T + B, :]

    h1 = jnp.dot(o_scr[...], wo_ref[...], preferred_element_type=F32) + bo_ref[...] + x
    _store_token_major(o_ref, h1)

    hn2 = _rmsnorm(h1, g2_ref[...]).astype(BF16)
    logits = jnp.dot(hn2, wr_ref[...], preferred_element_type=F32)
    lane_f = lane.astype(F32)
    lg = jnp.where(lane < N_EXPERTS, logits, -jnp.inf)
    v1 = jnp.max(lg, axis=-1, keepdims=True)
    i1 = jnp.min(jnp.where(lg == v1, lane_f, float(LANES)), axis=-1, keepdims=True)
    lg2 = jnp.where(lane_f == i1, -jnp.inf, lg)
    v2 = jnp.max(lg2, axis=-1, keepdims=True)
    i2 = jnp.min(jnp.where(lg2 == v2, lane_f, float(LANES)), axis=-1, keepdims=True)
    e2 = jnp.exp(v2 - v1)
    gate1 = 1.0 / (1.0 + e2)
    gate2 = e2 / (1.0 + e2)
    route_ref[...] = jnp.where(lane == 0, i1, jnp.where(lane == 1, i2,
                                                        jnp.where(lane == 2, gate1, jnp.where(lane == 3, gate2, 0.0))))


def _attn_mixer(x, pos_f, g, freq_lane, sign_lane, w_qkv_ext, b_qkv_ext, sinks, w_o, b_o, g2, w_router_pad):
    S, D = x.shape
    T = SEQ_TILE
    B = ATTN_BLOCK
    grid_spec = pltpu.PrefetchScalarGridSpec(
        num_scalar_prefetch=1,
        grid=(S // T,),
        in_specs=[
            pl.BlockSpec((T, D), lambda i, s: (i, 0)),
            pl.BlockSpec((T, 1), lambda i, s: (i, 0)),
            _const_spec((1, D)),
            _const_spec((1, LANES)),
            _const_spec((1, LANES)),
            _const_spec((D, QKV_EXT)),
            _const_spec((1, QKV_EXT)),
            _const_spec((D, D)),
            _const_spec((1, D)),
            _const_spec((1, D)),
            _const_spec((D, LANES)),
        ],
        out_specs=[pl.BlockSpec((T * ROWS_PER_TOKEN, LANES), lambda i, s: (i, 0)),
                   pl.BlockSpec((T, LANES), lambda i, s: (i, 0))],
        scratch_shapes=[
            pltpu.VMEM((T, D), BF16), pltpu.VMEM((T, D), BF16),
            pltpu.VMEM((T + B, KV_EXT), BF16), pltpu.VMEM((T + B, KV_EXT), BF16),
            pltpu.VMEM((T, D), BF16),
        ],
    )
    return pl.pallas_call(
        _attn_kernel,
        grid_spec=grid_spec,
        out_shape=[jax.ShapeDtypeStruct((S * ROWS_PER_TOKEN, LANES), F32), jax.ShapeDtypeStruct((S, LANES), F32)],
        compiler_params=pltpu.CompilerParams(dimension_semantics=("arbitrary",), vmem_limit_bytes=VMEM_LIMIT),
        name="attn_mixer",
    )(sinks, x, pos_f, g.reshape(1, D), freq_lane, sign_lane, w_qkv_ext, b_qkv_ext, w_o, b_o.reshape(1, D),
      g2.reshape(1, D), w_router_pad)


def _moe_kernel(texp_ref, nvt_ref, src0_ref, srcn_ref, dstp_ref, dstl_ref, h_hbm, g_ref, wg_ref, wu_ref, wd_ref,
                y_hbm, x_scr, xn_scr, acc_scr, y_scr, sems):
    i = pl.program_id(0)
    c = pl.program_id(1)
    n_tiles = pl.num_programs(0)
    n_chunks = pl.num_programs(1)
    T = xn_scr.shape[0]
    R = ROWS_PER_TOKEN
    valid = i < nvt_ref[0]
    slot = i % 2
    other = 1 - slot
    SCATTER_SEM = 2

    def row_copy(src, src_row, dst, dst_row, sem):
        return pltpu.make_async_copy(src.at[pl.ds(pl.multiple_of(src_row, R), R), :],
                                     dst.at[pl.ds(pl.multiple_of(dst_row, R), R), :], sem)

    def tile_rows(t):
        return nvt_ref[1 + t]

    def window_offset(t):
        return nvt_ref[1 + 2 * n_tiles + t]

    prev_tile = jnp.maximum(i - 1, 0)
    next_tile = jnp.minimum(i + 1, n_tiles - 1)
    rows_prev = jnp.where(i > 0, tile_rows(prev_tile), 0)
    pad_base = y_hbm.shape[0] - T * R

    def dest_row(window_ref, offset, r, n_rows):
        return jnp.where(r < n_rows, window_ref[0, offset + r], pad_base + r * R)

    def wait_gather(s):
        pltpu.make_async_copy(h_hbm.at[pl.ds(0, T * R), :], x_scr.at[s], sems.at[s]).wait()

    def wait_scatter():
        pltpu.make_async_copy(y_scr, y_hbm.at[pl.ds(0, T * R), :], sems.at[SCATTER_SEM]).wait()

    @pl.when((i == 0) & (c == 0))
    def _():
        y_scr[...] = jnp.zeros_like(y_scr)
        acc_scr[...] = jnp.zeros_like(acc_scr)

        def issue(r, carry):
            row_copy(h_hbm, src0_ref[0, window_offset(0) + r], x_scr.at[0], r * R, sems.at[0]).start()
            return carry
        lax.fori_loop(0, T, issue, 0)

    @pl.when(c == 0)
    def _():
        wait_gather(slot)

    def row_dmas():
        off_next = window_offset(next_tile)
        off_prev = window_offset(prev_tile)
        for r in range(T):
            row_copy(h_hbm, srcn_ref[0, off_next + r], x_scr.at[other], r * R, sems.at[other]).start()
            row_copy(y_scr, r * R, y_hbm, dest_row(dstp_ref, off_prev, r, rows_prev),
                     sems.at[SCATTER_SEM]).start()

    def expert_chunk(first, rows):
        xn = xn_scr[0:rows, :]
        gate = jnp.dot(xn, wg_ref[0], preferred_element_type=F32)
        up = jnp.dot(xn, wu_ref[0], preferred_element_type=F32)
        mid = (jax.nn.silu(gate) * up).astype(BF16)
        down = jnp.dot(mid, wd_ref[0], preferred_element_type=F32)
        if first:
            acc_scr[0:rows, :] = down
        else:
            acc_scr[0:rows, :] += down

    few_rows = tile_rows(i) <= T // 2

    for few, rows in ((False, T), (True, T // 2)):
        branch = valid & (few_rows if few else jnp.logical_not(few_rows))

        @pl.when(branch & (c == 0))
        def _():
            xn_scr[0:rows, :] = _rmsnorm(_load_token_major(x_scr.at[slot], rows), g_ref[...]).astype(BF16)
            row_dmas()
            expert_chunk(True, rows)

        @pl.when(branch & (c != 0))
        def _():
            expert_chunk(False, rows)

    @pl.when(jnp.logical_not(valid) & (c == 0))
    def _():
        row_dmas()

    @pl.when(c == n_chunks - 1)
    def _():
        wait_scatter()

    @pl.when(valid & (c == n_chunks - 1))
    def _():
        _store_token_major(y_scr, acc_scr[...])

    @pl.when((i == n_tiles - 1) & (c == n_chunks - 1))
    def _():
        wait_gather(other)


        def issue(r, carry):
            row_copy(y_scr, r * R, y_hbm, dest_row(dstl_ref, window_offset(n_tiles - 1), r, tile_rows(n_tiles - 1)),
                     sems.at[SCATTER_SEM]).start()
            return carry
        lax.fori_loop(0, T, issue, 0)
        wait_scatter()


def _moe_experts(h1_tm, g, tile_expert, n_valid_tiles, src_rows, dst_rows, w_gate, w_up, w_down, n_out_tokens):
    D = D_MODEL
    T = MOE_TILE
    NT = tile_expert.shape[0]
    NC = N_MOE_CHUNKS
    FC = D_FF // NC

    def chunk(i, c, nvt):
        return jnp.where(i < nvt[0], c, NC - 1)

    def window_spec(tile_of_step):
        return pl.BlockSpec((pl.Element(1), pl.Element(T + LANES)),
                            lambda i, c, te, nvt: (0, nvt[1 + NT + tile_of_step(i)] * LANES),
                            memory_space=pltpu.SMEM)

    grid_spec = pltpu.PrefetchScalarGridSpec(
        num_scalar_prefetch=2,
        grid=(NT, NC),
        in_specs=[
            window_spec(lambda i: 0),
            window_spec(lambda i: jnp.minimum(i + 1, NT - 1)),
            window_spec(lambda i: jnp.maximum(i - 1, 0)),
            window_spec(lambda i: NT - 1),
            pl.BlockSpec(memory_space=pl.ANY),
            pl.BlockSpec((1, D), lambda i, c, te, nvt: (0, 0)),
            pl.BlockSpec((1, D, FC), lambda i, c, te, nvt: (te[i], 0, chunk(i, c, nvt))),
            pl.BlockSpec((1, D, FC), lambda i, c, te, nvt: (te[i], 0, chunk(i, c, nvt))),
            pl.BlockSpec((1, FC, D), lambda i, c, te, nvt: (te[i], chunk(i, c, nvt), 0)),
        ],
        out_specs=pl.BlockSpec(memory_space=pl.ANY),
        scratch_shapes=[pltpu.VMEM((2, T * ROWS_PER_TOKEN, LANES), F32), pltpu.VMEM((T, D), BF16),
                        pltpu.VMEM((T, D), F32), pltpu.VMEM((T * ROWS_PER_TOKEN, LANES), F32),
                        pltpu.SemaphoreType.DMA((3,))],
    )
    return pl.pallas_call(
        _moe_kernel,
        grid_spec=grid_spec,
        out_shape=jax.ShapeDtypeStruct((n_out_tokens * ROWS_PER_TOKEN, LANES), F32),
        compiler_params=pltpu.CompilerParams(dimension_semantics=("arbitrary", "arbitrary"),
                                             vmem_limit_bytes=VMEM_LIMIT),
        name="moe_experts",
    )(tile_expert, n_valid_tiles, src_rows, src_rows, dst_rows, dst_rows, h1_tm, g.reshape(1, D),
      w_gate, w_up, w_down)


def _moe_plan(route, n_tokens):
    T = MOE_TILE
    n_assign = TOP_K * n_tokens
    n_tiles = n_assign // T + N_EXPERTS
    e_flat = route[:, 0:TOP_K].astype(jnp.int32).reshape(-1)
    counts = jnp.sum((e_flat[:, None] == jnp.arange(N_EXPERTS, dtype=jnp.int32)[None, :]).astype(jnp.int32), axis=0)
    order = jnp.argsort(e_flat, stable=True).astype(jnp.int32)
    tiles_per = (counts + T - 1) // T
    tile_end = jnp.cumsum(tiles_per)
    n_valid = tile_end[-1]
    tile_ids = jnp.arange(n_tiles, dtype=jnp.int32)
    last_valid = jnp.maximum(n_valid - 1, 0)
    texp = jnp.sum((jnp.minimum(tile_ids, last_valid)[:, None] >= tile_end[None, :]).astype(jnp.int32), axis=1)
    texp = jnp.minimum(texp, N_EXPERTS - 1)
    onehot = (texp[:, None] == jnp.arange(N_EXPERTS, dtype=jnp.int32)[None, :]).astype(jnp.int32)
    pick = lambda v: jnp.sum(onehot * v[None, :], axis=1)
    j = tile_ids - pick(tile_end - tiles_per)
    start = jnp.where(tile_ids < n_valid, pick(jnp.cumsum(counts) - counts) + j * T, 0)
    n_rows = jnp.where(tile_ids < n_valid, jnp.clip(pick(counts) - j * T, 0, T), 0)
    token = order // TOP_K
    slot = order % TOP_K
    spare = jnp.zeros((T + LANES,), jnp.int32)
    src_sorted = jnp.concatenate([token * ROWS_PER_TOKEN, spare]).reshape(1, -1)
    dst_sorted = jnp.concatenate([(slot * n_tokens + token) * ROWS_PER_TOKEN, spare]).reshape(1, -1)
    tile_counts = jnp.concatenate([n_valid.reshape(1), n_rows, start // LANES, start % LANES]).astype(jnp.int32)
    return texp, tile_counts, src_sorted, dst_sorted


def _combine_kernel(h_ref, y0_ref, y1_ref, route_ref, g_ref, o_ref):
    T = o_ref.shape[0]
    route = route_ref[...]
    moe = route[:, 2:3] * _load_token_major(y0_ref, T) + route[:, 3:4] * _load_token_major(y1_ref, T)
    o_ref[...] = _rmsnorm(_load_token_major(h_ref, T) + moe, g_ref[...])


def _combine(h1_tm, y2_tm, route, g):
    D = D_MODEL
    S = route.shape[0]
    T = OUT_TILE
    nb = S // T
    tm_block = (T * ROWS_PER_TOKEN, LANES)
    return pl.pallas_call(
        _combine_kernel,
        grid=(nb,),
        in_specs=[
            pl.BlockSpec(tm_block, lambda i: (i, 0)),
            pl.BlockSpec(tm_block, lambda i: (i, 0)),
            pl.BlockSpec(tm_block, lambda i: (i + nb, 0)),
            pl.BlockSpec((T, LANES), lambda i: (i, 0)),
            pl.BlockSpec((1, D), lambda i: (0, 0)),
        ],
        out_specs=pl.BlockSpec((T, D), lambda i: (i, 0)),
        out_shape=jax.ShapeDtypeStruct((S, D), F32),
        compiler_params=pltpu.CompilerParams(dimension_semantics=("arbitrary",), vmem_limit_bytes=VMEM_LIMIT),
        name="moe_combine",
    )(h1_tm, y2_tm, y2_tm, route, g.reshape(1, D))


def kernel(x, positions, norm_mix, norm_ffn, norm_final, lru_w_in, lru_conv_w, lru_conv_b, lru_w_a, lru_b_a, lru_w_x, lru_b_x, lru_lambda, lru_w_out, attn_w_qkv, attn_b_qkv, attn_sinks, attn_w_o, attn_b_o, ffn_w_gate, ffn_w_up, ffn_w_down, moe_w_router, moe_w_gate, moe_w_up, moe_w_down):
    B, S, D = x.shape
    assert B == 1 and D == D_MODEL and S % FFN_TILE == 0
    h = x.reshape(S, D)

    w_ax = jnp.concatenate([lru_w_a[0], lru_w_x[0]], axis=-1).astype(BF16)
    h, ffn_wg, ffn_wu, ffn_wd = _lru_mixer(h, norm_mix[0], lru_w_in[0].astype(BF16), lru_conv_w[0], lru_conv_b[0],
                                           w_ax, lru_b_a[0], lru_b_x[0], lru_lambda[0], lru_w_out[0].astype(BF16),
                                           ffn_w_gate[0], ffn_w_up[0], ffn_w_down[0])
    h, moe_wg, moe_wu, moe_wd = _dense_ffn(h, norm_ffn[0], ffn_wg, ffn_wu, ffn_wd,
                                           moe_w_gate[0], moe_w_up[0], moe_w_down[0])

    q_dim = N_HEADS * HEAD_DIM
    kv_dim = N_KV_HEADS * HEAD_DIM

    def dup_heads(w):
        w4 = w.reshape(w.shape[:-1] + (N_KV_HEADS, 1, HEAD_DIM))
        return jnp.broadcast_to(w4, w.shape[:-1] + (N_KV_HEADS, 2, HEAD_DIM)).reshape(w.shape[:-1] + (KV_EXT,))

    wqkv, bqkv = attn_w_qkv[0], attn_b_qkv[0]
    w_qkv_ext = jnp.concatenate([wqkv[:, :q_dim], dup_heads(wqkv[:, q_dim:q_dim + kv_dim]),
                                 dup_heads(wqkv[:, q_dim + kv_dim:])], axis=1).astype(BF16)
    b_qkv_ext = jnp.concatenate([bqkv[:q_dim], dup_heads(bqkv[q_dim:q_dim + kv_dim]),
                                 dup_heads(bqkv[q_dim + kv_dim:])]).reshape(1, QKV_EXT)
    inv_freq = ROPE_THETA ** (-jnp.arange(0, ROT_DIM, 2, dtype=F32) / ROT_DIM)
    d_in_head = jnp.arange(LANES) % HEAD_DIM
    freq_lane = jnp.where(d_in_head < ROT_DIM, inv_freq[d_in_head % (ROT_DIM // 2)], 0.0).reshape(1, LANES)
    sign_lane = jnp.where(d_in_head < ROT_DIM // 2, -1.0, jnp.where(d_in_head < ROT_DIM, 1.0, 0.0))
    sign_lane = sign_lane.astype(F32).reshape(1, LANES)
    w_router_pad = jnp.pad(moe_w_router[0], ((0, 0), (0, LANES - N_EXPERTS))).astype(BF16)
    pos_f = positions.reshape(S, 1).astype(F32)
    h1_tm, route = _attn_mixer(h, pos_f, norm_mix[1], freq_lane, sign_lane, w_qkv_ext, b_qkv_ext,
                               attn_sinks[0], attn_w_o[0].astype(BF16), attn_b_o[0], norm_ffn[1], w_router_pad)

    texp, n_valid, src_rows, dst_rows = _moe_plan(route, S)
    y2_tm = _moe_experts(h1_tm, norm_ffn[1], texp, n_valid, src_rows, dst_rows,
                         moe_wg, moe_wu, moe_wd,
                         TOP_K * S + MOE_TILE)
    out = _combine(h1_tm, y2_tm, route, norm_final)
    return out.reshape(B, S, D)
```

```python
import jax
import jax.numpy as jnp
from jax import lax
from jax.experimental import pallas as pl
from jax.experimental.pallas import tpu as pltpu

F32 = jnp.float32
BF16 = jnp.bfloat16

D_MODEL = 1024
N_RNN_BLOCKS = 8
RNN_BLOCK = D_MODEL // N_RNN_BLOCKS
CONV_WIDTH = 4
LRU_C = 8.0
N_HEADS = 16
N_KV_HEADS = 4
HEAD_DIM = 64
GROUP = N_HEADS // N_KV_HEADS
WINDOW = 128
ATTN_BLOCK = 128
assert WINDOW == ATTN_BLOCK
ROPE_THETA = 500000.0
ROT_DIM = HEAD_DIM // 4
D_FF = 3584
N_EXPERTS = 8
TOP_K = 2
EPS = 1e-6
NEG_INF = -1e30
LOG2_E = 1.4426950408889634

LANES = 128
SUBLANES = 8
VMEM_LIMIT = 56 * 1024 * 1024

SEQ_TILE = 512
ATTN_TILE = 1024
LRU_CHUNKS = 4
FFN_TILE = 1024
FF_CHUNK = 512
N_FF_CHUNKS = D_FF // FF_CHUNK
MOE_TILE = 512
N_MOE_CHUNKS = 2
OUT_TILE = 1024


def _rmsnorm(x, g):
    return x * lax.rsqrt(jnp.mean(x * x, axis=-1, keepdims=True) + EPS) * g


ROWS_PER_TOKEN = D_MODEL // LANES


def _store_token_major(ref, val):
    T = val.shape[0]
    for c in range(ROWS_PER_TOKEN):
        ref[pl.ds(c, T, stride=ROWS_PER_TOKEN), :] = val[:, LANES * c:LANES * (c + 1)]


def _load_token_major(ref, T):
    return jnp.concatenate([ref[pl.ds(c, T, stride=ROWS_PER_TOKEN), :] for c in range(ROWS_PER_TOKEN)], axis=1)


def _const_spec(shape):
    n = len(shape)
    return pl.BlockSpec(shape, lambda *_: (0,) * n, pipeline_mode=pl.Buffered(1))


def _time_permutation(T):
    rho = jnp.arange(T)
    t_of_rho = (rho % SUBLANES) * (T // SUBLANES) + rho // SUBLANES
    perm = (t_of_rho[:, None] == jnp.arange(T)[None, :]).astype(BF16)
    return perm, perm.T


def _lru_kernel(xin_ref, xres_ref, g_ref, perm_ref, unperm_ref, win_ref, cw_ref, cb_ref, wax_ref, ba_ref, bx_ref,
                lam_ref, wout_ref, cast_g_ref, cast_u_ref, cast_d_ref,
                o_ref, bf_g_ref, bf_u_ref, bf_d_ref, proj_scr, tail_scr, carry_scr):
    i = pl.program_id(0)
    n_tiles = pl.num_programs(0) - 1
    T = xin_ref.shape[0]
    D = D_MODEL
    C = T // LRU_CHUNKS
    GC = C // SUBLANES
    G = T // SUBLANES
    HALO = (CONV_WIDTH - 1) * SUBLANES

    def cast_slices():
        bf_g_ref[...] = cast_g_ref[...].astype(BF16)
        bf_u_ref[...] = cast_u_ref[...].astype(BF16)
        bf_d_ref[...] = cast_d_ref[...].astype(BF16)

    @pl.when(i == 0)
    def _():
        tail_scr[...] = jnp.zeros_like(tail_scr)
        carry_scr[...] = jnp.zeros_like(carry_scr)

    PW = 2 * D // LRU_CHUNKS

    def project_chunks():
        hn = _rmsnorm(xin_ref[...], g_ref[...]).astype(BF16)
        hp = jnp.dot(perm_ref[...], hn, preferred_element_type=F32).astype(BF16)
        for k in range(LRU_CHUNKS):
            yield jnp.dot(hp, win_ref[:, PW * k:PW * (k + 1)], preferred_element_type=F32)

    def recur_chunks():
        sub = lax.broadcasted_iota(jnp.int32, (SUBLANES, D), 0)
        cw = cw_ref[...]
        softplus_neg_lam = jax.nn.softplus(-lam_ref[...])
        last = proj_scr[T - HALO:T, D:]
        prev_groups = []
        for k in range(CONV_WIDTH - 1):
            rows = slice(SUBLANES * k, SUBLANES * (k + 1))
            prev_groups.append(pltpu.roll(jnp.where(sub == SUBLANES - 1, tail_scr[rows, :], last[rows, :]), 1, 0))
        tail_scr[...] = last

        ys, hs, decay = [], [], []
        for k in range(LRU_CHUNKS):
            pc = proj_scr[C * k:C * (k + 1), :]
            ys.append(jax.nn.gelu(pc[:, :D], approximate=True))
            xb = pc[:, D:]
            xc = cb_ref[...] + xb * cw[CONV_WIDTH - 1:CONV_WIDTH, :]
            for back in range(1, CONV_WIDTH):
                shifted = jnp.concatenate(prev_groups[CONV_WIDTH - 1 - back:] + [xb[:C - SUBLANES * back, :]], axis=0)
                xc = xc + shifted * cw[CONV_WIDTH - 1 - back:CONV_WIDTH - back, :]
            prev_groups = [xb[C - HALO + SUBLANES * q:C - HALO + SUBLANES * (q + 1), :] for q in range(CONV_WIDTH - 1)]

            xcb = xc.astype(BF16)
            r_parts, i_parts = [], []
            for n in range(N_RNN_BLOCKS):
                gn = jnp.dot(xcb[:, RNN_BLOCK * n:RNN_BLOCK * (n + 1)], wax_ref[n], preferred_element_type=F32)
                r_parts.append(gn[:, :RNN_BLOCK])
                i_parts.append(gn[:, RNN_BLOCK:])
            r = jax.nn.sigmoid(jnp.concatenate(r_parts, axis=1) + ba_ref[...])
            ig = jax.nn.sigmoid(jnp.concatenate(i_parts, axis=1) + bx_ref[...])
            log_a = -LRU_C * r * softplus_neg_lam
            a = jnp.exp(log_a)
            z = jnp.tanh(-log_a) * (1.0 + a * a)
            b = jnp.where(z > 0.0, z * lax.rsqrt(z), 0.0) * (ig * xc)

            for j in range(GC):
                rows = slice(SUBLANES * j, SUBLANES * (j + 1))
                if hs:
                    hs.append(a[rows, :] * hs[-1] + b[rows, :])
                    decay.append(a[rows, :] * decay[-1])
                else:
                    hs.append(b[rows, :])
                    decay.append(a[rows, :])
            yield None

        p_inc, e_inc = decay[-1], hs[-1]
        for k in (1, 2, 4):
            m = sub >= k
            p_sh = jnp.where(m, pltpu.roll(p_inc, k, 0), 1.0)
            e_sh = jnp.where(m, pltpu.roll(e_inc, k, 0), 0.0)
            e_inc = e_inc + p_inc * e_sh
            p_inc = p_inc * p_sh
        h0 = carry_scr[...]
        after = p_inc * h0 + e_inc
        carry_scr[...] = after[SUBLANES - 1:SUBLANES, :]
        enter = jnp.where(sub == 0, h0, pltpu.roll(after, 1, 0))
        h = jnp.concatenate([hs[j] + decay[j] * enter for j in range(G)], axis=0)
        hy = (h * jnp.concatenate(ys, axis=0)).astype(BF16)
        hy = jnp.dot(unperm_ref[...], hy, preferred_element_type=F32).astype(BF16)
        yield jnp.dot(hy, wout_ref[...], preferred_element_type=F32) + xres_ref[...]

    @pl.when(i == 0)
    def _():
        cast_slices()
        for k, pc in enumerate(project_chunks()):
            proj_scr[:, PW * k:PW * (k + 1)] = pc

    @pl.when((i > 0) & (i < n_tiles))
    def _():
        cast_slices()
        rec = recur_chunks()
        new_proj = []
        for pc in project_chunks():
            new_proj.append(pc)
            next(rec)
        o_ref[...] = next(rec)
        for k, pc in enumerate(new_proj):
            proj_scr[:, PW * k:PW * (k + 1)] = pc

    @pl.when(i == n_tiles)
    def _():
        o_ref[...] = list(recur_chunks())[-1]


def _lru_mixer(x, g, w_in, conv_w, conv_b, w_ax, b_a, b_x, lam, w_out, cast_gate, cast_up, cast_down):
    S, D = x.shape
    T = SEQ_TILE
    n_tiles = S // T
    row = lambda v: v.reshape(1, D)
    perm, unperm = _time_permutation(T)

    def cast_spec(w):
        return pl.BlockSpec((w.shape[0] // n_tiles, w.shape[1]), lambda i: (jnp.minimum(i, n_tiles - 1), 0))

    casts = (cast_gate, cast_up, cast_down)
    return pl.pallas_call(
        _lru_kernel,
        grid=(n_tiles + 1,),
        in_specs=[
            pl.BlockSpec((T, D), lambda i: (jnp.minimum(i, n_tiles - 1), 0)),
            pl.BlockSpec((T, D), lambda i: (jnp.maximum(i - 1, 0), 0)),
            _const_spec((1, D)),
            _const_spec((T, T)),
            _const_spec((T, T)),
            _const_spec((D, 2 * D)),
            _const_spec((CONV_WIDTH, D)),
            _const_spec((1, D)),
            _const_spec((N_RNN_BLOCKS, RNN_BLOCK, 2 * RNN_BLOCK)),
            _const_spec((1, D)),
            _const_spec((1, D)),
            _const_spec((1, D)),
            _const_spec((D, D)),
        ] + [cast_spec(w) for w in casts],
        out_specs=[pl.BlockSpec((T, D), lambda i: (jnp.maximum(i - 1, 0), 0))] + [cast_spec(w) for w in casts],
        out_shape=[jax.ShapeDtypeStruct((S, D), F32)] + [jax.ShapeDtypeStruct(w.shape, BF16) for w in casts],
        scratch_shapes=[pltpu.VMEM((T, 2 * D), F32), pltpu.VMEM(((CONV_WIDTH - 1) * SUBLANES, D), F32),
                        pltpu.VMEM((1, D), F32)],
        compiler_params=pltpu.CompilerParams(dimension_semantics=("arbitrary",), vmem_limit_bytes=VMEM_LIMIT),
        name="lru_mixer",
    )(x, x, row(g), perm, unperm, w_in, conv_w, row(conv_b), w_ax, row(b_a), row(b_x), row(lam), w_out, *casts)


def _ffn_kernel(x_ref, g_ref, wg_ref, wu_ref, wd_ref, cast_g_ref, cast_u_ref, cast_d_ref,
                o_ref, bf_g_ref, bf_u_ref, bf_d_ref, xn_scr, acc_scr):
    c = pl.program_id(1)
    last = pl.num_programs(1) - 1

    def ff_chunk():
        bf_g_ref[...] = cast_g_ref[...].astype(BF16)
        bf_u_ref[...] = cast_u_ref[...].astype(BF16)
        bf_d_ref[...] = cast_d_ref[...].astype(BF16)
        xn = xn_scr[...]
        gate = jnp.dot(xn, wg_ref[...], preferred_element_type=F32)
        up = jnp.dot(xn, wu_ref[...], preferred_element_type=F32)
        mid = (jax.nn.silu(gate) * up).astype(BF16)
        return jnp.dot(mid, wd_ref[...], preferred_element_type=F32)

    @pl.when(c == 0)
    def _():
        xn_scr[...] = _rmsnorm(x_ref[...], g_ref[...]).astype(BF16)
        acc_scr[...] = ff_chunk()

    @pl.when((c > 0) & (c < last))
    def _():
        acc_scr[...] += ff_chunk()

    @pl.when(c == last)
    def _():
        o_ref[...] = x_ref[...] + (acc_scr[...] + ff_chunk())


def _dense_ffn(x, g, w_gate, w_up, w_down, cast_gate, cast_up, cast_down):
    S, D = x.shape
    T = FFN_TILE
    n_steps = (S // T) * N_FF_CHUNKS
    E = cast_gate.shape[0]
    row_parts = n_steps // (E * N_FF_CHUNKS)
    assert row_parts * E * N_FF_CHUNKS == n_steps
    gu_block = (1, D // row_parts, D_FF // N_FF_CHUNKS)
    d_block = (1, D_FF // (row_parts * N_FF_CHUNKS), D)

    def gu_map(i, c):
        s = i * N_FF_CHUNKS + c
        return (s // (row_parts * N_FF_CHUNKS), (s // N_FF_CHUNKS) % row_parts, c)

    def d_map(i, c):
        s = i * N_FF_CHUNKS + c
        return (s // (row_parts * N_FF_CHUNKS), s % (row_parts * N_FF_CHUNKS), 0)

    return pl.pallas_call(
        _ffn_kernel,
        grid=(S // T, N_FF_CHUNKS),
        in_specs=[
            pl.BlockSpec((T, D), lambda i, c: (i, 0)),
            pl.BlockSpec((1, D), lambda i, c: (0, 0)),
            pl.BlockSpec((D, FF_CHUNK), lambda i, c: (0, c)),
            pl.BlockSpec((D, FF_CHUNK), lambda i, c: (0, c)),
            pl.BlockSpec((FF_CHUNK, D), lambda i, c: (c, 0)),
            pl.BlockSpec(gu_block, gu_map),
            pl.BlockSpec(gu_block, gu_map),
            pl.BlockSpec(d_block, d_map),
        ],
        out_specs=[pl.BlockSpec((T, D), lambda i, c: (i, 0)), pl.BlockSpec(gu_block, gu_map),
                   pl.BlockSpec(gu_block, gu_map), pl.BlockSpec(d_block, d_map)],
        out_shape=[jax.ShapeDtypeStruct((S, D), F32), jax.ShapeDtypeStruct(cast_gate.shape, BF16),
                   jax.ShapeDtypeStruct(cast_up.shape, BF16), jax.ShapeDtypeStruct(cast_down.shape, BF16)],
        scratch_shapes=[pltpu.VMEM((T, D), BF16), pltpu.VMEM((T, D), F32)],
        compiler_params=pltpu.CompilerParams(dimension_semantics=("arbitrary", "arbitrary"),
                                             vmem_limit_bytes=VMEM_LIMIT),
        name="dense_ffn",
    )(x, g.reshape(1, D), w_gate, w_up, w_down, cast_gate, cast_up, cast_down)


KV_EXT = N_KV_HEADS * LANES
QKV_EXT = N_HEADS * HEAD_DIM + 2 * KV_EXT


def _attn_kernel(sink_ref, x_ref, pos_ref, g_ref, freq_ref, sign_ref, wqkv_ref, bqkv_ref, wo_ref, bo_ref,
                 g2_ref, wr_ref, o_ref, route_ref,
                 qlo_scr, qhi_scr, k_scr, v_scr, o_scr):
    T = x_ref.shape[0]
    D = D_MODEL
    B = ATTN_BLOCK
    first_tile = pl.program_id(0) == 0

    @pl.when(first_tile)
    def _():
        k_scr[0:B, :] = jnp.zeros((B, KV_EXT), BF16)
        v_scr[0:B, :] = jnp.zeros((B, KV_EXT), BF16)

    x = x_ref[...]
    hn = _rmsnorm(x, g_ref[...]).astype(BF16)
    qkv = jnp.dot(hn, wqkv_ref[...], preferred_element_type=F32) + bqkv_ref[...]

    ang = pos_ref[...] * freq_ref[...]
    cos_t = jnp.cos(ang)
    sin_t = jnp.sin(ang) * sign_ref[...]
    lane = lax.broadcasted_iota(jnp.int32, (T, LANES), 1)
    first_half = (lane % HEAD_DIM) < (ROT_DIM // 2)
    lo_half = lane < HEAD_DIM

    def rope(col):
        partner = jnp.where(first_half, pltpu.roll(col, LANES - ROT_DIM // 2, 1), pltpu.roll(col, ROT_DIM // 2, 1))
        return col * cos_t + partner * sin_t

    scale = HEAD_DIM ** -0.5 * LOG2_E
    for c in range(D // LANES):
        qc = rope(qkv[:, LANES * c:LANES * (c + 1)]) * scale
        qlo_scr[:, LANES * c:LANES * (c + 1)] = jnp.where(lo_half, qc, 0.0).astype(BF16)
        qhi_scr[:, LANES * c:LANES * (c + 1)] = jnp.where(lo_half, 0.0, qc).astype(BF16)
    for g in range(N_KV_HEADS):
        kc = rope(qkv[:, D + LANES * g:D + LANES * (g + 1)])
        k_scr[B:B + T, LANES * g:LANES * (g + 1)] = kc.astype(BF16)
        vc = qkv[:, D + KV_EXT + LANES * g:D + KV_EXT + LANES * (g + 1)]
        v_scr[B:B + T, LANES * g:LANES * (g + 1)] = vc.astype(BF16)

    qi = lax.broadcasted_iota(jnp.int32, (B, B), 0)
    kj = lax.broadcasted_iota(jnp.int32, (B, B), 1)
    causal_own = kj <= qi
    window_prev = kj > qi
    key0 = kj == 0
    key0_row = lax.broadcasted_iota(jnp.int32, (1, B), 1) == 0
    lo_blk = lax.broadcasted_iota(jnp.int32, (B, LANES), 1) < HEAD_DIM

    def block_body(blk, _):
        r0 = pl.multiple_of(blk * B, B)
        k_lo = jnp.where(first_tile & (blk == 0), B, 0)
        allowed_prev = window_prev & (kj >= k_lo)
        for g in range(N_KV_HEADS):
            kk = k_scr[pl.ds(r0, 2 * B), LANES * g:LANES * (g + 1)]
            vv = v_scr[pl.ds(r0, 2 * B), LANES * g:LANES * (g + 1)]
            cols = [slice(LANES * c, LANES * (c + 1)) for c in range((GROUP // 2) * g, (GROUP // 2) * (g + 1))]
            q_all = jnp.concatenate([q_scr[pl.ds(r0, B), col] for col in cols for q_scr in (qlo_scr, qhi_scr)], axis=0)
            s_all = lax.dot_general(q_all, kk, (((1,), (1,)), ((), ())), preferred_element_type=F32)
            p_all, denoms = [], []
            for hh in range(GROUP):
                rows = slice(B * hh, B * (hh + 1))
                sink_fill = jnp.where(key0_row, sink_ref[GROUP * g + hh] * LOG2_E, NEG_INF)
                s_prev = jnp.where(allowed_prev, s_all[rows, :B], sink_fill)
                s_own = jnp.where(causal_own, s_all[rows, B:], NEG_INF)
                m = jnp.max(jnp.maximum(s_prev, s_own), axis=-1, keepdims=True)
                p_prev = jnp.exp2(s_prev - m)
                p_own = jnp.exp2(s_own - m)
                denoms.append(jnp.sum(p_prev + p_own, axis=-1, keepdims=True))
                p_all.append(jnp.concatenate([jnp.where(key0, 0.0, p_prev), p_own], axis=1).astype(BF16))
            o_all = jnp.dot(jnp.concatenate(p_all, axis=0), vv, preferred_element_type=F32)
            o_heads = [o_all[B * hh:B * (hh + 1), :] / denoms[hh] for hh in range(GROUP)]
            for p, col in enumerate(cols):
                o_scr[pl.ds(r0, B), col] = jnp.where(lo_blk, o_heads[2 * p], o_heads[2 * p + 1]).astype(BF16)
        return 0

    lax.fori_loop(0, T // B, block_body, 0)

    k_scr[0:B, :] = k_scr[T:T + B, :]
    v_scr[0:B, :] = v_scr[T:T + B, :]

    h1 = jnp.dot(o_scr[...], wo_ref[...], preferred_element_type=F32) + bo_ref[...] + x
    _store_token_major(o_ref, h1)

    hn2 = _rmsnorm(h1, g2_ref[...]).astype(BF16)
    logits = jnp.dot(hn2, wr_ref[...], preferred_element_type=F32)
    lane_f = lane.astype(F32)
    lg = jnp.where(lane < N_EXPERTS, logits, -jnp.inf)
    v1 = jnp.max(lg, axis=-1, keepdims=True)
    i1 = jnp.min(jnp.where(lg == v1, lane_f, float(LANES)), axis=-1, keepdims=True)
    lg2 = jnp.where(lane_f == i1, -jnp.inf, lg)
    v2 = jnp.max(lg2, axis=-1, keepdims=True)
    i2 = jnp.min(jnp.where(lg2 == v2, lane_f, float(LANES)), axis=-1, keepdims=True)
    e2 = jnp.exp(v2 - v1)
    gate1 = 1.0 / (1.0 + e2)
    gate2 = e2 / (1.0 + e2)
    route_ref[...] = jnp.where(lane == 0, i1, jnp.where(lane == 1, i2,
                                                        jnp.where(lane == 2, gate1, jnp.where(lane == 3, gate2, 0.0))))


def _attn_mixer(x, pos_f, g, freq_lane, sign_lane, w_qkv_ext, b_qkv_ext, sinks, w_o, b_o, g2, w_router_pad):
    S, D = x.shape
    T = ATTN_TILE
    B = ATTN_BLOCK
    grid_spec = pltpu.PrefetchScalarGridSpec(
        num_scalar_prefetch=1,
        grid=(S // T,),
        in_specs=[
            pl.BlockSpec((T, D), lambda i, s: (i, 0)),
            pl.BlockSpec((T, 1), lambda i, s: (i, 0)),
            _const_spec((1, D)),
            _const_spec((1, LANES)),
            _const_spec((1, LANES)),
            _const_spec((D, QKV_EXT)),
            _const_spec((1, QKV_EXT)),
            _const_spec((D, D)),
            _const_spec((1, D)),
            _const_spec((1, D)),
            _const_spec((D, LANES)),
        ],
        out_specs=[pl.BlockSpec((T * ROWS_PER_TOKEN, LANES), lambda i, s: (i, 0)),
                   pl.BlockSpec((T, LANES), lambda i, s: (i, 0))],
        scratch_shapes=[
            pltpu.VMEM((T, D), BF16), pltpu.VMEM((T, D), BF16),
            pltpu.VMEM((T + B, KV_EXT), BF16), pltpu.VMEM((T + B, KV_EXT), BF16),
            pltpu.VMEM((T, D), BF16),
        ],
    )
    return pl.pallas_call(
        _attn_kernel,
        grid_spec=grid_spec,
        out_shape=[jax.ShapeDtypeStruct((S * ROWS_PER_TOKEN, LANES), F32), jax.ShapeDtypeStruct((S, LANES), F32)],
        compiler_params=pltpu.CompilerParams(dimension_semantics=("arbitrary",), vmem_limit_bytes=VMEM_LIMIT),
        name="attn_mixer",
    )(sinks, x, pos_f, g.reshape(1, D), freq_lane, sign_lane, w_qkv_ext, b_qkv_ext, w_o, b_o.reshape(1, D),
      g2.reshape(1, D), w_router_pad)


def _moe_kernel(texp_ref, nvt_ref, src0_ref, srcn_ref, dstp_ref, dstl_ref, h_hbm, g_ref, wg_ref, wu_ref, wd_ref,
                y_hbm, x_scr, xn_scr, acc_scr, y_scr, sems):
    i = pl.program_id(0)
    c = pl.program_id(1)
    n_tiles = pl.num_programs(0)
    n_chunks = pl.num_programs(1)
    T = xn_scr.shape[0]
    R = ROWS_PER_TOKEN
    valid = i < nvt_ref[0]
    slot = i % 2
    other = 1 - slot
    SCATTER_SEM = 2

    def row_copy(src, src_row, dst, dst_row, sem):
        return pltpu.make_async_copy(src.at[pl.ds(pl.multiple_of(src_row, R), R), :],
                                     dst.at[pl.ds(pl.multiple_of(dst_row, R), R), :], sem)

    def tile_rows(t):
        return nvt_ref[1 + t]

    def window_offset(t):
        return nvt_ref[1 + 2 * n_tiles + t]

    prev_tile = jnp.maximum(i - 1, 0)
    next_tile = jnp.minimum(i + 1, n_tiles - 1)
    rows_prev = jnp.where(i > 0, tile_rows(prev_tile), 0)
    pad_base = y_hbm.shape[0] - T * R

    def dest_row(window_ref, offset, r, n_rows):
        return jnp.where(r < n_rows, window_ref[0, offset + r], pad_base + r * R)

    def wait_gather(s):
        pltpu.make_async_copy(h_hbm.at[pl.ds(0, T * R), :], x_scr.at[s], sems.at[s]).wait()

    def wait_scatter():
        pltpu.make_async_copy(y_scr, y_hbm.at[pl.ds(0, T * R), :], sems.at[SCATTER_SEM]).wait()

    @pl.when((i == 0) & (c == 0))
    def _():
        y_scr[...] = jnp.zeros_like(y_scr)
        acc_scr[...] = jnp.zeros_like(acc_scr)

        def issue(r, carry):
            row_copy(h_hbm, src0_ref[0, window_offset(0) + r], x_scr.at[0], r * R, sems.at[0]).start()
            return carry
        lax.fori_loop(0, T, issue, 0)

    @pl.when(c == 0)
    def _():
        wait_gather(slot)

    def row_dmas():
        off_next = window_offset(next_tile)
        off_prev = window_offset(prev_tile)
        for r in range(T):
            row_copy(h_hbm, srcn_ref[0, off_next + r], x_scr.at[other], r * R, sems.at[other]).start()
            row_copy(y_scr, r * R, y_hbm, dest_row(dstp_ref, off_prev, r, rows_prev),
                     sems.at[SCATTER_SEM]).start()

    def expert_chunk(first, rows):
        xn = xn_scr[0:rows, :]
        gate = jnp.dot(xn, wg_ref[0], preferred_element_type=F32)
        up = jnp.dot(xn, wu_ref[0], preferred_element_type=F32)
        mid = (jax.nn.silu(gate) * up).astype(BF16)
        down = jnp.dot(mid, wd_ref[0], preferred_element_type=F32)
        if first:
            acc_scr[0:rows, :] = down
        else:
            acc_scr[0:rows, :] += down

    few_rows = tile_rows(i) <= T // 2

    for few, rows in ((False, T), (True, T // 2)):
        branch = valid & (few_rows if few else jnp.logical_not(few_rows))

        @pl.when(branch & (c == 0))
        def _():
            xn_scr[0:rows, :] = _rmsnorm(_load_token_major(x_scr.at[slot], rows), g_ref[...]).astype(BF16)
            row_dmas()
            expert_chunk(True, rows)

        @pl.when(branch & (c != 0))
        def _():
            expert_chunk(False, rows)

    @pl.when(jnp.logical_not(valid) & (c == 0))
    def _():
        row_dmas()

    @pl.when(c == n_chunks - 1)
    def _():
        wait_scatter()

    @pl.when(valid & (c == n_chunks - 1))
    def _():
        _store_token_major(y_scr, acc_scr[...])

    @pl.when((i == n_tiles - 1) & (c == n_chunks - 1))
    def _():
        wait_gather(other)


        def issue(r, carry):
            row_copy(y_scr, r * R, y_hbm, dest_row(dstl_ref, window_offset(n_tiles - 1), r, tile_rows(n_tiles - 1)),
                     sems.at[SCATTER_SEM]).start()
            return carry
        lax.fori_loop(0, T, issue, 0)
        wait_scatter()


def _moe_experts(h1_tm, g, tile_expert, n_valid_tiles, src_rows, dst_rows, w_gate, w_up, w_down, n_out_tokens):
    D = D_MODEL
    T = MOE_TILE
    NT = tile_expert.shape[0]
    NC = N_MOE_CHUNKS
    FC = D_FF // NC

    def chunk(i, c, nvt):
        return jnp.where(i < nvt[0], c, NC - 1)

    def window_spec(tile_of_step):
        return pl.BlockSpec((pl.Element(1), pl.Element(T + LANES)),
                            lambda i, c, te, nvt: (0, nvt[1 + NT + tile_of_step(i)] * LANES),
                            memory_space=pltpu.SMEM)

    grid_spec = pltpu.PrefetchScalarGridSpec(
        num_scalar_prefetch=2,
        grid=(NT, NC),
        in_specs=[
            window_spec(lambda i: 0),
            window_spec(lambda i: jnp.minimum(i + 1, NT - 1)),
            window_spec(lambda i: jnp.maximum(i - 1, 0)),
            window_spec(lambda i: NT - 1),
            pl.BlockSpec(memory_space=pl.ANY),
            pl.BlockSpec((1, D), lambda i, c, te, nvt: (0, 0)),
            pl.BlockSpec((1, D, FC), lambda i, c, te, nvt: (te[i], 0, chunk(i, c, nvt))),
            pl.BlockSpec((1, D, FC), lambda i, c, te, nvt: (te[i], 0, chunk(i, c, nvt))),
            pl.BlockSpec((1, FC, D), lambda i, c, te, nvt: (te[i], chunk(i, c, nvt), 0)),
        ],
        out_specs=pl.BlockSpec(memory_space=pl.ANY),
        scratch_shapes=[pltpu.VMEM((2, T * ROWS_PER_TOKEN, LANES), F32), pltpu.VMEM((T, D), BF16),
                        pltpu.VMEM((T, D), F32), pltpu.VMEM((T * ROWS_PER_TOKEN, LANES), F32),
                        pltpu.SemaphoreType.DMA((3,))],
    )
    return pl.pallas_call(
        _moe_kernel,
        grid_spec=grid_spec,
        out_shape=jax.ShapeDtypeStruct((n_out_tokens * ROWS_PER_TOKEN, LANES), F32),
        compiler_params=pltpu.CompilerParams(dimension_semantics=("arbitrary", "arbitrary"),
                                             vmem_limit_bytes=VMEM_LIMIT),
        name="moe_experts",
    )(tile_expert, n_valid_tiles, src_rows, src_rows, dst_rows, dst_rows, h1_tm, g.reshape(1, D),
      w_gate, w_up, w_down)


def _moe_plan(route, n_tokens):
    T = MOE_TILE
    n_assign = TOP_K * n_tokens
    n_tiles = n_assign // T + N_EXPERTS
    e_flat = route[:, 0:TOP_K].astype(jnp.int32).reshape(-1)
    counts = jnp.sum((e_flat[:, None] == jnp.arange(N_EXPERTS, dtype=jnp.int32)[None, :]).astype(jnp.int32), axis=0)
    order = jnp.argsort(e_flat, stable=True).astype(jnp.int32)
    tiles_per = (counts + T - 1) // T
    tile_end = jnp.cumsum(tiles_per)
    n_valid = tile_end[-1]
    tile_ids = jnp.arange(n_tiles, dtype=jnp.int32)
    last_valid = jnp.maximum(n_valid - 1, 0)
    texp = jnp.sum((jnp.minimum(tile_ids, last_valid)[:, None] >= tile_end[None, :]).astype(jnp.int32), axis=1)
    texp = jnp.minimum(texp, N_EXPERTS - 1)
    onehot = (texp[:, None] == jnp.arange(N_EXPERTS, dtype=jnp.int32)[None, :]).astype(jnp.int32)
    pick = lambda v: jnp.sum(onehot * v[None, :], axis=1)
    j = tile_ids - pick(tile_end - tiles_per)
    start = jnp.where(tile_ids < n_valid, pick(jnp.cumsum(counts) - counts) + j * T, 0)
    n_rows = jnp.where(tile_ids < n_valid, jnp.clip(pick(counts) - j * T, 0, T), 0)
    token = order // TOP_K
    slot = order % TOP_K
    spare = jnp.zeros((T + LANES,), jnp.int32)
    src_sorted = jnp.concatenate([token * ROWS_PER_TOKEN, spare]).reshape(1, -1)
    dst_sorted = jnp.concatenate([(slot * n_tokens + token) * ROWS_PER_TOKEN, spare]).reshape(1, -1)
    tile_counts = jnp.concatenate([n_valid.reshape(1), n_rows, start // LANES, start % LANES]).astype(jnp.int32)
    return texp, tile_counts, src_sorted, dst_sorted


def _combine_kernel(h_ref, y0_ref, y1_ref, route_ref, g_ref, o_ref):
    T = o_ref.shape[0]
    route = route_ref[...]
    moe = route[:, 2:3] * _load_token_major(y0_ref, T) + route[:, 3:4] * _load_token_major(y1_ref, T)
    o_ref[...] = _rmsnorm(_load_token_major(h_ref, T) + moe, g_ref[...])


def _combine(h1_tm, y2_tm, route, g):
    D = D_MODEL
    S = route.shape[0]
    T = OUT_TILE
    nb = S // T
    tm_block = (T * ROWS_PER_TOKEN, LANES)
    return pl.pallas_call(
        _combine_kernel,
        grid=(nb,),
        in_specs=[
            pl.BlockSpec(tm_block, lambda i: (i, 0)),
            pl.BlockSpec(tm_block, lambda i: (i, 0)),
            pl.BlockSpec(tm_block, lambda i: (i + nb, 0)),
            pl.BlockSpec((T, LANES), lambda i: (i, 0)),
            pl.BlockSpec((1, D), lambda i: (0, 0)),
        ],
        out_specs=pl.BlockSpec((T, D), lambda i: (i, 0)),
        out_shape=jax.ShapeDtypeStruct((S, D), F32),
        compiler_params=pltpu.CompilerParams(dimension_semantics=("arbitrary",), vmem_limit_bytes=VMEM_LIMIT),
        name="moe_combine",
    )(h1_tm, y2_tm, y2_tm, route, g.reshape(1, D))


def kernel(x, positions, norm_mix, norm_ffn, norm_final, lru_w_in, lru_conv_w, lru_conv_b, lru_w_a, lru_b_a, lru_w_x, lru_b_x, lru_lambda, lru_w_out, attn_w_qkv, attn_b_qkv, attn_sinks, attn_w_o, attn_b_o, ffn_w_gate, ffn_w_up, ffn_w_down, moe_w_router, moe_w_gate, moe_w_up, moe_w_down):
    B, S, D = x.shape
    assert B == 1 and D == D_MODEL and S % FFN_TILE == 0
    h = x.reshape(S, D)

    w_ax = jnp.concatenate([lru_w_a[0], lru_w_x[0]], axis=-1).astype(BF16)
    h, ffn_wg, ffn_wu, ffn_wd = _lru_mixer(h, norm_mix[0], lru_w_in[0].astype(BF16), lru_conv_w[0], lru_conv_b[0],
                                           w_ax, lru_b_a[0], lru_b_x[0], lru_lambda[0], lru_w_out[0].astype(BF16),
                                           ffn_w_gate[0], ffn_w_up[0], ffn_w_down[0])
    h, moe_wg, moe_wu, moe_wd = _dense_ffn(h, norm_ffn[0], ffn_wg, ffn_wu, ffn_wd,
                                           moe_w_gate[0], moe_w_up[0], moe_w_down[0])

    q_dim = N_HEADS * HEAD_DIM
    kv_dim = N_KV_HEADS * HEAD_DIM

    def dup_heads(w):
        w4 = w.reshape(w.shape[:-1] + (N_KV_HEADS, 1, HEAD_DIM))
        return jnp.broadcast_to(w4, w.shape[:-1] + (N_KV_HEADS, 2, HEAD_DIM)).reshape(w.shape[:-1] + (KV_EXT,))

    wqkv, bqkv = attn_w_qkv[0], attn_b_qkv[0]
    w_qkv_ext = jnp.concatenate([wqkv[:, :q_dim], dup_heads(wqkv[:, q_dim:q_dim + kv_dim]),
                                 dup_heads(wqkv[:, q_dim + kv_dim:])], axis=1).astype(BF16)
    b_qkv_ext = jnp.concatenate([bqkv[:q_dim], dup_heads(bqkv[q_dim:q_dim + kv_dim]),
                                 dup_heads(bqkv[q_dim + kv_dim:])]).reshape(1, QKV_EXT)
    inv_freq = ROPE_THETA ** (-jnp.arange(0, ROT_DIM, 2, dtype=F32) / ROT_DIM)
    d_in_head = jnp.arange(LANES) % HEAD_DIM
    freq_lane = jnp.where(d_in_head < ROT_DIM, inv_freq[d_in_head % (ROT_DIM // 2)], 0.0).reshape(1, LANES)
    sign_lane = jnp.where(d_in_head < ROT_DIM // 2, -1.0, jnp.where(d_in_head < ROT_DIM, 1.0, 0.0))
    sign_lane = sign_lane.astype(F32).reshape(1, LANES)
    w_router_pad = jnp.pad(moe_w_router[0], ((0, 0), (0, LANES - N_EXPERTS))).astype(BF16)
    pos_f = positions.reshape(S, 1).astype(F32)
    h1_tm, route = _attn_mixer(h, pos_f, norm_mix[1], freq_lane, sign_lane, w_qkv_ext, b_qkv_ext,
                               attn_sinks[0], attn_w_o[0].astype(BF16), attn_b_o[0], norm_ffn[1], w_router_pad)

    texp, n_valid, src_rows, dst_rows = _moe_plan(route, S)
    y2_tm = _moe_experts(h1_tm, norm_ffn[1], texp, n_valid, src_rows, dst_rows,
                         moe_wg, moe_wu, moe_wd,
                         TOP_K * S + MOE_TILE)
    out = _combine(h1_tm, y2_tm, route, norm_final)
    return out.reshape(B, S, D)
```

```python
import jax
import jax.numpy as jnp
from jax import lax
from jax.experimental import pallas as pl
from jax.experimental.pallas import tpu as pltpu

F32 = jnp.float32
BF16 = jnp.bfloat16

D_MODEL = 1024
N_RNN_BLOCKS = 8
RNN_BLOCK = D_MODEL // N_RNN_BLOCKS
CONV_WIDTH = 4
LRU_C = 8.0
N_HEADS = 16
N_KV_HEADS = 4
HEAD_DIM = 64
GROUP = N_HEADS // N_KV_HEADS
WINDOW = 128
ATTN_BLOCK = 128
assert WINDOW == ATTN_BLOCK
ROPE_THETA = 500000.0
ROT_DIM = HEAD_DIM // 4
D_FF = 3584
N_EXPERTS = 8
TOP_K = 2
EPS = 1e-6
NEG_INF = -1e30
LOG2_E = 1.4426950408889634

LANES = 128
SUBLANES = 8
VMEM_LIMIT = 56 * 1024 * 1024

SEQ_TILE = 512
ATTN_TILE = 1024
LRU_CHUNKS = 4
FFN_TILE = 1024
FF_CHUNK = 512
N_FF_CHUNKS = D_FF // FF_CHUNK
MOE_TILE = 512
N_MOE_CHUNKS = 2
OUT_TILE = 1024


def _rmsnorm(x, g):
    return x * lax.rsqrt(jnp.mean(x * x, axis=-1, keepdims=True) + EPS) * g


ROWS_PER_TOKEN = D_MODEL // LANES


def _store_token_major(ref, val):
    T = val.shape[0]
    for c in range(ROWS_PER_TOKEN):
        ref[pl.ds(c, T, stride=ROWS_PER_TOKEN), :] = val[:, LANES * c:LANES * (c + 1)]


def _load_token_major(ref, T):
    return jnp.concatenate([ref[pl.ds(c, T, stride=ROWS_PER_TOKEN), :] for c in range(ROWS_PER_TOKEN)], axis=1)


def _const_spec(shape):
    n = len(shape)
    return pl.BlockSpec(shape, lambda *_: (0,) * n, pipeline_mode=pl.Buffered(1))


def _time_permutation(T):
    rho = jnp.arange(T)
    t_of_rho = (rho % SUBLANES) * (T // SUBLANES) + rho // SUBLANES
    perm = (t_of_rho[:, None] == jnp.arange(T)[None, :]).astype(BF16)
    return perm, perm.T


def _lru_kernel(xin_ref, xres_ref, g_ref, perm_ref, unperm_ref, win_ref, cw_ref, cb_ref, wax_ref, ba_ref, bx_ref,
                lam_ref, wout_ref, cast_g_ref, cast_u_ref, cast_d_ref,
                o_ref, bf_g_ref, bf_u_ref, bf_d_ref, proj_scr, tail_scr, carry_scr):
    i = pl.program_id(0)
    n_tiles = pl.num_programs(0) - 1
    T = xin_ref.shape[0]
    D = D_MODEL
    C = T // LRU_CHUNKS
    GC = C // SUBLANES
    G = T // SUBLANES
    HALO = (CONV_WIDTH - 1) * SUBLANES

    def cast_slices():
        bf_g_ref[...] = cast_g_ref[...].astype(BF16)
        bf_u_ref[...] = cast_u_ref[...].astype(BF16)
        bf_d_ref[...] = cast_d_ref[...].astype(BF16)

    @pl.when(i == 0)
    def _():
        tail_scr[...] = jnp.zeros_like(tail_scr)
        carry_scr[...] = jnp.zeros_like(carry_scr)

    PW = 2 * D // LRU_CHUNKS

    def project_chunks():
        hn = _rmsnorm(xin_ref[...], g_ref[...]).astype(BF16)
        hp = jnp.dot(perm_ref[...], hn, preferred_element_type=F32).astype(BF16)
        for k in range(LRU_CHUNKS):
            yield jnp.dot(hp, win_ref[:, PW * k:PW * (k + 1)], preferred_element_type=F32)

    def recur_chunks():
        sub = lax.broadcasted_iota(jnp.int32, (SUBLANES, D), 0)
        cw = cw_ref[...]
        softplus_neg_lam = jax.nn.softplus(-lam_ref[...])
        last = proj_scr[T - HALO:T, D:]
        prev_groups = []
        for k in range(CONV_WIDTH - 1):
            rows = slice(SUBLANES * k, SUBLANES * (k + 1))
            prev_groups.append(pltpu.roll(jnp.where(sub == SUBLANES - 1, tail_scr[rows, :], last[rows, :]), 1, 0))
        tail_scr[...] = last

        ys, hs, decay = [], [], []
        for k in range(LRU_CHUNKS):
            pc = proj_scr[C * k:C * (k + 1), :]
            ys.append(jax.nn.gelu(pc[:, :D], approximate=True))
            xb = pc[:, D:]
            xc = cb_ref[...] + xb * cw[CONV_WIDTH - 1:CONV_WIDTH, :]
            for back in range(1, CONV_WIDTH):
                shifted = jnp.concatenate(prev_groups[CONV_WIDTH - 1 - back:] + [xb[:C - SUBLANES * back, :]], axis=0)
                xc = xc + shifted * cw[CONV_WIDTH - 1 - back:CONV_WIDTH - back, :]
            prev_groups = [xb[C - HALO + SUBLANES * q:C - HALO + SUBLANES * (q + 1), :] for q in range(CONV_WIDTH - 1)]

            xcb = xc.astype(BF16)
            r_parts, i_parts = [], []
            for n in range(N_RNN_BLOCKS):
                gn = jnp.dot(xcb[:, RNN_BLOCK * n:RNN_BLOCK * (n + 1)], wax_ref[n], preferred_element_type=F32)
                r_parts.append(gn[:, :RNN_BLOCK])
                i_parts.append(gn[:, RNN_BLOCK:])
            r = jax.nn.sigmoid(jnp.concatenate(r_parts, axis=1) + ba_ref[...])
            ig = jax.nn.sigmoid(jnp.concatenate(i_parts, axis=1) + bx_ref[...])
            log_a = -LRU_C * r * softplus_neg_lam
            a = jnp.exp(log_a)
            z = jnp.tanh(-log_a) * (1.0 + a * a)
            b = jnp.where(z > 0.0, z * lax.rsqrt(z), 0.0) * (ig * xc)

            for j in range(GC):
                rows = slice(SUBLANES * j, SUBLANES * (j + 1))
                if hs:
                    hs.append(a[rows, :] * hs[-1] + b[rows, :])
                    decay.append(a[rows, :] * decay[-1])
                else:
                    hs.append(b[rows, :])
                    decay.append(a[rows, :])
            yield None

        p_inc, e_inc = decay[-1], hs[-1]
        for k in (1, 2, 4):
            m = sub >= k
            p_sh = jnp.where(m, pltpu.roll(p_inc, k, 0), 1.0)
            e_sh = jnp.where(m, pltpu.roll(e_inc, k, 0), 0.0)
            e_inc = e_inc + p_inc * e_sh
            p_inc = p_inc * p_sh
        h0 = carry_scr[...]
        after = p_inc * h0 + e_inc
        carry_scr[...] = after[SUBLANES - 1:SUBLANES, :]
        enter = jnp.where(sub == 0, h0, pltpu.roll(after, 1, 0))
        h = jnp.concatenate([hs[j] + decay[j] * enter for j in range(G)], axis=0)
        hy = (h * jnp.concatenate(ys, axis=0)).astype(BF16)
        hy = jnp.dot(unperm_ref[...], hy, preferred_element_type=F32).astype(BF16)
        yield jnp.dot(hy, wout_ref[...], preferred_element_type=F32) + xres_ref[...]

    @pl.when(i == 0)
    def _():
        cast_slices()
        for k, pc in enumerate(project_chunks()):
            proj_scr[:, PW * k:PW * (k + 1)] = pc

    @pl.when((i > 0) & (i < n_tiles))
    def _():
        cast_slices()
        rec = recur_chunks()
        new_proj = []
        for pc in project_chunks():
            new_proj.append(pc)
            next(rec)
        o_ref[...] = next(rec)
        for k, pc in enumerate(new_proj):
            proj_scr[:, PW * k:PW * (k + 1)] = pc

    @pl.when(i == n_tiles)
    def _():
        o_ref[...] = list(recur_chunks())[-1]


def _lru_mixer(x, g, w_in, conv_w, conv_b, w_ax, b_a, b_x, lam, w_out, cast_gate, cast_up, cast_down):
    S, D = x.shape
    T = SEQ_TILE
    n_tiles = S // T
    row = lambda v: v.reshape(1, D)
    perm, unperm = _time_permutation(T)

    def cast_spec(w):
        return pl.BlockSpec((w.shape[0] // n_tiles, w.shape[1]), lambda i: (jnp.minimum(i, n_tiles - 1), 0))

    casts = (cast_gate, cast_up, cast_down)
    return pl.pallas_call(
        _lru_kernel,
        grid=(n_tiles + 1,),
        in_specs=[
            pl.BlockSpec((T, D), lambda i: (jnp.minimum(i, n_tiles - 1), 0)),
            pl.BlockSpec((T, D), lambda i: (jnp.maximum(i - 1, 0), 0)),
            _const_spec((1, D)),
            _const_spec((T, T)),
            _const_spec((T, T)),
            _const_spec((D, 2 * D)),
            _const_spec((CONV_WIDTH, D)),
            _const_spec((1, D)),
            _const_spec((N_RNN_BLOCKS, RNN_BLOCK, 2 * RNN_BLOCK)),
            _const_spec((1, D)),
            _const_spec((1, D)),
            _const_spec((1, D)),
            _const_spec((D, D)),
        ] + [cast_spec(w) for w in casts],
        out_specs=[pl.BlockSpec((T, D), lambda i: (jnp.maximum(i - 1, 0), 0))] + [cast_spec(w) for w in casts],
        out_shape=[jax.ShapeDtypeStruct((S, D), F32)] + [jax.ShapeDtypeStruct(w.shape, BF16) for w in casts],
        scratch_shapes=[pltpu.VMEM((T, 2 * D), F32), pltpu.VMEM(((CONV_WIDTH - 1) * SUBLANES, D), F32),
                        pltpu.VMEM((1, D), F32)],
        compiler_params=pltpu.CompilerParams(dimension_semantics=("arbitrary",), vmem_limit_bytes=VMEM_LIMIT),
        name="lru_mixer",
    )(x, x, row(g), perm, unperm, w_in, conv_w, row(conv_b), w_ax, row(b_a), row(b_x), row(lam), w_out, *casts)


def _ffn_kernel(x_ref, g_ref, wg_ref, wu_ref, wd_ref, cast_g_ref, cast_u_ref, cast_d_ref,
                o_ref, bf_g_ref, bf_u_ref, bf_d_ref, xn_scr, acc_scr):
    c = pl.program_id(1)
    last = pl.num_programs(1) - 1

    def ff_chunk():
        bf_g_ref[...] = cast_g_ref[...].astype(BF16)
        bf_u_ref[...] = cast_u_ref[...].astype(BF16)
        bf_d_ref[...] = cast_d_ref[...].astype(BF16)
        xn = xn_scr[...]
        gate = jnp.dot(xn, wg_ref[...], preferred_element_type=F32)
        up = jnp.dot(xn, wu_ref[...], preferred_element_type=F32)
        mid = (jax.nn.silu(gate) * up).astype(BF16)
        return jnp.dot(mid, wd_ref[...], preferred_element_type=F32)

    @pl.when(c == 0)
    def _():
        xn_scr[...] = _rmsnorm(x_ref[...], g_ref[...]).astype(BF16)
        acc_scr[...] = ff_chunk()

    @pl.when((c > 0) & (c < last))
    def _():
        acc_scr[...] += ff_chunk()

    @pl.when(c == last)
    def _():
        o_ref[...] = x_ref[...] + (acc_scr[...] + ff_chunk())


def _dense_ffn(x, g, w_gate, w_up, w_down, cast_gate, cast_up, cast_down):
    S, D = x.shape
    T = FFN_TILE
    n_steps = (S // T) * N_FF_CHUNKS
    E = cast_gate.shape[0]
    row_parts = n_steps // (E * N_FF_CHUNKS)
    assert row_parts * E * N_FF_CHUNKS == n_steps
    gu_block = (1, D // row_parts, D_FF // N_FF_CHUNKS)
    d_block = (1, D_FF // (row_parts * N_FF_CHUNKS), D)

    def gu_map(i, c):
        s = i * N_FF_CHUNKS + c
        return (s // (row_parts * N_FF_CHUNKS), (s // N_FF_CHUNKS) % row_parts, c)

    def d_map(i, c):
        s = i * N_FF_CHUNKS + c
        return (s // (row_parts * N_FF_CHUNKS), s % (row_parts * N_FF_CHUNKS), 0)

    return pl.pallas_call(
        _ffn_kernel,
        grid=(S // T, N_FF_CHUNKS),
        in_specs=[
            pl.BlockSpec((T, D), lambda i, c: (i, 0)),
            pl.BlockSpec((1, D), lambda i, c: (0, 0)),
            pl.BlockSpec((D, FF_CHUNK), lambda i, c: (0, c)),
            pl.BlockSpec((D, FF_CHUNK), lambda i, c: (0, c)),
            pl.BlockSpec((FF_CHUNK, D), lambda i, c: (c, 0)),
            pl.BlockSpec(gu_block, gu_map),
            pl.BlockSpec(gu_block, gu_map),
            pl.BlockSpec(d_block, d_map),
        ],
        out_specs=[pl.BlockSpec((T, D), lambda i, c: (i, 0)), pl.BlockSpec(gu_block, gu_map),
                   pl.BlockSpec(gu_block, gu_map), pl.BlockSpec(d_block, d_map)],
        out_shape=[jax.ShapeDtypeStruct((S, D), F32), jax.ShapeDtypeStruct(cast_gate.shape, BF16),
                   jax.ShapeDtypeStruct(cast_up.shape, BF16), jax.ShapeDtypeStruct(cast_down.shape, BF16)],
        scratch_shapes=[pltpu.VMEM((T, D), BF16), pltpu.VMEM((T, D), F32)],
        compiler_params=pltpu.CompilerParams(dimension_semantics=("arbitrary", "arbitrary"),
                                             vmem_limit_bytes=VMEM_LIMIT),
        name="dense_ffn",
    )(x, g.reshape(1, D), w_gate, w_up, w_down, cast_gate, cast_up, cast_down)


KV_EXT = N_KV_HEADS * LANES
QKV_EXT = N_HEADS * HEAD_DIM + 2 * KV_EXT


def _attn_kernel(sink_ref, x_ref, xprev_ref, pos_ref, g_ref, freq_ref, sign_ref, wqkv_ref, bqkv_ref, wo_ref, bo_ref,
                 g2_ref, wr_ref, o_ref, route_ref,
                 qlo_scr, qhi_scr, k_scr, v_scr, o_scr, wo_scr):
    i = pl.program_id(0)
    n_tiles = pl.num_programs(0) - 1
    T = x_ref.shape[0]
    D = D_MODEL
    B = ATTN_BLOCK
    first_tile = i == 0
    slot = i % 2
    lane = lax.broadcasted_iota(jnp.int32, (T, LANES), 1)

    @pl.when(first_tile)
    def _():
        k_scr[0:B, :] = jnp.zeros((B, KV_EXT), BF16)
        v_scr[0:B, :] = jnp.zeros((B, KV_EXT), BF16)
        o_scr[...] = jnp.zeros_like(o_scr)

    @pl.when(i < n_tiles)
    def _():
        _attn_tile(sink_ref, x_ref, pos_ref, g_ref, freq_ref, sign_ref, wqkv_ref, bqkv_ref, wo_ref,
                   qlo_scr, qhi_scr, k_scr, v_scr, o_scr, wo_scr, first_tile, slot, lane)
        _attn_finish(wo_scr[...] + bo_ref[...] + xprev_ref[...], g2_ref, wr_ref, o_ref, route_ref, lane)

    @pl.when(i == n_tiles)
    def _():
        proj = jnp.dot(o_scr[1 - slot], wo_ref[...], preferred_element_type=F32)
        _attn_finish(proj + bo_ref[...] + xprev_ref[...], g2_ref, wr_ref, o_ref, route_ref, lane)


def _attn_tile(sink_ref, x_ref, pos_ref, g_ref, freq_ref, sign_ref, wqkv_ref, bqkv_ref, wo_ref,
               qlo_scr, qhi_scr, k_scr, v_scr, o_scr, wo_scr, first_tile, slot, lane):
    T = x_ref.shape[0]
    D = D_MODEL
    B = ATTN_BLOCK
    x = x_ref[...]
    hn = _rmsnorm(x, g_ref[...]).astype(BF16)
    qkv = jnp.dot(hn, wqkv_ref[...], preferred_element_type=F32) + bqkv_ref[...]

    ang = pos_ref[...] * freq_ref[...]
    cos_t = jnp.cos(ang)
    sin_t = jnp.sin(ang) * sign_ref[...]
    first_half = (lane % HEAD_DIM) < (ROT_DIM // 2)
    lo_half = lane < HEAD_DIM

    def rope(col):
        partner = jnp.where(first_half, pltpu.roll(col, LANES - ROT_DIM // 2, 1), pltpu.roll(col, ROT_DIM // 2, 1))
        return col * cos_t + partner * sin_t

    scale = HEAD_DIM ** -0.5 * LOG2_E
    for c in range(D // LANES):
        qc = rope(qkv[:, LANES * c:LANES * (c + 1)]) * scale
        qlo_scr[:, LANES * c:LANES * (c + 1)] = jnp.where(lo_half, qc, 0.0).astype(BF16)
        qhi_scr[:, LANES * c:LANES * (c + 1)] = jnp.where(lo_half, 0.0, qc).astype(BF16)
    for g in range(N_KV_HEADS):
        kc = rope(qkv[:, D + LANES * g:D + LANES * (g + 1)])
        k_scr[B:B + T, LANES * g:LANES * (g + 1)] = kc.astype(BF16)
        vc = qkv[:, D + KV_EXT + LANES * g:D + KV_EXT + LANES * (g + 1)]
        v_scr[B:B + T, LANES * g:LANES * (g + 1)] = vc.astype(BF16)

    qi = lax.broadcasted_iota(jnp.int32, (B, B), 0)
    kj = lax.broadcasted_iota(jnp.int32, (B, B), 1)
    causal_own = kj <= qi
    window_prev = kj > qi
    key0 = kj == 0
    key0_row = lax.broadcasted_iota(jnp.int32, (1, B), 1) == 0
    lo_blk = lax.broadcasted_iota(jnp.int32, (B, LANES), 1) < HEAD_DIM

    def block_body(blk):
        r0 = pl.multiple_of(blk * B, B)
        k_lo = jnp.where(first_tile & (blk == 0), B, 0)
        allowed_prev = window_prev & (kj >= k_lo)
        for g in range(N_KV_HEADS):
            kk = k_scr[pl.ds(r0, 2 * B), LANES * g:LANES * (g + 1)]
            vv = v_scr[pl.ds(r0, 2 * B), LANES * g:LANES * (g + 1)]
            cols = [slice(LANES * c, LANES * (c + 1)) for c in range((GROUP // 2) * g, (GROUP // 2) * (g + 1))]
            q_all = jnp.concatenate([q_scr[pl.ds(r0, B), col] for col in cols for q_scr in (qlo_scr, qhi_scr)], axis=0)
            s_all = lax.dot_general(q_all, kk, (((1,), (1,)), ((), ())), preferred_element_type=F32)
            p_all, denoms = [], []
            for hh in range(GROUP):
                rows = slice(B * hh, B * (hh + 1))
                sink_fill = jnp.where(key0_row, sink_ref[GROUP * g + hh] * LOG2_E, NEG_INF)
                s_prev = jnp.where(allowed_prev, s_all[rows, :B], sink_fill)
                s_own = jnp.where(causal_own, s_all[rows, B:], NEG_INF)
                m = jnp.max(jnp.maximum(s_prev, s_own), axis=-1, keepdims=True)
                p_prev = jnp.exp2(s_prev - m)
                p_own = jnp.exp2(s_own - m)
                denoms.append(jnp.sum(p_prev + p_own, axis=-1, keepdims=True))
                p_all.append(jnp.concatenate([jnp.where(key0, 0.0, p_prev), p_own], axis=1).astype(BF16))
            o_all = jnp.dot(jnp.concatenate(p_all, axis=0), vv, preferred_element_type=F32)
            o_heads = [o_all[B * hh:B * (hh + 1), :] / denoms[hh] for hh in range(GROUP)]
            for p, col in enumerate(cols):
                o_scr[slot, pl.ds(r0, B), col] = jnp.where(lo_blk, o_heads[2 * p], o_heads[2 * p + 1]).astype(BF16)

    n_pairs = T // (2 * B)
    wo_cols = D // n_pairs

    def pair_body(it, _):
        block_body(2 * it)
        block_body(2 * it + 1)
        c0 = pl.multiple_of(it * wo_cols, wo_cols)
        wo_scr[:, pl.ds(c0, wo_cols)] = jnp.dot(o_scr[1 - slot], wo_ref[:, pl.ds(c0, wo_cols)],
                                                preferred_element_type=F32)
        return 0

    lax.fori_loop(0, n_pairs, pair_body, 0)

    k_scr[0:B, :] = k_scr[T:T + B, :]
    v_scr[0:B, :] = v_scr[T:T + B, :]


def _attn_finish(h1, g2_ref, wr_ref, o_ref, route_ref, lane):
    _store_token_major(o_ref, h1)

    hn2 = _rmsnorm(h1, g2_ref[...]).astype(BF16)
    logits = jnp.dot(hn2, wr_ref[...], preferred_element_type=F32)
    lane_f = lane.astype(F32)
    lg = jnp.where(lane < N_EXPERTS, logits, -jnp.inf)
    v1 = jnp.max(lg, axis=-1, keepdims=True)
    i1 = jnp.min(jnp.where(lg == v1, lane_f, float(LANES)), axis=-1, keepdims=True)
    lg2 = jnp.where(lane_f == i1, -jnp.inf, lg)
    v2 = jnp.max(lg2, axis=-1, keepdims=True)
    i2 = jnp.min(jnp.where(lg2 == v2, lane_f, float(LANES)), axis=-1, keepdims=True)
    e2 = jnp.exp(v2 - v1)
    gate1 = 1.0 / (1.0 + e2)
    gate2 = e2 / (1.0 + e2)
    route_ref[...] = jnp.where(lane == 0, i1, jnp.where(lane == 1, i2,
                                                        jnp.where(lane == 2, gate1, jnp.where(lane == 3, gate2, 0.0))))


def _attn_mixer(x, pos_f, g, freq_lane, sign_lane, w_qkv_ext, b_qkv_ext, sinks, w_o, b_o, g2, w_router_pad):
    S, D = x.shape
    T = ATTN_TILE
    B = ATTN_BLOCK
    n_tiles = S // T
    cur = lambda i, s: (jnp.minimum(i, n_tiles - 1), 0)
    prev = lambda i, s: (jnp.maximum(i - 1, 0), 0)
    grid_spec = pltpu.PrefetchScalarGridSpec(
        num_scalar_prefetch=1,
        grid=(n_tiles + 1,),
        in_specs=[
            pl.BlockSpec((T, D), cur),
            pl.BlockSpec((T, D), prev),
            pl.BlockSpec((T, 1), cur),
            _const_spec((1, D)),
            _const_spec((1, LANES)),
            _const_spec((1, LANES)),
            _const_spec((D, QKV_EXT)),
            _const_spec((1, QKV_EXT)),
            _const_spec((D, D)),
            _const_spec((1, D)),
            _const_spec((1, D)),
            _const_spec((D, LANES)),
        ],
        out_specs=[pl.BlockSpec((T * ROWS_PER_TOKEN, LANES), prev), pl.BlockSpec((T, LANES), prev)],
        scratch_shapes=[
            pltpu.VMEM((T, D), BF16), pltpu.VMEM((T, D), BF16),
            pltpu.VMEM((T + B, KV_EXT), BF16), pltpu.VMEM((T + B, KV_EXT), BF16),
            pltpu.VMEM((2, T, D), BF16), pltpu.VMEM((T, D), F32),
        ],
    )
    return pl.pallas_call(
        _attn_kernel,
        grid_spec=grid_spec,
        out_shape=[jax.ShapeDtypeStruct((S * ROWS_PER_TOKEN, LANES), F32), jax.ShapeDtypeStruct((S, LANES), F32)],
        compiler_params=pltpu.CompilerParams(dimension_semantics=("arbitrary",), vmem_limit_bytes=VMEM_LIMIT),
        name="attn_mixer",
    )(sinks, x, x, pos_f, g.reshape(1, D), freq_lane, sign_lane, w_qkv_ext, b_qkv_ext, w_o, b_o.reshape(1, D),
      g2.reshape(1, D), w_router_pad)


def _moe_kernel(texp_ref, nvt_ref, src0_ref, srcn_ref, dstp_ref, dstl_ref, h_hbm, g_ref, wg_ref, wu_ref, wd_ref,
                y_hbm, x_scr, xn_scr, acc_scr, y_scr, sems):
    i = pl.program_id(0)
    c = pl.program_id(1)
    n_tiles = pl.num_programs(0)
    n_chunks = pl.num_programs(1)
    T = xn_scr.shape[0]
    R = ROWS_PER_TOKEN
    valid = i < nvt_ref[0]
    slot = i % 2
    other = 1 - slot
    SCATTER_SEM = 2

    def row_copy(src, src_row, dst, dst_row, sem):
        return pltpu.make_async_copy(src.at[pl.ds(pl.multiple_of(src_row, R), R), :],
                                     dst.at[pl.ds(pl.multiple_of(dst_row, R), R), :], sem)

    def tile_rows(t):
        return nvt_ref[1 + t]

    def window_offset(t):
        return nvt_ref[1 + 2 * n_tiles + t]

    prev_tile = jnp.maximum(i - 1, 0)
    next_tile = jnp.minimum(i + 1, n_tiles - 1)
    rows_prev = jnp.where(i > 0, tile_rows(prev_tile), 0)
    pad_base = y_hbm.shape[0] - T * R

    def dest_row(window_ref, offset, r, n_rows):
        return jnp.where(r < n_rows, window_ref[0, offset + r], pad_base + r * R)

    def wait_gather(s):
        pltpu.make_async_copy(h_hbm.at[pl.ds(0, T * R), :], x_scr.at[s], sems.at[s]).wait()

    def wait_scatter():
        pltpu.make_async_copy(y_scr, y_hbm.at[pl.ds(0, T * R), :], sems.at[SCATTER_SEM]).wait()

    @pl.when((i == 0) & (c == 0))
    def _():
        y_scr[...] = jnp.zeros_like(y_scr)
        acc_scr[...] = jnp.zeros_like(acc_scr)

        def issue(r, carry):
            row_copy(h_hbm, src0_ref[0, window_offset(0) + r], x_scr.at[0], r * R, sems.at[0]).start()
            return carry
        lax.fori_loop(0, T, issue, 0)

    @pl.when(c == 0)
    def _():
        wait_gather(slot)

    def row_dmas():
        off_next = window_offset(next_tile)
        off_prev = window_offset(prev_tile)
        for r in range(T):
            row_copy(h_hbm, srcn_ref[0, off_next + r], x_scr.at[other], r * R, sems.at[other]).start()
            row_copy(y_scr, r * R, y_hbm, dest_row(dstp_ref, off_prev, r, rows_prev),
                     sems.at[SCATTER_SEM]).start()

    def expert_chunk(first, rows):
        xn = xn_scr[0:rows, :]
        gate = jnp.dot(xn, wg_ref[0], preferred_element_type=F32)
        up = jnp.dot(xn, wu_ref[0], preferred_element_type=F32)
        mid = (jax.nn.silu(gate) * up).astype(BF16)
        down = jnp.dot(mid, wd_ref[0], preferred_element_type=F32)
        if first:
            acc_scr[0:rows, :] = down
        else:
            acc_scr[0:rows, :] += down

    few_rows = tile_rows(i) <= T // 2

    for few, rows in ((False, T), (True, T // 2)):
        branch = valid & (few_rows if few else jnp.logical_not(few_rows))

        @pl.when(branch & (c == 0))
        def _():
            xn_scr[0:rows, :] = _rmsnorm(_load_token_major(x_scr.at[slot], rows), g_ref[...]).astype(BF16)
            row_dmas()
            expert_chunk(True, rows)

        @pl.when(branch & (c != 0))
        def _():
            expert_chunk(False, rows)

    @pl.when(jnp.logical_not(valid) & (c == 0))
    def _():
        row_dmas()

    @pl.when(c == n_chunks - 1)
    def _():
        wait_scatter()

    @pl.when(valid & (c == n_chunks - 1))
    def _():
        _store_token_major(y_scr, acc_scr[...])

    @pl.when((i == n_tiles - 1) & (c == n_chunks - 1))
    def _():
        wait_gather(other)


        def issue(r, carry):
            row_copy(y_scr, r * R, y_hbm, dest_row(dstl_ref, window_offset(n_tiles - 1), r, tile_rows(n_tiles - 1)),
                     sems.at[SCATTER_SEM]).start()
            return carry
        lax.fori_loop(0, T, issue, 0)
        wait_scatter()


def _moe_experts(h1_tm, g, tile_expert, n_valid_tiles, src_rows, dst_rows, w_gate, w_up, w_down, n_out_tokens):
    D = D_MODEL
    T = MOE_TILE
    NT = tile_expert.shape[0]
    NC = N_MOE_CHUNKS
    FC = D_FF // NC

    def chunk(i, c, nvt):
        return jnp.where(i < nvt[0], c, NC - 1)

    def window_spec(tile_of_step):
        return pl.BlockSpec((pl.Element(1), pl.Element(T + LANES)),
                            lambda i, c, te, nvt: (0, nvt[1 + NT + tile_of_step(i)] * LANES),
                            memory_space=pltpu.SMEM)

    grid_spec = pltpu.PrefetchScalarGridSpec(
        num_scalar_prefetch=2,
        grid=(NT, NC),
        in_specs=[
            window_spec(lambda i: 0),
            window_spec(lambda i: jnp.minimum(i + 1, NT - 1)),
            window_spec(lambda i: jnp.maximum(i - 1, 0)),
            window_spec(lambda i: NT - 1),
            pl.BlockSpec(memory_space=pl.ANY),
            pl.BlockSpec((1, D), lambda i, c, te, nvt: (0, 0)),
            pl.BlockSpec((1, D, FC), lambda i, c, te, nvt: (te[i], 0, chunk(i, c, nvt))),
            pl.BlockSpec((1, D, FC), lambda i, c, te, nvt: (te[i], 0, chunk(i, c, nvt))),
            pl.BlockSpec((1, FC, D), lambda i, c, te, nvt: (te[i], chunk(i, c, nvt), 0)),
        ],
        out_specs=pl.BlockSpec(memory_space=pl.ANY),
        scratch_shapes=[pltpu.VMEM((2, T * ROWS_PER_TOKEN, LANES), F32), pltpu.VMEM((T, D), BF16),
                        pltpu.VMEM((T, D), F32), pltpu.VMEM((T * ROWS_PER_TOKEN, LANES), F32),
                        pltpu.SemaphoreType.DMA((3,))],
    )
    return pl.pallas_call(
        _moe_kernel,
        grid_spec=grid_spec,
        out_shape=jax.ShapeDtypeStruct((n_out_tokens * ROWS_PER_TOKEN, LANES), F32),
        compiler_params=pltpu.CompilerParams(dimension_semantics=("arbitrary", "arbitrary"),
                                             vmem_limit_bytes=VMEM_LIMIT),
        name="moe_experts",
    )(tile_expert, n_valid_tiles, src_rows, src_rows, dst_rows, dst_rows, h1_tm, g.reshape(1, D),
      w_gate, w_up, w_down)


def _moe_plan(route, n_tokens):
    T = MOE_TILE
    n_assign = TOP_K * n_tokens
    n_tiles = n_assign // T + N_EXPERTS
    e_flat = route[:, 0:TOP_K].astype(jnp.int32).reshape(-1)
    counts = jnp.sum((e_flat[:, None] == jnp.arange(N_EXPERTS, dtype=jnp.int32)[None, :]).astype(jnp.int32), axis=0)
    order = jnp.argsort(e_flat, stable=True).astype(jnp.int32)
    tiles_per = (counts + T - 1) // T
    tile_end = jnp.cumsum(tiles_per)
    n_valid = tile_end[-1]
    tile_ids = jnp.arange(n_tiles, dtype=jnp.int32)
    last_valid = jnp.maximum(n_valid - 1, 0)
    texp = jnp.sum((jnp.minimum(tile_ids, last_valid)[:, None] >= tile_end[None, :]).astype(jnp.int32), axis=1)
    texp = jnp.minimum(texp, N_EXPERTS - 1)
    onehot = (texp[:, None] == jnp.arange(N_EXPERTS, dtype=jnp.int32)[None, :]).astype(jnp.int32)
    pick = lambda v: jnp.sum(onehot * v[None, :], axis=1)
    j = tile_ids - pick(tile_end - tiles_per)
    start = jnp.where(tile_ids < n_valid, pick(jnp.cumsum(counts) - counts) + j * T, 0)
    n_rows = jnp.where(tile_ids < n_valid, jnp.clip(pick(counts) - j * T, 0, T), 0)
    token = order // TOP_K
    slot = order % TOP_K
    spare = jnp.zeros((T + LANES,), jnp.int32)
    src_sorted = jnp.concatenate([token * ROWS_PER_TOKEN, spare]).reshape(1, -1)
    dst_sorted = jnp.concatenate([(slot * n_tokens + token) * ROWS_PER_TOKEN, spare]).reshape(1, -1)
    tile_counts = jnp.concatenate([n_valid.reshape(1), n_rows, start // LANES, start % LANES]).astype(jnp.int32)
    return texp, tile_counts, src_sorted, dst_sorted


def _combine_kernel(h_ref, y0_ref, y1_ref, route_ref, g_ref, o_ref):
    T = o_ref.shape[0]
    route = route_ref[...]
    moe = route[:, 2:3] * _load_token_major(y0_ref, T) + route[:, 3:4] * _load_token_major(y1_ref, T)
    o_ref[...] = _rmsnorm(_load_token_major(h_ref, T) + moe, g_ref[...])


def _combine(h1_tm, y2_tm, route, g):
    D = D_MODEL
    S = route.shape[0]
    T = OUT_TILE
    nb = S // T
    tm_block = (T * ROWS_PER_TOKEN, LANES)
    return pl.pallas_call(
        _combine_kernel,
        grid=(nb,),
        in_specs=[
            pl.BlockSpec(tm_block, lambda i: (i, 0)),
            pl.BlockSpec(tm_block, lambda i: (i, 0)),
            pl.BlockSpec(tm_block, lambda i: (i + nb, 0)),
            pl.BlockSpec((T, LANES), lambda i: (i, 0)),
            pl.BlockSpec((1, D), lambda i: (0, 0)),
        ],
        out_specs=pl.BlockSpec((T, D), lambda i: (i, 0)),
        out_shape=jax.ShapeDtypeStruct((S, D), F32),
        compiler_params=pltpu.CompilerParams(dimension_semantics=("arbitrary",), vmem_limit_bytes=VMEM_LIMIT),
        name="moe_combine",
    )(h1_tm, y2_tm, y2_tm, route, g.reshape(1, D))


def kernel(x, positions, norm_mix, norm_ffn, norm_final, lru_w_in, lru_conv_w, lru_conv_b, lru_w_a, lru_b_a, lru_w_x, lru_b_x, lru_lambda, lru_w_out, attn_w_qkv, attn_b_qkv, attn_sinks, attn_w_o, attn_b_o, ffn_w_gate, ffn_w_up, ffn_w_down, moe_w_router, moe_w_gate, moe_w_up, moe_w_down):
    B, S, D = x.shape
    assert B == 1 and D == D_MODEL and S % FFN_TILE == 0
    h = x.reshape(S, D)

    w_ax = jnp.concatenate([lru_w_a[0], lru_w_x[0]], axis=-1).astype(BF16)
    h, ffn_wg, ffn_wu, ffn_wd = _lru_mixer(h, norm_mix[0], lru_w_in[0].astype(BF16), lru_conv_w[0], lru_conv_b[0],
                                           w_ax, lru_b_a[0], lru_b_x[0], lru_lambda[0], lru_w_out[0].astype(BF16),
                                           ffn_w_gate[0], ffn_w_up[0], ffn_w_down[0])
    h, moe_wg, moe_wu, moe_wd = _dense_ffn(h, norm_ffn[0], ffn_wg, ffn_wu, ffn_wd,
                                           moe_w_gate[0], moe_w_up[0], moe_w_down[0])

    q_dim = N_HEADS * HEAD_DIM
    kv_dim = N_KV_HEADS * HEAD_DIM

    def dup_heads(w):
        w4 = w.reshape(w.shape[:-1] + (N_KV_HEADS, 1, HEAD_DIM))
        return jnp.broadcast_to(w4, w.shape[:-1] + (N_KV_HEADS, 2, HEAD_DIM)).reshape(w.shape[:-1] + (KV_EXT,))

    wqkv, bqkv = attn_w_qkv[0], attn_b_qkv[0]
    w_qkv_ext = jnp.concatenate([wqkv[:, :q_dim], dup_heads(wqkv[:, q_dim:q_dim + kv_dim]),
                                 dup_heads(wqkv[:, q_dim + kv_dim:])], axis=1).astype(BF16)
    b_qkv_ext = jnp.concatenate([bqkv[:q_dim], dup_heads(bqkv[q_dim:q_dim + kv_dim]),
                                 dup_heads(bqkv[q_dim + kv_dim:])]).reshape(1, QKV_EXT)
    inv_freq = ROPE_THETA ** (-jnp.arange(0, ROT_DIM, 2, dtype=F32) / ROT_DIM)
    d_in_head = jnp.arange(LANES) % HEAD_DIM
    freq_lane = jnp.where(d_in_head < ROT_DIM, inv_freq[d_in_head % (ROT_DIM // 2)], 0.0).reshape(1, LANES)
    sign_lane = jnp.where(d_in_head < ROT_DIM // 2, -1.0, jnp.where(d_in_head < ROT_DIM, 1.0, 0.0))
    sign_lane = sign_lane.astype(F32).reshape(1, LANES)
    w_router_pad = jnp.pad(moe_w_router[0], ((0, 0), (0, LANES - N_EXPERTS))).astype(BF16)
    pos_f = positions.reshape(S, 1).astype(F32)
    h1_tm, route = _attn_mixer(h, pos_f, norm_mix[1], freq_lane, sign_lane, w_qkv_ext, b_qkv_ext,
                               attn_sinks[0], attn_w_o[0].astype(BF16), attn_b_o[0], norm_ffn[1], w_router_pad)

    texp, n_valid, src_rows, dst_rows = _moe_plan(route, S)
    y2_tm = _moe_experts(h1_tm, norm_ffn[1], texp, n_valid, src_rows, dst_rows,
                         moe_wg, moe_wu, moe_wd,
                         TOP_K * S + MOE_TILE)
    out = _combine(h1_tm, y2_tm, route, norm_final)
    return out.reshape(B, S, D)
```

```python
import jax
import jax.numpy as jnp
from jax import lax
from jax.experimental import pallas as pl
from jax.experimental.pallas import tpu as pltpu

F32 = jnp.float32
BF16 = jnp.bfloat16

D_MODEL = 1024
N_RNN_BLOCKS = 8
RNN_BLOCK = D_MODEL // N_RNN_BLOCKS
CONV_WIDTH = 4
LRU_C = 8.0
N_HEADS = 16
N_KV_HEADS = 4
HEAD_DIM = 64
GROUP = N_HEADS // N_KV_HEADS
WINDOW = 128
ATTN_BLOCK = 128
assert WINDOW == ATTN_BLOCK
ROPE_THETA = 500000.0
ROT_DIM = HEAD_DIM // 4
D_FF = 3584
N_EXPERTS = 8
TOP_K = 2
EPS = 1e-6
NEG_INF = -1e30
LOG2_E = 1.4426950408889634

LANES = 128
SUBLANES = 8
VMEM_LIMIT = 56 * 1024 * 1024

SEQ_TILE = 512
ATTN_TILE = 1024
LRU_CHUNKS = 4
FFN_TILE = 1024
FF_CHUNK = 512
N_FF_CHUNKS = D_FF // FF_CHUNK
MOE_TILE = 512
N_MOE_CHUNKS = 2
OUT_TILE = 1024


def _rmsnorm(x, g):
    return x * lax.rsqrt(jnp.mean(x * x, axis=-1, keepdims=True) + EPS) * g


ROWS_PER_TOKEN = D_MODEL // LANES


def _store_token_major(ref, val):
    T = val.shape[0]
    for c in range(ROWS_PER_TOKEN):
        ref[pl.ds(c, T, stride=ROWS_PER_TOKEN), :] = val[:, LANES * c:LANES * (c + 1)]


def _load_token_major(ref, T):
    return jnp.concatenate([ref[pl.ds(c, T, stride=ROWS_PER_TOKEN), :] for c in range(ROWS_PER_TOKEN)], axis=1)


def _const_spec(shape):
    n = len(shape)
    return pl.BlockSpec(shape, lambda *_: (0,) * n, pipeline_mode=pl.Buffered(1))


def _time_permutation(T):
    rho = jnp.arange(T)
    t_of_rho = (rho % SUBLANES) * (T // SUBLANES) + rho // SUBLANES
    perm = (t_of_rho[:, None] == jnp.arange(T)[None, :]).astype(BF16)
    return perm, perm.T


def _lru_kernel(xin_ref, xres_ref, g_ref, perm_ref, unperm_ref, win_ref, cw_ref, cb_ref, wax_ref, ba_ref, bx_ref,
                lam_ref, wout_ref, cast_g_ref, cast_u_ref, cast_d_ref,
                o_ref, bf_g_ref, bf_u_ref, bf_d_ref, proj_scr, tail_scr, carry_scr):
    i = pl.program_id(0)
    n_tiles = pl.num_programs(0) - 1
    T = xin_ref.shape[0]
    D = D_MODEL
    C = T // LRU_CHUNKS
    GC = C // SUBLANES
    G = T // SUBLANES
    HALO = (CONV_WIDTH - 1) * SUBLANES

    def cast_slices():
        bf_g_ref[...] = cast_g_ref[...].astype(BF16)
        bf_u_ref[...] = cast_u_ref[...].astype(BF16)
        bf_d_ref[...] = cast_d_ref[...].astype(BF16)

    @pl.when(i == 0)
    def _():
        tail_scr[...] = jnp.zeros_like(tail_scr)
        carry_scr[...] = jnp.zeros_like(carry_scr)

    PW = 2 * D // LRU_CHUNKS

    def project_chunks():
        hn = _rmsnorm(xin_ref[...], g_ref[...]).astype(BF16)
        hp = jnp.dot(perm_ref[...], hn, preferred_element_type=F32).astype(BF16)
        for k in range(LRU_CHUNKS):
            yield jnp.dot(hp, win_ref[:, PW * k:PW * (k + 1)], preferred_element_type=F32)

    def recur_chunks():
        sub = lax.broadcasted_iota(jnp.int32, (SUBLANES, D), 0)
        cw = cw_ref[...]
        softplus_neg_lam = jax.nn.softplus(-lam_ref[...])
        last = proj_scr[T - HALO:T, D:]
        prev_groups = []
        for k in range(CONV_WIDTH - 1):
            rows = slice(SUBLANES * k, SUBLANES * (k + 1))
            prev_groups.append(pltpu.roll(jnp.where(sub == SUBLANES - 1, tail_scr[rows, :], last[rows, :]), 1, 0))
        tail_scr[...] = last

        ys, hs, decay = [], [], []
        for k in range(LRU_CHUNKS):
            pc = proj_scr[C * k:C * (k + 1), :]
            ys.append(jax.nn.gelu(pc[:, :D], approximate=True))
            xb = pc[:, D:]
            xc = cb_ref[...] + xb * cw[CONV_WIDTH - 1:CONV_WIDTH, :]
            for back in range(1, CONV_WIDTH):
                shifted = jnp.concatenate(prev_groups[CONV_WIDTH - 1 - back:] + [xb[:C - SUBLANES * back, :]], axis=0)
                xc = xc + shifted * cw[CONV_WIDTH - 1 - back:CONV_WIDTH - back, :]
            prev_groups = [xb[C - HALO + SUBLANES * q:C - HALO + SUBLANES * (q + 1), :] for q in range(CONV_WIDTH - 1)]

            xcb = xc.astype(BF16)
            r_parts, i_parts = [], []
            for n in range(N_RNN_BLOCKS):
                gn = jnp.dot(xcb[:, RNN_BLOCK * n:RNN_BLOCK * (n + 1)], wax_ref[n], preferred_element_type=F32)
                r_parts.append(gn[:, :RNN_BLOCK])
                i_parts.append(gn[:, RNN_BLOCK:])
            r = jax.nn.sigmoid(jnp.concatenate(r_parts, axis=1) + ba_ref[...])
            ig = jax.nn.sigmoid(jnp.concatenate(i_parts, axis=1) + bx_ref[...])
            log_a = -LRU_C * r * softplus_neg_lam
            a = jnp.exp(log_a)
            z = jnp.tanh(-log_a) * (1.0 + a * a)
            b = jnp.where(z > 0.0, z * lax.rsqrt(z), 0.0) * (ig * xc)

            for j in range(GC):
                rows = slice(SUBLANES * j, SUBLANES * (j + 1))
                if hs:
                    hs.append(a[rows, :] * hs[-1] + b[rows, :])
                    decay.append(a[rows, :] * decay[-1])
                else:
                    hs.append(b[rows, :])
                    decay.append(a[rows, :])
            yield None

        p_inc, e_inc = decay[-1], hs[-1]
        for k in (1, 2, 4):
            m = sub >= k
            p_sh = jnp.where(m, pltpu.roll(p_inc, k, 0), 1.0)
            e_sh = jnp.where(m, pltpu.roll(e_inc, k, 0), 0.0)
            e_inc = e_inc + p_inc * e_sh
            p_inc = p_inc * p_sh
        h0 = carry_scr[...]
        after = p_inc * h0 + e_inc
        carry_scr[...] = after[SUBLANES - 1:SUBLANES, :]
        enter = jnp.where(sub == 0, h0, pltpu.roll(after, 1, 0))
        h = jnp.concatenate([hs[j] + decay[j] * enter for j in range(G)], axis=0)
        hy = (h * jnp.concatenate(ys, axis=0)).astype(BF16)
        hy = jnp.dot(unperm_ref[...], hy, preferred_element_type=F32).astype(BF16)
        yield jnp.dot(hy, wout_ref[...], preferred_element_type=F32) + xres_ref[...]

    @pl.when(i == 0)
    def _():
        cast_slices()
        for k, pc in enumerate(project_chunks()):
            proj_scr[:, PW * k:PW * (k + 1)] = pc

    @pl.when((i > 0) & (i < n_tiles))
    def _():
        cast_slices()
        rec = recur_chunks()
        new_proj = []
        for pc in project_chunks():
            new_proj.append(pc)
            next(rec)
        o_ref[...] = next(rec)
        for k, pc in enumerate(new_proj):
            proj_scr[:, PW * k:PW * (k + 1)] = pc

    @pl.when(i == n_tiles)
    def _():
        o_ref[...] = list(recur_chunks())[-1]


def _lru_mixer(x, g, w_in, conv_w, conv_b, w_ax, b_a, b_x, lam, w_out, cast_gate, cast_up, cast_down):
    S, D = x.shape
    T = SEQ_TILE
    n_tiles = S // T
    row = lambda v: v.reshape(1, D)
    perm, unperm = _time_permutation(T)

    def cast_spec(w):
        return pl.BlockSpec((w.shape[0] // n_tiles, w.shape[1]), lambda i: (jnp.minimum(i, n_tiles - 1), 0))

    casts = (cast_gate, cast_up, cast_down)
    return pl.pallas_call(
        _lru_kernel,
        grid=(n_tiles + 1,),
        in_specs=[
            pl.BlockSpec((T, D), lambda i: (jnp.minimum(i, n_tiles - 1), 0)),
            pl.BlockSpec((T, D), lambda i: (jnp.maximum(i - 1, 0), 0)),
            _const_spec((1, D)),
            _const_spec((T, T)),
            _const_spec((T, T)),
            _const_spec((D, 2 * D)),
            _const_spec((CONV_WIDTH, D)),
            _const_spec((1, D)),
            _const_spec((N_RNN_BLOCKS, RNN_BLOCK, 2 * RNN_BLOCK)),
            _const_spec((1, D)),
            _const_spec((1, D)),
            _const_spec((1, D)),
            _const_spec((D, D)),
        ] + [cast_spec(w) for w in casts],
        out_specs=[pl.BlockSpec((T, D), lambda i: (jnp.maximum(i - 1, 0), 0))] + [cast_spec(w) for w in casts],
        out_shape=[jax.ShapeDtypeStruct((S, D), F32)] + [jax.ShapeDtypeStruct(w.shape, BF16) for w in casts],
        scratch_shapes=[pltpu.VMEM((T, 2 * D), F32), pltpu.VMEM(((CONV_WIDTH - 1) * SUBLANES, D), F32),
                        pltpu.VMEM((1, D), F32)],
        compiler_params=pltpu.CompilerParams(dimension_semantics=("arbitrary",), vmem_limit_bytes=VMEM_LIMIT),
        name="lru_mixer",
    )(x, x, row(g), perm, unperm, w_in, conv_w, row(conv_b), w_ax, row(b_a), row(b_x), row(lam), w_out, *casts)


def _ffn_kernel(x_ref, g_ref, wg_ref, wu_ref, wd_ref, cast_g_ref, cast_u_ref, cast_d_ref,
                o_ref, bf_g_ref, bf_u_ref, bf_d_ref, xn_scr, acc_scr):
    c = pl.program_id(1)
    last = pl.num_programs(1) - 1

    def ff_chunk():
        xn = xn_scr[...]
        gate = jnp.dot(xn, wg_ref[...], preferred_element_type=F32)
        bf_g_ref[...] = cast_g_ref[...].astype(BF16)
        up = jnp.dot(xn, wu_ref[...], preferred_element_type=F32)
        bf_u_ref[...] = cast_u_ref[...].astype(BF16)
        mid = (jax.nn.silu(gate) * up).astype(BF16)
        down = jnp.dot(mid, wd_ref[...], preferred_element_type=F32)
        bf_d_ref[...] = cast_d_ref[...].astype(BF16)
        return down

    @pl.when(c == 0)
    def _():
        xn_scr[...] = _rmsnorm(x_ref[...], g_ref[...]).astype(BF16)
        acc_scr[...] = ff_chunk()

    @pl.when((c > 0) & (c < last))
    def _():
        acc_scr[...] += ff_chunk()

    @pl.when(c == last)
    def _():
        o_ref[...] = x_ref[...] + (acc_scr[...] + ff_chunk())


def _dense_ffn(x, g, w_gate, w_up, w_down, cast_gate, cast_up, cast_down):
    S, D = x.shape
    T = FFN_TILE
    n_steps = (S // T) * N_FF_CHUNKS
    E = cast_gate.shape[0]
    row_parts = n_steps // (E * N_FF_CHUNKS)
    assert row_parts * E * N_FF_CHUNKS == n_steps
    gu_block = (1, D // row_parts, D_FF // N_FF_CHUNKS)
    d_block = (1, D_FF // (row_parts * N_FF_CHUNKS), D)

    def gu_map(i, c):
        s = i * N_FF_CHUNKS + c
        return (s // (row_parts * N_FF_CHUNKS), (s // N_FF_CHUNKS) % row_parts, c)

    def d_map(i, c):
        s = i * N_FF_CHUNKS + c
        return (s // (row_parts * N_FF_CHUNKS), s % (row_parts * N_FF_CHUNKS), 0)

    return pl.pallas_call(
        _ffn_kernel,
        grid=(S // T, N_FF_CHUNKS),
        in_specs=[
            pl.BlockSpec((T, D), lambda i, c: (i, 0)),
            pl.BlockSpec((1, D), lambda i, c: (0, 0)),
            pl.BlockSpec((D, FF_CHUNK), lambda i, c: (0, c)),
            pl.BlockSpec((D, FF_CHUNK), lambda i, c: (0, c)),
            pl.BlockSpec((FF_CHUNK, D), lambda i, c: (c, 0)),
            pl.BlockSpec(gu_block, gu_map),
            pl.BlockSpec(gu_block, gu_map),
            pl.BlockSpec(d_block, d_map),
        ],
        out_specs=[pl.BlockSpec((T, D), lambda i, c: (i, 0)), pl.BlockSpec(gu_block, gu_map),
                   pl.BlockSpec(gu_block, gu_map), pl.BlockSpec(d_block, d_map)],
        out_shape=[jax.ShapeDtypeStruct((S, D), F32), jax.ShapeDtypeStruct(cast_gate.shape, BF16),
                   jax.ShapeDtypeStruct(cast_up.shape, BF16), jax.ShapeDtypeStruct(cast_down.shape, BF16)],
        scratch_shapes=[pltpu.VMEM((T, D), BF16), pltpu.VMEM((T, D), F32)],
        compiler_params=pltpu.CompilerParams(dimension_semantics=("arbitrary", "arbitrary"),
                                             vmem_limit_bytes=VMEM_LIMIT),
        name="dense_ffn",
    )(x, g.reshape(1, D), w_gate, w_up, w_down, cast_gate, cast_up, cast_down)


KV_EXT = N_KV_HEADS * LANES
QKV_EXT = N_HEADS * HEAD_DIM + 2 * KV_EXT


def _attn_kernel(sink_ref, x_ref, pos_ref, g_ref, freq_ref, sign_ref, wqkv_ref, bqkv_ref, wo_ref, bo_ref,
                 g2_ref, wr_ref, o_ref, route_ref,
                 qlo_scr, qhi_scr, k_scr, v_scr, o_scr):
    T = x_ref.shape[0]
    D = D_MODEL
    B = ATTN_BLOCK
    first_tile = pl.program_id(0) == 0

    @pl.when(first_tile)
    def _():
        k_scr[0:B, :] = jnp.zeros((B, KV_EXT), BF16)
        v_scr[0:B, :] = jnp.zeros((B, KV_EXT), BF16)

    x = x_ref[...]
    hn = _rmsnorm(x, g_ref[...]).astype(BF16)
    qkv = jnp.dot(hn, wqkv_ref[...], preferred_element_type=F32) + bqkv_ref[...]

    ang = pos_ref[...] * freq_ref[...]
    cos_t = jnp.cos(ang)
    sin_t = jnp.sin(ang) * sign_ref[...]
    lane = lax.broadcasted_iota(jnp.int32, (T, LANES), 1)
    first_half = (lane % HEAD_DIM) < (ROT_DIM // 2)
    lo_half = lane < HEAD_DIM

    def rope(col):
        partner = jnp.where(first_half, pltpu.roll(col, LANES - ROT_DIM // 2, 1), pltpu.roll(col, ROT_DIM // 2, 1))
        return col * cos_t + partner * sin_t

    scale = HEAD_DIM ** -0.5 * LOG2_E
    for c in range(D // LANES):
        qc = rope(qkv[:, LANES * c:LANES * (c + 1)]) * scale
        qlo_scr[:, LANES * c:LANES * (c + 1)] = jnp.where(lo_half, qc, 0.0).astype(BF16)
        qhi_scr[:, LANES * c:LANES * (c + 1)] = jnp.where(lo_half, 0.0, qc).astype(BF16)
    for g in range(N_KV_HEADS):
        kc = rope(qkv[:, D + LANES * g:D + LANES * (g + 1)])
        k_scr[B:B + T, LANES * g:LANES * (g + 1)] = kc.astype(BF16)
        vc = qkv[:, D + KV_EXT + LANES * g:D + KV_EXT + LANES * (g + 1)]
        v_scr[B:B + T, LANES * g:LANES * (g + 1)] = vc.astype(BF16)

    qi = lax.broadcasted_iota(jnp.int32, (B, B), 0)
    kj = lax.broadcasted_iota(jnp.int32, (B, B), 1)
    causal_own = kj <= qi
    window_prev = kj > qi
    key0 = kj == 0
    key0_row = lax.broadcasted_iota(jnp.int32, (1, B), 1) == 0
    lo_blk = lax.broadcasted_iota(jnp.int32, (B, LANES), 1) < HEAD_DIM

    def block_body(blk, _):
        r0 = pl.multiple_of(blk * B, B)
        k_lo = jnp.where(first_tile & (blk == 0), B, 0)
        allowed_prev = window_prev & (kj >= k_lo)
        for g in range(N_KV_HEADS):
            kk = k_scr[pl.ds(r0, 2 * B), LANES * g:LANES * (g + 1)]
            vv = v_scr[pl.ds(r0, 2 * B), LANES * g:LANES * (g + 1)]
            cols = [slice(LANES * c, LANES * (c + 1)) for c in range((GROUP // 2) * g, (GROUP // 2) * (g + 1))]
            q_all = jnp.concatenate([q_scr[pl.ds(r0, B), col] for col in cols for q_scr in (qlo_scr, qhi_scr)], axis=0)
            s_all = lax.dot_general(q_all, kk, (((1,), (1,)), ((), ())), preferred_element_type=F32)
            p_all, denoms = [], []
            for hh in range(GROUP):
                rows = slice(B * hh, B * (hh + 1))
                sink_fill = jnp.where(key0_row, sink_ref[GROUP * g + hh] * LOG2_E, NEG_INF)
                s_prev = jnp.where(allowed_prev, s_all[rows, :B], sink_fill)
                s_own = jnp.where(causal_own, s_all[rows, B:], NEG_INF)
                m = jnp.max(jnp.maximum(s_prev, s_own), axis=-1, keepdims=True)
                p_prev = jnp.exp2(s_prev - m)
                p_own = jnp.exp2(s_own - m)
                denoms.append(jnp.sum(p_prev + p_own, axis=-1, keepdims=True))
                p_all.append(jnp.concatenate([jnp.where(key0, 0.0, p_prev), p_own], axis=1).astype(BF16))
            o_all = jnp.dot(jnp.concatenate(p_all, axis=0), vv, preferred_element_type=F32)
            o_heads = [o_all[B * hh:B * (hh + 1), :] / denoms[hh] for hh in range(GROUP)]
            for p, col in enumerate(cols):
                o_scr[pl.ds(r0, B), col] = jnp.where(lo_blk, o_heads[2 * p], o_heads[2 * p + 1]).astype(BF16)
        return 0

    lax.fori_loop(0, T // B, block_body, 0)

    k_scr[0:B, :] = k_scr[T:T + B, :]
    v_scr[0:B, :] = v_scr[T:T + B, :]

    h1 = jnp.dot(o_scr[...], wo_ref[...], preferred_element_type=F32) + bo_ref[...] + x
    _store_token_major(o_ref, h1)

    hn2 = _rmsnorm(h1, g2_ref[...]).astype(BF16)
    logits = jnp.dot(hn2, wr_ref[...], preferred_element_type=F32)
    lane_f = lane.astype(F32)
    lg = jnp.where(lane < N_EXPERTS, logits, -jnp.inf)
    v1 = jnp.max(lg, axis=-1, keepdims=True)
    i1 = jnp.min(jnp.where(lg == v1, lane_f, float(LANES)), axis=-1, keepdims=True)
    lg2 = jnp.where(lane_f == i1, -jnp.inf, lg)
    v2 = jnp.max(lg2, axis=-1, keepdims=True)
    i2 = jnp.min(jnp.where(lg2 == v2, lane_f, float(LANES)), axis=-1, keepdims=True)
    e2 = jnp.exp(v2 - v1)
    gate1 = 1.0 / (1.0 + e2)
    gate2 = e2 / (1.0 + e2)
    route_ref[...] = jnp.where(lane == 0, i1, jnp.where(lane == 1, i2,
                                                        jnp.where(lane == 2, gate1, jnp.where(lane == 3, gate2, 0.0))))


def _attn_mixer(x, pos_f, g, freq_lane, sign_lane, w_qkv_ext, b_qkv_ext, sinks, w_o, b_o, g2, w_router_pad):
    S, D = x.shape
    T = ATTN_TILE
    B = ATTN_BLOCK
    grid_spec = pltpu.PrefetchScalarGridSpec(
        num_scalar_prefetch=1,
        grid=(S // T,),
        in_specs=[
            pl.BlockSpec((T, D), lambda i, s: (i, 0)),
            pl.BlockSpec((T, 1), lambda i, s: (i, 0)),
            _const_spec((1, D)),
            _const_spec((1, LANES)),
            _const_spec((1, LANES)),
            _const_spec((D, QKV_EXT)),
            _const_spec((1, QKV_EXT)),
            _const_spec((D, D)),
            _const_spec((1, D)),
            _const_spec((1, D)),
            _const_spec((D, LANES)),
        ],
        out_specs=[pl.BlockSpec((T * ROWS_PER_TOKEN, LANES), lambda i, s: (i, 0)),
                   pl.BlockSpec((T, LANES), lambda i, s: (i, 0))],
        scratch_shapes=[
            pltpu.VMEM((T, D), BF16), pltpu.VMEM((T, D), BF16),
            pltpu.VMEM((T + B, KV_EXT), BF16), pltpu.VMEM((T + B, KV_EXT), BF16),
            pltpu.VMEM((T, D), BF16),
        ],
    )
    return pl.pallas_call(
        _attn_kernel,
        grid_spec=grid_spec,
        out_shape=[jax.ShapeDtypeStruct((S * ROWS_PER_TOKEN, LANES), F32), jax.ShapeDtypeStruct((S, LANES), F32)],
        compiler_params=pltpu.CompilerParams(dimension_semantics=("arbitrary",), vmem_limit_bytes=VMEM_LIMIT),
        name="attn_mixer",
    )(sinks, x, pos_f, g.reshape(1, D), freq_lane, sign_lane, w_qkv_ext, b_qkv_ext, w_o, b_o.reshape(1, D),
      g2.reshape(1, D), w_router_pad)


def _moe_kernel(texp_ref, nvt_ref, src0_ref, srcn_ref, dstp_ref, dstl_ref, h_hbm, g_ref, wg_ref, wu_ref, wd_ref,
                y_hbm, x_scr, xn_scr, acc_scr, y_scr, sems):
    i = pl.program_id(0)
    c = pl.program_id(1)
    n_tiles = pl.num_programs(0)
    n_chunks = pl.num_programs(1)
    T = xn_scr.shape[0]
    R = ROWS_PER_TOKEN
    valid = i < nvt_ref[0]
    slot = i % 2
    other = 1 - slot
    SCATTER_SEM = 2

    def row_copy(src, src_row, dst, dst_row, sem):
        return pltpu.make_async_copy(src.at[pl.ds(pl.multiple_of(src_row, R), R), :],
                                     dst.at[pl.ds(pl.multiple_of(dst_row, R), R), :], sem)

    def tile_rows(t):
        return nvt_ref[1 + t]

    def window_offset(t):
        return nvt_ref[1 + 2 * n_tiles + t]

    prev_tile = jnp.maximum(i - 1, 0)
    next_tile = jnp.minimum(i + 1, n_tiles - 1)
    rows_prev = jnp.where(i > 0, tile_rows(prev_tile), 0)
    pad_base = y_hbm.shape[0] - T * R

    def dest_row(window_ref, offset, r, n_rows):
        return jnp.where(r < n_rows, window_ref[0, offset + r], pad_base + r * R)

    def wait_gather(s):
        pltpu.make_async_copy(h_hbm.at[pl.ds(0, T * R), :], x_scr.at[s], sems.at[s]).wait()

    def wait_scatter():
        pltpu.make_async_copy(y_scr, y_hbm.at[pl.ds(0, T * R), :], sems.at[SCATTER_SEM]).wait()

    @pl.when((i == 0) & (c == 0))
    def _():
        y_scr[...] = jnp.zeros_like(y_scr)
        acc_scr[...] = jnp.zeros_like(acc_scr)

        def issue(r, carry):
            row_copy(h_hbm, src0_ref[0, window_offset(0) + r], x_scr.at[0], r * R, sems.at[0]).start()
            return carry
        lax.fori_loop(0, T, issue, 0)

    @pl.when(c == 0)
    def _():
        wait_gather(slot)

    def row_dmas():
        off_next = window_offset(next_tile)
        off_prev = window_offset(prev_tile)
        for r in range(T):
            row_copy(h_hbm, srcn_ref[0, off_next + r], x_scr.at[other], r * R, sems.at[other]).start()
            row_copy(y_scr, r * R, y_hbm, dest_row(dstp_ref, off_prev, r, rows_prev),
                     sems.at[SCATTER_SEM]).start()

    def expert_chunk(first, rows):
        xn = xn_scr[0:rows, :]
        gate = jnp.dot(xn, wg_ref[0], preferred_element_type=F32)
        up = jnp.dot(xn, wu_ref[0], preferred_element_type=F32)
        mid = (jax.nn.silu(gate) * up).astype(BF16)
        down = jnp.dot(mid, wd_ref[0], preferred_element_type=F32)
        if first:
            acc_scr[0:rows, :] = down
        else:
            acc_scr[0:rows, :] += down

    few_rows = tile_rows(i) <= T // 2

    for few, rows in ((False, T), (True, T // 2)):
        branch = valid & (few_rows if few else jnp.logical_not(few_rows))

        @pl.when(branch & (c == 0))
        def _():
            xn_scr[0:rows, :] = _rmsnorm(_load_token_major(x_scr.at[slot], rows), g_ref[...]).astype(BF16)
            row_dmas()
            expert_chunk(True, rows)

        @pl.when(branch & (c != 0))
        def _():
            expert_chunk(False, rows)

    @pl.when(jnp.logical_not(valid) & (c == 0))
    def _():
        row_dmas()

    @pl.when(c == n_chunks - 1)
    def _():
        wait_scatter()

    @pl.when(valid & (c == n_chunks - 1))
    def _():
        _store_token_major(y_scr, acc_scr[...])

    @pl.when((i == n_tiles - 1) & (c == n_chunks - 1))
    def _():
        wait_gather(other)


        def issue(r, carry):
            row_copy(y_scr, r * R, y_hbm, dest_row(dstl_ref, window_offset(n_tiles - 1), r, tile_rows(n_tiles - 1)),
                     sems.at[SCATTER_SEM]).start()
            return carry
        lax.fori_loop(0, T, issue, 0)
        wait_scatter()


def _moe_experts(h1_tm, g, tile_expert, n_valid_tiles, src_rows, dst_rows, w_gate, w_up, w_down, n_out_tokens):
    D = D_MODEL
    T = MOE_TILE
    NT = tile_expert.shape[0]
    NC = N_MOE_CHUNKS
    FC = D_FF // NC

    def chunk(i, c, nvt):
        return jnp.where(i < nvt[0], c, NC - 1)

    def window_spec(tile_of_step):
        return pl.BlockSpec((pl.Element(1), pl.Element(T + LANES)),
                            lambda i, c, te, nvt: (0, nvt[1 + NT + tile_of_step(i)] * LANES),
                            memory_space=pltpu.SMEM)

    grid_spec = pltpu.PrefetchScalarGridSpec(
        num_scalar_prefetch=2,
        grid=(NT, NC),
        in_specs=[
            window_spec(lambda i: 0),
            window_spec(lambda i: jnp.minimum(i + 1, NT - 1)),
            window_spec(lambda i: jnp.maximum(i - 1, 0)),
            window_spec(lambda i: NT - 1),
            pl.BlockSpec(memory_space=pl.ANY),
            pl.BlockSpec((1, D), lambda i, c, te, nvt: (0, 0)),
            pl.BlockSpec((1, D, FC), lambda i, c, te, nvt: (te[i], 0, chunk(i, c, nvt))),
            pl.BlockSpec((1, D, FC), lambda i, c, te, nvt: (te[i], 0, chunk(i, c, nvt))),
            pl.BlockSpec((1, FC, D), lambda i, c, te, nvt: (te[i], chunk(i, c, nvt), 0)),
        ],
        out_specs=pl.BlockSpec(memory_space=pl.ANY),
        scratch_shapes=[pltpu.VMEM((2, T * ROWS_PER_TOKEN, LANES), F32), pltpu.VMEM((T, D), BF16),
                        pltpu.VMEM((T, D), F32), pltpu.VMEM((T * ROWS_PER_TOKEN, LANES), F32),
                        pltpu.SemaphoreType.DMA((3,))],
    )
    return pl.pallas_call(
        _moe_kernel,
        grid_spec=grid_spec,
        out_shape=jax.ShapeDtypeStruct((n_out_tokens * ROWS_PER_TOKEN, LANES), F32),
        compiler_params=pltpu.CompilerParams(dimension_semantics=("arbitrary", "arbitrary"),
                                             vmem_limit_bytes=VMEM_LIMIT),
        name="moe_experts",
    )(tile_expert, n_valid_tiles, src_rows, src_rows, dst_rows, dst_rows, h1_tm, g.reshape(1, D),
      w_gate, w_up, w_down)


def _moe_plan(route, n_tokens):
    T = MOE_TILE
    n_assign = TOP_K * n_tokens
    n_tiles = n_assign // T + N_EXPERTS
    e_flat = route[:, 0:TOP_K].astype(jnp.int32).reshape(-1)
    counts = jnp.sum((e_flat[:, None] == jnp.arange(N_EXPERTS, dtype=jnp.int32)[None, :]).astype(jnp.int32), axis=0)
    order = jnp.argsort(e_flat, stable=True).astype(jnp.int32)
    tiles_per = (counts + T - 1) // T
    tile_end = jnp.cumsum(tiles_per)
    n_valid = tile_end[-1]
    tile_ids = jnp.arange(n_tiles, dtype=jnp.int32)
    last_valid = jnp.maximum(n_valid - 1, 0)
    texp = jnp.sum((jnp.minimum(tile_ids, last_valid)[:, None] >= tile_end[None, :]).astype(jnp.int32), axis=1)
    texp = jnp.minimum(texp, N_EXPERTS - 1)
    onehot = (texp[:, None] == jnp.arange(N_EXPERTS, dtype=jnp.int32)[None, :]).astype(jnp.int32)
    pick = lambda v: jnp.sum(onehot * v[None, :], axis=1)
    j = tile_ids - pick(tile_end - tiles_per)
    start = jnp.where(tile_ids < n_valid, pick(jnp.cumsum(counts) - counts) + j * T, 0)
    n_rows = jnp.where(tile_ids < n_valid, jnp.clip(pick(counts) - j * T, 0, T), 0)
    token = order // TOP_K
    slot = order % TOP_K
    spare = jnp.zeros((T + LANES,), jnp.int32)
    src_sorted = jnp.concatenate([token * ROWS_PER_TOKEN, spare]).reshape(1, -1)
    dst_sorted = jnp.concatenate([(slot * n_tokens + token) * ROWS_PER_TOKEN, spare]).reshape(1, -1)
    tile_counts = jnp.concatenate([n_valid.reshape(1), n_rows, start // LANES, start % LANES]).astype(jnp.int32)
    return texp, tile_counts, src_sorted, dst_sorted


def _combine_kernel(h_ref, y0_ref, y1_ref, route_ref, g_ref, o_ref):
    T = o_ref.shape[0]
    route = route_ref[...]
    moe = route[:, 2:3] * _load_token_major(y0_ref, T) + route[:, 3:4] * _load_token_major(y1_ref, T)
    o_ref[...] = _rmsnorm(_load_token_major(h_ref, T) + moe, g_ref[...])


def _combine(h1_tm, y2_tm, route, g):
    D = D_MODEL
    S = route.shape[0]
    T = OUT_TILE
    nb = S // T
    tm_block = (T * ROWS_PER_TOKEN, LANES)
    return pl.pallas_call(
        _combine_kernel,
        grid=(nb,),
        in_specs=[
            pl.BlockSpec(tm_block, lambda i: (i, 0)),
            pl.BlockSpec(tm_block, lambda i: (i, 0)),
            pl.BlockSpec(tm_block, lambda i: (i + nb, 0)),
            pl.BlockSpec((T, LANES), lambda i: (i, 0)),
            pl.BlockSpec((1, D), lambda i: (0, 0)),
        ],
        out_specs=pl.BlockSpec((T, D), lambda i: (i, 0)),
        out_shape=jax.ShapeDtypeStruct((S, D), F32),
        compiler_params=pltpu.CompilerParams(dimension_semantics=("arbitrary",), vmem_limit_bytes=VMEM_LIMIT),
        name="moe_combine",
    )(h1_tm, y2_tm, y2_tm, route, g.reshape(1, D))


def kernel(x, positions, norm_mix, norm_ffn, norm_final, lru_w_in, lru_conv_w, lru_conv_b, lru_w_a, lru_b_a, lru_w_x, lru_b_x, lru_lambda, lru_w_out, attn_w_qkv, attn_b_qkv, attn_sinks, attn_w_o, attn_b_o, ffn_w_gate, ffn_w_up, ffn_w_down, moe_w_router, moe_w_gate, moe_w_up, moe_w_down):
    B, S, D = x.shape
    assert B == 1 and D == D_MODEL and S % FFN_TILE == 0
    h = x.reshape(S, D)

    w_ax = jnp.concatenate([lru_w_a[0], lru_w_x[0]], axis=-1).astype(BF16)
    h, ffn_wg, ffn_wu, ffn_wd = _lru_mixer(h, norm_mix[0], lru_w_in[0].astype(BF16), lru_conv_w[0], lru_conv_b[0],
                                           w_ax, lru_b_a[0], lru_b_x[0], lru_lambda[0], lru_w_out[0].astype(BF16),
                                           ffn_w_gate[0], ffn_w_up[0], ffn_w_down[0])
    h, moe_wg, moe_wu, moe_wd = _dense_ffn(h, norm_ffn[0], ffn_wg, ffn_wu, ffn_wd,
                                           moe_w_gate[0], moe_w_up[0], moe_w_down[0])

    q_dim = N_HEADS * HEAD_DIM
    kv_dim = N_KV_HEADS * HEAD_DIM

    def dup_heads(w):
        w4 = w.reshape(w.shape[:-1] + (N_KV_HEADS, 1, HEAD_DIM))
        return jnp.broadcast_to(w4, w.shape[:-1] + (N_KV_HEADS, 2, HEAD_DIM)).reshape(w.shape[:-1] + (KV_EXT,))

    wqkv, bqkv = attn_w_qkv[0], attn_b_qkv[0]
    w_qkv_ext = jnp.concatenate([wqkv[:, :q_dim], dup_heads(wqkv[:, q_dim:q_dim + kv_dim]),
                                 dup_heads(wqkv[:, q_dim + kv_dim:])], axis=1).astype(BF16)
    b_qkv_ext = jnp.concatenate([bqkv[:q_dim], dup_heads(bqkv[q_dim:q_dim + kv_dim]),
                                 dup_heads(bqkv[q_dim + kv_dim:])]).reshape(1, QKV_EXT)
    inv_freq = ROPE_THETA ** (-jnp.arange(0, ROT_DIM, 2, dtype=F32) / ROT_DIM)
    d_in_head = jnp.arange(LANES) % HEAD_DIM
    freq_lane = jnp.where(d_in_head < ROT_DIM, inv_freq[d_in_head % (ROT_DIM // 2)], 0.0).reshape(1, LANES)
    sign_lane = jnp.where(d_in_head < ROT_DIM // 2, -1.0, jnp.where(d_in_head < ROT_DIM, 1.0, 0.0))
    sign_lane = sign_lane.astype(F32).reshape(1, LANES)
    w_router_pad = jnp.pad(moe_w_router[0], ((0, 0), (0, LANES - N_EXPERTS))).astype(BF16)
    pos_f = positions.reshape(S, 1).astype(F32)
    h1_tm, route = _attn_mixer(h, pos_f, norm_mix[1], freq_lane, sign_lane, w_qkv_ext, b_qkv_ext,
                               attn_sinks[0], attn_w_o[0].astype(BF16), attn_b_o[0], norm_ffn[1], w_router_pad)

    texp, n_valid, src_rows, dst_rows = _moe_plan(route, S)
    y2_tm = _moe_experts(h1_tm, norm_ffn[1], texp, n_valid, src_rows, dst_rows,
                         moe_wg, moe_wu, moe_wd,
                         TOP_K * S + MOE_TILE)
    out = _combine(h1_tm, y2_tm, route, norm_final)
    return out.reshape(B, S, D)
```

```python
import jax
import jax.numpy as jnp
from jax import lax
from jax.experimental import pallas as pl
from jax.experimental.pallas import tpu as pltpu

F32 = jnp.float32
BF16 = jnp.bfloat16

D_MODEL = 1024
N_RNN_BLOCKS = 8
RNN_BLOCK = D_MODEL // N_RNN_BLOCKS
CONV_WIDTH = 4
LRU_C = 8.0
N_HEADS = 16
N_KV_HEADS = 4
HEAD_DIM = 64
GROUP = N_HEADS // N_KV_HEADS
WINDOW = 128
ATTN_BLOCK = 128
assert WINDOW == ATTN_BLOCK
ROPE_THETA = 500000.0
ROT_DIM = HEAD_DIM // 4
D_FF = 3584
N_EXPERTS = 8
TOP_K = 2
EPS = 1e-6
NEG_INF = -1e30
LOG2_E = 1.4426950408889634

LANES = 128
SUBLANES = 8
VMEM_LIMIT = 56 * 1024 * 1024

SEQ_TILE = 512
ATTN_TILE = 1024
LRU_CHUNKS = 4
FFN_TILE = 1024
FF_CHUNK = 512
N_FF_CHUNKS = D_FF // FF_CHUNK
MOE_TILE = 512
N_MOE_CHUNKS = 2
OUT_TILE = 1024


def _rmsnorm(x, g):
    return x * lax.rsqrt(jnp.mean(x * x, axis=-1, keepdims=True) + EPS) * g


ROWS_PER_TOKEN = D_MODEL // LANES


def _store_token_major(ref, val):
    T = val.shape[0]
    for c in range(ROWS_PER_TOKEN):
        ref[pl.ds(c, T, stride=ROWS_PER_TOKEN), :] = val[:, LANES * c:LANES * (c + 1)]


def _load_token_major(ref, T):
    return jnp.concatenate([ref[pl.ds(c, T, stride=ROWS_PER_TOKEN), :] for c in range(ROWS_PER_TOKEN)], axis=1)


def _const_spec(shape):
    n = len(shape)
    return pl.BlockSpec(shape, lambda *_: (0,) * n, pipeline_mode=pl.Buffered(1))


def _time_permutation(T):
    rho = jnp.arange(T)
    t_of_rho = (rho % SUBLANES) * (T // SUBLANES) + rho // SUBLANES
    perm = (t_of_rho[:, None] == jnp.arange(T)[None, :]).astype(BF16)
    return perm, perm.T


def _lru_kernel(xin_ref, xres_ref, g_ref, perm_ref, unperm_ref, win_ref, cw_ref, cb_ref, wax_ref, ba_ref, bx_ref,
                lam_ref, wout_ref, cast_g_ref, cast_u_ref, cast_d_ref,
                o_ref, bf_g_ref, bf_u_ref, bf_d_ref, proj_scr, tail_scr, carry_scr):
    i = pl.program_id(0)
    n_tiles = pl.num_programs(0) - 1
    T = xin_ref.shape[0]
    D = D_MODEL
    C = T // LRU_CHUNKS
    GC = C // SUBLANES
    G = T // SUBLANES
    HALO = (CONV_WIDTH - 1) * SUBLANES

    def cast_slices():
        for c in range(N_FF_CHUNKS):
            cols = slice(FF_CHUNK * c, FF_CHUNK * (c + 1))
            bf_g_ref[c] = cast_g_ref[:, cols].astype(BF16)
            bf_u_ref[c] = cast_u_ref[:, cols].astype(BF16)
        bf_d_ref[...] = cast_d_ref[...].astype(BF16)

    @pl.when(i == 0)
    def _():
        tail_scr[...] = jnp.zeros_like(tail_scr)
        carry_scr[...] = jnp.zeros_like(carry_scr)

    PW = 2 * D // LRU_CHUNKS

    def project_chunks():
        hn = _rmsnorm(xin_ref[...], g_ref[...]).astype(BF16)
        hp = jnp.dot(perm_ref[...], hn, preferred_element_type=F32).astype(BF16)
        for k in range(LRU_CHUNKS):
            yield jnp.dot(hp, win_ref[:, PW * k:PW * (k + 1)], preferred_element_type=F32)

    def recur_chunks():
        sub = lax.broadcasted_iota(jnp.int32, (SUBLANES, D), 0)
        cw = cw_ref[...]
        softplus_neg_lam = jax.nn.softplus(-lam_ref[...])
        last = proj_scr[T - HALO:T, D:]
        prev_groups = []
        for k in range(CONV_WIDTH - 1):
            rows = slice(SUBLANES * k, SUBLANES * (k + 1))
            prev_groups.append(pltpu.roll(jnp.where(sub == SUBLANES - 1, tail_scr[rows, :], last[rows, :]), 1, 0))
        tail_scr[...] = last

        ys, hs, decay = [], [], []
        for k in range(LRU_CHUNKS):
            pc = proj_scr[C * k:C * (k + 1), :]
            ys.append(jax.nn.gelu(pc[:, :D], approximate=True))
            xb = pc[:, D:]
            xc = cb_ref[...] + xb * cw[CONV_WIDTH - 1:CONV_WIDTH, :]
            for back in range(1, CONV_WIDTH):
                shifted = jnp.concatenate(prev_groups[CONV_WIDTH - 1 - back:] + [xb[:C - SUBLANES * back, :]], axis=0)
                xc = xc + shifted * cw[CONV_WIDTH - 1 - back:CONV_WIDTH - back, :]
            prev_groups = [xb[C - HALO + SUBLANES * q:C - HALO + SUBLANES * (q + 1), :] for q in range(CONV_WIDTH - 1)]

            xcb = xc.astype(BF16)
            r_parts, i_parts = [], []
            for n in range(N_RNN_BLOCKS):
                gn = jnp.dot(xcb[:, RNN_BLOCK * n:RNN_BLOCK * (n + 1)], wax_ref[n], preferred_element_type=F32)
                r_parts.append(gn[:, :RNN_BLOCK])
                i_parts.append(gn[:, RNN_BLOCK:])
            r = jax.nn.sigmoid(jnp.concatenate(r_parts, axis=1) + ba_ref[...])
            ig = jax.nn.sigmoid(jnp.concatenate(i_parts, axis=1) + bx_ref[...])
            log_a = -LRU_C * r * softplus_neg_lam
            a = jnp.exp(log_a)
            z = jnp.tanh(-log_a) * (1.0 + a * a)
            b = jnp.where(z > 0.0, z * lax.rsqrt(z), 0.0) * (ig * xc)

            for j in range(GC):
                rows = slice(SUBLANES * j, SUBLANES * (j + 1))
                if hs:
                    hs.append(a[rows, :] * hs[-1] + b[rows, :])
                    decay.append(a[rows, :] * decay[-1])
                else:
                    hs.append(b[rows, :])
                    decay.append(a[rows, :])
            yield None

        p_inc, e_inc = decay[-1], hs[-1]
        for k in (1, 2, 4):
            m = sub >= k
            p_sh = jnp.where(m, pltpu.roll(p_inc, k, 0), 1.0)
            e_sh = jnp.where(m, pltpu.roll(e_inc, k, 0), 0.0)
            e_inc = e_inc + p_inc * e_sh
            p_inc = p_inc * p_sh
        h0 = carry_scr[...]
        after = p_inc * h0 + e_inc
        carry_scr[...] = after[SUBLANES - 1:SUBLANES, :]
        enter = jnp.where(sub == 0, h0, pltpu.roll(after, 1, 0))
        h = jnp.concatenate([hs[j] + decay[j] * enter for j in range(G)], axis=0)
        hy = (h * jnp.concatenate(ys, axis=0)).astype(BF16)
        hy = jnp.dot(unperm_ref[...], hy, preferred_element_type=F32).astype(BF16)
        yield jnp.dot(hy, wout_ref[...], preferred_element_type=F32) + xres_ref[...]

    @pl.when(i == 0)
    def _():
        cast_slices()
        for k, pc in enumerate(project_chunks()):
            proj_scr[:, PW * k:PW * (k + 1)] = pc

    @pl.when((i > 0) & (i < n_tiles))
    def _():
        cast_slices()
        rec = recur_chunks()
        new_proj = []
        for pc in project_chunks():
            new_proj.append(pc)
            next(rec)
        o_ref[...] = next(rec)
        for k, pc in enumerate(new_proj):
            proj_scr[:, PW * k:PW * (k + 1)] = pc

    @pl.when(i == n_tiles)
    def _():
        o_ref[...] = list(recur_chunks())[-1]


def _lru_mixer(x, g, w_in, conv_w, conv_b, w_ax, b_a, b_x, lam, w_out, cast_gate, cast_up, cast_down):
    S, D = x.shape
    T = SEQ_TILE
    n_tiles = S // T
    row = lambda v: v.reshape(1, D)
    perm, unperm = _time_permutation(T)

    def cast_spec(w):
        return pl.BlockSpec((w.shape[0] // n_tiles, w.shape[1]), lambda i: (jnp.minimum(i, n_tiles - 1), 0))

    casts = (cast_gate, cast_up, cast_down)
    chunk_major_spec = pl.BlockSpec((N_FF_CHUNKS, D // n_tiles, FF_CHUNK),
                                    lambda i: (0, jnp.minimum(i, n_tiles - 1), 0))
    return pl.pallas_call(
        _lru_kernel,
        grid=(n_tiles + 1,),
        in_specs=[
            pl.BlockSpec((T, D), lambda i: (jnp.minimum(i, n_tiles - 1), 0)),
            pl.BlockSpec((T, D), lambda i: (jnp.maximum(i - 1, 0), 0)),
            _const_spec((1, D)),
            _const_spec((T, T)),
            _const_spec((T, T)),
            _const_spec((D, 2 * D)),
            _const_spec((CONV_WIDTH, D)),
            _const_spec((1, D)),
            _const_spec((N_RNN_BLOCKS, RNN_BLOCK, 2 * RNN_BLOCK)),
            _const_spec((1, D)),
            _const_spec((1, D)),
            _const_spec((1, D)),
            _const_spec((D, D)),
        ] + [cast_spec(w) for w in casts],
        out_specs=[pl.BlockSpec((T, D), lambda i: (jnp.maximum(i - 1, 0), 0)),
                   chunk_major_spec, chunk_major_spec, cast_spec(cast_down)],
        out_shape=[jax.ShapeDtypeStruct((S, D), F32),
                   jax.ShapeDtypeStruct((N_FF_CHUNKS, D, FF_CHUNK), BF16),
                   jax.ShapeDtypeStruct((N_FF_CHUNKS, D, FF_CHUNK), BF16),
                   jax.ShapeDtypeStruct(cast_down.shape, BF16)],
        scratch_shapes=[pltpu.VMEM((T, 2 * D), F32), pltpu.VMEM(((CONV_WIDTH - 1) * SUBLANES, D), F32),
                        pltpu.VMEM((1, D), F32)],
        compiler_params=pltpu.CompilerParams(dimension_semantics=("arbitrary",), vmem_limit_bytes=VMEM_LIMIT),
        name="lru_mixer",
    )(x, x, row(g), perm, unperm, w_in, conv_w, row(conv_b), w_ax, row(b_a), row(b_x), row(lam), w_out, *casts)


def _ffn_kernel(x_ref, g_ref, wg_ref, wu_ref, wd_ref, cast_g_ref, cast_u_ref, cast_d_ref,
                o_ref, bf_g_ref, bf_u_ref, bf_d_ref, xn_scr, acc_scr):
    c = pl.program_id(1)
    last = pl.num_programs(1) - 1

    def ff_chunk():
        xn = xn_scr[...]
        gate = jnp.dot(xn, wg_ref[0], preferred_element_type=F32)
        bf_g_ref[...] = cast_g_ref[...].astype(BF16)
        up = jnp.dot(xn, wu_ref[0], preferred_element_type=F32)
        bf_u_ref[...] = cast_u_ref[...].astype(BF16)
        mid = (jax.nn.silu(gate) * up).astype(BF16)
        down = jnp.dot(mid, wd_ref[...], preferred_element_type=F32)
        bf_d_ref[...] = cast_d_ref[...].astype(BF16)
        return down

    @pl.when(c == 0)
    def _():
        xn_scr[...] = _rmsnorm(x_ref[...], g_ref[...]).astype(BF16)
        acc_scr[...] = ff_chunk()

    @pl.when((c > 0) & (c < last))
    def _():
        acc_scr[...] += ff_chunk()

    @pl.when(c == last)
    def _():
        o_ref[...] = x_ref[...] + (acc_scr[...] + ff_chunk())


def _dense_ffn(x, g, w_gate, w_up, w_down, cast_gate, cast_up, cast_down):
    S, D = x.shape
    T = FFN_TILE
    n_steps = (S // T) * N_FF_CHUNKS
    E = cast_gate.shape[0]
    row_parts = n_steps // (E * N_FF_CHUNKS)
    assert row_parts * E * N_FF_CHUNKS == n_steps
    gu_block = (1, D // row_parts, D_FF // N_FF_CHUNKS)
    d_block = (1, D_FF // (row_parts * N_FF_CHUNKS), D)

    def gu_map(i, c):
        s = i * N_FF_CHUNKS + c
        return (s // (row_parts * N_FF_CHUNKS), (s // N_FF_CHUNKS) % row_parts, c)

    def d_map(i, c):
        s = i * N_FF_CHUNKS + c
        return (s // (row_parts * N_FF_CHUNKS), s % (row_parts * N_FF_CHUNKS), 0)

    return pl.pallas_call(
        _ffn_kernel,
        grid=(S // T, N_FF_CHUNKS),
        in_specs=[
            pl.BlockSpec((T, D), lambda i, c: (i, 0)),
            pl.BlockSpec((1, D), lambda i, c: (0, 0)),
            pl.BlockSpec((1, D, FF_CHUNK), lambda i, c: (c, 0, 0)),
            pl.BlockSpec((1, D, FF_CHUNK), lambda i, c: (c, 0, 0)),
            pl.BlockSpec((FF_CHUNK, D), lambda i, c: (c, 0)),
            pl.BlockSpec(gu_block, gu_map),
            pl.BlockSpec(gu_block, gu_map),
            pl.BlockSpec(d_block, d_map),
        ],
        out_specs=[pl.BlockSpec((T, D), lambda i, c: (i, 0)), pl.BlockSpec(gu_block, gu_map),
                   pl.BlockSpec(gu_block, gu_map), pl.BlockSpec(d_block, d_map)],
        out_shape=[jax.ShapeDtypeStruct((S, D), F32), jax.ShapeDtypeStruct(cast_gate.shape, BF16),
                   jax.ShapeDtypeStruct(cast_up.shape, BF16), jax.ShapeDtypeStruct(cast_down.shape, BF16)],
        scratch_shapes=[pltpu.VMEM((T, D), BF16), pltpu.VMEM((T, D), F32)],
        compiler_params=pltpu.CompilerParams(dimension_semantics=("arbitrary", "arbitrary"),
                                             vmem_limit_bytes=VMEM_LIMIT),
        name="dense_ffn",
    )(x, g.reshape(1, D), w_gate, w_up, w_down, cast_gate, cast_up, cast_down)


KV_EXT = N_KV_HEADS * LANES
QKV_EXT = N_HEADS * HEAD_DIM + 2 * KV_EXT


def _attn_kernel(sink_ref, x_ref, pos_ref, g_ref, freq_ref, sign_ref, wqkv_ref, bqkv_ref, wo_ref, bo_ref,
                 g2_ref, wr_ref, o_ref, route_ref,
                 qlo_scr, qhi_scr, k_scr, v_scr, o_scr):
    T = x_ref.shape[0]
    D = D_MODEL
    B = ATTN_BLOCK
    first_tile = pl.program_id(0) == 0

    @pl.when(first_tile)
    def _():
        k_scr[0:B, :] = jnp.zeros((B, KV_EXT), BF16)
        v_scr[0:B, :] = jnp.zeros((B, KV_EXT), BF16)

    x = x_ref[...]
    hn = _rmsnorm(x, g_ref[...]).astype(BF16)
    qkv = jnp.dot(hn, wqkv_ref[...], preferred_element_type=F32) + bqkv_ref[...]

    ang = pos_ref[...] * freq_ref[...]
    cos_t = jnp.cos(ang)
    sin_t = jnp.sin(ang) * sign_ref[...]
    lane = lax.broadcasted_iota(jnp.int32, (T, LANES), 1)
    first_half = (lane % HEAD_DIM) < (ROT_DIM // 2)
    lo_half = lane < HEAD_DIM

    def rope(col):
        partner = jnp.where(first_half, pltpu.roll(col, LANES - ROT_DIM // 2, 1), pltpu.roll(col, ROT_DIM // 2, 1))
        return col * cos_t + partner * sin_t

    scale = HEAD_DIM ** -0.5 * LOG2_E
    for c in range(D // LANES):
        qc = rope(qkv[:, LANES * c:LANES * (c + 1)]) * scale
        qlo_scr[:, LANES * c:LANES * (c + 1)] = jnp.where(lo_half, qc, 0.0).astype(BF16)
        qhi_scr[:, LANES * c:LANES * (c + 1)] = jnp.where(lo_half, 0.0, qc).astype(BF16)
    for g in range(N_KV_HEADS):
        kc = rope(qkv[:, D + LANES * g:D + LANES * (g + 1)])
        k_scr[B:B + T, LANES * g:LANES * (g + 1)] = kc.astype(BF16)
        vc = qkv[:, D + KV_EXT + LANES * g:D + KV_EXT + LANES * (g + 1)]
        v_scr[B:B + T, LANES * g:LANES * (g + 1)] = vc.astype(BF16)

    qi = lax.broadcasted_iota(jnp.int32, (B, B), 0)
    kj = lax.broadcasted_iota(jnp.int32, (B, B), 1)
    causal_own = kj <= qi
    window_prev = kj > qi
    key0 = kj == 0
    key0_row = lax.broadcasted_iota(jnp.int32, (1, B), 1) == 0
    lo_blk = lax.broadcasted_iota(jnp.int32, (B, LANES), 1) < HEAD_DIM

    def block_body(blk, _):
        r0 = pl.multiple_of(blk * B, B)
        k_lo = jnp.where(first_tile & (blk == 0), B, 0)
        allowed_prev = window_prev & (kj >= k_lo)
        for g in range(N_KV_HEADS):
            kk = k_scr[pl.ds(r0, 2 * B), LANES * g:LANES * (g + 1)]
            vv = v_scr[pl.ds(r0, 2 * B), LANES * g:LANES * (g + 1)]
            cols = [slice(LANES * c, LANES * (c + 1)) for c in range((GROUP // 2) * g, (GROUP // 2) * (g + 1))]
            q_all = jnp.concatenate([q_scr[pl.ds(r0, B), col] for col in cols for q_scr in (qlo_scr, qhi_scr)], axis=0)
            s_all = lax.dot_general(q_all, kk, (((1,), (1,)), ((), ())), preferred_element_type=F32)
            p_all, denoms = [], []
            for hh in range(GROUP):
                rows = slice(B * hh, B * (hh + 1))
                sink_fill = jnp.where(key0_row, sink_ref[GROUP * g + hh] * LOG2_E, NEG_INF)
                s_prev = jnp.where(allowed_prev, s_all[rows, :B], sink_fill)
                s_own = jnp.where(causal_own, s_all[rows, B:], NEG_INF)
                m = jnp.max(jnp.maximum(s_prev, s_own), axis=-1, keepdims=True)
                p_prev = jnp.exp2(s_prev - m)
                p_own = jnp.exp2(s_own - m)
                denoms.append(jnp.sum(p_prev + p_own, axis=-1, keepdims=True))
                p_all.append(jnp.concatenate([jnp.where(key0, 0.0, p_prev), p_own], axis=1).astype(BF16))
            o_all = jnp.dot(jnp.concatenate(p_all, axis=0), vv, preferred_element_type=F32)
            o_heads = [o_all[B * hh:B * (hh + 1), :] / denoms[hh] for hh in range(GROUP)]
            for p, col in enumerate(cols):
                o_scr[pl.ds(r0, B), col] = jnp.where(lo_blk, o_heads[2 * p], o_heads[2 * p + 1]).astype(BF16)
        return 0

    lax.fori_loop(0, T // B, block_body, 0)

    k_scr[0:B, :] = k_scr[T:T + B, :]
    v_scr[0:B, :] = v_scr[T:T + B, :]

    h1 = jnp.dot(o_scr[...], wo_ref[...], preferred_element_type=F32) + bo_ref[...] + x
    _store_token_major(o_ref, h1)

    hn2 = _rmsnorm(h1, g2_ref[...]).astype(BF16)
    logits = jnp.dot(hn2, wr_ref[...], preferred_element_type=F32)
    lane_f = lane.astype(F32)
    lg = jnp.where(lane < N_EXPERTS, logits, -jnp.inf)
    v1 = jnp.max(lg, axis=-1, keepdims=True)
    i1 = jnp.min(jnp.where(lg == v1, lane_f, float(LANES)), axis=-1, keepdims=True)
    lg2 = jnp.where(lane_f == i1, -jnp.inf, lg)
    v2 = jnp.max(lg2, axis=-1, keepdims=True)
    i2 = jnp.min(jnp.where(lg2 == v2, lane_f, float(LANES)), axis=-1, keepdims=True)
    e2 = jnp.exp(v2 - v1)
    gate1 = 1.0 / (1.0 + e2)
    gate2 = e2 / (1.0 + e2)
    route_ref[...] = jnp.where(lane == 0, i1, jnp.where(lane == 1, i2,
                                                        jnp.where(lane == 2, gate1, jnp.where(lane == 3, gate2, 0.0))))


def _attn_mixer(x, pos_f, g, freq_lane, sign_lane, w_qkv_ext, b_qkv_ext, sinks, w_o, b_o, g2, w_router_pad):
    S, D = x.shape
    T = ATTN_TILE
    B = ATTN_BLOCK
    grid_spec = pltpu.PrefetchScalarGridSpec(
        num_scalar_prefetch=1,
        grid=(S // T,),
        in_specs=[
            pl.BlockSpec((T, D), lambda i, s: (i, 0)),
            pl.BlockSpec((T, 1), lambda i, s: (i, 0)),
            _const_spec((1, D)),
            _const_spec((1, LANES)),
            _const_spec((1, LANES)),
            _const_spec((D, QKV_EXT)),
            _const_spec((1, QKV_EXT)),
            _const_spec((D, D)),
            _const_spec((1, D)),
            _const_spec((1, D)),
            _const_spec((D, LANES)),
        ],
        out_specs=[pl.BlockSpec((T * ROWS_PER_TOKEN, LANES), lambda i, s: (i, 0)),
                   pl.BlockSpec((T, LANES), lambda i, s: (i, 0))],
        scratch_shapes=[
            pltpu.VMEM((T, D), BF16), pltpu.VMEM((T, D), BF16),
            pltpu.VMEM((T + B, KV_EXT), BF16), pltpu.VMEM((T + B, KV_EXT), BF16),
            pltpu.VMEM((T, D), BF16),
        ],
    )
    return pl.pallas_call(
        _attn_kernel,
        grid_spec=grid_spec,
        out_shape=[jax.ShapeDtypeStruct((S * ROWS_PER_TOKEN, LANES), F32), jax.ShapeDtypeStruct((S, LANES), F32)],
        compiler_params=pltpu.CompilerParams(dimension_semantics=("arbitrary",), vmem_limit_bytes=VMEM_LIMIT),
        name="attn_mixer",
    )(sinks, x, pos_f, g.reshape(1, D), freq_lane, sign_lane, w_qkv_ext, b_qkv_ext, w_o, b_o.reshape(1, D),
      g2.reshape(1, D), w_router_pad)


def _moe_kernel(texp_ref, nvt_ref, src0_ref, srcn_ref, dstp_ref, dstl_ref, h_hbm, g_ref, wg_ref, wu_ref, wd_ref,
                y_hbm, x_scr, xn_scr, acc_scr, y_scr, sems):
    i = pl.program_id(0)
    c = pl.program_id(1)
    n_tiles = pl.num_programs(0)
    n_chunks = pl.num_programs(1)
    T = xn_scr.shape[0]
    R = ROWS_PER_TOKEN
    valid = i < nvt_ref[0]
    slot = i % 2
    other = 1 - slot
    SCATTER_SEM = 2

    def row_copy(src, src_row, dst, dst_row, sem):
        return pltpu.make_async_copy(src.at[pl.ds(pl.multiple_of(src_row, R), R), :],
                                     dst.at[pl.ds(pl.multiple_of(dst_row, R), R), :], sem)

    def tile_rows(t):
        return nvt_ref[1 + t]

    def window_offset(t):
        return nvt_ref[1 + 2 * n_tiles + t]

    prev_tile = jnp.maximum(i - 1, 0)
    next_tile = jnp.minimum(i + 1, n_tiles - 1)
    rows_prev = jnp.where(i > 0, tile_rows(prev_tile), 0)
    pad_base = y_hbm.shape[0] - T * R

    def dest_row(window_ref, offset, r, n_rows):
        return jnp.where(r < n_rows, window_ref[0, offset + r], pad_base + r * R)

    def wait_gather(s):
        pltpu.make_async_copy(h_hbm.at[pl.ds(0, T * R), :], x_scr.at[s], sems.at[s]).wait()

    def wait_scatter():
        pltpu.make_async_copy(y_scr, y_hbm.at[pl.ds(0, T * R), :], sems.at[SCATTER_SEM]).wait()

    @pl.when((i == 0) & (c == 0))
    def _():
        y_scr[...] = jnp.zeros_like(y_scr)
        acc_scr[...] = jnp.zeros_like(acc_scr)

        def issue(r, carry):
            row_copy(h_hbm, src0_ref[0, window_offset(0) + r], x_scr.at[0], r * R, sems.at[0]).start()
            return carry
        lax.fori_loop(0, T, issue, 0)

    @pl.when(c == 0)
    def _():
        wait_gather(slot)

    def row_dmas():
        off_next = window_offset(next_tile)
        off_prev = window_offset(prev_tile)
        for r in range(T):
            row_copy(h_hbm, srcn_ref[0, off_next + r], x_scr.at[other], r * R, sems.at[other]).start()
            row_copy(y_scr, r * R, y_hbm, dest_row(dstp_ref, off_prev, r, rows_prev),
                     sems.at[SCATTER_SEM]).start()

    def expert_chunk(first, rows):
        xn = xn_scr[0:rows, :]
        gate = jnp.dot(xn, wg_ref[0], preferred_element_type=F32)
        up = jnp.dot(xn, wu_ref[0], preferred_element_type=F32)
        mid = (jax.nn.silu(gate) * up).astype(BF16)
        down = jnp.dot(mid, wd_ref[0], preferred_element_type=F32)
        if first:
            acc_scr[0:rows, :] = down
        else:
            acc_scr[0:rows, :] += down

    few_rows = tile_rows(i) <= T // 2

    for few, rows in ((False, T), (True, T // 2)):
        branch = valid & (few_rows if few else jnp.logical_not(few_rows))

        @pl.when(branch & (c == 0))
        def _():
            xn_scr[0:rows, :] = _rmsnorm(_load_token_major(x_scr.at[slot], rows), g_ref[...]).astype(BF16)
            row_dmas()
            expert_chunk(True, rows)

        @pl.when(branch & (c != 0))
        def _():
            expert_chunk(False, rows)

    @pl.when(jnp.logical_not(valid) & (c == 0))
    def _():
        row_dmas()

    @pl.when(c == n_chunks - 1)
    def _():
        wait_scatter()

    @pl.when(valid & (c == n_chunks - 1))
    def _():
        _store_token_major(y_scr, acc_scr[...])

    @pl.when((i == n_tiles - 1) & (c == n_chunks - 1))
    def _():
        wait_gather(other)


        def issue(r, carry):
            row_copy(y_scr, r * R, y_hbm, dest_row(dstl_ref, window_offset(n_tiles - 1), r, tile_rows(n_tiles - 1)),
                     sems.at[SCATTER_SEM]).start()
            return carry
        lax.fori_loop(0, T, issue, 0)
        wait_scatter()


def _moe_experts(h1_tm, g, tile_expert, n_valid_tiles, src_rows, dst_rows, w_gate, w_up, w_down, n_out_tokens):
    D = D_MODEL
    T = MOE_TILE
    NT = tile_expert.shape[0]
    NC = N_MOE_CHUNKS
    FC = D_FF // NC

    def chunk(i, c, nvt):
        return jnp.where(i < nvt[0], c, NC - 1)

    def window_spec(tile_of_step):
        return pl.BlockSpec((pl.Element(1), pl.Element(T + LANES)),
                            lambda i, c, te, nvt: (0, nvt[1 + NT + tile_of_step(i)] * LANES),
                            memory_space=pltpu.SMEM)

    grid_spec = pltpu.PrefetchScalarGridSpec(
        num_scalar_prefetch=2,
        grid=(NT, NC),
        in_specs=[
            window_spec(lambda i: 0),
            window_spec(lambda i: jnp.minimum(i + 1, NT - 1)),
            window_spec(lambda i: jnp.maximum(i - 1, 0)),
            window_spec(lambda i: NT - 1),
            pl.BlockSpec(memory_space=pl.ANY),
            pl.BlockSpec((1, D), lambda i, c, te, nvt: (0, 0)),
            pl.BlockSpec((1, D, FC), lambda i, c, te, nvt: (te[i], 0, chunk(i, c, nvt))),
            pl.BlockSpec((1, D, FC), lambda i, c, te, nvt: (te[i], 0, chunk(i, c, nvt))),
            pl.BlockSpec((1, FC, D), lambda i, c, te, nvt: (te[i], chunk(i, c, nvt), 0)),
        ],
        out_specs=pl.BlockSpec(memory_space=pl.ANY),
        scratch_shapes=[pltpu.VMEM((2, T * ROWS_PER_TOKEN, LANES), F32), pltpu.VMEM((T, D), BF16),
                        pltpu.VMEM((T, D), F32), pltpu.VMEM((T * ROWS_PER_TOKEN, LANES), F32),
                        pltpu.SemaphoreType.DMA((3,))],
    )
    return pl.pallas_call(
        _moe_kernel,
        grid_spec=grid_spec,
        out_shape=jax.ShapeDtypeStruct((n_out_tokens * ROWS_PER_TOKEN, LANES), F32),
        compiler_params=pltpu.CompilerParams(dimension_semantics=("arbitrary", "arbitrary"),
                                             vmem_limit_bytes=VMEM_LIMIT),
        name="moe_experts",
    )(tile_expert, n_valid_tiles, src_rows, src_rows, dst_rows, dst_rows, h1_tm, g.reshape(1, D),
      w_gate, w_up, w_down)


def _moe_plan(route, n_tokens):
    T = MOE_TILE
    n_assign = TOP_K * n_tokens
    n_tiles = n_assign // T + N_EXPERTS
    e_flat = route[:, 0:TOP_K].astype(jnp.int32).reshape(-1)
    counts = jnp.sum((e_flat[:, None] == jnp.arange(N_EXPERTS, dtype=jnp.int32)[None, :]).astype(jnp.int32), axis=0)
    order = jnp.argsort(e_flat, stable=True).astype(jnp.int32)
    tiles_per = (counts + T - 1) // T
    tile_end = jnp.cumsum(tiles_per)
    n_valid = tile_end[-1]
    tile_ids = jnp.arange(n_tiles, dtype=jnp.int32)
    last_valid = jnp.maximum(n_valid - 1, 0)
    texp = jnp.sum((jnp.minimum(tile_ids, last_valid)[:, None] >= tile_end[None, :]).astype(jnp.int32), axis=1)
    texp = jnp.minimum(texp, N_EXPERTS - 1)
    onehot = (texp[:, None] == jnp.arange(N_EXPERTS, dtype=jnp.int32)[None, :]).astype(jnp.int32)
    pick = lambda v: jnp.sum(onehot * v[None, :], axis=1)
    j = tile_ids - pick(tile_end - tiles_per)
    start = jnp.where(tile_ids < n_valid, pick(jnp.cumsum(counts) - counts) + j * T, 0)
    n_rows = jnp.where(tile_ids < n_valid, jnp.clip(pick(counts) - j * T, 0, T), 0)
    token = order // TOP_K
    slot = order % TOP_K
    spare = jnp.zeros((T + LANES,), jnp.int32)
    src_sorted = jnp.concatenate([token * ROWS_PER_TOKEN, spare]).reshape(1, -1)
    dst_sorted = jnp.concatenate([(slot * n_tokens + token) * ROWS_PER_TOKEN, spare]).reshape(1, -1)
    tile_counts = jnp.concatenate([n_valid.reshape(1), n_rows, start // LANES, start % LANES]).astype(jnp.int32)
    return texp, tile_counts, src_sorted, dst_sorted


def _combine_kernel(h_ref, y0_ref, y1_ref, route_ref, g_ref, o_ref):
    T = o_ref.shape[0]
    route = route_ref[...]
    moe = route[:, 2:3] * _load_token_major(y0_ref, T) + route[:, 3:4] * _load_token_major(y1_ref, T)
    o_ref[...] = _rmsnorm(_load_token_major(h_ref, T) + moe, g_ref[...])


def _combine(h1_tm, y2_tm, route, g):
    D = D_MODEL
    S = route.shape[0]
    T = OUT_TILE
    nb = S // T
    tm_block = (T * ROWS_PER_TOKEN, LANES)
    return pl.pallas_call(
        _combine_kernel,
        grid=(nb,),
        in_specs=[
            pl.BlockSpec(tm_block, lambda i: (i, 0)),
            pl.BlockSpec(tm_block, lambda i: (i, 0)),
            pl.BlockSpec(tm_block, lambda i: (i + nb, 0)),
            pl.BlockSpec((T, LANES), lambda i: (i, 0)),
            pl.BlockSpec((1, D), lambda i: (0, 0)),
        ],
        out_specs=pl.BlockSpec((T, D), lambda i: (i, 0)),
        out_shape=jax.ShapeDtypeStruct((S, D), F32),
        compiler_params=pltpu.CompilerParams(dimension_semantics=("arbitrary",), vmem_limit_bytes=VMEM_LIMIT),
        name="moe_combine",
    )(h1_tm, y2_tm, y2_tm, route, g.reshape(1, D))


def kernel(x, positions, norm_mix, norm_ffn, norm_final, lru_w_in, lru_conv_w, lru_conv_b, lru_w_a, lru_b_a, lru_w_x, lru_b_x, lru_lambda, lru_w_out, attn_w_qkv, attn_b_qkv, attn_sinks, attn_w_o, attn_b_o, ffn_w_gate, ffn_w_up, ffn_w_down, moe_w_router, moe_w_gate, moe_w_up, moe_w_down):
    B, S, D = x.shape
    assert B == 1 and D == D_MODEL and S % FFN_TILE == 0
    h = x.reshape(S, D)

    w_ax = jnp.concatenate([lru_w_a[0], lru_w_x[0]], axis=-1).astype(BF16)
    h, ffn_wg, ffn_wu, ffn_wd = _lru_mixer(h, norm_mix[0], lru_w_in[0].astype(BF16), lru_conv_w[0], lru_conv_b[0],
                                           w_ax, lru_b_a[0], lru_b_x[0], lru_lambda[0], lru_w_out[0].astype(BF16),
                                           ffn_w_gate[0], ffn_w_up[0], ffn_w_down[0])
    h, moe_wg, moe_wu, moe_wd = _dense_ffn(h, norm_ffn[0], ffn_wg, ffn_wu, ffn_wd,
                                           moe_w_gate[0], moe_w_up[0], moe_w_down[0])

    q_dim = N_HEADS * HEAD_DIM
    kv_dim = N_KV_HEADS * HEAD_DIM

    def dup_heads(w):
        w4 = w.reshape(w.shape[:-1] + (N_KV_HEADS, 1, HEAD_DIM))
        return jnp.broadcast_to(w4, w.shape[:-1] + (N_KV_HEADS, 2, HEAD_DIM)).reshape(w.shape[:-1] + (KV_EXT,))

    wqkv, bqkv = attn_w_qkv[0], attn_b_qkv[0]
    w_qkv_ext = jnp.concatenate([wqkv[:, :q_dim], dup_heads(wqkv[:, q_dim:q_dim + kv_dim]),
                                 dup_heads(wqkv[:, q_dim + kv_dim:])], axis=1).astype(BF16)
    b_qkv_ext = jnp.concatenate([bqkv[:q_dim], dup_heads(bqkv[q_dim:q_dim + kv_dim]),
                                 dup_heads(bqkv[q_dim + kv_dim:])]).reshape(1, QKV_EXT)
    inv_freq = ROPE_THETA ** (-jnp.arange(0, ROT_DIM, 2, dtype=F32) / ROT_DIM)
    d_in_head = jnp.arange(LANES) % HEAD_DIM
    freq_lane = jnp.where(d_in_head < ROT_DIM, inv_freq[d_in_head % (ROT_DIM // 2)], 0.0).reshape(1, LANES)
    sign_lane = jnp.where(d_in_head < ROT_DIM // 2, -1.0, jnp.where(d_in_head < ROT_DIM, 1.0, 0.0))
    sign_lane = sign_lane.astype(F32).reshape(1, LANES)
    w_router_pad = jnp.pad(moe_w_router[0], ((0, 0), (0, LANES - N_EXPERTS))).astype(BF16)
    pos_f = positions.reshape(S, 1).astype(F32)
    h1_tm, route = _attn_mixer(h, pos_f, norm_mix[1], freq_lane, sign_lane, w_qkv_ext, b_qkv_ext,
                               attn_sinks[0], attn_w_o[0].astype(BF16), attn_b_o[0], norm_ffn[1], w_router_pad)

    texp, n_valid, src_rows, dst_rows = _moe_plan(route, S)
    y2_tm = _moe_experts(h1_tm, norm_ffn[1], texp, n_valid, src_rows, dst_rows,
                         moe_wg, moe_wu, moe_wd,
                         TOP_K * S + MOE_TILE)
    out = _combine(h1_tm, y2_tm, route, norm_final)
    return out.reshape(B, S, D)
```

```python
import jax
import jax.numpy as jnp
from jax import lax
from jax.experimental import pallas as pl
from jax.experimental.pallas import tpu as pltpu

F32 = jnp.float32
BF16 = jnp.bfloat16

D_MODEL = 1024
N_RNN_BLOCKS = 8
RNN_BLOCK = D_MODEL // N_RNN_BLOCKS
CONV_WIDTH = 4
LRU_C = 8.0
N_HEADS = 16
N_KV_HEADS = 4
HEAD_DIM = 64
GROUP = N_HEADS // N_KV_HEADS
WINDOW = 128
ATTN_BLOCK = 128
assert WINDOW == ATTN_BLOCK
ROPE_THETA = 500000.0
ROT_DIM = HEAD_DIM // 4
D_FF = 3584
N_EXPERTS = 8
TOP_K = 2
EPS = 1e-6
NEG_INF = -1e30
LOG2_E = 1.4426950408889634

LANES = 128
SUBLANES = 8
VMEM_LIMIT = 56 * 1024 * 1024

SEQ_TILE = 512
ATTN_TILE = 1024
LRU_CHUNKS = 4
FFN_TILE = 1024
FF_CHUNK = 512
N_FF_CHUNKS = D_FF // FF_CHUNK
MOE_TILE = 512
N_MOE_CHUNKS = 2
OUT_TILE = 1024


def _rmsnorm(x, g):
    return x * lax.rsqrt(jnp.mean(x * x, axis=-1, keepdims=True) + EPS) * g


ROWS_PER_TOKEN = D_MODEL // LANES


def _store_token_major(ref, val):
    T = val.shape[0]
    for c in range(ROWS_PER_TOKEN):
        ref[pl.ds(c, T, stride=ROWS_PER_TOKEN), :] = val[:, LANES * c:LANES * (c + 1)]


def _load_token_major(ref, T):
    return jnp.concatenate([ref[pl.ds(c, T, stride=ROWS_PER_TOKEN), :] for c in range(ROWS_PER_TOKEN)], axis=1)


def _const_spec(shape):
    n = len(shape)
    return pl.BlockSpec(shape, lambda *_: (0,) * n, pipeline_mode=pl.Buffered(1))


def _time_permutation(T):
    rho = jnp.arange(T)
    t_of_rho = (rho % SUBLANES) * (T // SUBLANES) + rho // SUBLANES
    perm = (t_of_rho[:, None] == jnp.arange(T)[None, :]).astype(BF16)
    return perm, perm.T


def _lru_kernel(xin_ref, xres_ref, g_ref, perm_ref, unperm_ref, win_ref, cw_ref, cb_ref, wax_ref, ba_ref, bx_ref,
                lam_ref, wout_ref, cast_g_ref, cast_u_ref, cast_d_ref,
                o_ref, bf_g_ref, bf_u_ref, bf_d_ref, proj_scr, tail_scr, carry_scr):
    i = pl.program_id(0)
    n_tiles = pl.num_programs(0) - 1
    T = xin_ref.shape[0]
    D = D_MODEL
    C = T // LRU_CHUNKS
    GC = C // SUBLANES
    G = T // SUBLANES
    HALO = (CONV_WIDTH - 1) * SUBLANES

    def cast_slices():
        bf_g_ref[...] = cast_g_ref[...].astype(BF16)
        bf_u_ref[...] = cast_u_ref[...].astype(BF16)
        bf_d_ref[...] = cast_d_ref[...].astype(BF16)

    @pl.when(i == 0)
    def _():
        tail_scr[...] = jnp.zeros_like(tail_scr)
        carry_scr[...] = jnp.zeros_like(carry_scr)

    PW = 2 * D // LRU_CHUNKS

    def project_chunks():
        hn = _rmsnorm(xin_ref[...], g_ref[...]).astype(BF16)
        hp = jnp.dot(perm_ref[...], hn, preferred_element_type=F32).astype(BF16)
        for k in range(LRU_CHUNKS):
            yield jnp.dot(hp, win_ref[:, PW * k:PW * (k + 1)], preferred_element_type=F32)

    def recur_chunks():
        sub = lax.broadcasted_iota(jnp.int32, (SUBLANES, D), 0)
        cw = cw_ref[...]
        softplus_neg_lam = jax.nn.softplus(-lam_ref[...])
        last = proj_scr[T - HALO:T, D:]
        prev_groups = []
        for k in range(CONV_WIDTH - 1):
            rows = slice(SUBLANES * k, SUBLANES * (k + 1))
            prev_groups.append(pltpu.roll(jnp.where(sub == SUBLANES - 1, tail_scr[rows, :], last[rows, :]), 1, 0))
        tail_scr[...] = last

        ys, hs, decay = [], [], []
        for k in range(LRU_CHUNKS):
            pc = proj_scr[C * k:C * (k + 1), :]
            ys.append(jax.nn.gelu(pc[:, :D], approximate=True))
            xb = pc[:, D:]
            xc = cb_ref[...] + xb * cw[CONV_WIDTH - 1:CONV_WIDTH, :]
            for back in range(1, CONV_WIDTH):
                shifted = jnp.concatenate(prev_groups[CONV_WIDTH - 1 - back:] + [xb[:C - SUBLANES * back, :]], axis=0)
                xc = xc + shifted * cw[CONV_WIDTH - 1 - back:CONV_WIDTH - back, :]
            prev_groups = [xb[C - HALO + SUBLANES * q:C - HALO + SUBLANES * (q + 1), :] for q in range(CONV_WIDTH - 1)]

            xcb = xc.astype(BF16)
            r_parts, i_parts = [], []
            for n in range(N_RNN_BLOCKS):
                gn = jnp.dot(xcb[:, RNN_BLOCK * n:RNN_BLOCK * (n + 1)], wax_ref[n], preferred_element_type=F32)
                r_parts.append(gn[:, :RNN_BLOCK])
                i_parts.append(gn[:, RNN_BLOCK:])
            r = jax.nn.sigmoid(jnp.concatenate(r_parts, axis=1) + ba_ref[...])
            ig = jax.nn.sigmoid(jnp.concatenate(i_parts, axis=1) + bx_ref[...])
            log_a = -LRU_C * r * softplus_neg_lam
            a = jnp.exp(log_a)
            z = jnp.tanh(-log_a) * (1.0 + a * a)
            b = jnp.where(z > 0.0, z * lax.rsqrt(z), 0.0) * (ig * xc)

            for j in range(GC):
                rows = slice(SUBLANES * j, SUBLANES * (j + 1))
                if hs:
                    hs.append(a[rows, :] * hs[-1] + b[rows, :])
                    decay.append(a[rows, :] * decay[-1])
                else:
                    hs.append(b[rows, :])
                    decay.append(a[rows, :])
            yield None

        p_inc, e_inc = decay[-1], hs[-1]
        for k in (1, 2, 4):
            m = sub >= k
            p_sh = jnp.where(m, pltpu.roll(p_inc, k, 0), 1.0)
            e_sh = jnp.where(m, pltpu.roll(e_inc, k, 0), 0.0)
            e_inc = e_inc + p_inc * e_sh
            p_inc = p_inc * p_sh
        h0 = carry_scr[...]
        after = p_inc * h0 + e_inc
        carry_scr[...] = after[SUBLANES - 1:SUBLANES, :]
        enter = jnp.where(sub == 0, h0, pltpu.roll(after, 1, 0))
        h = jnp.concatenate([hs[j] + decay[j] * enter for j in range(G)], axis=0)
        hy = (h * jnp.concatenate(ys, axis=0)).astype(BF16)
        hy = jnp.dot(unperm_ref[...], hy, preferred_element_type=F32).astype(BF16)
        yield jnp.dot(hy, wout_ref[...], preferred_element_type=F32) + xres_ref[...]

    @pl.when(i == 0)
    def _():
        cast_slices()
        for k, pc in enumerate(project_chunks()):
            proj_scr[:, PW * k:PW * (k + 1)] = pc

    @pl.when((i > 0) & (i < n_tiles))
    def _():
        cast_slices()
        rec = recur_chunks()
        new_proj = []
        for pc in project_chunks():
            new_proj.append(pc)
            next(rec)
        o_ref[...] = next(rec)
        for k, pc in enumerate(new_proj):
            proj_scr[:, PW * k:PW * (k + 1)] = pc

    @pl.when(i == n_tiles)
    def _():
        o_ref[...] = list(recur_chunks())[-1]


def _lru_mixer(x, g, w_in, conv_w, conv_b, w_ax, b_a, b_x, lam, w_out, cast_gate, cast_up, cast_down):
    S, D = x.shape
    T = SEQ_TILE
    n_tiles = S // T
    row = lambda v: v.reshape(1, D)
    perm, unperm = _time_permutation(T)

    def cast_spec(w):
        return pl.BlockSpec((w.shape[0] // n_tiles, w.shape[1]), lambda i: (jnp.minimum(i, n_tiles - 1), 0))

    casts = (cast_gate, cast_up, cast_down)
    return pl.pallas_call(
        _lru_kernel,
        grid=(n_tiles + 1,),
        in_specs=[
            pl.BlockSpec((T, D), lambda i: (jnp.minimum(i, n_tiles - 1), 0)),
            pl.BlockSpec((T, D), lambda i: (jnp.maximum(i - 1, 0), 0)),
            _const_spec((1, D)),
            _const_spec((T, T)),
            _const_spec((T, T)),
            _const_spec((D, 2 * D)),
            _const_spec((CONV_WIDTH, D)),
            _const_spec((1, D)),
            _const_spec((N_RNN_BLOCKS, RNN_BLOCK, 2 * RNN_BLOCK)),
            _const_spec((1, D)),
            _const_spec((1, D)),
            _const_spec((1, D)),
            _const_spec((D, D)),
        ] + [cast_spec(w) for w in casts],
        out_specs=[pl.BlockSpec((T, D), lambda i: (jnp.maximum(i - 1, 0), 0))] + [cast_spec(w) for w in casts],
        out_shape=[jax.ShapeDtypeStruct((S, D), F32)] + [jax.ShapeDtypeStruct(w.shape, BF16) for w in casts],
        scratch_shapes=[pltpu.VMEM((T, 2 * D), F32), pltpu.VMEM(((CONV_WIDTH - 1) * SUBLANES, D), F32),
                        pltpu.VMEM((1, D), F32)],
        compiler_params=pltpu.CompilerParams(dimension_semantics=("arbitrary",), vmem_limit_bytes=VMEM_LIMIT),
        name="lru_mixer",
    )(x, x, row(g), perm, unperm, w_in, conv_w, row(conv_b), w_ax, row(b_a), row(b_x), row(lam), w_out, *casts)


def _ffn_kernel(x_ref, g_ref, wg_ref, wu_ref, wd_ref, cast_g_ref, cast_u_ref, cast_d_ref,
                o_ref, bf_g_ref, bf_u_ref, bf_d_ref, xn_scr, acc_scr):
    c = pl.program_id(1)
    last = pl.num_programs(1) - 1

    def ff_chunk():
        bf_g_ref[...] = cast_g_ref[...].astype(BF16)
        bf_u_ref[...] = cast_u_ref[...].astype(BF16)
        bf_d_ref[...] = cast_d_ref[...].astype(BF16)
        xn = xn_scr[...]
        gate = jnp.dot(xn, wg_ref[...], preferred_element_type=F32)
        up = jnp.dot(xn, wu_ref[...], preferred_element_type=F32)
        mid = (jax.nn.silu(gate) * up).astype(BF16)
        return jnp.dot(mid, wd_ref[...], preferred_element_type=F32)

    @pl.when(c == 0)
    def _():
        xn_scr[...] = _rmsnorm(x_ref[...], g_ref[...]).astype(BF16)
        acc_scr[...] = ff_chunk()

    @pl.when((c > 0) & (c < last))
    def _():
        acc_scr[...] += ff_chunk()

    @pl.when(c == last)
    def _():
        o_ref[...] = x_ref[...] + (acc_scr[...] + ff_chunk())


def _dense_ffn(x, g, w_gate, w_up, w_down, cast_gate, cast_up, cast_down):
    S, D = x.shape
    T = FFN_TILE
    n_steps = (S // T) * N_FF_CHUNKS
    E = cast_gate.shape[0]
    row_parts = n_steps // (E * N_FF_CHUNKS)
    assert row_parts * E * N_FF_CHUNKS == n_steps
    gu_block = (1, D // row_parts, D_FF // N_FF_CHUNKS)
    d_block = (1, D_FF // (row_parts * N_FF_CHUNKS), D)

    def gu_map(i, c):
        s = i * N_FF_CHUNKS + c
        return (s // (row_parts * N_FF_CHUNKS), (s // N_FF_CHUNKS) % row_parts, c)

    def d_map(i, c):
        s = i * N_FF_CHUNKS + c
        return (s // (row_parts * N_FF_CHUNKS), s % (row_parts * N_FF_CHUNKS), 0)

    return pl.pallas_call(
        _ffn_kernel,
        grid=(S // T, N_FF_CHUNKS),
        in_specs=[
            pl.BlockSpec((T, D), lambda i, c: (i, 0)),
            pl.BlockSpec((1, D), lambda i, c: (0, 0)),
            pl.BlockSpec((D, FF_CHUNK), lambda i, c: (0, c)),
            pl.BlockSpec((D, FF_CHUNK), lambda i, c: (0, c)),
            pl.BlockSpec((FF_CHUNK, D), lambda i, c: (c, 0)),
            pl.BlockSpec(gu_block, gu_map),
            pl.BlockSpec(gu_block, gu_map),
            pl.BlockSpec(d_block, d_map),
        ],
        out_specs=[pl.BlockSpec((T, D), lambda i, c: (i, 0)), pl.BlockSpec(gu_block, gu_map),
                   pl.BlockSpec(gu_block, gu_map), pl.BlockSpec(d_block, d_map)],
        out_shape=[jax.ShapeDtypeStruct((S, D), F32), jax.ShapeDtypeStruct(cast_gate.shape, BF16),
                   jax.ShapeDtypeStruct(cast_up.shape, BF16), jax.ShapeDtypeStruct(cast_down.shape, BF16)],
        scratch_shapes=[pltpu.VMEM((T, D), BF16), pltpu.VMEM((T, D), F32)],
        compiler_params=pltpu.CompilerParams(dimension_semantics=("arbitrary", "arbitrary"),
                                             vmem_limit_bytes=VMEM_LIMIT),
        name="dense_ffn",
    )(x, g.reshape(1, D), w_gate, w_up, w_down, cast_gate, cast_up, cast_down)


KV_EXT = N_KV_HEADS * LANES
QKV_EXT = N_HEADS * HEAD_DIM + 2 * KV_EXT


def _attn_kernel(sink_ref, x_ref, pos_ref, g_ref, freq_ref, sign_ref, wqkv_ref, bqkv_ref, wo_ref, bo_ref,
                 g2_ref, wr_ref, o_ref, route_ref,
                 qlo_scr, qhi_scr, k_scr, v_scr, o_scr):
    T = x_ref.shape[0]
    D = D_MODEL
    B = ATTN_BLOCK
    first_tile = pl.program_id(0) == 0

    @pl.when(first_tile)
    def _():
        k_scr[0:B, :] = jnp.zeros((B, KV_EXT), BF16)
        v_scr[0:B, :] = jnp.zeros((B, KV_EXT), BF16)

    x = x_ref[...]
    hn = _rmsnorm(x, g_ref[...]).astype(BF16)
    qkv = jnp.dot(hn, wqkv_ref[...], preferred_element_type=F32) + bqkv_ref[...]

    ang = pos_ref[...] * freq_ref[...]
    cos_t = jnp.cos(ang)
    sin_t = jnp.sin(ang) * sign_ref[...]
    lane = lax.broadcasted_iota(jnp.int32, (T, LANES), 1)
    first_half = (lane % HEAD_DIM) < (ROT_DIM // 2)
    lo_half = lane < HEAD_DIM

    def rope(col):
        partner = jnp.where(first_half, pltpu.roll(col, LANES - ROT_DIM // 2, 1), pltpu.roll(col, ROT_DIM // 2, 1))
        return col * cos_t + partner * sin_t

    scale = HEAD_DIM ** -0.5 * LOG2_E
    for c in range(D // LANES):
        qc = rope(qkv[:, LANES * c:LANES * (c + 1)]) * scale
        qlo_scr[:, LANES * c:LANES * (c + 1)] = jnp.where(lo_half, qc, 0.0).astype(BF16)
        qhi_scr[:, LANES * c:LANES * (c + 1)] = jnp.where(lo_half, 0.0, qc).astype(BF16)
    for g in range(N_KV_HEADS):
        kc = rope(qkv[:, D + LANES * g:D + LANES * (g + 1)])
        k_scr[B:B + T, LANES * g:LANES * (g + 1)] = kc.astype(BF16)
        vc = qkv[:, D + KV_EXT + LANES * g:D + KV_EXT + LANES * (g + 1)]
        v_scr[B:B + T, LANES * g:LANES * (g + 1)] = vc.astype(BF16)

    qi = lax.broadcasted_iota(jnp.int32, (B, B), 0)
    kj = lax.broadcasted_iota(jnp.int32, (B, B), 1)
    causal_own = kj <= qi
    window_prev = kj > qi
    key0 = kj == 0
    key0_row = lax.broadcasted_iota(jnp.int32, (1, B), 1) == 0
    lo_blk = lax.broadcasted_iota(jnp.int32, (B, LANES), 1) < HEAD_DIM

    def block_body(blk, _):
        r0 = pl.multiple_of(blk * B, B)
        k_lo = jnp.where(first_tile & (blk == 0), B, 0)
        allowed_prev = window_prev & (kj >= k_lo)
        for g in range(N_KV_HEADS):
            kk = k_scr[pl.ds(r0, 2 * B), LANES * g:LANES * (g + 1)]
            vv = v_scr[pl.ds(r0, 2 * B), LANES * g:LANES * (g + 1)]
            cols = [slice(LANES * c, LANES * (c + 1)) for c in range((GROUP // 2) * g, (GROUP // 2) * (g + 1))]
            q_all = jnp.concatenate([q_scr[pl.ds(r0, B), col] for col in cols for q_scr in (qlo_scr, qhi_scr)], axis=0)
            s_all = lax.dot_general(q_all, kk, (((1,), (1,)), ((), ())), preferred_element_type=F32)
            p_all, denoms = [], []
            for hh in range(GROUP):
                rows = slice(B * hh, B * (hh + 1))
                sink_fill = jnp.where(key0_row, sink_ref[GROUP * g + hh] * LOG2_E, NEG_INF)
                s_prev = jnp.where(allowed_prev, s_all[rows, :B], sink_fill)
                s_own = jnp.where(causal_own, s_all[rows, B:], NEG_INF)
                m = jnp.max(jnp.maximum(s_prev, s_own), axis=-1, keepdims=True)
                p_prev = jnp.exp2(s_prev - m)
                p_own = jnp.exp2(s_own - m)
                denoms.append(jnp.sum(p_prev + p_own, axis=-1, keepdims=True))
                p_all.append(jnp.concatenate([jnp.where(key0, 0.0, p_prev), p_own], axis=1).astype(BF16))
            o_all = jnp.dot(jnp.concatenate(p_all, axis=0), vv, preferred_element_type=F32)
            o_heads = [o_all[B * hh:B * (hh + 1), :] / denoms[hh] for hh in range(GROUP)]
            for p, col in enumerate(cols):
                o_scr[pl.ds(r0, B), col] = jnp.where(lo_blk, o_heads[2 * p], o_heads[2 * p + 1]).astype(BF16)
        return 0

    lax.fori_loop(0, T // B, block_body, 0)

    k_scr[0:B, :] = k_scr[T:T + B, :]
    v_scr[0:B, :] = v_scr[T:T + B, :]

    h1 = jnp.dot(o_scr[...], wo_ref[...], preferred_element_type=F32) + bo_ref[...] + x
    _store_token_major(o_ref, h1)

    hn2 = _rmsnorm(h1, g2_ref[...]).astype(BF16)
    logits = jnp.dot(hn2, wr_ref[...], preferred_element_type=F32)
    lane_f = lane.astype(F32)
    lg = jnp.where(lane < N_EXPERTS, logits, -jnp.inf)
    v1 = jnp.max(lg, axis=-1, keepdims=True)
    i1 = jnp.min(jnp.where(lg == v1, lane_f, float(LANES)), axis=-1, keepdims=True)
    lg2 = jnp.where(lane_f == i1, -jnp.inf, lg)
    v2 = jnp.max(lg2, axis=-1, keepdims=True)
    i2 = jnp.min(jnp.where(lg2 == v2, lane_f, float(LANES)), axis=-1, keepdims=True)
    e2 = jnp.exp(v2 - v1)
    gate1 = 1.0 / (1.0 + e2)
    gate2 = e2 / (1.0 + e2)
    route_ref[...] = jnp.where(lane == 0, i1, jnp.where(lane == 1, i2,
                                                        jnp.where(lane == 2, gate1, jnp.where(lane == 3, gate2, 0.0))))


def _attn_mixer(x, pos_f, g, freq_lane, sign_lane, w_qkv_ext, b_qkv_ext, sinks, w_o, b_o, g2, w_router_pad):
    S, D = x.shape
    T = ATTN_TILE
    B = ATTN_BLOCK
    grid_spec = pltpu.PrefetchScalarGridSpec(
        num_scalar_prefetch=1,
        grid=(S // T,),
        in_specs=[
            pl.BlockSpec((T, D), lambda i, s: (i, 0)),
            pl.BlockSpec((T, 1), lambda i, s: (i, 0)),
            _const_spec((1, D)),
            _const_spec((1, LANES)),
            _const_spec((1, LANES)),
            _const_spec((D, QKV_EXT)),
            _const_spec((1, QKV_EXT)),
            _const_spec((D, D)),
            _const_spec((1, D)),
            _const_spec((1, D)),
            _const_spec((D, LANES)),
        ],
        out_specs=[pl.BlockSpec((T * ROWS_PER_TOKEN, LANES), lambda i, s: (i, 0)),
                   pl.BlockSpec((T, LANES), lambda i, s: (i, 0))],
        scratch_shapes=[
            pltpu.VMEM((T, D), BF16), pltpu.VMEM((T, D), BF16),
            pltpu.VMEM((T + B, KV_EXT), BF16), pltpu.VMEM((T + B, KV_EXT), BF16),
            pltpu.VMEM((T, D), BF16),
        ],
    )
    return pl.pallas_call(
        _attn_kernel,
        grid_spec=grid_spec,
        out_shape=[jax.ShapeDtypeStruct((S * ROWS_PER_TOKEN, LANES), F32), jax.ShapeDtypeStruct((S, LANES), F32)],
        compiler_params=pltpu.CompilerParams(dimension_semantics=("arbitrary",), vmem_limit_bytes=VMEM_LIMIT),
        name="attn_mixer",
    )(sinks, x, pos_f, g.reshape(1, D), freq_lane, sign_lane, w_qkv_ext, b_qkv_ext, w_o, b_o.reshape(1, D),
      g2.reshape(1, D), w_router_pad)


def _moe_kernel(texp_ref, nvt_ref, src0_ref, srcn_ref, dstp_ref, dstl_ref, h_hbm, g_ref, wg_ref, wu_ref, wd_ref,
                y_hbm, x_scr, xn_scr, acc_scr, y_scr, sems):
    i = pl.program_id(0)
    c = pl.program_id(1)
    n_tiles = pl.num_programs(0)
    n_chunks = pl.num_programs(1)
    T = xn_scr.shape[0]
    R = ROWS_PER_TOKEN
    valid = i < nvt_ref[0]
    slot = i % 2
    other = 1 - slot
    SCATTER_SEM = 2

    def row_copy(src, src_row, dst, dst_row, sem):
        return pltpu.make_async_copy(src.at[pl.ds(pl.multiple_of(src_row, R), R), :],
                                     dst.at[pl.ds(pl.multiple_of(dst_row, R), R), :], sem)

    def tile_rows(t):
        return nvt_ref[1 + t]

    def window_offset(t):
        return nvt_ref[1 + 2 * n_tiles + t]

    prev_tile = jnp.maximum(i - 1, 0)
    next_tile = jnp.minimum(i + 1, n_tiles - 1)
    rows_prev = jnp.where(i > 0, tile_rows(prev_tile), 0)
    pad_base = y_hbm.shape[0] - T * R

    def dest_row(window_ref, offset, r, n_rows):
        return jnp.where(r < n_rows, window_ref[0, offset + r], pad_base + r * R)

    def wait_gather(s):
        pltpu.make_async_copy(h_hbm.at[pl.ds(0, T * R), :], x_scr.at[s], sems.at[s]).wait()

    def wait_scatter():
        pltpu.make_async_copy(y_scr, y_hbm.at[pl.ds(0, T * R), :], sems.at[SCATTER_SEM]).wait()

    @pl.when((i == 0) & (c == 0))
    def _():
        y_scr[...] = jnp.zeros_like(y_scr)
        acc_scr[...] = jnp.zeros_like(acc_scr)

        def issue(r, carry):
            row_copy(h_hbm, src0_ref[0, window_offset(0) + r], x_scr.at[0], r * R, sems.at[0]).start()
            return carry
        lax.fori_loop(0, T, issue, 0)

    @pl.when(c == 0)
    def _():
        wait_gather(slot)

    def row_dmas():
        off_next = window_offset(next_tile)
        off_prev = window_offset(prev_tile)
        for r in range(T):
            row_copy(h_hbm, srcn_ref[0, off_next + r], x_scr.at[other], r * R, sems.at[other]).start()
            row_copy(y_scr, r * R, y_hbm, dest_row(dstp_ref, off_prev, r, rows_prev),
                     sems.at[SCATTER_SEM]).start()

    def expert_chunk(first, rows):
        xn = xn_scr[0:rows, :]
        gate = jnp.dot(xn, wg_ref[0], preferred_element_type=F32)
        up = jnp.dot(xn, wu_ref[0], preferred_element_type=F32)
        mid = (jax.nn.silu(gate) * up).astype(BF16)
        down = jnp.dot(mid, wd_ref[0], preferred_element_type=F32)
        if first:
            acc_scr[0:rows, :] = down
        else:
            acc_scr[0:rows, :] += down

    few_rows = tile_rows(i) <= T // 2

    for few, rows in ((False, T), (True, T // 2)):
        branch = valid & (few_rows if few else jnp.logical_not(few_rows))

        @pl.when(branch & (c == 0))
        def _():
            xn_scr[0:rows, :] = _rmsnorm(_load_token_major(x_scr.at[slot], rows), g_ref[...]).astype(BF16)
            row_dmas()
            expert_chunk(True, rows)

        @pl.when(branch & (c != 0))
        def _():
            expert_chunk(False, rows)

    @pl.when(jnp.logical_not(valid) & (c == 0))
    def _():
        row_dmas()

    @pl.when(c == n_chunks - 1)
    def _():
        wait_scatter()

    @pl.when(valid & (c == n_chunks - 1))
    def _():
        _store_token_major(y_scr, acc_scr[...])

    @pl.when((i == n_tiles - 1) & (c == n_chunks - 1))
    def _():
        wait_gather(other)


        def issue(r, carry):
            row_copy(y_scr, r * R, y_hbm, dest_row(dstl_ref, window_offset(n_tiles - 1), r, tile_rows(n_tiles - 1)),
                     sems.at[SCATTER_SEM]).start()
            return carry
        lax.fori_loop(0, T, issue, 0)
        wait_scatter()


def _moe_experts(h1_tm, g, tile_expert, n_valid_tiles, src_rows, dst_rows, w_gate, w_up, w_down, n_out_tokens):
    D = D_MODEL
    T = MOE_TILE
    NT = tile_expert.shape[0]
    NC = N_MOE_CHUNKS
    FC = D_FF // NC

    def chunk(i, c, nvt):
        return jnp.where(i < nvt[0], c, NC - 1)

    def window_spec(tile_of_step):
        return pl.BlockSpec((pl.Element(1), pl.Element(T + LANES)),
                            lambda i, c, te, nvt: (0, nvt[1 + NT + tile_of_step(i)] * LANES),
                            memory_space=pltpu.SMEM)

    grid_spec = pltpu.PrefetchScalarGridSpec(
        num_scalar_prefetch=2,
        grid=(NT, NC),
        in_specs=[
            window_spec(lambda i: 0),
            window_spec(lambda i: jnp.minimum(i + 1, NT - 1)),
            window_spec(lambda i: jnp.maximum(i - 1, 0)),
            window_spec(lambda i: NT - 1),
            pl.BlockSpec(memory_space=pl.ANY),
            pl.BlockSpec((1, D), lambda i, c, te, nvt: (0, 0)),
            pl.BlockSpec((1, D, FC), lambda i, c, te, nvt: (te[i], 0, chunk(i, c, nvt))),
            pl.BlockSpec((1, D, FC), lambda i, c, te, nvt: (te[i], 0, chunk(i, c, nvt))),
            pl.BlockSpec((1, FC, D), lambda i, c, te, nvt: (te[i], chunk(i, c, nvt), 0)),
        ],
        out_specs=pl.BlockSpec(memory_space=pl.ANY),
        scratch_shapes=[pltpu.VMEM((2, T * ROWS_PER_TOKEN, LANES), F32), pltpu.VMEM((T, D), BF16),
                        pltpu.VMEM((T, D), F32), pltpu.VMEM((T * ROWS_PER_TOKEN, LANES), F32),
                        pltpu.SemaphoreType.DMA((3,))],
    )
    return pl.pallas_call(
        _moe_kernel,
        grid_spec=grid_spec,
        out_shape=jax.ShapeDtypeStruct((n_out_tokens * ROWS_PER_TOKEN, LANES), F32),
        compiler_params=pltpu.CompilerParams(dimension_semantics=("arbitrary", "arbitrary"),
                                             vmem_limit_bytes=VMEM_LIMIT),
        name="moe_experts",
    )(tile_expert, n_valid_tiles, src_rows, src_rows, dst_rows, dst_rows, h1_tm, g.reshape(1, D),
      w_gate, w_up, w_down)


def _moe_plan(route, n_tokens):
    T = MOE_TILE
    n_assign = TOP_K * n_tokens
    n_tiles = n_assign // T + N_EXPERTS
    e_flat = route[:, 0:TOP_K].astype(jnp.int32).reshape(-1)
    counts = jnp.sum((e_flat[:, None] == jnp.arange(N_EXPERTS, dtype=jnp.int32)[None, :]).astype(jnp.int32), axis=0)
    order = jnp.argsort(e_flat, stable=True).astype(jnp.int32)
    tiles_per = (counts + T - 1) // T
    tile_end = jnp.cumsum(tiles_per)
    n_valid = tile_end[-1]
    tile_ids = jnp.arange(n_tiles, dtype=jnp.int32)
    last_valid = jnp.maximum(n_valid - 1, 0)
    texp = jnp.sum((jnp.minimum(tile_ids, last_valid)[:, None] >= tile_end[None, :]).astype(jnp.int32), axis=1)
    texp = jnp.minimum(texp, N_EXPERTS - 1)
    onehot = (texp[:, None] == jnp.arange(N_EXPERTS, dtype=jnp.int32)[None, :]).astype(jnp.int32)
    pick = lambda v: jnp.sum(onehot * v[None, :], axis=1)
    j = tile_ids - pick(tile_end - tiles_per)
    start = jnp.where(tile_ids < n_valid, pick(jnp.cumsum(counts) - counts) + j * T, 0)
    n_rows = jnp.where(tile_ids < n_valid, jnp.clip(pick(counts) - j * T, 0, T), 0)
    token = order // TOP_K
    slot = order % TOP_K
    spare = jnp.zeros((T + LANES,), jnp.int32)
    src_sorted = jnp.concatenate([token * ROWS_PER_TOKEN, spare]).reshape(1, -1)
    dst_sorted = jnp.concatenate([(slot * n_tokens + token) * ROWS_PER_TOKEN, spare]).reshape(1, -1)
    tile_counts = jnp.concatenate([n_valid.reshape(1), n_rows, start // LANES, start % LANES]).astype(jnp.int32)
    return texp, tile_counts, src_sorted, dst_sorted


def _combine_kernel(h_ref, y0_ref, y1_ref, route_ref, g_ref, o_ref):
    T = o_ref.shape[0]
    route = route_ref[...]
    moe = route[:, 2:3] * _load_token_major(y0_ref, T) + route[:, 3:4] * _load_token_major(y1_ref, T)
    o_ref[...] = _rmsnorm(_load_token_major(h_ref, T) + moe, g_ref[...])


def _combine(h1_tm, y2_tm, route, g):
    D = D_MODEL
    S = route.shape[0]
    T = OUT_TILE
    nb = S // T
    tm_block = (T * ROWS_PER_TOKEN, LANES)
    return pl.pallas_call(
        _combine_kernel,
        grid=(nb,),
        in_specs=[
            pl.BlockSpec(tm_block, lambda i: (i, 0)),
            pl.BlockSpec(tm_block, lambda i: (i, 0)),
            pl.BlockSpec(tm_block, lambda i: (i + nb, 0)),
            pl.BlockSpec((T, LANES), lambda i: (i, 0)),
            pl.BlockSpec((1, D), lambda i: (0, 0)),
        ],
        out_specs=pl.BlockSpec((T, D), lambda i: (i, 0)),
        out_shape=jax.ShapeDtypeStruct((S, D), F32),
        compiler_params=pltpu.CompilerParams(dimension_semantics=("arbitrary",), vmem_limit_bytes=VMEM_LIMIT),
        name="moe_combine",
    )(h1_tm, y2_tm, y2_tm, route, g.reshape(1, D))


def kernel(x, positions, norm_mix, norm_ffn, norm_final, lru_w_in, lru_conv_w, lru_conv_b, lru_w_a, lru_b_a, lru_w_x, lru_b_x, lru_lambda, lru_w_out, attn_w_qkv, attn_b_qkv, attn_sinks, attn_w_o, attn_b_o, ffn_w_gate, ffn_w_up, ffn_w_down, moe_w_router, moe_w_gate, moe_w_up, moe_w_down):
    B, S, D = x.shape
    assert B == 1 and D == D_MODEL and S % FFN_TILE == 0
    h = x.reshape(S, D)

    w_ax = jnp.concatenate([lru_w_a[0], lru_w_x[0]], axis=-1).astype(BF16)
    h, ffn_wg, ffn_wu, ffn_wd = _lru_mixer(h, norm_mix[0], lru_w_in[0].astype(BF16), lru_conv_w[0], lru_conv_b[0],
                                           w_ax, lru_b_a[0], lru_b_x[0], lru_lambda[0], lru_w_out[0].astype(BF16),
                                           ffn_w_gate[0], ffn_w_up[0], ffn_w_down[0])
    h, moe_wg, moe_wu, moe_wd = _dense_ffn(h, norm_ffn[0], ffn_wg, ffn_wu, ffn_wd,
                                           moe_w_gate[0], moe_w_up[0], moe_w_down[0])

    q_dim = N_HEADS * HEAD_DIM
    kv_dim = N_KV_HEADS * HEAD_DIM

    def dup_heads(w):
        w4 = w.reshape(w.shape[:-1] + (N_KV_HEADS, 1, HEAD_DIM))
        return jnp.broadcast_to(w4, w.shape[:-1] + (N_KV_HEADS, 2, HEAD_DIM)).reshape(w.shape[:-1] + (KV_EXT,))

    wqkv, bqkv = attn_w_qkv[0], attn_b_qkv[0]
    w_qkv_ext = jnp.concatenate([wqkv[:, :q_dim], dup_heads(wqkv[:, q_dim:q_dim + kv_dim]),
                                 dup_heads(wqkv[:, q_dim + kv_dim:])], axis=1).astype(BF16)
    b_qkv_ext = jnp.concatenate([bqkv[:q_dim], dup_heads(bqkv[q_dim:q_dim + kv_dim]),
                                 dup_heads(bqkv[q_dim + kv_dim:])]).reshape(1, QKV_EXT)
    inv_freq = ROPE_THETA ** (-jnp.arange(0, ROT_DIM, 2, dtype=F32) / ROT_DIM)
    d_in_head = jnp.arange(LANES) % HEAD_DIM
    freq_lane = jnp.where(d_in_head < ROT_DIM, inv_freq[d_in_head % (ROT_DIM // 2)], 0.0).reshape(1, LANES)
    sign_lane = jnp.where(d_in_head < ROT_DIM // 2, -1.0, jnp.where(d_in_head < ROT_DIM, 1.0, 0.0))
    sign_lane = sign_lane.astype(F32).reshape(1, LANES)
    w_router_pad = jnp.pad(moe_w_router[0], ((0, 0), (0, LANES - N_EXPERTS))).astype(BF16)
    pos_f = positions.reshape(S, 1).astype(F32)
    h1_tm, route = _attn_mixer(h, pos_f, norm_mix[1], freq_lane, sign_lane, w_qkv_ext, b_qkv_ext,
                               attn_sinks[0], attn_w_o[0].astype(BF16), attn_b_o[0], norm_ffn[1], w_router_pad)

    texp, n_valid, src_rows, dst_rows = _moe_plan(route, S)
    y2_tm = _moe_experts(h1_tm, norm_ffn[1], texp, n_valid, src_rows, dst_rows,
                         moe_wg, moe_wu, moe_wd,
                         TOP_K * S + MOE_TILE)
    out = _combine(h1_tm, y2_tm, route, norm_final)
    return out.reshape(B, S, D)
```

```python
import jax
import jax.numpy as jnp
from jax import lax
from jax.experimental import pallas as pl
from jax.experimental.pallas import tpu as pltpu

F32 = jnp.float32
BF16 = jnp.bfloat16

D_MODEL = 1024
N_RNN_BLOCKS = 8
RNN_BLOCK = D_MODEL // N_RNN_BLOCKS
CONV_WIDTH = 4
LRU_C = 8.0
N_HEADS = 16
N_KV_HEADS = 4
HEAD_DIM = 64
GROUP = N_HEADS // N_KV_HEADS
WINDOW = 128
ATTN_BLOCK = 128
assert WINDOW == ATTN_BLOCK
ROPE_THETA = 500000.0
ROT_DIM = HEAD_DIM // 4
D_FF = 3584
N_EXPERTS = 8
TOP_K = 2
EPS = 1e-6
NEG_INF = -1e30
LOG2_E = 1.4426950408889634

LANES = 128
SUBLANES = 8
VMEM_LIMIT = 56 * 1024 * 1024

SEQ_TILE = 256
ATTN_TILE = 1024
LRU_CHUNKS = 4
FFN_TILE = 1024
FF_CHUNK = 512
N_FF_CHUNKS = D_FF // FF_CHUNK
MOE_TILE = 512
N_MOE_CHUNKS = 2
OUT_TILE = 1024


def _rmsnorm(x, g):
    return x * lax.rsqrt(jnp.mean(x * x, axis=-1, keepdims=True) + EPS) * g


ROWS_PER_TOKEN = D_MODEL // LANES


def _store_token_major(ref, val):
    T = val.shape[0]
    for c in range(ROWS_PER_TOKEN):
        ref[pl.ds(c, T, stride=ROWS_PER_TOKEN), :] = val[:, LANES * c:LANES * (c + 1)]


def _load_token_major(ref, T):
    return jnp.concatenate([ref[pl.ds(c, T, stride=ROWS_PER_TOKEN), :] for c in range(ROWS_PER_TOKEN)], axis=1)


def _const_spec(shape):
    n = len(shape)
    return pl.BlockSpec(shape, lambda *_: (0,) * n, pipeline_mode=pl.Buffered(1))


def _time_permutation(T):
    rho = jnp.arange(T)
    t_of_rho = (rho % SUBLANES) * (T // SUBLANES) + rho // SUBLANES
    perm = (t_of_rho[:, None] == jnp.arange(T)[None, :]).astype(BF16)
    return perm, perm.T


def _lru_kernel(xin_ref, xres_ref, g_ref, perm_ref, unperm_ref, win_ref, cw_ref, cb_ref, wax_ref, ba_ref, bx_ref,
                lam_ref, wout_ref, cast_g_ref, cast_u_ref, cast_d_ref,
                o_ref, bf_g_ref, bf_u_ref, bf_d_ref, proj_scr, tail_scr, carry_scr):
    i = pl.program_id(0)
    n_tiles = pl.num_programs(0) - 1
    T = xin_ref.shape[0]
    D = D_MODEL
    C = T // LRU_CHUNKS
    GC = C // SUBLANES
    G = T // SUBLANES
    HALO = (CONV_WIDTH - 1) * SUBLANES

    def cast_slices():
        bf_g_ref[...] = cast_g_ref[...].astype(BF16)
        bf_u_ref[...] = cast_u_ref[...].astype(BF16)
        bf_d_ref[...] = cast_d_ref[...].astype(BF16)

    @pl.when(i == 0)
    def _():
        tail_scr[...] = jnp.zeros_like(tail_scr)
        carry_scr[...] = jnp.zeros_like(carry_scr)

    PW = 2 * D // LRU_CHUNKS

    def project_chunks():
        hn = _rmsnorm(xin_ref[...], g_ref[...]).astype(BF16)
        hp = jnp.dot(perm_ref[...], hn, preferred_element_type=F32).astype(BF16)
        for k in range(LRU_CHUNKS):
            yield jnp.dot(hp, win_ref[:, PW * k:PW * (k + 1)], preferred_element_type=F32)

    def recur_chunks():
        sub = lax.broadcasted_iota(jnp.int32, (SUBLANES, D), 0)
        cw = cw_ref[...]
        softplus_neg_lam = jax.nn.softplus(-lam_ref[...])
        last = proj_scr[T - HALO:T, D:]
        prev_groups = []
        for k in range(CONV_WIDTH - 1):
            rows = slice(SUBLANES * k, SUBLANES * (k + 1))
            prev_groups.append(pltpu.roll(jnp.where(sub == SUBLANES - 1, tail_scr[rows, :], last[rows, :]), 1, 0))
        tail_scr[...] = last

        ys, hs, decay = [], [], []
        for k in range(LRU_CHUNKS):
            pc = proj_scr[C * k:C * (k + 1), :]
            ys.append(jax.nn.gelu(pc[:, :D], approximate=True))
            xb = pc[:, D:]
            xc = cb_ref[...] + xb * cw[CONV_WIDTH - 1:CONV_WIDTH, :]
            for back in range(1, CONV_WIDTH):
                shifted = jnp.concatenate(prev_groups[CONV_WIDTH - 1 - back:] + [xb[:C - SUBLANES * back, :]], axis=0)
                xc = xc + shifted * cw[CONV_WIDTH - 1 - back:CONV_WIDTH - back, :]
            prev_groups = [xb[C - HALO + SUBLANES * q:C - HALO + SUBLANES * (q + 1), :] for q in range(CONV_WIDTH - 1)]

            xcb = xc.astype(BF16)
            r_parts, i_parts = [], []
            for n in range(N_RNN_BLOCKS):
                gn = jnp.dot(xcb[:, RNN_BLOCK * n:RNN_BLOCK * (n + 1)], wax_ref[n], preferred_element_type=F32)
                r_parts.append(gn[:, :RNN_BLOCK])
                i_parts.append(gn[:, RNN_BLOCK:])
            r = jax.nn.sigmoid(jnp.concatenate(r_parts, axis=1) + ba_ref[...])
            ig = jax.nn.sigmoid(jnp.concatenate(i_parts, axis=1) + bx_ref[...])
            log_a = -LRU_C * r * softplus_neg_lam
            a = jnp.exp(log_a)
            z = jnp.tanh(-log_a) * (1.0 + a * a)
            b = jnp.where(z > 0.0, z * lax.rsqrt(z), 0.0) * (ig * xc)

            for j in range(GC):
                rows = slice(SUBLANES * j, SUBLANES * (j + 1))
                if hs:
                    hs.append(a[rows, :] * hs[-1] + b[rows, :])
                    decay.append(a[rows, :] * decay[-1])
                else:
                    hs.append(b[rows, :])
                    decay.append(a[rows, :])
            yield None

        p_inc, e_inc = decay[-1], hs[-1]
        for k in (1, 2, 4):
            m = sub >= k
            p_sh = jnp.where(m, pltpu.roll(p_inc, k, 0), 1.0)
            e_sh = jnp.where(m, pltpu.roll(e_inc, k, 0), 0.0)
            e_inc = e_inc + p_inc * e_sh
            p_inc = p_inc * p_sh
        h0 = carry_scr[...]
        after = p_inc * h0 + e_inc
        carry_scr[...] = after[SUBLANES - 1:SUBLANES, :]
        enter = jnp.where(sub == 0, h0, pltpu.roll(after, 1, 0))
        h = jnp.concatenate([hs[j] + decay[j] * enter for j in range(G)], axis=0)
        hy = (h * jnp.concatenate(ys, axis=0)).astype(BF16)
        hy = jnp.dot(unperm_ref[...], hy, preferred_element_type=F32).astype(BF16)
        yield jnp.dot(hy, wout_ref[...], preferred_element_type=F32) + xres_ref[...]

    @pl.when(i == 0)
    def _():
        cast_slices()
        for k, pc in enumerate(project_chunks()):
            proj_scr[:, PW * k:PW * (k + 1)] = pc

    @pl.when((i > 0) & (i < n_tiles))
    def _():
        cast_slices()
        rec = recur_chunks()
        new_proj = []
        for pc in project_chunks():
            new_proj.append(pc)
            next(rec)
        o_ref[...] = next(rec)
        for k, pc in enumerate(new_proj):
            proj_scr[:, PW * k:PW * (k + 1)] = pc

    @pl.when(i == n_tiles)
    def _():
        o_ref[...] = list(recur_chunks())[-1]


def _lru_mixer(x, g, w_in, conv_w, conv_b, w_ax, b_a, b_x, lam, w_out, cast_gate, cast_up, cast_down):
    S, D = x.shape
    T = SEQ_TILE
    n_tiles = S // T
    row = lambda v: v.reshape(1, D)
    perm, unperm = _time_permutation(T)

    def cast_spec(w):
        return pl.BlockSpec((w.shape[0] // n_tiles, w.shape[1]), lambda i: (jnp.minimum(i, n_tiles - 1), 0))

    casts = (cast_gate, cast_up, cast_down)
    return pl.pallas_call(
        _lru_kernel,
        grid=(n_tiles + 1,),
        in_specs=[
            pl.BlockSpec((T, D), lambda i: (jnp.minimum(i, n_tiles - 1), 0)),
            pl.BlockSpec((T, D), lambda i: (jnp.maximum(i - 1, 0), 0)),
            _const_spec((1, D)),
            _const_spec((T, T)),
            _const_spec((T, T)),
            _const_spec((D, 2 * D)),
            _const_spec((CONV_WIDTH, D)),
            _const_spec((1, D)),
            _const_spec((N_RNN_BLOCKS, RNN_BLOCK, 2 * RNN_BLOCK)),
            _const_spec((1, D)),
            _const_spec((1, D)),
            _const_spec((1, D)),
            _const_spec((D, D)),
        ] + [cast_spec(w) for w in casts],
        out_specs=[pl.BlockSpec((T, D), lambda i: (jnp.maximum(i - 1, 0), 0))] + [cast_spec(w) for w in casts],
        out_shape=[jax.ShapeDtypeStruct((S, D), F32)] + [jax.ShapeDtypeStruct(w.shape, BF16) for w in casts],
        scratch_shapes=[pltpu.VMEM((T, 2 * D), F32), pltpu.VMEM(((CONV_WIDTH - 1) * SUBLANES, D), F32),
                        pltpu.VMEM((1, D), F32)],
        compiler_params=pltpu.CompilerParams(dimension_semantics=("arbitrary",), vmem_limit_bytes=VMEM_LIMIT),
        name="lru_mixer",
    )(x, x, row(g), perm, unperm, w_in, conv_w, row(conv_b), w_ax, row(b_a), row(b_x), row(lam), w_out, *casts)


def _ffn_kernel(x_ref, g_ref, wg_ref, wu_ref, wd_ref, cast_g_ref, cast_u_ref, cast_d_ref,
                o_ref, bf_g_ref, bf_u_ref, bf_d_ref, xn_scr, acc_scr):
    c = pl.program_id(1)
    last = pl.num_programs(1) - 1

    def ff_chunk():
        bf_g_ref[...] = cast_g_ref[...].astype(BF16)
        bf_u_ref[...] = cast_u_ref[...].astype(BF16)
        bf_d_ref[...] = cast_d_ref[...].astype(BF16)
        xn = xn_scr[...]
        gate = jnp.dot(xn, wg_ref[...], preferred_element_type=F32)
        up = jnp.dot(xn, wu_ref[...], preferred_element_type=F32)
        mid = (jax.nn.silu(gate) * up).astype(BF16)
        return jnp.dot(mid, wd_ref[...], preferred_element_type=F32)

    @pl.when(c == 0)
    def _():
        xn_scr[...] = _rmsnorm(x_ref[...], g_ref[...]).astype(BF16)
        acc_scr[...] = ff_chunk()

    @pl.when((c > 0) & (c < last))
    def _():
        acc_scr[...] += ff_chunk()

    @pl.when(c == last)
    def _():
        o_ref[...] = x_ref[...] + (acc_scr[...] + ff_chunk())


def _dense_ffn(x, g, w_gate, w_up, w_down, cast_gate, cast_up, cast_down):
    S, D = x.shape
    T = FFN_TILE
    n_steps = (S // T) * N_FF_CHUNKS
    E = cast_gate.shape[0]
    row_parts = n_steps // (E * N_FF_CHUNKS)
    assert row_parts * E * N_FF_CHUNKS == n_steps
    gu_block = (1, D // row_parts, D_FF // N_FF_CHUNKS)
    d_block = (1, D_FF // (row_parts * N_FF_CHUNKS), D)

    def gu_map(i, c):
        s = i * N_FF_CHUNKS + c
        return (s // (row_parts * N_FF_CHUNKS), (s // N_FF_CHUNKS) % row_parts, c)

    def d_map(i, c):
        s = i * N_FF_CHUNKS + c
        return (s // (row_parts * N_FF_CHUNKS), s % (row_parts * N_FF_CHUNKS), 0)

    return pl.pallas_call(
        _ffn_kernel,
        grid=(S // T, N_FF_CHUNKS),
        in_specs=[
            pl.BlockSpec((T, D), lambda i, c: (i, 0)),
            pl.BlockSpec((1, D), lambda i, c: (0, 0)),
            pl.BlockSpec((D, FF_CHUNK), lambda i, c: (0, c)),
            pl.BlockSpec((D, FF_CHUNK), lambda i, c: (0, c)),
            pl.BlockSpec((FF_CHUNK, D), lambda i, c: (c, 0)),
            pl.BlockSpec(gu_block, gu_map),
            pl.BlockSpec(gu_block, gu_map),
            pl.BlockSpec(d_block, d_map),
        ],
        out_specs=[pl.BlockSpec((T, D), lambda i, c: (i, 0)), pl.BlockSpec(gu_block, gu_map),
                   pl.BlockSpec(gu_block, gu_map), pl.BlockSpec(d_block, d_map)],
        out_shape=[jax.ShapeDtypeStruct((S, D), F32), jax.ShapeDtypeStruct(cast_gate.shape, BF16),
                   jax.ShapeDtypeStruct(cast_up.shape, BF16), jax.ShapeDtypeStruct(cast_down.shape, BF16)],
        scratch_shapes=[pltpu.VMEM((T, D), BF16), pltpu.VMEM((T, D), F32)],
        compiler_params=pltpu.CompilerParams(dimension_semantics=("arbitrary", "arbitrary"),
                                             vmem_limit_bytes=VMEM_LIMIT),
        name="dense_ffn",
    )(x, g.reshape(1, D), w_gate, w_up, w_down, cast_gate, cast_up, cast_down)


KV_EXT = N_KV_HEADS * LANES
QKV_EXT = N_HEADS * HEAD_DIM + 2 * KV_EXT


def _attn_kernel(sink_ref, x_ref, pos_ref, g_ref, freq_ref, sign_ref, wqkv_ref, bqkv_ref, wo_ref, bo_ref,
                 g2_ref, wr_ref, o_ref, route_ref,
                 qlo_scr, qhi_scr, k_scr, v_scr, o_scr):
    T = x_ref.shape[0]
    D = D_MODEL
    B = ATTN_BLOCK
    first_tile = pl.program_id(0) == 0

    @pl.when(first_tile)
    def _():
        k_scr[0:B, :] = jnp.zeros((B, KV_EXT), BF16)
        v_scr[0:B, :] = jnp.zeros((B, KV_EXT), BF16)

    x = x_ref[...]
    hn = _rmsnorm(x, g_ref[...]).astype(BF16)
    qkv = jnp.dot(hn, wqkv_ref[...], preferred_element_type=F32) + bqkv_ref[...]

    ang = pos_ref[...] * freq_ref[...]
    cos_t = jnp.cos(ang)
    sin_t = jnp.sin(ang) * sign_ref[...]
    lane = lax.broadcasted_iota(jnp.int32, (T, LANES), 1)
    first_half = (lane % HEAD_DIM) < (ROT_DIM // 2)
    lo_half = lane < HEAD_DIM

    def rope(col):
        partner = jnp.where(first_half, pltpu.roll(col, LANES - ROT_DIM // 2, 1), pltpu.roll(col, ROT_DIM // 2, 1))
        return col * cos_t + partner * sin_t

    scale = HEAD_DIM ** -0.5 * LOG2_E
    for c in range(D // LANES):
        qc = rope(qkv[:, LANES * c:LANES * (c + 1)]) * scale
        qlo_scr[:, LANES * c:LANES * (c + 1)] = jnp.where(lo_half, qc, 0.0).astype(BF16)
        qhi_scr[:, LANES * c:LANES * (c + 1)] = jnp.where(lo_half, 0.0, qc).astype(BF16)
    for g in range(N_KV_HEADS):
        kc = rope(qkv[:, D + LANES * g:D + LANES * (g + 1)])
        k_scr[B:B + T, LANES * g:LANES * (g + 1)] = kc.astype(BF16)
        vc = qkv[:, D + KV_EXT + LANES * g:D + KV_EXT + LANES * (g + 1)]
        v_scr[B:B + T, LANES * g:LANES * (g + 1)] = vc.astype(BF16)

    qi = lax.broadcasted_iota(jnp.int32, (B, B), 0)
    kj = lax.broadcasted_iota(jnp.int32, (B, B), 1)
    causal_own = kj <= qi
    window_prev = kj > qi
    key0 = kj == 0
    key0_row = lax.broadcasted_iota(jnp.int32, (1, B), 1) == 0
    lo_blk = lax.broadcasted_iota(jnp.int32, (B, LANES), 1) < HEAD_DIM

    def block_body(blk, _):
        r0 = pl.multiple_of(blk * B, B)
        k_lo = jnp.where(first_tile & (blk == 0), B, 0)
        allowed_prev = window_prev & (kj >= k_lo)
        for g in range(N_KV_HEADS):
            kk = k_scr[pl.ds(r0, 2 * B), LANES * g:LANES * (g + 1)]
            vv = v_scr[pl.ds(r0, 2 * B), LANES * g:LANES * (g + 1)]
            cols = [slice(LANES * c, LANES * (c + 1)) for c in range((GROUP // 2) * g, (GROUP // 2) * (g + 1))]
            q_all = jnp.concatenate([q_scr[pl.ds(r0, B), col] for col in cols for q_scr in (qlo_scr, qhi_scr)], axis=0)
            s_all = lax.dot_general(q_all, kk, (((1,), (1,)), ((), ())), preferred_element_type=F32)
            p_all, denoms = [], []
            for hh in range(GROUP):
                rows = slice(B * hh, B * (hh + 1))
                sink_fill = jnp.where(key0_row, sink_ref[GROUP * g + hh] * LOG2_E, NEG_INF)
                s_prev = jnp.where(allowed_prev, s_all[rows, :B], sink_fill)
                s_own = jnp.where(causal_own, s_all[rows, B:], NEG_INF)
                m = jnp.max(jnp.maximum(s_prev, s_own), axis=-1, keepdims=True)
                p_prev = jnp.exp2(s_prev - m)
                p_own = jnp.exp2(s_own - m)
                denoms.append(jnp.sum(p_prev + p_own, axis=-1, keepdims=True))
                p_all.append(jnp.concatenate([jnp.where(key0, 0.0, p_prev), p_own], axis=1).astype(BF16))
            o_all = jnp.dot(jnp.concatenate(p_all, axis=0), vv, preferred_element_type=F32)
            o_heads = [o_all[B * hh:B * (hh + 1), :] / denoms[hh] for hh in range(GROUP)]
            for p, col in enumerate(cols):
                o_scr[pl.ds(r0, B), col] = jnp.where(lo_blk, o_heads[2 * p], o_heads[2 * p + 1]).astype(BF16)
        return 0

    lax.fori_loop(0, T // B, block_body, 0)

    k_scr[0:B, :] = k_scr[T:T + B, :]
    v_scr[0:B, :] = v_scr[T:T + B, :]

    h1 = jnp.dot(o_scr[...], wo_ref[...], preferred_element_type=F32) + bo_ref[...] + x
    _store_token_major(o_ref, h1)

    hn2 = _rmsnorm(h1, g2_ref[...]).astype(BF16)
    logits = jnp.dot(hn2, wr_ref[...], preferred_element_type=F32)
    lane_f = lane.astype(F32)
    lg = jnp.where(lane < N_EXPERTS, logits, -jnp.inf)
    v1 = jnp.max(lg, axis=-1, keepdims=True)
    i1 = jnp.min(jnp.where(lg == v1, lane_f, float(LANES)), axis=-1, keepdims=True)
    lg2 = jnp.where(lane_f == i1, -jnp.inf, lg)
    v2 = jnp.max(lg2, axis=-1, keepdims=True)
    i2 = jnp.min(jnp.where(lg2 == v2, lane_f, float(LANES)), axis=-1, keepdims=True)
    e2 = jnp.exp(v2 - v1)
    gate1 = 1.0 / (1.0 + e2)
    gate2 = e2 / (1.0 + e2)
    route_ref[...] = jnp.where(lane == 0, i1, jnp.where(lane == 1, i2,
                                                        jnp.where(lane == 2, gate1, jnp.where(lane == 3, gate2, 0.0))))


def _attn_mixer(x, pos_f, g, freq_lane, sign_lane, w_qkv_ext, b_qkv_ext, sinks, w_o, b_o, g2, w_router_pad):
    S, D = x.shape
    T = ATTN_TILE
    B = ATTN_BLOCK
    grid_spec = pltpu.PrefetchScalarGridSpec(
        num_scalar_prefetch=1,
        grid=(S // T,),
        in_specs=[
            pl.BlockSpec((T, D), lambda i, s: (i, 0)),
            pl.BlockSpec((T, 1), lambda i, s: (i, 0)),
            _const_spec((1, D)),
            _const_spec((1, LANES)),
            _const_spec((1, LANES)),
            _const_spec((D, QKV_EXT)),
            _const_spec((1, QKV_EXT)),
            _const_spec((D, D)),
            _const_spec((1, D)),
            _const_spec((1, D)),
            _const_spec((D, LANES)),
        ],
        out_specs=[pl.BlockSpec((T * ROWS_PER_TOKEN, LANES), lambda i, s: (i, 0)),
                   pl.BlockSpec((T, LANES), lambda i, s: (i, 0))],
        scratch_shapes=[
            pltpu.VMEM((T, D), BF16), pltpu.VMEM((T, D), BF16),
            pltpu.VMEM((T + B, KV_EXT), BF16), pltpu.VMEM((T + B, KV_EXT), BF16),
            pltpu.VMEM((T, D), BF16),
        ],
    )
    return pl.pallas_call(
        _attn_kernel,
        grid_spec=grid_spec,
        out_shape=[jax.ShapeDtypeStruct((S * ROWS_PER_TOKEN, LANES), F32), jax.ShapeDtypeStruct((S, LANES), F32)],
        compiler_params=pltpu.CompilerParams(dimension_semantics=("arbitrary",), vmem_limit_bytes=VMEM_LIMIT),
        name="attn_mixer",
    )(sinks, x, pos_f, g.reshape(1, D), freq_lane, sign_lane, w_qkv_ext, b_qkv_ext, w_o, b_o.reshape(1, D),
      g2.reshape(1, D), w_router_pad)


def _moe_kernel(texp_ref, nvt_ref, src0_ref, srcn_ref, dstp_ref, dstl_ref, h_hbm, g_ref, wg_ref, wu_ref, wd_ref,
                y_hbm, x_scr, xn_scr, acc_scr, y_scr, sems):
    i = pl.program_id(0)
    c = pl.program_id(1)
    n_tiles = pl.num_programs(0)
    n_chunks = pl.num_programs(1)
    T = xn_scr.shape[0]
    R = ROWS_PER_TOKEN
    valid = i < nvt_ref[0]
    slot = i % 2
    other = 1 - slot
    SCATTER_SEM = 2

    def row_copy(src, src_row, dst, dst_row, sem):
        return pltpu.make_async_copy(src.at[pl.ds(pl.multiple_of(src_row, R), R), :],
                                     dst.at[pl.ds(pl.multiple_of(dst_row, R), R), :], sem)

    def tile_rows(t):
        return nvt_ref[1 + t]

    def window_offset(t):
        return nvt_ref[1 + 2 * n_tiles + t]

    prev_tile = jnp.maximum(i - 1, 0)
    next_tile = jnp.minimum(i + 1, n_tiles - 1)
    rows_prev = jnp.where(i > 0, tile_rows(prev_tile), 0)
    pad_base = y_hbm.shape[0] - T * R

    def dest_row(window_ref, offset, r, n_rows):
        return jnp.where(r < n_rows, window_ref[0, offset + r], pad_base + r * R)

    def wait_gather(s):
        pltpu.make_async_copy(h_hbm.at[pl.ds(0, T * R), :], x_scr.at[s], sems.at[s]).wait()

    def wait_scatter():
        pltpu.make_async_copy(y_scr, y_hbm.at[pl.ds(0, T * R), :], sems.at[SCATTER_SEM]).wait()

    @pl.when((i == 0) & (c == 0))
    def _():
        y_scr[...] = jnp.zeros_like(y_scr)
        acc_scr[...] = jnp.zeros_like(acc_scr)

        def issue(r, carry):
            row_copy(h_hbm, src0_ref[0, window_offset(0) + r], x_scr.at[0], r * R, sems.at[0]).start()
            return carry
        lax.fori_loop(0, T, issue, 0)

    @pl.when(c == 0)
    def _():
        wait_gather(slot)

    def row_dmas():
        off_next = window_offset(next_tile)
        off_prev = window_offset(prev_tile)
        for r in range(T):
            row_copy(h_hbm, srcn_ref[0, off_next + r], x_scr.at[other], r * R, sems.at[other]).start()
            row_copy(y_scr, r * R, y_hbm, dest_row(dstp_ref, off_prev, r, rows_prev),
                     sems.at[SCATTER_SEM]).start()

    def expert_chunk(first, rows):
        xn = xn_scr[0:rows, :]
        gate = jnp.dot(xn, wg_ref[0], preferred_element_type=F32)
        up = jnp.dot(xn, wu_ref[0], preferred_element_type=F32)
        mid = (jax.nn.silu(gate) * up).astype(BF16)
        down = jnp.dot(mid, wd_ref[0], preferred_element_type=F32)
        if first:
            acc_scr[0:rows, :] = down
        else:
            acc_scr[0:rows, :] += down

    few_rows = tile_rows(i) <= T // 2

    for few, rows in ((False, T), (True, T // 2)):
        branch = valid & (few_rows if few else jnp.logical_not(few_rows))

        @pl.when(branch & (c == 0))
        def _():
            xn_scr[0:rows, :] = _rmsnorm(_load_token_major(x_scr.at[slot], rows), g_ref[...]).astype(BF16)
            row_dmas()
            expert_chunk(True, rows)

        @pl.when(branch & (c != 0))
        def _():
            expert_chunk(False, rows)

    @pl.when(jnp.logical_not(valid) & (c == 0))
    def _():
        row_dmas()

    @pl.when(c == n_chunks - 1)
    def _():
        wait_scatter()

    @pl.when(valid & (c == n_chunks - 1))
    def _():
        _store_token_major(y_scr, acc_scr[...])

    @pl.when((i == n_tiles - 1) & (c == n_chunks - 1))
    def _():
        wait_gather(other)


        def issue(r, carry):
            row_copy(y_scr, r * R, y_hbm, dest_row(dstl_ref, window_offset(n_tiles - 1), r, tile_rows(n_tiles - 1)),
                     sems.at[SCATTER_SEM]).start()
            return carry
        lax.fori_loop(0, T, issue, 0)
        wait_scatter()


def _moe_experts(h1_tm, g, tile_expert, n_valid_tiles, src_rows, dst_rows, w_gate, w_up, w_down, n_out_tokens):
    D = D_MODEL
    T = MOE_TILE
    NT = tile_expert.shape[0]
    NC = N_MOE_CHUNKS
    FC = D_FF // NC

    def chunk(i, c, nvt):
        return jnp.where(i < nvt[0], c, NC - 1)

    def window_spec(tile_of_step):
        return pl.BlockSpec((pl.Element(1), pl.Element(T + LANES)),
                            lambda i, c, te, nvt: (0, nvt[1 + NT + tile_of_step(i)] * LANES),
                            memory_space=pltpu.SMEM)

    grid_spec = pltpu.PrefetchScalarGridSpec(
        num_scalar_prefetch=2,
        grid=(NT, NC),
        in_specs=[
            window_spec(lambda i: 0),
            window_spec(lambda i: jnp.minimum(i + 1, NT - 1)),
            window_spec(lambda i: jnp.maximum(i - 1, 0)),
            window_spec(lambda i: NT - 1),
            pl.BlockSpec(memory_space=pl.ANY),
            pl.BlockSpec((1, D), lambda i, c, te, nvt: (0, 0)),
            pl.BlockSpec((1, D, FC), lambda i, c, te, nvt: (te[i], 0, chunk(i, c, nvt))),
            pl.BlockSpec((1, D, FC), lambda i, c, te, nvt: (te[i], 0, chunk(i, c, nvt))),
            pl.BlockSpec((1, FC, D), lambda i, c, te, nvt: (te[i], chunk(i, c, nvt), 0)),
        ],
        out_specs=pl.BlockSpec(memory_space=pl.ANY),
        scratch_shapes=[pltpu.VMEM((2, T * ROWS_PER_TOKEN, LANES), F32), pltpu.VMEM((T, D), BF16),
                        pltpu.VMEM((T, D), F32), pltpu.VMEM((T * ROWS_PER_TOKEN, LANES), F32),
                        pltpu.SemaphoreType.DMA((3,))],
    )
    return pl.pallas_call(
        _moe_kernel,
        grid_spec=grid_spec,
        out_shape=jax.ShapeDtypeStruct((n_out_tokens * ROWS_PER_TOKEN, LANES), F32),
        compiler_params=pltpu.CompilerParams(dimension_semantics=("arbitrary", "arbitrary"),
                                             vmem_limit_bytes=VMEM_LIMIT),
        name="moe_experts",
    )(tile_expert, n_valid_tiles, src_rows, src_rows, dst_rows, dst_rows, h1_tm, g.reshape(1, D),
      w_gate, w_up, w_down)


def _moe_plan(route, n_tokens):
    T = MOE_TILE
    n_assign = TOP_K * n_tokens
    n_tiles = n_assign // T + N_EXPERTS
    e_flat = route[:, 0:TOP_K].astype(jnp.int32).reshape(-1)
    counts = jnp.sum((e_flat[:, None] == jnp.arange(N_EXPERTS, dtype=jnp.int32)[None, :]).astype(jnp.int32), axis=0)
    order = jnp.argsort(e_flat, stable=True).astype(jnp.int32)
    tiles_per = (counts + T - 1) // T
    tile_end = jnp.cumsum(tiles_per)
    n_valid = tile_end[-1]
    tile_ids = jnp.arange(n_tiles, dtype=jnp.int32)
    last_valid = jnp.maximum(n_valid - 1, 0)
    texp = jnp.sum((jnp.minimum(tile_ids, last_valid)[:, None] >= tile_end[None, :]).astype(jnp.int32), axis=1)
    texp = jnp.minimum(texp, N_EXPERTS - 1)
    onehot = (texp[:, None] == jnp.arange(N_EXPERTS, dtype=jnp.int32)[None, :]).astype(jnp.int32)
    pick = lambda v: jnp.sum(onehot * v[None, :], axis=1)
    j = tile_ids - pick(tile_end - tiles_per)
    start = jnp.where(tile_ids < n_valid, pick(jnp.cumsum(counts) - counts) + j * T, 0)
    n_rows = jnp.where(tile_ids < n_valid, jnp.clip(pick(counts) - j * T, 0, T), 0)
    token = order // TOP_K
    slot = order % TOP_K
    spare = jnp.zeros((T + LANES,), jnp.int32)
    src_sorted = jnp.concatenate([token * ROWS_PER_TOKEN, spare]).reshape(1, -1)
    dst_sorted = jnp.concatenate([(slot * n_tokens + token) * ROWS_PER_TOKEN, spare]).reshape(1, -1)
    tile_counts = jnp.concatenate([n_valid.reshape(1), n_rows, start // LANES, start % LANES]).astype(jnp.int32)
    return texp, tile_counts, src_sorted, dst_sorted


def _combine_kernel(h_ref, y0_ref, y1_ref, route_ref, g_ref, o_ref):
    T = o_ref.shape[0]
    route = route_ref[...]
    moe = route[:, 2:3] * _load_token_major(y0_ref, T) + route[:, 3:4] * _load_token_major(y1_ref, T)
    o_ref[...] = _rmsnorm(_load_token_major(h_ref, T) + moe, g_ref[...])


def _combine(h1_tm, y2_tm, route, g):
    D = D_MODEL
    S = route.shape[0]
    T = OUT_TILE
    nb = S // T
    tm_block = (T * ROWS_PER_TOKEN, LANES)
    return pl.pallas_call(
        _combine_kernel,
        grid=(nb,),
        in_specs=[
            pl.BlockSpec(tm_block, lambda i: (i, 0)),
            pl.BlockSpec(tm_block, lambda i: (i, 0)),
            pl.BlockSpec(tm_block, lambda i: (i + nb, 0)),
            pl.BlockSpec((T, LANES), lambda i: (i, 0)),
            pl.BlockSpec((1, D), lambda i: (0, 0)),
        ],
        out_specs=pl.BlockSpec((T, D), lambda i: (i, 0)),
        out_shape=jax.ShapeDtypeStruct((S, D), F32),
        compiler_params=pltpu.CompilerParams(dimension_semantics=("arbitrary",), vmem_limit_bytes=VMEM_LIMIT),
        name="moe_combine",
    )(h1_tm, y2_tm, y2_tm, route, g.reshape(1, D))


def kernel(x, positions, norm_mix, norm_ffn, norm_final, lru_w_in, lru_conv_w, lru_conv_b, lru_w_a, lru_b_a, lru_w_x, lru_b_x, lru_lambda, lru_w_out, attn_w_qkv, attn_b_qkv, attn_sinks, attn_w_o, attn_b_o, ffn_w_gate, ffn_w_up, ffn_w_down, moe_w_router, moe_w_gate, moe_w_up, moe_w_down):
    B, S, D = x.shape
    assert B == 1 and D == D_MODEL and S % FFN_TILE == 0
    h = x.reshape(S, D)

    w_ax = jnp.concatenate([lru_w_a[0], lru_w_x[0]], axis=-1).astype(BF16)
    h, ffn_wg, ffn_wu, ffn_wd = _lru_mixer(h, norm_mix[0], lru_w_in[0].astype(BF16), lru_conv_w[0], lru_conv_b[0],
                                           w_ax, lru_b_a[0], lru_b_x[0], lru_lambda[0], lru_w_out[0].astype(BF16),
                                           ffn_w_gate[0], ffn_w_up[0], ffn_w_down[0])
    h, moe_wg, moe_wu, moe_wd = _dense_ffn(h, norm_ffn[0], ffn_wg, ffn_wu, ffn_wd,
                                           moe_w_gate[0], moe_w_up[0], moe_w_down[0])

    q_dim = N_HEADS * HEAD_DIM
    kv_dim = N_KV_HEADS * HEAD_DIM

    def dup_heads(w):
        w4 = w.reshape(w.shape[:-1] + (N_KV_HEADS, 1, HEAD_DIM))
        return jnp.broadcast_to(w4, w.shape[:-1] + (N_KV_HEADS, 2, HEAD_DIM)).reshape(w.shape[:-1] + (KV_EXT,))

    wqkv, bqkv = attn_w_qkv[0], attn_b_qkv[0]
    w_qkv_ext = jnp.concatenate([wqkv[:, :q_dim], dup_heads(wqkv[:, q_dim:q_dim + kv_dim]),
                                 dup_heads(wqkv[:, q_dim + kv_dim:])], axis=1).astype(BF16)
    b_qkv_ext = jnp.concatenate([bqkv[:q_dim], dup_heads(bqkv[q_dim:q_dim + kv_dim]),
                                 dup_heads(bqkv[q_dim + kv_dim:])]).reshape(1, QKV_EXT)
    inv_freq = ROPE_THETA ** (-jnp.arange(0, ROT_DIM, 2, dtype=F32) / ROT_DIM)
    d_in_head = jnp.arange(LANES) % HEAD_DIM
    freq_lane = jnp.where(d_in_head < ROT_DIM, inv_freq[d_in_head % (ROT_DIM // 2)], 0.0).reshape(1, LANES)
    sign_lane = jnp.where(d_in_head < ROT_DIM // 2, -1.0, jnp.where(d_in_head < ROT_DIM, 1.0, 0.0))
    sign_lane = sign_lane.astype(F32).reshape(1, LANES)
    w_router_pad = jnp.pad(moe_w_router[0], ((0, 0), (0, LANES - N_EXPERTS))).astype(BF16)
    pos_f = positions.reshape(S, 1).astype(F32)
    h1_tm, route = _attn_mixer(h, pos_f, norm_mix[1], freq_lane, sign_lane, w_qkv_ext, b_qkv_ext,
                               attn_sinks[0], attn_w_o[0].astype(BF16), attn_b_o[0], norm_ffn[1], w_router_pad)

    texp, n_valid, src_rows, dst_rows = _moe_plan(route, S)
    y2_tm = _moe_experts(h1_tm, norm_ffn[1], texp, n_valid, src_rows, dst_rows,
                         moe_wg, moe_wu, moe_wd,
                         TOP_K * S + MOE_TILE)
    out = _combine(h1_tm, y2_tm, route, norm_final)
    return out.reshape(B, S, D)
```

```python
import jax
import jax.numpy as jnp
from jax import lax
from jax.experimental import pallas as pl
from jax.experimental.pallas import tpu as pltpu

F32 = jnp.float32
BF16 = jnp.bfloat16

D_MODEL = 1024
N_RNN_BLOCKS = 8
RNN_BLOCK = D_MODEL // N_RNN_BLOCKS
CONV_WIDTH = 4
LRU_C = 8.0
N_HEADS = 16
N_KV_HEADS = 4
HEAD_DIM = 64
GROUP = N_HEADS // N_KV_HEADS
WINDOW = 128
ATTN_BLOCK = 128
assert WINDOW == ATTN_BLOCK
ROPE_THETA = 500000.0
ROT_DIM = HEAD_DIM // 4
D_FF = 3584
N_EXPERTS = 8
TOP_K = 2
EPS = 1e-6
NEG_INF = -1e30
LOG2_E = 1.4426950408889634

LANES = 128
SUBLANES = 8
VMEM_LIMIT = 56 * 1024 * 1024

SEQ_TILE = 512
ATTN_TILE = 1024
LRU_CHUNKS = 4
FFN_TILE = 1024
FF_CHUNK = 512
N_FF_CHUNKS = D_FF // FF_CHUNK
MOE_TILE = 512
N_MOE_CHUNKS = 2
OUT_TILE = 1024


def _rmsnorm(x, g):
    return x * lax.rsqrt(jnp.mean(x * x, axis=-1, keepdims=True) + EPS) * g


ROWS_PER_TOKEN = D_MODEL // LANES


def _store_token_major(ref, val):
    T = val.shape[0]
    for c in range(ROWS_PER_TOKEN):
        ref[pl.ds(c, T, stride=ROWS_PER_TOKEN), :] = val[:, LANES * c:LANES * (c + 1)]


def _load_token_major(ref, T):
    return jnp.concatenate([ref[pl.ds(c, T, stride=ROWS_PER_TOKEN), :] for c in range(ROWS_PER_TOKEN)], axis=1)


def _const_spec(shape):
    n = len(shape)
    return pl.BlockSpec(shape, lambda *_: (0,) * n, pipeline_mode=pl.Buffered(1))


def _time_permutation(T):
    rho = jnp.arange(T)
    t_of_rho = (rho % SUBLANES) * (T // SUBLANES) + rho // SUBLANES
    perm = (t_of_rho[:, None] == jnp.arange(T)[None, :]).astype(BF16)
    return perm, perm.T


def _lru_kernel(xin_ref, xres_ref, g_ref, perm_ref, unperm_ref, win_ref, cw_ref, cb_ref, wax_ref, ba_ref, bx_ref,
                lam_ref, wout_ref, cast_g_ref, cast_u_ref, cast_d_ref,
                o_ref, bf_g_ref, bf_u_ref, bf_d_ref, proj_scr, tail_scr, carry_scr):
    i = pl.program_id(0)
    n_tiles = pl.num_programs(0) - 1
    T = xin_ref.shape[0]
    D = D_MODEL
    C = T // LRU_CHUNKS
    GC = C // SUBLANES
    G = T // SUBLANES
    HALO = (CONV_WIDTH - 1) * SUBLANES

    def cast_slices():
        bf_g_ref[...] = cast_g_ref[...].astype(BF16)
        bf_u_ref[...] = cast_u_ref[...].astype(BF16)
        bf_d_ref[...] = cast_d_ref[...].astype(BF16)

    @pl.when(i == 0)
    def _():
        tail_scr[...] = jnp.zeros_like(tail_scr)
        carry_scr[...] = jnp.zeros_like(carry_scr)

    PW = 2 * D // LRU_CHUNKS

    def project_chunks():
        hn = _rmsnorm(xin_ref[...], g_ref[...]).astype(BF16)
        hp = jnp.dot(perm_ref[...], hn, preferred_element_type=F32).astype(BF16)
        for k in range(LRU_CHUNKS):
            yield jnp.dot(hp, win_ref[:, PW * k:PW * (k + 1)], preferred_element_type=F32)

    def recur_chunks():
        sub = lax.broadcasted_iota(jnp.int32, (SUBLANES, D), 0)
        cw = cw_ref[...]
        softplus_neg_lam = jax.nn.softplus(-lam_ref[...])
        last = proj_scr[T - HALO:T, D:]
        prev_groups = []
        for k in range(CONV_WIDTH - 1):
            rows = slice(SUBLANES * k, SUBLANES * (k + 1))
            prev_groups.append(pltpu.roll(jnp.where(sub == SUBLANES - 1, tail_scr[rows, :], last[rows, :]), 1, 0))
        tail_scr[...] = last

        ys, hs, decay = [], [], []
        for k in range(LRU_CHUNKS):
            pc = proj_scr[C * k:C * (k + 1), :]
            ys.append(jax.nn.gelu(pc[:, :D], approximate=True))
            xb = pc[:, D:]
            xc = cb_ref[...] + xb * cw[CONV_WIDTH - 1:CONV_WIDTH, :]
            for back in range(1, CONV_WIDTH):
                shifted = jnp.concatenate(prev_groups[CONV_WIDTH - 1 - back:] + [xb[:C - SUBLANES * back, :]], axis=0)
                xc = xc + shifted * cw[CONV_WIDTH - 1 - back:CONV_WIDTH - back, :]
            prev_groups = [xb[C - HALO + SUBLANES * q:C - HALO + SUBLANES * (q + 1), :] for q in range(CONV_WIDTH - 1)]

            xcb = xc.astype(BF16)
            r_parts, i_parts = [], []
            for n in range(N_RNN_BLOCKS):
                gn = jnp.dot(xcb[:, RNN_BLOCK * n:RNN_BLOCK * (n + 1)], wax_ref[n], preferred_element_type=F32)
                r_parts.append(gn[:, :RNN_BLOCK])
                i_parts.append(gn[:, RNN_BLOCK:])
            r = jax.nn.sigmoid(jnp.concatenate(r_parts, axis=1) + ba_ref[...])
            ig = jax.nn.sigmoid(jnp.concatenate(i_parts, axis=1) + bx_ref[...])
            log_a = -LRU_C * r * softplus_neg_lam
            a = jnp.exp(log_a)
            z = jnp.tanh(-log_a) * (1.0 + a * a)
            b = jnp.where(z > 0.0, z * lax.rsqrt(z), 0.0) * (ig * xc)

            for j in range(GC):
                rows = slice(SUBLANES * j, SUBLANES * (j + 1))
                if hs:
                    hs.append(a[rows, :] * hs[-1] + b[rows, :])
                    decay.append(a[rows, :] * decay[-1])
                else:
                    hs.append(b[rows, :])
                    decay.append(a[rows, :])
            yield None

        p_inc, e_inc = decay[-1], hs[-1]
        for k in (1, 2, 4):
            m = sub >= k
            p_sh = jnp.where(m, pltpu.roll(p_inc, k, 0), 1.0)
            e_sh = jnp.where(m, pltpu.roll(e_inc, k, 0), 0.0)
            e_inc = e_inc + p_inc * e_sh
            p_inc = p_inc * p_sh
        h0 = carry_scr[...]
        after = p_inc * h0 + e_inc
        carry_scr[...] = after[SUBLANES - 1:SUBLANES, :]
        enter = jnp.where(sub == 0, h0, pltpu.roll(after, 1, 0))
        h = jnp.concatenate([hs[j] + decay[j] * enter for j in range(G)], axis=0)
        hy = (h * jnp.concatenate(ys, axis=0)).astype(BF16)
        hy = jnp.dot(unperm_ref[...], hy, preferred_element_type=F32).astype(BF16)
        yield jnp.dot(hy, wout_ref[...], preferred_element_type=F32) + xres_ref[...]

    @pl.when(i == 0)
    def _():
        cast_slices()
        for k, pc in enumerate(project_chunks()):
            proj_scr[:, PW * k:PW * (k + 1)] = pc

    @pl.when((i > 0) & (i < n_tiles))
    def _():
        cast_slices()
        rec = recur_chunks()
        new_proj = []
        for pc in project_chunks():
            new_proj.append(pc)
            next(rec)
        o_ref[...] = next(rec)
        for k, pc in enumerate(new_proj):
            proj_scr[:, PW * k:PW * (k + 1)] = pc

    @pl.when(i == n_tiles)
    def _():
        o_ref[...] = list(recur_chunks())[-1]


def _lru_mixer(x, g, w_in, conv_w, conv_b, w_ax, b_a, b_x, lam, w_out, cast_gate, cast_up, cast_down):
    S, D = x.shape
    T = SEQ_TILE
    n_tiles = S // T
    row = lambda v: v.reshape(1, D)
    perm, unperm = _time_permutation(T)

    def cast_spec(w):
        return pl.BlockSpec((w.shape[0] // n_tiles, w.shape[1]), lambda i: (jnp.minimum(i, n_tiles - 1), 0))

    casts = (cast_gate, cast_up, cast_down)
    return pl.pallas_call(
        _lru_kernel,
        grid=(n_tiles + 1,),
        in_specs=[
            pl.BlockSpec((T, D), lambda i: (jnp.minimum(i, n_tiles - 1), 0)),
            pl.BlockSpec((T, D), lambda i: (jnp.maximum(i - 1, 0), 0)),
            _const_spec((1, D)),
            _const_spec((T, T)),
            _const_spec((T, T)),
            _const_spec((D, 2 * D)),
            _const_spec((CONV_WIDTH, D)),
            _const_spec((1, D)),
            _const_spec((N_RNN_BLOCKS, RNN_BLOCK, 2 * RNN_BLOCK)),
            _const_spec((1, D)),
            _const_spec((1, D)),
            _const_spec((1, D)),
            _const_spec((D, D)),
        ] + [cast_spec(w) for w in casts],
        out_specs=[pl.BlockSpec((T, D), lambda i: (jnp.maximum(i - 1, 0), 0))] + [cast_spec(w) for w in casts],
        out_shape=[jax.ShapeDtypeStruct((S, D), F32)] + [jax.ShapeDtypeStruct(w.shape, BF16) for w in casts],
        scratch_shapes=[pltpu.VMEM((T, 2 * D), F32), pltpu.VMEM(((CONV_WIDTH - 1) * SUBLANES, D), F32),
                        pltpu.VMEM((1, D), F32)],
        compiler_params=pltpu.CompilerParams(dimension_semantics=("arbitrary",), vmem_limit_bytes=VMEM_LIMIT),
        name="lru_mixer",
    )(x, x, row(g), perm, unperm, w_in, conv_w, row(conv_b), w_ax, row(b_a), row(b_x), row(lam), w_out, *casts)


def _ffn_kernel(x_ref, g_ref, wg_ref, wu_ref, wd_ref, cast_g_ref, cast_u_ref, cast_d_ref,
                o_ref, bf_g_ref, bf_u_ref, bf_d_ref, xn_scr, acc_scr):
    c = pl.program_id(1)
    last = pl.num_programs(1) - 1

    def ff_chunk():
        bf_g_ref[...] = cast_g_ref[...].astype(BF16)
        bf_u_ref[...] = cast_u_ref[...].astype(BF16)
        bf_d_ref[...] = cast_d_ref[...].astype(BF16)
        xn = xn_scr[...]
        gate = jnp.dot(xn, wg_ref[...], preferred_element_type=F32)
        up = jnp.dot(xn, wu_ref[...], preferred_element_type=F32)
        mid = (jax.nn.silu(gate) * up).astype(BF16)
        return jnp.dot(mid, wd_ref[...], preferred_element_type=F32)

    @pl.when(c == 0)
    def _():
        xn_scr[...] = _rmsnorm(x_ref[...], g_ref[...]).astype(BF16)
        acc_scr[...] = ff_chunk()

    @pl.when((c > 0) & (c < last))
    def _():
        acc_scr[...] += ff_chunk()

    @pl.when(c == last)
    def _():
        o_ref[...] = x_ref[...] + (acc_scr[...] + ff_chunk())


def _dense_ffn(x, g, w_gate, w_up, w_down, cast_gate, cast_up, cast_down):
    S, D = x.shape
    T = FFN_TILE
    n_steps = (S // T) * N_FF_CHUNKS
    E = cast_gate.shape[0]
    row_parts = n_steps // (E * N_FF_CHUNKS)
    assert row_parts * E * N_FF_CHUNKS == n_steps
    gu_block = (1, D // row_parts, D_FF // N_FF_CHUNKS)
    d_block = (1, D_FF // (row_parts * N_FF_CHUNKS), D)

    def gu_map(i, c):
        s = i * N_FF_CHUNKS + c
        return (s // (row_parts * N_FF_CHUNKS), (s // N_FF_CHUNKS) % row_parts, c)

    def d_map(i, c):
        s = i * N_FF_CHUNKS + c
        return (s // (row_parts * N_FF_CHUNKS), s % (row_parts * N_FF_CHUNKS), 0)

    return pl.pallas_call(
        _ffn_kernel,
        grid=(S // T, N_FF_CHUNKS),
        in_specs=[
            pl.BlockSpec((T, D), lambda i, c: (i, 0)),
            pl.BlockSpec((1, D), lambda i, c: (0, 0)),
            pl.BlockSpec((D, FF_CHUNK), lambda i, c: (0, c)),
            pl.BlockSpec((D, FF_CHUNK), lambda i, c: (0, c)),
            pl.BlockSpec((FF_CHUNK, D), lambda i, c: (c, 0)),
            pl.BlockSpec(gu_block, gu_map),
            pl.BlockSpec(gu_block, gu_map),
            pl.BlockSpec(d_block, d_map),
        ],
        out_specs=[pl.BlockSpec((T, D), lambda i, c: (i, 0)), pl.BlockSpec(gu_block, gu_map),
                   pl.BlockSpec(gu_block, gu_map), pl.BlockSpec(d_block, d_map)],
        out_shape=[jax.ShapeDtypeStruct((S, D), F32), jax.ShapeDtypeStruct(cast_gate.shape, BF16),
                   jax.ShapeDtypeStruct(cast_up.shape, BF16), jax.ShapeDtypeStruct(cast_down.shape, BF16)],
        scratch_shapes=[pltpu.VMEM((T, D), BF16), pltpu.VMEM((T, D), F32)],
        compiler_params=pltpu.CompilerParams(dimension_semantics=("arbitrary", "arbitrary"),
                                             vmem_limit_bytes=VMEM_LIMIT),
        name="dense_ffn",
    )(x, g.reshape(1, D), w_gate, w_up, w_down, cast_gate, cast_up, cast_down)


KV_EXT = N_KV_HEADS * LANES
QKV_EXT = N_HEADS * HEAD_DIM + 2 * KV_EXT


def _attn_kernel(sink_ref, x_ref, pos_ref, g_ref, freq_ref, sign_ref, wqkv_ref, bqkv_ref, wo_ref, bo_ref,
                 g2_ref, wr_ref, o_ref, route_ref,
                 qlo_scr, qhi_scr, k_scr, v_scr, o_scr):
    T = x_ref.shape[0]
    D = D_MODEL
    B = ATTN_BLOCK
    first_tile = pl.program_id(0) == 0

    @pl.when(first_tile)
    def _():
        k_scr[0:B, :] = jnp.zeros((B, KV_EXT), BF16)
        v_scr[0:B, :] = jnp.zeros((B, KV_EXT), BF16)

    x = x_ref[...]
    hn = _rmsnorm(x, g_ref[...]).astype(BF16)
    qkv = jnp.dot(hn, wqkv_ref[...], preferred_element_type=F32) + bqkv_ref[...]

    ang = pos_ref[...] * freq_ref[...]
    cos_t = jnp.cos(ang)
    sin_t = jnp.sin(ang) * sign_ref[...]
    lane = lax.broadcasted_iota(jnp.int32, (T, LANES), 1)
    first_half = (lane % HEAD_DIM) < (ROT_DIM // 2)
    lo_half = lane < HEAD_DIM

    def rope(col):
        partner = jnp.where(first_half, pltpu.roll(col, LANES - ROT_DIM // 2, 1), pltpu.roll(col, ROT_DIM // 2, 1))
        return col * cos_t + partner * sin_t

    scale = HEAD_DIM ** -0.5 * LOG2_E
    for c in range(D // LANES):
        qc = rope(qkv[:, LANES * c:LANES * (c + 1)]) * scale
        qlo_scr[:, LANES * c:LANES * (c + 1)] = jnp.where(lo_half, qc, 0.0).astype(BF16)
        qhi_scr[:, LANES * c:LANES * (c + 1)] = jnp.where(lo_half, 0.0, qc).astype(BF16)
    for g in range(N_KV_HEADS):
        kc = rope(qkv[:, D + LANES * g:D + LANES * (g + 1)])
        k_scr[B:B + T, LANES * g:LANES * (g + 1)] = kc.astype(BF16)
        vc = qkv[:, D + KV_EXT + LANES * g:D + KV_EXT + LANES * (g + 1)]
        v_scr[B:B + T, LANES * g:LANES * (g + 1)] = vc.astype(BF16)

    qi = lax.broadcasted_iota(jnp.int32, (B, B), 0)
    kj = lax.broadcasted_iota(jnp.int32, (B, B), 1)
    causal_own = kj <= qi
    window_prev = kj > qi
    key0 = kj == 0
    key0_row = lax.broadcasted_iota(jnp.int32, (1, B), 1) == 0
    lo_blk = lax.broadcasted_iota(jnp.int32, (B, LANES), 1) < HEAD_DIM

    def block_body(blk, _):
        r0 = pl.multiple_of(blk * B, B)
        k_lo = jnp.where(first_tile & (blk == 0), B, 0)
        allowed_prev = window_prev & (kj >= k_lo)
        for g in range(N_KV_HEADS):
            kk = k_scr[pl.ds(r0, 2 * B), LANES * g:LANES * (g + 1)]
            vv = v_scr[pl.ds(r0, 2 * B), LANES * g:LANES * (g + 1)]
            cols = [slice(LANES * c, LANES * (c + 1)) for c in range((GROUP // 2) * g, (GROUP // 2) * (g + 1))]
            q_all = jnp.concatenate([q_scr[pl.ds(r0, B), col] for col in cols for q_scr in (qlo_scr, qhi_scr)], axis=0)
            s_all = lax.dot_general(q_all, kk, (((1,), (1,)), ((), ())), preferred_element_type=F32)
            p_all, denoms = [], []
            for hh in range(GROUP):
                rows = slice(B * hh, B * (hh + 1))
                sink_fill = jnp.where(key0_row, sink_ref[GROUP * g + hh] * LOG2_E, NEG_INF)
                s_prev = jnp.where(allowed_prev, s_all[rows, :B], sink_fill)
                s_own = jnp.where(causal_own, s_all[rows, B:], NEG_INF)
                m = jnp.max(jnp.maximum(s_prev, s_own), axis=-1, keepdims=True)
                p_prev = jnp.exp2(s_prev - m)
                p_own = jnp.exp2(s_own - m)
                denoms.append(jnp.sum(p_prev + p_own, axis=-1, keepdims=True))
                p_all.append(jnp.concatenate([jnp.where(key0, 0.0, p_prev), p_own], axis=1).astype(BF16))
            o_all = jnp.dot(jnp.concatenate(p_all, axis=0), vv, preferred_element_type=F32)
            o_heads = [o_all[B * hh:B * (hh + 1), :] / denoms[hh] for hh in range(GROUP)]
            for p, col in enumerate(cols):
                o_scr[pl.ds(r0, B), col] = jnp.where(lo_blk, o_heads[2 * p], o_heads[2 * p + 1]).astype(BF16)
        return 0

    lax.fori_loop(0, T // B, block_body, 0)

    k_scr[0:B, :] = k_scr[T:T + B, :]
    v_scr[0:B, :] = v_scr[T:T + B, :]

    h1 = jnp.dot(o_scr[...], wo_ref[...], preferred_element_type=F32) + bo_ref[...] + x
    _store_token_major(o_ref, h1)

    hn2 = _rmsnorm(h1, g2_ref[...]).astype(BF16)
    logits = jnp.dot(hn2, wr_ref[...], preferred_element_type=F32)
    lane_f = lane.astype(F32)
    lg = jnp.where(lane < N_EXPERTS, logits, -jnp.inf)
    v1 = jnp.max(lg, axis=-1, keepdims=True)
    i1 = jnp.min(jnp.where(lg == v1, lane_f, float(LANES)), axis=-1, keepdims=True)
    lg2 = jnp.where(lane_f == i1, -jnp.inf, lg)
    v2 = jnp.max(lg2, axis=-1, keepdims=True)
    i2 = jnp.min(jnp.where(lg2 == v2, lane_f, float(LANES)), axis=-1, keepdims=True)
    e2 = jnp.exp(v2 - v1)
    gate1 = 1.0 / (1.0 + e2)
    gate2 = e2 / (1.0 + e2)
    route_ref[...] = jnp.where(lane == 0, i1, jnp.where(lane == 1, i2,
                                                        jnp.where(lane == 2, gate1, jnp.where(lane == 3, gate2, 0.0))))


def _attn_mixer(x, pos_f, g, freq_lane, sign_lane, w_qkv_ext, b_qkv_ext, sinks, w_o, b_o, g2, w_router_pad):
    S, D = x.shape
    T = ATTN_TILE
    B = ATTN_BLOCK
    grid_spec = pltpu.PrefetchScalarGridSpec(
        num_scalar_prefetch=1,
        grid=(S // T,),
        in_specs=[
            pl.BlockSpec((T, D), lambda i, s: (i, 0)),
            pl.BlockSpec((T, 1), lambda i, s: (i, 0)),
            _const_spec((1, D)),
            _const_spec((1, LANES)),
            _const_spec((1, LANES)),
            _const_spec((D, QKV_EXT)),
            _const_spec((1, QKV_EXT)),
            _const_spec((D, D)),
            _const_spec((1, D)),
            _const_spec((1, D)),
            _const_spec((D, LANES)),
        ],
        out_specs=[pl.BlockSpec((T * ROWS_PER_TOKEN, LANES), lambda i, s: (i, 0)),
                   pl.BlockSpec((T, LANES), lambda i, s: (i, 0))],
        scratch_shapes=[
            pltpu.VMEM((T, D), BF16), pltpu.VMEM((T, D), BF16),
            pltpu.VMEM((T + B, KV_EXT), BF16), pltpu.VMEM((T + B, KV_EXT), BF16),
            pltpu.VMEM((T, D), BF16),
        ],
    )
    return pl.pallas_call(
        _attn_kernel,
        grid_spec=grid_spec,
        out_shape=[jax.ShapeDtypeStruct((S * ROWS_PER_TOKEN, LANES), F32), jax.ShapeDtypeStruct((S, LANES), F32)],
        compiler_params=pltpu.CompilerParams(dimension_semantics=("arbitrary",), vmem_limit_bytes=VMEM_LIMIT),
        name="attn_mixer",
    )(sinks, x, pos_f, g.reshape(1, D), freq_lane, sign_lane, w_qkv_ext, b_qkv_ext, w_o, b_o.reshape(1, D),
      g2.reshape(1, D), w_router_pad)


def _moe_kernel(texp_ref, nvt_ref, src0_ref, srcn_ref, dstp_ref, dstl_ref, h_hbm, g_ref, wg_ref, wu_ref, wd_ref,
                y_hbm, x_scr, xn_scr, acc_scr, y_scr, sems):
    i = pl.program_id(0)
    c = pl.program_id(1)
    n_tiles = pl.num_programs(0)
    n_chunks = pl.num_programs(1)
    T = xn_scr.shape[0]
    R = ROWS_PER_TOKEN
    valid = i < nvt_ref[0]
    slot = i % 2
    other = 1 - slot
    SCATTER_SEM = 2

    def row_copy(src, src_row, dst, dst_row, sem):
        return pltpu.make_async_copy(src.at[pl.ds(pl.multiple_of(src_row, R), R), :],
                                     dst.at[pl.ds(pl.multiple_of(dst_row, R), R), :], sem)

    def tile_rows(t):
        return nvt_ref[1 + t]

    def window_offset(t):
        return nvt_ref[1 + 2 * n_tiles + t]

    prev_tile = jnp.maximum(i - 1, 0)
    next_tile = jnp.minimum(i + 1, n_tiles - 1)
    rows_prev = jnp.where(i > 0, tile_rows(prev_tile), 0)
    pad_base = y_hbm.shape[0] - T * R

    def dest_row(window_ref, offset, r, n_rows):
        return jnp.where(r < n_rows, window_ref[0, offset + r], pad_base + r * R)

    def wait_gather(s):
        pltpu.make_async_copy(h_hbm.at[pl.ds(0, T * R), :], x_scr.at[s], sems.at[s]).wait()

    def wait_scatter():
        pltpu.make_async_copy(y_scr.at[0], y_hbm.at[pl.ds(0, T * R), :], sems.at[SCATTER_SEM]).wait()

    @pl.when((i == 0) & (c == 0))
    def _():
        y_scr[...] = jnp.zeros_like(y_scr)

        def issue(r, carry):
            row_copy(h_hbm, src0_ref[0, window_offset(0) + r], x_scr.at[0], r * R, sems.at[0]).start()
            return carry
        lax.fori_loop(0, T, issue, 0)

    @pl.when(c == 0)
    def _():
        wait_gather(slot)

    def row_dmas():
        off_next = window_offset(next_tile)
        off_prev = window_offset(prev_tile)
        for r in range(T):
            row_copy(h_hbm, srcn_ref[0, off_next + r], x_scr.at[other], r * R, sems.at[other]).start()
            row_copy(y_scr.at[other], r * R, y_hbm, dest_row(dstp_ref, off_prev, r, rows_prev),
                     sems.at[SCATTER_SEM]).start()

    def expert_chunk(position, rows):
        xn = xn_scr[0:rows, :]
        gate = jnp.dot(xn, wg_ref[0], preferred_element_type=F32)
        up = jnp.dot(xn, wu_ref[0], preferred_element_type=F32)
        mid = (jax.nn.silu(gate) * up).astype(BF16)
        down = jnp.dot(mid, wd_ref[0], preferred_element_type=F32)
        if position == "first":
            acc_scr[0:rows, :] = down
        elif position == "middle":
            acc_scr[0:rows, :] += down
        else:
            _store_token_major(y_scr.at[slot], acc_scr[0:rows, :] + down)

    few_rows = tile_rows(i) <= T // 2

    for few, rows in ((False, T), (True, T // 2)):
        branch = valid & (few_rows if few else jnp.logical_not(few_rows))

        @pl.when(branch & (c == 0))
        def _():
            xn_scr[0:rows, :] = _rmsnorm(_load_token_major(x_scr.at[slot], rows), g_ref[...]).astype(BF16)
            row_dmas()
            expert_chunk("first", rows)

        if N_MOE_CHUNKS > 2:
            @pl.when(branch & (c > 0) & (c < n_chunks - 1))
            def _():
                expert_chunk("middle", rows)

        @pl.when(branch & (c == n_chunks - 1))
        def _():
            expert_chunk("last", rows)

    @pl.when(jnp.logical_not(valid) & (c == 0))
    def _():
        row_dmas()

    @pl.when(c == n_chunks - 1)
    def _():
        wait_scatter()

    @pl.when((i == n_tiles - 1) & (c == n_chunks - 1))
    def _():
        wait_gather(other)

        def issue(r, carry):
            row_copy(y_scr.at[slot], r * R, y_hbm,
                     dest_row(dstl_ref, window_offset(n_tiles - 1), r, tile_rows(n_tiles - 1)),
                     sems.at[SCATTER_SEM]).start()
            return carry
        lax.fori_loop(0, T, issue, 0)
        wait_scatter()


def _moe_experts(h1_tm, g, tile_expert, n_valid_tiles, src_rows, dst_rows, w_gate, w_up, w_down, n_out_tokens):
    D = D_MODEL
    T = MOE_TILE
    NT = tile_expert.shape[0]
    NC = N_MOE_CHUNKS
    assert NC >= 2
    FC = D_FF // NC

    def chunk(i, c, nvt):
        return jnp.where(i < nvt[0], c, NC - 1)

    def window_spec(tile_of_step):
        return pl.BlockSpec((pl.Element(1), pl.Element(T + LANES)),
                            lambda i, c, te, nvt: (0, nvt[1 + NT + tile_of_step(i)] * LANES),
                            memory_space=pltpu.SMEM)

    grid_spec = pltpu.PrefetchScalarGridSpec(
        num_scalar_prefetch=2,
        grid=(NT, NC),
        in_specs=[
            window_spec(lambda i: 0),
            window_spec(lambda i: jnp.minimum(i + 1, NT - 1)),
            window_spec(lambda i: jnp.maximum(i - 1, 0)),
            window_spec(lambda i: NT - 1),
            pl.BlockSpec(memory_space=pl.ANY),
            pl.BlockSpec((1, D), lambda i, c, te, nvt: (0, 0)),
            pl.BlockSpec((1, D, FC), lambda i, c, te, nvt: (te[i], 0, chunk(i, c, nvt))),
            pl.BlockSpec((1, D, FC), lambda i, c, te, nvt: (te[i], 0, chunk(i, c, nvt))),
            pl.BlockSpec((1, FC, D), lambda i, c, te, nvt: (te[i], chunk(i, c, nvt), 0)),
        ],
        out_specs=pl.BlockSpec(memory_space=pl.ANY),
        scratch_shapes=[pltpu.VMEM((2, T * ROWS_PER_TOKEN, LANES), F32), pltpu.VMEM((T, D), BF16),
                        pltpu.VMEM((T, D), F32), pltpu.VMEM((2, T * ROWS_PER_TOKEN, LANES), F32),
                        pltpu.SemaphoreType.DMA((3,))],
    )
    return pl.pallas_call(
        _moe_kernel,
        grid_spec=grid_spec,
        out_shape=jax.ShapeDtypeStruct((n_out_tokens * ROWS_PER_TOKEN, LANES), F32),
        compiler_params=pltpu.CompilerParams(dimension_semantics=("arbitrary", "arbitrary"),
                                             vmem_limit_bytes=VMEM_LIMIT),
        name="moe_experts",
    )(tile_expert, n_valid_tiles, src_rows, src_rows, dst_rows, dst_rows, h1_tm, g.reshape(1, D),
      w_gate, w_up, w_down)


def _moe_plan(route, n_tokens):
    T = MOE_TILE
    n_assign = TOP_K * n_tokens
    n_tiles = n_assign // T + N_EXPERTS
    e_flat = route[:, 0:TOP_K].astype(jnp.int32).reshape(-1)
    counts = jnp.sum((e_flat[:, None] == jnp.arange(N_EXPERTS, dtype=jnp.int32)[None, :]).astype(jnp.int32), axis=0)
    order = jnp.argsort(e_flat, stable=True).astype(jnp.int32)
    tiles_per = (counts + T - 1) // T
    tile_end = jnp.cumsum(tiles_per)
    n_valid = tile_end[-1]
    tile_ids = jnp.arange(n_tiles, dtype=jnp.int32)
    last_valid = jnp.maximum(n_valid - 1, 0)
    texp = jnp.sum((jnp.minimum(tile_ids, last_valid)[:, None] >= tile_end[None, :]).astype(jnp.int32), axis=1)
    texp = jnp.minimum(texp, N_EXPERTS - 1)
    onehot = (texp[:, None] == jnp.arange(N_EXPERTS, dtype=jnp.int32)[None, :]).astype(jnp.int32)
    pick = lambda v: jnp.sum(onehot * v[None, :], axis=1)
    j = tile_ids - pick(tile_end - tiles_per)
    start = jnp.where(tile_ids < n_valid, pick(jnp.cumsum(counts) - counts) + j * T, 0)
    n_rows = jnp.where(tile_ids < n_valid, jnp.clip(pick(counts) - j * T, 0, T), 0)
    token = order // TOP_K
    slot = order % TOP_K
    spare = jnp.zeros((T + LANES,), jnp.int32)
    src_sorted = jnp.concatenate([token * ROWS_PER_TOKEN, spare]).reshape(1, -1)
    dst_sorted = jnp.concatenate([(slot * n_tokens + token) * ROWS_PER_TOKEN, spare]).reshape(1, -1)
    tile_counts = jnp.concatenate([n_valid.reshape(1), n_rows, start // LANES, start % LANES]).astype(jnp.int32)
    return texp, tile_counts, src_sorted, dst_sorted


def _combine_kernel(h_ref, y0_ref, y1_ref, route_ref, g_ref, o_ref):
    T = o_ref.shape[0]
    route = route_ref[...]
    moe = route[:, 2:3] * _load_token_major(y0_ref, T) + route[:, 3:4] * _load_token_major(y1_ref, T)
    o_ref[...] = _rmsnorm(_load_token_major(h_ref, T) + moe, g_ref[...])


def _combine(h1_tm, y2_tm, route, g):
    D = D_MODEL
    S = route.shape[0]
    T = OUT_TILE
    nb = S // T
    tm_block = (T * ROWS_PER_TOKEN, LANES)
    return pl.pallas_call(
        _combine_kernel,
        grid=(nb,),
        in_specs=[
            pl.BlockSpec(tm_block, lambda i: (i, 0)),
            pl.BlockSpec(tm_block, lambda i: (i, 0)),
            pl.BlockSpec(tm_block, lambda i: (i + nb, 0)),
            pl.BlockSpec((T, LANES), lambda i: (i, 0)),
            pl.BlockSpec((1, D), lambda i: (0, 0)),
        ],
        out_specs=pl.BlockSpec((T, D), lambda i: (i, 0)),
        out_shape=jax.ShapeDtypeStruct((S, D), F32),
        compiler_params=pltpu.CompilerParams(dimension_semantics=("arbitrary",), vmem_limit_bytes=VMEM_LIMIT),
        name="moe_combine",
    )(h1_tm, y2_tm, y2_tm, route, g.reshape(1, D))


def kernel(x, positions, norm_mix, norm_ffn, norm_final, lru_w_in, lru_conv_w, lru_conv_b, lru_w_a, lru_b_a, lru_w_x, lru_b_x, lru_lambda, lru_w_out, attn_w_qkv, attn_b_qkv, attn_sinks, attn_w_o, attn_b_o, ffn_w_gate, ffn_w_up, ffn_w_down, moe_w_router, moe_w_gate, moe_w_up, moe_w_down):
    B, S, D = x.shape
    assert B == 1 and D == D_MODEL and S % FFN_TILE == 0
    h = x.reshape(S, D)

    w_ax = jnp.concatenate([lru_w_a[0], lru_w_x[0]], axis=-1).astype(BF16)
    h, ffn_wg, ffn_wu, ffn_wd = _lru_mixer(h, norm_mix[0], lru_w_in[0].astype(BF16), lru_conv_w[0], lru_conv_b[0],
                                           w_ax, lru_b_a[0], lru_b_x[0], lru_lambda[0], lru_w_out[0].astype(BF16),
                                           ffn_w_gate[0], ffn_w_up[0], ffn_w_down[0])
    h, moe_wg, moe_wu, moe_wd = _dense_ffn(h, norm_ffn[0], ffn_wg, ffn_wu, ffn_wd,
                                           moe_w_gate[0], moe_w_up[0], moe_w_down[0])

    q_dim = N_HEADS * HEAD_DIM
    kv_dim = N_KV_HEADS * HEAD_DIM

    def dup_heads(w):
        w4 = w.reshape(w.shape[:-1] + (N_KV_HEADS, 1, HEAD_DIM))
        return jnp.broadcast_to(w4, w.shape[:-1] + (N_KV_HEADS, 2, HEAD_DIM)).reshape(w.shape[:-1] + (KV_EXT,))

    wqkv, bqkv = attn_w_qkv[0], attn_b_qkv[0]
    w_qkv_ext = jnp.concatenate([wqkv[:, :q_dim], dup_heads(wqkv[:, q_dim:q_dim + kv_dim]),
                                 dup_heads(wqkv[:, q_dim + kv_dim:])], axis=1).astype(BF16)
    b_qkv_ext = jnp.concatenate([bqkv[:q_dim], dup_heads(bqkv[q_dim:q_dim + kv_dim]),
                                 dup_heads(bqkv[q_dim + kv_dim:])]).reshape(1, QKV_EXT)
    inv_freq = ROPE_THETA ** (-jnp.arange(0, ROT_DIM, 2, dtype=F32) / ROT_DIM)
    d_in_head = jnp.arange(LANES) % HEAD_DIM
    freq_lane = jnp.where(d_in_head < ROT_DIM, inv_freq[d_in_head % (ROT_DIM // 2)], 0.0).reshape(1, LANES)
    sign_lane = jnp.where(d_in_head < ROT_DIM // 2, -1.0, jnp.where(d_in_head < ROT_DIM, 1.0, 0.0))
    sign_lane = sign_lane.astype(F32).reshape(1, LANES)
    w_router_pad = jnp.pad(moe_w_router[0], ((0, 0), (0, LANES - N_EXPERTS))).astype(BF16)
    pos_f = positions.reshape(S, 1).astype(F32)
    h1_tm, route = _attn_mixer(h, pos_f, norm_mix[1], freq_lane, sign_lane, w_qkv_ext, b_qkv_ext,
                               attn_sinks[0], attn_w_o[0].astype(BF16), attn_b_o[0], norm_ffn[1], w_router_pad)

    texp, n_valid, src_rows, dst_rows = _moe_plan(route, S)
    y2_tm = _moe_experts(h1_tm, norm_ffn[1], texp, n_valid, src_rows, dst_rows,
                         moe_wg, moe_wu, moe_wd,
                         TOP_K * S + MOE_TILE)
    out = _combine(h1_tm, y2_tm, route, norm_final)
    return out.reshape(B, S, D)
```

```python
import jax
import jax.numpy as jnp
from jax import lax
from jax.experimental import pallas as pl
from jax.experimental.pallas import tpu as pltpu

F32 = jnp.float32
BF16 = jnp.bfloat16

D_MODEL = 1024
N_RNN_BLOCKS = 8
RNN_BLOCK = D_MODEL // N_RNN_BLOCKS
CONV_WIDTH = 4
LRU_C = 8.0
N_HEADS = 16
N_KV_HEADS = 4
HEAD_DIM = 64
GROUP = N_HEADS // N_KV_HEADS
WINDOW = 128
ATTN_BLOCK = 128
assert WINDOW == ATTN_BLOCK
ROPE_THETA = 500000.0
ROT_DIM = HEAD_DIM // 4
D_FF = 3584
N_EXPERTS = 8
TOP_K = 2
EPS = 1e-6
NEG_INF = -1e30
LOG2_E = 1.4426950408889634

LANES = 128
SUBLANES = 8
VMEM_LIMIT = 56 * 1024 * 1024

SEQ_TILE = 512
ATTN_TILE = 1024
LRU_CHUNKS = 4
FFN_TILE = 1024
FF_CHUNK = 512
N_FF_CHUNKS = D_FF // FF_CHUNK
MOE_TILE = 512
N_MOE_CHUNKS = 2
OUT_TILE = 1024


def _rmsnorm(x, g):
    return x * lax.rsqrt(jnp.mean(x * x, axis=-1, keepdims=True) + EPS) * g


ROWS_PER_TOKEN = D_MODEL // LANES


def _store_token_major(ref, val):
    T = val.shape[0]
    for c in range(ROWS_PER_TOKEN):
        ref[pl.ds(c, T, stride=ROWS_PER_TOKEN), :] = val[:, LANES * c:LANES * (c + 1)]


def _load_token_major(ref, T):
    return jnp.concatenate([ref[pl.ds(c, T, stride=ROWS_PER_TOKEN), :] for c in range(ROWS_PER_TOKEN)], axis=1)


def _const_spec(shape):
    n = len(shape)
    return pl.BlockSpec(shape, lambda *_: (0,) * n, pipeline_mode=pl.Buffered(1))


def _time_permutation(T):
    rho = jnp.arange(T)
    t_of_rho = (rho % SUBLANES) * (T // SUBLANES) + rho // SUBLANES
    perm = (t_of_rho[:, None] == jnp.arange(T)[None, :]).astype(BF16)
    return perm, perm.T


def _lru_kernel(xin_ref, xres_ref, g_ref, perm_ref, unperm_ref, win_ref, cw_ref, cb_ref, wax_ref, ba_ref, bx_ref,
                lam_ref, wout_ref, cast_g_ref, cast_u_ref, cast_d_ref,
                o_ref, bf_g_ref, bf_u_ref, bf_d_ref, proj_scr, tail_scr, carry_scr):
    i = pl.program_id(0)
    n_tiles = pl.num_programs(0) - 1
    T = xin_ref.shape[0]
    D = D_MODEL
    C = T // LRU_CHUNKS
    GC = C // SUBLANES
    G = T // SUBLANES
    HALO = (CONV_WIDTH - 1) * SUBLANES

    def cast_slices():
        bf_g_ref[...] = cast_g_ref[...].astype(BF16)
        bf_u_ref[...] = cast_u_ref[...].astype(BF16)
        bf_d_ref[...] = cast_d_ref[...].astype(BF16)

    @pl.when(i == 0)
    def _():
        tail_scr[...] = jnp.zeros_like(tail_scr)
        carry_scr[...] = jnp.zeros_like(carry_scr)

    PW = 2 * D // LRU_CHUNKS

    def project_chunks():
        hn = _rmsnorm(xin_ref[...], g_ref[...]).astype(BF16)
        hp = jnp.dot(perm_ref[...], hn, preferred_element_type=F32).astype(BF16)
        for k in range(LRU_CHUNKS):
            yield jnp.dot(hp, win_ref[:, PW * k:PW * (k + 1)], preferred_element_type=F32)

    def recur_chunks():
        sub = lax.broadcasted_iota(jnp.int32, (SUBLANES, D), 0)
        cw = cw_ref[...]
        softplus_neg_lam = jax.nn.softplus(-lam_ref[...])
        last = proj_scr[T - HALO:T, D:]
        prev_groups = []
        for k in range(CONV_WIDTH - 1):
            rows = slice(SUBLANES * k, SUBLANES * (k + 1))
            prev_groups.append(pltpu.roll(jnp.where(sub == SUBLANES - 1, tail_scr[rows, :], last[rows, :]), 1, 0))
        tail_scr[...] = last

        ys, hs, decay = [], [], []
        for k in range(LRU_CHUNKS):
            pc = proj_scr[C * k:C * (k + 1), :]
            ys.append(jax.nn.gelu(pc[:, :D], approximate=True))
            xb = pc[:, D:]
            xc = cb_ref[...] + xb * cw[CONV_WIDTH - 1:CONV_WIDTH, :]
            for back in range(1, CONV_WIDTH):
                shifted = jnp.concatenate(prev_groups[CONV_WIDTH - 1 - back:] + [xb[:C - SUBLANES * back, :]], axis=0)
                xc = xc + shifted * cw[CONV_WIDTH - 1 - back:CONV_WIDTH - back, :]
            prev_groups = [xb[C - HALO + SUBLANES * q:C - HALO + SUBLANES * (q + 1), :] for q in range(CONV_WIDTH - 1)]

            xcb = xc.astype(BF16)
            r_parts, i_parts = [], []
            for n in range(N_RNN_BLOCKS):
                gn = jnp.dot(xcb[:, RNN_BLOCK * n:RNN_BLOCK * (n + 1)], wax_ref[n], preferred_element_type=F32)
                r_parts.append(gn[:, :RNN_BLOCK])
                i_parts.append(gn[:, RNN_BLOCK:])
            r = jax.nn.sigmoid(jnp.concatenate(r_parts, axis=1) + ba_ref[...])
            ig = jax.nn.sigmoid(jnp.concatenate(i_parts, axis=1) + bx_ref[...])
            log_a = -LRU_C * r * softplus_neg_lam
            a = jnp.exp(log_a)
            z = jnp.tanh(-log_a) * (1.0 + a * a)
            b = jnp.where(z > 0.0, z * lax.rsqrt(z), 0.0) * (ig * xc)

            for j in range(GC):
                rows = slice(SUBLANES * j, SUBLANES * (j + 1))
                if hs:
                    hs.append(a[rows, :] * hs[-1] + b[rows, :])
                    decay.append(a[rows, :] * decay[-1])
                else:
                    hs.append(b[rows, :])
                    decay.append(a[rows, :])
            yield None

        p_inc, e_inc = decay[-1], hs[-1]
        for k in (1, 2, 4):
            m = sub >= k
            p_sh = jnp.where(m, pltpu.roll(p_inc, k, 0), 1.0)
            e_sh = jnp.where(m, pltpu.roll(e_inc, k, 0), 0.0)
            e_inc = e_inc + p_inc * e_sh
            p_inc = p_inc * p_sh
        h0 = carry_scr[...]
        after = p_inc * h0 + e_inc
        carry_scr[...] = after[SUBLANES - 1:SUBLANES, :]
        enter = jnp.where(sub == 0, h0, pltpu.roll(after, 1, 0))
        h = jnp.concatenate([hs[j] + decay[j] * enter for j in range(G)], axis=0)
        hy = (h * jnp.concatenate(ys, axis=0)).astype(BF16)
        hy = jnp.dot(unperm_ref[...], hy, preferred_element_type=F32).astype(BF16)
        yield jnp.dot(hy, wout_ref[...], preferred_element_type=F32) + xres_ref[...]

    @pl.when(i == 0)
    def _():
        cast_slices()
        for k, pc in enumerate(project_chunks()):
            proj_scr[:, PW * k:PW * (k + 1)] = pc

    @pl.when((i > 0) & (i < n_tiles))
    def _():
        cast_slices()
        rec = recur_chunks()
        new_proj = []
        for pc in project_chunks():
            new_proj.append(pc)
            next(rec)
        o_ref[...] = next(rec)
        for k, pc in enumerate(new_proj):
            proj_scr[:, PW * k:PW * (k + 1)] = pc

    @pl.when(i == n_tiles)
    def _():
        o_ref[...] = list(recur_chunks())[-1]


def _lru_mixer(x, g, w_in, conv_w, conv_b, w_ax, b_a, b_x, lam, w_out, cast_gate, cast_up, cast_down):
    S, D = x.shape
    T = SEQ_TILE
    n_tiles = S // T
    row = lambda v: v.reshape(1, D)
    perm, unperm = _time_permutation(T)

    def cast_spec(w):
        return pl.BlockSpec((w.shape[0] // n_tiles, w.shape[1]), lambda i: (jnp.minimum(i, n_tiles - 1), 0))

    casts = (cast_gate, cast_up, cast_down)
    return pl.pallas_call(
        _lru_kernel,
        grid=(n_tiles + 1,),
        in_specs=[
            pl.BlockSpec((T, D), lambda i: (jnp.minimum(i, n_tiles - 1), 0)),
            pl.BlockSpec((T, D), lambda i: (jnp.maximum(i - 1, 0), 0)),
            _const_spec((1, D)),
            _const_spec((T, T)),
            _const_spec((T, T)),
            _const_spec((D, 2 * D)),
            _const_spec((CONV_WIDTH, D)),
            _const_spec((1, D)),
            _const_spec((N_RNN_BLOCKS, RNN_BLOCK, 2 * RNN_BLOCK)),
            _const_spec((1, D)),
            _const_spec((1, D)),
            _const_spec((1, D)),
            _const_spec((D, D)),
        ] + [cast_spec(w) for w in casts],
        out_specs=[pl.BlockSpec((T, D), lambda i: (jnp.maximum(i - 1, 0), 0))] + [cast_spec(w) for w in casts],
        out_shape=[jax.ShapeDtypeStruct((S, D), F32)] + [jax.ShapeDtypeStruct(w.shape, BF16) for w in casts],
        scratch_shapes=[pltpu.VMEM((T, 2 * D), F32), pltpu.VMEM(((CONV_WIDTH - 1) * SUBLANES, D), F32),
                        pltpu.VMEM((1, D), F32)],
        compiler_params=pltpu.CompilerParams(dimension_semantics=("arbitrary",), vmem_limit_bytes=VMEM_LIMIT),
        name="lru_mixer",
    )(x, x, row(g), perm, unperm, w_in, conv_w, row(conv_b), w_ax, row(b_a), row(b_x), row(lam), w_out, *casts)


CAST_RING = 3


def _ffn_kernel(x_ref, g_ref, wg_ref, wu_ref, wd_ref, cast_g_hbm, cast_u_hbm, cast_d_hbm,
                o_ref, bf_g_ref, bf_u_ref, bf_d_ref, xn_scr, acc_scr, ring_g, ring_u, ring_d, ring_sem):
    c = pl.program_id(1)
    last = pl.num_programs(1) - 1
    step = pl.program_id(0) * N_FF_CHUNKS + c
    n_steps = pl.num_programs(0) * N_FF_CHUNKS
    _, gu_rows, gu_cols = ring_g.shape
    _, d_rows, _ = ring_d.shape
    row_parts = cast_g_hbm.shape[1] // gu_rows

    def cast_copies(t):
        slot = t % CAST_RING
        e = t // (row_parts * N_FF_CHUNKS)
        r0 = pl.multiple_of(((t // N_FF_CHUNKS) % row_parts) * gu_rows, gu_rows)
        c0 = pl.multiple_of((t % N_FF_CHUNKS) * gu_cols, gu_cols)
        d0 = pl.multiple_of((t % (row_parts * N_FF_CHUNKS)) * d_rows, d_rows)
        return (
            pltpu.make_async_copy(cast_g_hbm.at[e, pl.ds(r0, gu_rows), pl.ds(c0, gu_cols)], ring_g.at[slot],
                                  ring_sem.at[0, slot]),
            pltpu.make_async_copy(cast_u_hbm.at[e, pl.ds(r0, gu_rows), pl.ds(c0, gu_cols)], ring_u.at[slot],
                                  ring_sem.at[1, slot]),
            pltpu.make_async_copy(cast_d_hbm.at[e, pl.ds(d0, d_rows), :], ring_d.at[slot], ring_sem.at[2, slot]),
        )

    @pl.when(step == 0)
    def _():
        for t in range(CAST_RING - 1):
            for cp in cast_copies(t):
                cp.start()

    @pl.when(step + (CAST_RING - 1) < n_steps)
    def _():
        for cp in cast_copies(step + (CAST_RING - 1)):
            cp.start()

    for cp in cast_copies(step):
        cp.wait()
    slot = step % CAST_RING

    def ff_chunk():
        bf_g_ref[0] = ring_g[slot].astype(BF16)
        bf_u_ref[0] = ring_u[slot].astype(BF16)
        bf_d_ref[0] = ring_d[slot].astype(BF16)
        xn = xn_scr[...]
        gate = jnp.dot(xn, wg_ref[...], preferred_element_type=F32)
        up = jnp.dot(xn, wu_ref[...], preferred_element_type=F32)
        mid = (jax.nn.silu(gate) * up).astype(BF16)
        return jnp.dot(mid, wd_ref[...], preferred_element_type=F32)

    @pl.when(c == 0)
    def _():
        xn_scr[...] = _rmsnorm(x_ref[...], g_ref[...]).astype(BF16)
        acc_scr[...] = ff_chunk()

    @pl.when((c > 0) & (c < last))
    def _():
        acc_scr[...] += ff_chunk()

    @pl.when(c == last)
    def _():
        o_ref[...] = x_ref[...] + (acc_scr[...] + ff_chunk())


def _dense_ffn(x, g, w_gate, w_up, w_down, cast_gate, cast_up, cast_down):
    S, D = x.shape
    T = FFN_TILE
    n_steps = (S // T) * N_FF_CHUNKS
    E = cast_gate.shape[0]
    row_parts = n_steps // (E * N_FF_CHUNKS)
    assert row_parts * E * N_FF_CHUNKS == n_steps
    gu_block = (1, D // row_parts, D_FF // N_FF_CHUNKS)
    d_block = (1, D_FF // (row_parts * N_FF_CHUNKS), D)

    def gu_map(i, c):
        s = i * N_FF_CHUNKS + c
        return (s // (row_parts * N_FF_CHUNKS), (s // N_FF_CHUNKS) % row_parts, c)

    def d_map(i, c):
        s = i * N_FF_CHUNKS + c
        return (s // (row_parts * N_FF_CHUNKS), s % (row_parts * N_FF_CHUNKS), 0)

    return pl.pallas_call(
        _ffn_kernel,
        grid=(S // T, N_FF_CHUNKS),
        in_specs=[
            pl.BlockSpec((T, D), lambda i, c: (i, 0)),
            pl.BlockSpec((1, D), lambda i, c: (0, 0)),
            pl.BlockSpec((D, FF_CHUNK), lambda i, c: (0, c)),
            pl.BlockSpec((D, FF_CHUNK), lambda i, c: (0, c)),
            pl.BlockSpec((FF_CHUNK, D), lambda i, c: (c, 0)),
            pl.BlockSpec(memory_space=pl.ANY),
            pl.BlockSpec(memory_space=pl.ANY),
            pl.BlockSpec(memory_space=pl.ANY),
        ],
        out_specs=[pl.BlockSpec((T, D), lambda i, c: (i, 0)), pl.BlockSpec(gu_block, gu_map),
                   pl.BlockSpec(gu_block, gu_map), pl.BlockSpec(d_block, d_map)],
        out_shape=[jax.ShapeDtypeStruct((S, D), F32), jax.ShapeDtypeStruct(cast_gate.shape, BF16),
                   jax.ShapeDtypeStruct(cast_up.shape, BF16), jax.ShapeDtypeStruct(cast_down.shape, BF16)],
        scratch_shapes=[pltpu.VMEM((T, D), BF16), pltpu.VMEM((T, D), F32),
                        pltpu.VMEM((CAST_RING,) + gu_block[1:], F32), pltpu.VMEM((CAST_RING,) + gu_block[1:], F32),
                        pltpu.VMEM((CAST_RING,) + d_block[1:], F32), pltpu.SemaphoreType.DMA((3, CAST_RING))],
        compiler_params=pltpu.CompilerParams(dimension_semantics=("arbitrary", "arbitrary"),
                                             vmem_limit_bytes=VMEM_LIMIT),
        name="dense_ffn",
    )(x, g.reshape(1, D), w_gate, w_up, w_down, cast_gate, cast_up, cast_down)


KV_EXT = N_KV_HEADS * LANES
QKV_EXT = N_HEADS * HEAD_DIM + 2 * KV_EXT


def _attn_kernel(sink_ref, x_ref, pos_ref, g_ref, freq_ref, sign_ref, wqkv_ref, bqkv_ref, wo_ref, bo_ref,
                 g2_ref, wr_ref, o_ref, route_ref,
                 qlo_scr, qhi_scr, k_scr, v_scr, o_scr):
    T = x_ref.shape[0]
    D = D_MODEL
    B = ATTN_BLOCK
    first_tile = pl.program_id(0) == 0

    @pl.when(first_tile)
    def _():
        k_scr[0:B, :] = jnp.zeros((B, KV_EXT), BF16)
        v_scr[0:B, :] = jnp.zeros((B, KV_EXT), BF16)

    x = x_ref[...]
    hn = _rmsnorm(x, g_ref[...]).astype(BF16)
    qkv = jnp.dot(hn, wqkv_ref[...], preferred_element_type=F32) + bqkv_ref[...]

    ang = pos_ref[...] * freq_ref[...]
    cos_t = jnp.cos(ang)
    sin_t = jnp.sin(ang) * sign_ref[...]
    lane = lax.broadcasted_iota(jnp.int32, (T, LANES), 1)
    first_half = (lane % HEAD_DIM) < (ROT_DIM // 2)
    lo_half = lane < HEAD_DIM

    def rope(col):
        partner = jnp.where(first_half, pltpu.roll(col, LANES - ROT_DIM // 2, 1), pltpu.roll(col, ROT_DIM // 2, 1))
        return col * cos_t + partner * sin_t

    scale = HEAD_DIM ** -0.5 * LOG2_E
    for c in range(D // LANES):
        qc = rope(qkv[:, LANES * c:LANES * (c + 1)]) * scale
        qlo_scr[:, LANES * c:LANES * (c + 1)] = jnp.where(lo_half, qc, 0.0).astype(BF16)
        qhi_scr[:, LANES * c:LANES * (c + 1)] = jnp.where(lo_half, 0.0, qc).astype(BF16)
    for g in range(N_KV_HEADS):
        kc = rope(qkv[:, D + LANES * g:D + LANES * (g + 1)])
        k_scr[B:B + T, LANES * g:LANES * (g + 1)] = kc.astype(BF16)
        vc = qkv[:, D + KV_EXT + LANES * g:D + KV_EXT + LANES * (g + 1)]
        v_scr[B:B + T, LANES * g:LANES * (g + 1)] = vc.astype(BF16)

    qi = lax.broadcasted_iota(jnp.int32, (B, B), 0)
    kj = lax.broadcasted_iota(jnp.int32, (B, B), 1)
    causal_own = kj <= qi
    window_prev = kj > qi
    key0 = kj == 0
    key0_row = lax.broadcasted_iota(jnp.int32, (1, B), 1) == 0
    lo_blk = lax.broadcasted_iota(jnp.int32, (B, LANES), 1) < HEAD_DIM

    def block_body(blk, _):
        r0 = pl.multiple_of(blk * B, B)
        k_lo = jnp.where(first_tile & (blk == 0), B, 0)
        allowed_prev = window_prev & (kj >= k_lo)
        for g in range(N_KV_HEADS):
            kk = k_scr[pl.ds(r0, 2 * B), LANES * g:LANES * (g + 1)]
            vv = v_scr[pl.ds(r0, 2 * B), LANES * g:LANES * (g + 1)]
            cols = [slice(LANES * c, LANES * (c + 1)) for c in range((GROUP // 2) * g, (GROUP // 2) * (g + 1))]
            q_all = jnp.concatenate([q_scr[pl.ds(r0, B), col] for col in cols for q_scr in (qlo_scr, qhi_scr)], axis=0)
            s_all = lax.dot_general(q_all, kk, (((1,), (1,)), ((), ())), preferred_element_type=F32)
            p_all, denoms = [], []
            for hh in range(GROUP):
                rows = slice(B * hh, B * (hh + 1))
                sink_fill = jnp.where(key0_row, sink_ref[GROUP * g + hh] * LOG2_E, NEG_INF)
                s_prev = jnp.where(allowed_prev, s_all[rows, :B], sink_fill)
                s_own = jnp.where(causal_own, s_all[rows, B:], NEG_INF)
                m = jnp.max(jnp.maximum(s_prev, s_own), axis=-1, keepdims=True)
                p_prev = jnp.exp2(s_prev - m)
                p_own = jnp.exp2(s_own - m)
                denoms.append(jnp.sum(p_prev + p_own, axis=-1, keepdims=True))
                p_all.append(jnp.concatenate([jnp.where(key0, 0.0, p_prev), p_own], axis=1).astype(BF16))
            o_all = jnp.dot(jnp.concatenate(p_all, axis=0), vv, preferred_element_type=F32)
            o_heads = [o_all[B * hh:B * (hh + 1), :] / denoms[hh] for hh in range(GROUP)]
            for p, col in enumerate(cols):
                o_scr[pl.ds(r0, B), col] = jnp.where(lo_blk, o_heads[2 * p], o_heads[2 * p + 1]).astype(BF16)
        return 0

    lax.fori_loop(0, T // B, block_body, 0)

    k_scr[0:B, :] = k_scr[T:T + B, :]
    v_scr[0:B, :] = v_scr[T:T + B, :]

    h1 = jnp.dot(o_scr[...], wo_ref[...], preferred_element_type=F32) + bo_ref[...] + x
    _store_token_major(o_ref, h1)

    hn2 = _rmsnorm(h1, g2_ref[...]).astype(BF16)
    logits = jnp.dot(hn2, wr_ref[...], preferred_element_type=F32)
    lane_f = lane.astype(F32)
    lg = jnp.where(lane < N_EXPERTS, logits, -jnp.inf)
    v1 = jnp.max(lg, axis=-1, keepdims=True)
    i1 = jnp.min(jnp.where(lg == v1, lane_f, float(LANES)), axis=-1, keepdims=True)
    lg2 = jnp.where(lane_f == i1, -jnp.inf, lg)
    v2 = jnp.max(lg2, axis=-1, keepdims=True)
    i2 = jnp.min(jnp.where(lg2 == v2, lane_f, float(LANES)), axis=-1, keepdims=True)
    e2 = jnp.exp(v2 - v1)
    gate1 = 1.0 / (1.0 + e2)
    gate2 = e2 / (1.0 + e2)
    route_ref[...] = jnp.where(lane == 0, i1, jnp.where(lane == 1, i2,
                                                        jnp.where(lane == 2, gate1, jnp.where(lane == 3, gate2, 0.0))))


def _attn_mixer(x, pos_f, g, freq_lane, sign_lane, w_qkv_ext, b_qkv_ext, sinks, w_o, b_o, g2, w_router_pad):
    S, D = x.shape
    T = ATTN_TILE
    B = ATTN_BLOCK
    grid_spec = pltpu.PrefetchScalarGridSpec(
        num_scalar_prefetch=1,
        grid=(S // T,),
        in_specs=[
            pl.BlockSpec((T, D), lambda i, s: (i, 0)),
            pl.BlockSpec((T, 1), lambda i, s: (i, 0)),
            _const_spec((1, D)),
            _const_spec((1, LANES)),
            _const_spec((1, LANES)),
            _const_spec((D, QKV_EXT)),
            _const_spec((1, QKV_EXT)),
            _const_spec((D, D)),
            _const_spec((1, D)),
            _const_spec((1, D)),
            _const_spec((D, LANES)),
        ],
        out_specs=[pl.BlockSpec((T * ROWS_PER_TOKEN, LANES), lambda i, s: (i, 0)),
                   pl.BlockSpec((T, LANES), lambda i, s: (i, 0))],
        scratch_shapes=[
            pltpu.VMEM((T, D), BF16), pltpu.VMEM((T, D), BF16),
            pltpu.VMEM((T + B, KV_EXT), BF16), pltpu.VMEM((T + B, KV_EXT), BF16),
            pltpu.VMEM((T, D), BF16),
        ],
    )
    return pl.pallas_call(
        _attn_kernel,
        grid_spec=grid_spec,
        out_shape=[jax.ShapeDtypeStruct((S * ROWS_PER_TOKEN, LANES), F32), jax.ShapeDtypeStruct((S, LANES), F32)],
        compiler_params=pltpu.CompilerParams(dimension_semantics=("arbitrary",), vmem_limit_bytes=VMEM_LIMIT),
        name="attn_mixer",
    )(sinks, x, pos_f, g.reshape(1, D), freq_lane, sign_lane, w_qkv_ext, b_qkv_ext, w_o, b_o.reshape(1, D),
      g2.reshape(1, D), w_router_pad)


def _moe_kernel(texp_ref, nvt_ref, src0_ref, srcn_ref, dstp_ref, dstl_ref, h_hbm, g_ref, wg_ref, wu_ref, wd_ref,
                y_hbm, x_scr, xn_scr, acc_scr, y_scr, sems):
    i = pl.program_id(0)
    c = pl.program_id(1)
    n_tiles = pl.num_programs(0)
    n_chunks = pl.num_programs(1)
    T = xn_scr.shape[0]
    R = ROWS_PER_TOKEN
    valid = i < nvt_ref[0]
    slot = i % 2
    other = 1 - slot
    SCATTER_SEM = 2

    def row_copy(src, src_row, dst, dst_row, sem):
        return pltpu.make_async_copy(src.at[pl.ds(pl.multiple_of(src_row, R), R), :],
                                     dst.at[pl.ds(pl.multiple_of(dst_row, R), R), :], sem)

    def tile_rows(t):
        return nvt_ref[1 + t]

    def window_offset(t):
        return nvt_ref[1 + 2 * n_tiles + t]

    prev_tile = jnp.maximum(i - 1, 0)
    next_tile = jnp.minimum(i + 1, n_tiles - 1)
    rows_prev = jnp.where(i > 0, tile_rows(prev_tile), 0)
    pad_base = y_hbm.shape[0] - T * R

    def dest_row(window_ref, offset, r, n_rows):
        return jnp.where(r < n_rows, window_ref[0, offset + r], pad_base + r * R)

    def wait_gather(s):
        pltpu.make_async_copy(h_hbm.at[pl.ds(0, T * R), :], x_scr.at[s], sems.at[s]).wait()

    def wait_scatter():
        pltpu.make_async_copy(y_scr.at[0], y_hbm.at[pl.ds(0, T * R), :], sems.at[SCATTER_SEM]).wait()

    @pl.when((i == 0) & (c == 0))
    def _():
        y_scr[...] = jnp.zeros_like(y_scr)

        def issue(r, carry):
            row_copy(h_hbm, src0_ref[0, window_offset(0) + r], x_scr.at[0], r * R, sems.at[0]).start()
            return carry
        lax.fori_loop(0, T, issue, 0)

    @pl.when(c == 0)
    def _():
        wait_gather(slot)

    def row_dmas():
        off_next = window_offset(next_tile)
        off_prev = window_offset(prev_tile)
        for r in range(T):
            row_copy(h_hbm, srcn_ref[0, off_next + r], x_scr.at[other], r * R, sems.at[other]).start()
            row_copy(y_scr.at[other], r * R, y_hbm, dest_row(dstp_ref, off_prev, r, rows_prev),
                     sems.at[SCATTER_SEM]).start()

    def expert_chunk(position, rows):
        xn = xn_scr[0:rows, :]
        gate = jnp.dot(xn, wg_ref[0], preferred_element_type=F32)
        up = jnp.dot(xn, wu_ref[0], preferred_element_type=F32)
        mid = (jax.nn.silu(gate) * up).astype(BF16)
        down = jnp.dot(mid, wd_ref[0], preferred_element_type=F32)
        if position == "first":
            acc_scr[0:rows, :] = down
        elif position == "middle":
            acc_scr[0:rows, :] += down
        else:
            _store_token_major(y_scr.at[slot], acc_scr[0:rows, :] + down)

    few_rows = tile_rows(i) <= T // 2

    for few, rows in ((False, T), (True, T // 2)):
        branch = valid & (few_rows if few else jnp.logical_not(few_rows))

        @pl.when(branch & (c == 0))
        def _():
            xn_scr[0:rows, :] = _rmsnorm(_load_token_major(x_scr.at[slot], rows), g_ref[...]).astype(BF16)
            row_dmas()
            expert_chunk("first", rows)

        if N_MOE_CHUNKS > 2:
            @pl.when(branch & (c > 0) & (c < n_chunks - 1))
            def _():
                expert_chunk("middle", rows)

        @pl.when(branch & (c == n_chunks - 1))
        def _():
            expert_chunk("last", rows)

    @pl.when(jnp.logical_not(valid) & (c == 0))
    def _():
        row_dmas()

    @pl.when(c == n_chunks - 1)
    def _():
        wait_scatter()

    @pl.when((i == n_tiles - 1) & (c == n_chunks - 1))
    def _():
        wait_gather(other)

        def issue(r, carry):
            row_copy(y_scr.at[slot], r * R, y_hbm,
                     dest_row(dstl_ref, window_offset(n_tiles - 1), r, tile_rows(n_tiles - 1)),
                     sems.at[SCATTER_SEM]).start()
            return carry
        lax.fori_loop(0, T, issue, 0)
        wait_scatter()


def _moe_experts(h1_tm, g, tile_expert, n_valid_tiles, src_rows, dst_rows, w_gate, w_up, w_down, n_out_tokens):
    D = D_MODEL
    T = MOE_TILE
    NT = tile_expert.shape[0]
    NC = N_MOE_CHUNKS
    assert NC >= 2
    FC = D_FF // NC

    def chunk(i, c, nvt):
        return jnp.where(i < nvt[0], c, NC - 1)

    def window_spec(tile_of_step):
        return pl.BlockSpec((pl.Element(1), pl.Element(T + LANES)),
                            lambda i, c, te, nvt: (0, nvt[1 + NT + tile_of_step(i)] * LANES),
                            memory_space=pltpu.SMEM)

    grid_spec = pltpu.PrefetchScalarGridSpec(
        num_scalar_prefetch=2,
        grid=(NT, NC),
        in_specs=[
            window_spec(lambda i: 0),
            window_spec(lambda i: jnp.minimum(i + 1, NT - 1)),
            window_spec(lambda i: jnp.maximum(i - 1, 0)),
            window_spec(lambda i: NT - 1),
            pl.BlockSpec(memory_space=pl.ANY),
            pl.BlockSpec((1, D), lambda i, c, te, nvt: (0, 0)),
            pl.BlockSpec((1, D, FC), lambda i, c, te, nvt: (te[i], 0, chunk(i, c, nvt))),
            pl.BlockSpec((1, D, FC), lambda i, c, te, nvt: (te[i], 0, chunk(i, c, nvt))),
            pl.BlockSpec((1, FC, D), lambda i, c, te, nvt: (te[i], chunk(i, c, nvt), 0)),
        ],
        out_specs=pl.BlockSpec(memory_space=pl.ANY),
        scratch_shapes=[pltpu.VMEM((2, T * ROWS_PER_TOKEN, LANES), F32), pltpu.VMEM((T, D), BF16),
                        pltpu.VMEM((T, D), F32), pltpu.VMEM((2, T * ROWS_PER_TOKEN, LANES), F32),
                        pltpu.SemaphoreType.DMA((3,))],
    )
    return pl.pallas_call(
        _moe_kernel,
        grid_spec=grid_spec,
        out_shape=jax.ShapeDtypeStruct((n_out_tokens * ROWS_PER_TOKEN, LANES), F32),
        compiler_params=pltpu.CompilerParams(dimension_semantics=("arbitrary", "arbitrary"),
                                             vmem_limit_bytes=VMEM_LIMIT),
        name="moe_experts",
    )(tile_expert, n_valid_tiles, src_rows, src_rows, dst_rows, dst_rows, h1_tm, g.reshape(1, D),
      w_gate, w_up, w_down)


def _moe_plan(route, n_tokens):
    T = MOE_TILE
    n_assign = TOP_K * n_tokens
    n_tiles = n_assign // T + N_EXPERTS
    e_flat = route[:, 0:TOP_K].astype(jnp.int32).reshape(-1)
    counts = jnp.sum((e_flat[:, None] == jnp.arange(N_EXPERTS, dtype=jnp.int32)[None, :]).astype(jnp.int32), axis=0)
    order = jnp.argsort(e_flat, stable=True).astype(jnp.int32)
    tiles_per = (counts + T - 1) // T
    tile_end = jnp.cumsum(tiles_per)
    n_valid = tile_end[-1]
    tile_ids = jnp.arange(n_tiles, dtype=jnp.int32)
    last_valid = jnp.maximum(n_valid - 1, 0)
    texp = jnp.sum((jnp.minimum(tile_ids, last_valid)[:, None] >= tile_end[None, :]).astype(jnp.int32), axis=1)
    texp = jnp.minimum(texp, N_EXPERTS - 1)
    onehot = (texp[:, None] == jnp.arange(N_EXPERTS, dtype=jnp.int32)[None, :]).astype(jnp.int32)
    pick = lambda v: jnp.sum(onehot * v[None, :], axis=1)
    j = tile_ids - pick(tile_end - tiles_per)
    start = jnp.where(tile_ids < n_valid, pick(jnp.cumsum(counts) - counts) + j * T, 0)
    n_rows = jnp.where(tile_ids < n_valid, jnp.clip(pick(counts) - j * T, 0, T), 0)
    token = order // TOP_K
    slot = order % TOP_K
    spare = jnp.zeros((T + LANES,), jnp.int32)
    src_sorted = jnp.concatenate([token * ROWS_PER_TOKEN, spare]).reshape(1, -1)
    dst_sorted = jnp.concatenate([(slot * n_tokens + token) * ROWS_PER_TOKEN, spare]).reshape(1, -1)
    tile_counts = jnp.concatenate([n_valid.reshape(1), n_rows, start // LANES, start % LANES]).astype(jnp.int32)
    return texp, tile_counts, src_sorted, dst_sorted


def _combine_kernel(h_ref, y0_ref, y1_ref, route_ref, g_ref, o_ref):
    T = o_ref.shape[0]
    route = route_ref[...]
    moe = route[:, 2:3] * _load_token_major(y0_ref, T) + route[:, 3:4] * _load_token_major(y1_ref, T)
    o_ref[...] = _rmsnorm(_load_token_major(h_ref, T) + moe, g_ref[...])


def _combine(h1_tm, y2_tm, route, g):
    D = D_MODEL
    S = route.shape[0]
    T = OUT_TILE
    nb = S // T
    tm_block = (T * ROWS_PER_TOKEN, LANES)
    return pl.pallas_call(
        _combine_kernel,
        grid=(nb,),
        in_specs=[
            pl.BlockSpec(tm_block, lambda i: (i, 0)),
            pl.BlockSpec(tm_block, lambda i: (i, 0)),
            pl.BlockSpec(tm_block, lambda i: (i + nb, 0)),
            pl.BlockSpec((T, LANES), lambda i: (i, 0)),
            pl.BlockSpec((1, D), lambda i: (0, 0)),
        ],
        out_specs=pl.BlockSpec((T, D), lambda i: (i, 0)),
        out_shape=jax.ShapeDtypeStruct((S, D), F32),
        compiler_params=pltpu.CompilerParams(dimension_semantics=("arbitrary",), vmem_limit_bytes=VMEM_LIMIT),
        name="moe_combine",
    )(h1_tm, y2_tm, y2_tm, route, g.reshape(1, D))


def kernel(x, positions, norm_mix, norm_ffn, norm_final, lru_w_in, lru_conv_w, lru_conv_b, lru_w_a, lru_b_a, lru_w_x, lru_b_x, lru_lambda, lru_w_out, attn_w_qkv, attn_b_qkv, attn_sinks, attn_w_o, attn_b_o, ffn_w_gate, ffn_w_up, ffn_w_down, moe_w_router, moe_w_gate, moe_w_up, moe_w_down):
    B, S, D = x.shape
    assert B == 1 and D == D_MODEL and S % FFN_TILE == 0
    h = x.reshape(S, D)

    w_ax = jnp.concatenate([lru_w_a[0], lru_w_x[0]], axis=-1).astype(BF16)
    h, ffn_wg, ffn_wu, ffn_wd = _lru_mixer(h, norm_mix[0], lru_w_in[0].astype(BF16), lru_conv_w[0], lru_conv_b[0],
                                           w_ax, lru_b_a[0], lru_b_x[0], lru_lambda[0], lru_w_out[0].astype(BF16),
                                           ffn_w_gate[0], ffn_w_up[0], ffn_w_down[0])
    h, moe_wg, moe_wu, moe_wd = _dense_ffn(h, norm_ffn[0], ffn_wg, ffn_wu, ffn_wd,
                                           moe_w_gate[0], moe_w_up[0], moe_w_down[0])

    q_dim = N_HEADS * HEAD_DIM
    kv_dim = N_KV_HEADS * HEAD_DIM

    def dup_heads(w):
        w4 = w.reshape(w.shape[:-1] + (N_KV_HEADS, 1, HEAD_DIM))
        return jnp.broadcast_to(w4, w.shape[:-1] + (N_KV_HEADS, 2, HEAD_DIM)).reshape(w.shape[:-1] + (KV_EXT,))

    wqkv, bqkv = attn_w_qkv[0], attn_b_qkv[0]
    w_qkv_ext = jnp.concatenate([wqkv[:, :q_dim], dup_heads(wqkv[:, q_dim:q_dim + kv_dim]),
                                 dup_heads(wqkv[:, q_dim + kv_dim:])], axis=1).astype(BF16)
    b_qkv_ext = jnp.concatenate([bqkv[:q_dim], dup_heads(bqkv[q_dim:q_dim + kv_dim]),
                                 dup_heads(bqkv[q_dim + kv_dim:])]).reshape(1, QKV_EXT)
    inv_freq = ROPE_THETA ** (-jnp.arange(0, ROT_DIM, 2, dtype=F32) / ROT_DIM)
    d_in_head = jnp.arange(LANES) % HEAD_DIM
    freq_lane = jnp.where(d_in_head < ROT_DIM, inv_freq[d_in_head % (ROT_DIM // 2)], 0.0).reshape(1, LANES)
    sign_lane = jnp.where(d_in_head < ROT_DIM // 2, -1.0, jnp.where(d_in_head < ROT_DIM, 1.0, 0.0))
    sign_lane = sign_lane.astype(F32).reshape(1, LANES)
    w_router_pad = jnp.pad(moe_w_router[0], ((0, 0), (0, LANES - N_EXPERTS))).astype(BF16)
    pos_f = positions.reshape(S, 1).astype(F32)
    h1_tm, route = _attn_mixer(h, pos_f, norm_mix[1], freq_lane, sign_lane, w_qkv_ext, b_qkv_ext,
                               attn_sinks[0], attn_w_o[0].astype(BF16), attn_b_o[0], norm_ffn[1], w_router_pad)

    texp, n_valid, src_rows, dst_rows = _moe_plan(route, S)
    y2_tm = _moe_experts(h1_tm, norm_ffn[1], texp, n_valid, src_rows, dst_rows,
                         moe_wg, moe_wu, moe_wd,
                         TOP_K * S + MOE_TILE)
    out = _combine(h1_tm, y2_tm, route, norm_final)
    return out.reshape(B, S, D)
```

```python
import jax
import jax.numpy as jnp
from jax import lax
from jax.experimental import pallas as pl
from jax.experimental.pallas import tpu as pltpu

F32 = jnp.float32
BF16 = jnp.bfloat16

D_MODEL = 1024
N_RNN_BLOCKS = 8
RNN_BLOCK = D_MODEL // N_RNN_BLOCKS
CONV_WIDTH = 4
LRU_C = 8.0
N_HEADS = 16
N_KV_HEADS = 4
HEAD_DIM = 64
GROUP = N_HEADS // N_KV_HEADS
WINDOW = 128
ATTN_BLOCK = 128
assert WINDOW == ATTN_BLOCK
ROPE_THETA = 500000.0
ROT_DIM = HEAD_DIM // 4
D_FF = 3584
N_EXPERTS = 8
TOP_K = 2
EPS = 1e-6
NEG_INF = -1e30
LOG2_E = 1.4426950408889634

LANES = 128
SUBLANES = 8
VMEM_LIMIT = 56 * 1024 * 1024

SEQ_TILE = 512
ATTN_TILE = 1024
LRU_CHUNKS = 4
FFN_TILE = 1024
FF_CHUNK = 512
N_FF_CHUNKS = D_FF // FF_CHUNK
MOE_TILE = 512
N_MOE_CHUNKS = 2
OUT_TILE = 1024


def _rmsnorm(x, g):
    return x * lax.rsqrt(jnp.mean(x * x, axis=-1, keepdims=True) + EPS) * g


ROWS_PER_TOKEN = D_MODEL // LANES


def _store_token_major(ref, val):
    T = val.shape[0]
    for c in range(ROWS_PER_TOKEN):
        ref[pl.ds(c, T, stride=ROWS_PER_TOKEN), :] = val[:, LANES * c:LANES * (c + 1)]


def _load_token_major(ref, T):
    return jnp.concatenate([ref[pl.ds(c, T, stride=ROWS_PER_TOKEN), :] for c in range(ROWS_PER_TOKEN)], axis=1)


def _const_spec(shape):
    n = len(shape)
    return pl.BlockSpec(shape, lambda *_: (0,) * n, pipeline_mode=pl.Buffered(1))


def _time_permutation(T):
    rho = jnp.arange(T)
    t_of_rho = (rho % SUBLANES) * (T // SUBLANES) + rho // SUBLANES
    perm = (t_of_rho[:, None] == jnp.arange(T)[None, :]).astype(BF16)
    return perm, perm.T


def _lru_kernel(xin_ref, xres_ref, g_ref, perm_ref, unperm_ref, win_ref, cw_ref, cb_ref, wax_ref, ba_ref, bx_ref,
                lam_ref, wout_ref, cast_g_ref, cast_u_ref, cast_d_ref,
                o_ref, bf_g_ref, bf_u_ref, bf_d_ref, proj_scr, tail_scr, carry_scr):
    i = pl.program_id(0)
    n_tiles = pl.num_programs(0) - 1
    T = xin_ref.shape[0]
    D = D_MODEL
    C = T // LRU_CHUNKS
    GC = C // SUBLANES
    G = T // SUBLANES
    HALO = (CONV_WIDTH - 1) * SUBLANES

    def cast_slices():
        bf_g_ref[...] = cast_g_ref[...].astype(BF16)
        bf_u_ref[...] = cast_u_ref[...].astype(BF16)
        bf_d_ref[...] = cast_d_ref[...].astype(BF16)

    @pl.when(i == 0)
    def _():
        tail_scr[...] = jnp.zeros_like(tail_scr)
        carry_scr[...] = jnp.zeros_like(carry_scr)

    PW = 2 * D // LRU_CHUNKS

    def project_chunks():
        hn = _rmsnorm(xin_ref[...], g_ref[...]).astype(BF16)
        hp = jnp.dot(perm_ref[...], hn, preferred_element_type=F32).astype(BF16)
        for k in range(LRU_CHUNKS):
            yield jnp.dot(hp, win_ref[:, PW * k:PW * (k + 1)], preferred_element_type=F32)

    def recur_chunks():
        sub = lax.broadcasted_iota(jnp.int32, (SUBLANES, D), 0)
        cw = cw_ref[...]
        softplus_neg_lam = jax.nn.softplus(-lam_ref[...])
        last = proj_scr[T - HALO:T, D:]
        prev_groups = []
        for k in range(CONV_WIDTH - 1):
            rows = slice(SUBLANES * k, SUBLANES * (k + 1))
            prev_groups.append(pltpu.roll(jnp.where(sub == SUBLANES - 1, tail_scr[rows, :], last[rows, :]), 1, 0))
        tail_scr[...] = last

        ys, hs, decay = [], [], []
        for k in range(LRU_CHUNKS):
            pc = proj_scr[C * k:C * (k + 1), :]
            ys.append(jax.nn.gelu(pc[:, :D], approximate=True))
            xb = pc[:, D:]
            xc = cb_ref[...] + xb * cw[CONV_WIDTH - 1:CONV_WIDTH, :]
            for back in range(1, CONV_WIDTH):
                shifted = jnp.concatenate(prev_groups[CONV_WIDTH - 1 - back:] + [xb[:C - SUBLANES * back, :]], axis=0)
                xc = xc + shifted * cw[CONV_WIDTH - 1 - back:CONV_WIDTH - back, :]
            prev_groups = [xb[C - HALO + SUBLANES * q:C - HALO + SUBLANES * (q + 1), :] for q in range(CONV_WIDTH - 1)]

            xcb = xc.astype(BF16)
            r_parts, i_parts = [], []
            for n in range(N_RNN_BLOCKS):
                gn = jnp.dot(xcb[:, RNN_BLOCK * n:RNN_BLOCK * (n + 1)], wax_ref[n], preferred_element_type=F32)
                r_parts.append(gn[:, :RNN_BLOCK])
                i_parts.append(gn[:, RNN_BLOCK:])
            r = jax.nn.sigmoid(jnp.concatenate(r_parts, axis=1) + ba_ref[...])
            ig = jax.nn.sigmoid(jnp.concatenate(i_parts, axis=1) + bx_ref[...])
            log_a = -LRU_C * r * softplus_neg_lam
            a = jnp.exp(log_a)
            z = jnp.tanh(-log_a) * (1.0 + a * a)
            b = jnp.where(z > 0.0, z * lax.rsqrt(z), 0.0) * (ig * xc)

            for j in range(GC):
                rows = slice(SUBLANES * j, SUBLANES * (j + 1))
                if hs:
                    hs.append(a[rows, :] * hs[-1] + b[rows, :])
                    decay.append(a[rows, :] * decay[-1])
                else:
                    hs.append(b[rows, :])
                    decay.append(a[rows, :])
            yield None

        p_inc, e_inc = decay[-1], hs[-1]
        for k in (1, 2, 4):
            m = sub >= k
            p_sh = jnp.where(m, pltpu.roll(p_inc, k, 0), 1.0)
            e_sh = jnp.where(m, pltpu.roll(e_inc, k, 0), 0.0)
            e_inc = e_inc + p_inc * e_sh
            p_inc = p_inc * p_sh
        h0 = carry_scr[...]
        after = p_inc * h0 + e_inc
        carry_scr[...] = after[SUBLANES - 1:SUBLANES, :]
        enter = jnp.where(sub == 0, h0, pltpu.roll(after, 1, 0))
        h = jnp.concatenate([hs[j] + decay[j] * enter for j in range(G)], axis=0)
        hy = (h * jnp.concatenate(ys, axis=0)).astype(BF16)
        hy = jnp.dot(unperm_ref[...], hy, preferred_element_type=F32).astype(BF16)
        yield jnp.dot(hy, wout_ref[...], preferred_element_type=F32) + xres_ref[...]

    @pl.when(i == 0)
    def _():
        cast_slices()
        for k, pc in enumerate(project_chunks()):
            proj_scr[:, PW * k:PW * (k + 1)] = pc

    @pl.when((i > 0) & (i < n_tiles))
    def _():
        cast_slices()
        rec = recur_chunks()
        new_proj = []
        for pc in project_chunks():
            new_proj.append(pc)
            next(rec)
        o_ref[...] = next(rec)
        for k, pc in enumerate(new_proj):
            proj_scr[:, PW * k:PW * (k + 1)] = pc

    @pl.when(i == n_tiles)
    def _():
        o_ref[...] = list(recur_chunks())[-1]


def _lru_mixer(x, g, w_in, conv_w, conv_b, w_ax, b_a, b_x, lam, w_out, cast_gate, cast_up, cast_down):
    S, D = x.shape
    T = SEQ_TILE
    n_tiles = S // T
    row = lambda v: v.reshape(1, D)
    perm, unperm = _time_permutation(T)

    def cast_spec(w):
        return pl.BlockSpec((w.shape[0] // n_tiles, w.shape[1]), lambda i: (jnp.minimum(i, n_tiles - 1), 0))

    casts = (cast_gate, cast_up, cast_down)
    return pl.pallas_call(
        _lru_kernel,
        grid=(n_tiles + 1,),
        in_specs=[
            pl.BlockSpec((T, D), lambda i: (jnp.minimum(i, n_tiles - 1), 0)),
            pl.BlockSpec((T, D), lambda i: (jnp.maximum(i - 1, 0), 0)),
            _const_spec((1, D)),
            _const_spec((T, T)),
            _const_spec((T, T)),
            _const_spec((D, 2 * D)),
            _const_spec((CONV_WIDTH, D)),
            _const_spec((1, D)),
            _const_spec((N_RNN_BLOCKS, RNN_BLOCK, 2 * RNN_BLOCK)),
            _const_spec((1, D)),
            _const_spec((1, D)),
            _const_spec((1, D)),
            _const_spec((D, D)),
        ] + [cast_spec(w) for w in casts],
        out_specs=[pl.BlockSpec((T, D), lambda i: (jnp.maximum(i - 1, 0), 0))] + [cast_spec(w) for w in casts],
        out_shape=[jax.ShapeDtypeStruct((S, D), F32)] + [jax.ShapeDtypeStruct(w.shape, BF16) for w in casts],
        scratch_shapes=[pltpu.VMEM((T, 2 * D), F32), pltpu.VMEM(((CONV_WIDTH - 1) * SUBLANES, D), F32),
                        pltpu.VMEM((1, D), F32)],
        compiler_params=pltpu.CompilerParams(dimension_semantics=("arbitrary",), vmem_limit_bytes=VMEM_LIMIT),
        name="lru_mixer",
    )(x, x, row(g), perm, unperm, w_in, conv_w, row(conv_b), w_ax, row(b_a), row(b_x), row(lam), w_out, *casts)


CAST_RING = 3


def _ffn_kernel(x_ref, g_ref, wg_hbm, wu_hbm, wd_hbm, cast_g_hbm, cast_u_hbm, cast_d_hbm,
                o_ref, bf_g_ref, bf_u_ref, bf_d_ref, xn_scr, acc_scr, ring_g, ring_u, ring_d,
                ring_wg, ring_wu, ring_wd, ring_sem):
    c = pl.program_id(1)
    last = pl.num_programs(1) - 1
    step = pl.program_id(0) * N_FF_CHUNKS + c
    n_steps = pl.num_programs(0) * N_FF_CHUNKS
    _, gu_rows, gu_cols = ring_g.shape
    _, d_rows, _ = ring_d.shape
    row_parts = cast_g_hbm.shape[1] // gu_rows

    def cast_copies(t):
        slot = t % CAST_RING
        e = t // (row_parts * N_FF_CHUNKS)
        r0 = pl.multiple_of(((t // N_FF_CHUNKS) % row_parts) * gu_rows, gu_rows)
        c0 = pl.multiple_of((t % N_FF_CHUNKS) * gu_cols, gu_cols)
        d0 = pl.multiple_of((t % (row_parts * N_FF_CHUNKS)) * d_rows, d_rows)
        f0 = pl.multiple_of((t % N_FF_CHUNKS) * FF_CHUNK, FF_CHUNK)
        return (
            pltpu.make_async_copy(cast_g_hbm.at[e, pl.ds(r0, gu_rows), pl.ds(c0, gu_cols)], ring_g.at[slot],
                                  ring_sem.at[0, slot]),
            pltpu.make_async_copy(cast_u_hbm.at[e, pl.ds(r0, gu_rows), pl.ds(c0, gu_cols)], ring_u.at[slot],
                                  ring_sem.at[1, slot]),
            pltpu.make_async_copy(cast_d_hbm.at[e, pl.ds(d0, d_rows), :], ring_d.at[slot], ring_sem.at[2, slot]),
            pltpu.make_async_copy(wg_hbm.at[:, pl.ds(f0, FF_CHUNK)], ring_wg.at[slot], ring_sem.at[3, slot]),
            pltpu.make_async_copy(wu_hbm.at[:, pl.ds(f0, FF_CHUNK)], ring_wu.at[slot], ring_sem.at[4, slot]),
            pltpu.make_async_copy(wd_hbm.at[pl.ds(f0, FF_CHUNK), :], ring_wd.at[slot], ring_sem.at[5, slot]),
        )

    @pl.when(step == 0)
    def _():
        for t in range(CAST_RING - 1):
            for cp in cast_copies(t):
                cp.start()

    @pl.when(step + (CAST_RING - 1) < n_steps)
    def _():
        for cp in cast_copies(step + (CAST_RING - 1)):
            cp.start()

    for cp in cast_copies(step):
        cp.wait()
    slot = step % CAST_RING

    def ff_chunk():
        bf_g_ref[0] = ring_g[slot].astype(BF16)
        bf_u_ref[0] = ring_u[slot].astype(BF16)
        bf_d_ref[0] = ring_d[slot].astype(BF16)
        xn = xn_scr[...]
        gate = jnp.dot(xn, ring_wg[slot], preferred_element_type=F32)
        up = jnp.dot(xn, ring_wu[slot], preferred_element_type=F32)
        mid = (jax.nn.silu(gate) * up).astype(BF16)
        return jnp.dot(mid, ring_wd[slot], preferred_element_type=F32)

    @pl.when(c == 0)
    def _():
        xn_scr[...] = _rmsnorm(x_ref[...], g_ref[...]).astype(BF16)
        acc_scr[...] = ff_chunk()

    @pl.when((c > 0) & (c < last))
    def _():
        acc_scr[...] += ff_chunk()

    @pl.when(c == last)
    def _():
        o_ref[...] = x_ref[...] + (acc_scr[...] + ff_chunk())


def _dense_ffn(x, g, w_gate, w_up, w_down, cast_gate, cast_up, cast_down):
    S, D = x.shape
    T = FFN_TILE
    n_steps = (S // T) * N_FF_CHUNKS
    E = cast_gate.shape[0]
    row_parts = n_steps // (E * N_FF_CHUNKS)
    assert row_parts * E * N_FF_CHUNKS == n_steps
    gu_block = (1, D // row_parts, D_FF // N_FF_CHUNKS)
    d_block = (1, D_FF // (row_parts * N_FF_CHUNKS), D)

    def gu_map(i, c):
        s = i * N_FF_CHUNKS + c
        return (s // (row_parts * N_FF_CHUNKS), (s // N_FF_CHUNKS) % row_parts, c)

    def d_map(i, c):
        s = i * N_FF_CHUNKS + c
        return (s // (row_parts * N_FF_CHUNKS), s % (row_parts * N_FF_CHUNKS), 0)

    return pl.pallas_call(
        _ffn_kernel,
        grid=(S // T, N_FF_CHUNKS),
        in_specs=[
            pl.BlockSpec((T, D), lambda i, c: (i, 0)),
            pl.BlockSpec((1, D), lambda i, c: (0, 0)),
            pl.BlockSpec(memory_space=pl.ANY),
            pl.BlockSpec(memory_space=pl.ANY),
            pl.BlockSpec(memory_space=pl.ANY),
            pl.BlockSpec(memory_space=pl.ANY),
            pl.BlockSpec(memory_space=pl.ANY),
            pl.BlockSpec(memory_space=pl.ANY),
        ],
        out_specs=[pl.BlockSpec((T, D), lambda i, c: (i, 0)), pl.BlockSpec(gu_block, gu_map),
                   pl.BlockSpec(gu_block, gu_map), pl.BlockSpec(d_block, d_map)],
        out_shape=[jax.ShapeDtypeStruct((S, D), F32), jax.ShapeDtypeStruct(cast_gate.shape, BF16),
                   jax.ShapeDtypeStruct(cast_up.shape, BF16), jax.ShapeDtypeStruct(cast_down.shape, BF16)],
        scratch_shapes=[pltpu.VMEM((T, D), BF16), pltpu.VMEM((T, D), F32),
                        pltpu.VMEM((CAST_RING,) + gu_block[1:], F32), pltpu.VMEM((CAST_RING,) + gu_block[1:], F32),
                        pltpu.VMEM((CAST_RING,) + d_block[1:], F32),
                        pltpu.VMEM((CAST_RING, D, FF_CHUNK), BF16), pltpu.VMEM((CAST_RING, D, FF_CHUNK), BF16),
                        pltpu.VMEM((CAST_RING, FF_CHUNK, D), BF16), pltpu.SemaphoreType.DMA((6, CAST_RING))],
        compiler_params=pltpu.CompilerParams(dimension_semantics=("arbitrary", "arbitrary"),
                                             vmem_limit_bytes=VMEM_LIMIT),
        name="dense_ffn",
    )(x, g.reshape(1, D), w_gate, w_up, w_down, cast_gate, cast_up, cast_down)


KV_EXT = N_KV_HEADS * LANES
QKV_EXT = N_HEADS * HEAD_DIM + 2 * KV_EXT


def _attn_kernel(sink_ref, x_ref, pos_ref, g_ref, freq_ref, sign_ref, wqkv_ref, bqkv_ref, wo_ref, bo_ref,
                 g2_ref, wr_ref, o_ref, route_ref,
                 qlo_scr, qhi_scr, k_scr, v_scr, o_scr):
    T = x_ref.shape[0]
    D = D_MODEL
    B = ATTN_BLOCK
    first_tile = pl.program_id(0) == 0

    @pl.when(first_tile)
    def _():
        k_scr[0:B, :] = jnp.zeros((B, KV_EXT), BF16)
        v_scr[0:B, :] = jnp.zeros((B, KV_EXT), BF16)

    x = x_ref[...]
    hn = _rmsnorm(x, g_ref[...]).astype(BF16)
    qkv = jnp.dot(hn, wqkv_ref[...], preferred_element_type=F32) + bqkv_ref[...]

    ang = pos_ref[...] * freq_ref[...]
    cos_t = jnp.cos(ang)
    sin_t = jnp.sin(ang) * sign_ref[...]
    lane = lax.broadcasted_iota(jnp.int32, (T, LANES), 1)
    first_half = (lane % HEAD_DIM) < (ROT_DIM // 2)
    lo_half = lane < HEAD_DIM

    def rope(col):
        partner = jnp.where(first_half, pltpu.roll(col, LANES - ROT_DIM // 2, 1), pltpu.roll(col, ROT_DIM // 2, 1))
        return col * cos_t + partner * sin_t

    scale = HEAD_DIM ** -0.5 * LOG2_E
    for c in range(D // LANES):
        qc = rope(qkv[:, LANES * c:LANES * (c + 1)]) * scale
        qlo_scr[:, LANES * c:LANES * (c + 1)] = jnp.where(lo_half, qc, 0.0).astype(BF16)
        qhi_scr[:, LANES * c:LANES * (c + 1)] = jnp.where(lo_half, 0.0, qc).astype(BF16)
    for g in range(N_KV_HEADS):
        kc = rope(qkv[:, D + LANES * g:D + LANES * (g + 1)])
        k_scr[B:B + T, LANES * g:LANES * (g + 1)] = kc.astype(BF16)
        vc = qkv[:, D + KV_EXT + LANES * g:D + KV_EXT + LANES * (g + 1)]
        v_scr[B:B + T, LANES * g:LANES * (g + 1)] = vc.astype(BF16)

    qi = lax.broadcasted_iota(jnp.int32, (B, B), 0)
    kj = lax.broadcasted_iota(jnp.int32, (B, B), 1)
    causal_own = kj <= qi
    window_prev = kj > qi
    key0 = kj == 0
    key0_row = lax.broadcasted_iota(jnp.int32, (1, B), 1) == 0
    lo_blk = lax.broadcasted_iota(jnp.int32, (B, LANES), 1) < HEAD_DIM

    def block_body(blk, _):
        r0 = pl.multiple_of(blk * B, B)
        k_lo = jnp.where(first_tile & (blk == 0), B, 0)
        allowed_prev = window_prev & (kj >= k_lo)
        for g in range(N_KV_HEADS):
            kk = k_scr[pl.ds(r0, 2 * B), LANES * g:LANES * (g + 1)]
            vv = v_scr[pl.ds(r0, 2 * B), LANES * g:LANES * (g + 1)]
            cols = [slice(LANES * c, LANES * (c + 1)) for c in range((GROUP // 2) * g, (GROUP // 2) * (g + 1))]
            q_all = jnp.concatenate([q_scr[pl.ds(r0, B), col] for col in cols for q_scr in (qlo_scr, qhi_scr)], axis=0)
            s_all = lax.dot_general(q_all, kk, (((1,), (1,)), ((), ())), preferred_element_type=F32)
            p_all, denoms = [], []
            for hh in range(GROUP):
                rows = slice(B * hh, B * (hh + 1))
                sink_fill = jnp.where(key0_row, sink_ref[GROUP * g + hh] * LOG2_E, NEG_INF)
                s_prev = jnp.where(allowed_prev, s_all[rows, :B], sink_fill)
                s_own = jnp.where(causal_own, s_all[rows, B:], NEG_INF)
                m = jnp.max(jnp.maximum(s_prev, s_own), axis=-1, keepdims=True)
                p_prev = jnp.exp2(s_prev - m)
                p_own = jnp.exp2(s_own - m)
                denoms.append(jnp.sum(p_prev + p_own, axis=-1, keepdims=True))
                p_all.append(jnp.concatenate([jnp.where(key0, 0.0, p_prev), p_own], axis=1).astype(BF16))
            o_all = jnp.dot(jnp.concatenate(p_all, axis=0), vv, preferred_element_type=F32)
            o_heads = [o_all[B * hh:B * (hh + 1), :] / denoms[hh] for hh in range(GROUP)]
            for p, col in enumerate(cols):
                o_scr[pl.ds(r0, B), col] = jnp.where(lo_blk, o_heads[2 * p], o_heads[2 * p + 1]).astype(BF16)
        return 0

    lax.fori_loop(0, T // B, block_body, 0)

    k_scr[0:B, :] = k_scr[T:T + B, :]
    v_scr[0:B, :] = v_scr[T:T + B, :]

    h1 = jnp.dot(o_scr[...], wo_ref[...], preferred_element_type=F32) + bo_ref[...] + x
    _store_token_major(o_ref, h1)

    hn2 = _rmsnorm(h1, g2_ref[...]).astype(BF16)
    logits = jnp.dot(hn2, wr_ref[...], preferred_element_type=F32)
    lane_f = lane.astype(F32)
    lg = jnp.where(lane < N_EXPERTS, logits, -jnp.inf)
    v1 = jnp.max(lg, axis=-1, keepdims=True)
    i1 = jnp.min(jnp.where(lg == v1, lane_f, float(LANES)), axis=-1, keepdims=True)
    lg2 = jnp.where(lane_f == i1, -jnp.inf, lg)
    v2 = jnp.max(lg2, axis=-1, keepdims=True)
    i2 = jnp.min(jnp.where(lg2 == v2, lane_f, float(LANES)), axis=-1, keepdims=True)
    e2 = jnp.exp(v2 - v1)
    gate1 = 1.0 / (1.0 + e2)
    gate2 = e2 / (1.0 + e2)
    route_ref[...] = jnp.where(lane == 0, i1, jnp.where(lane == 1, i2,
                                                        jnp.where(lane == 2, gate1, jnp.where(lane == 3, gate2, 0.0))))


def _attn_mixer(x, pos_f, g, freq_lane, sign_lane, w_qkv_ext, b_qkv_ext, sinks, w_o, b_o, g2, w_router_pad):
    S, D = x.shape
    T = ATTN_TILE
    B = ATTN_BLOCK
    grid_spec = pltpu.PrefetchScalarGridSpec(
        num_scalar_prefetch=1,
        grid=(S // T,),
        in_specs=[
            pl.BlockSpec((T, D), lambda i, s: (i, 0)),
            pl.BlockSpec((T, 1), lambda i, s: (i, 0)),
            _const_spec((1, D)),
            _const_spec((1, LANES)),
            _const_spec((1, LANES)),
            _const_spec((D, QKV_EXT)),
            _const_spec((1, QKV_EXT)),
            _const_spec((D, D)),
            _const_spec((1, D)),
            _const_spec((1, D)),
            _const_spec((D, LANES)),
        ],
        out_specs=[pl.BlockSpec((T * ROWS_PER_TOKEN, LANES), lambda i, s: (i, 0)),
                   pl.BlockSpec((T, LANES), lambda i, s: (i, 0))],
        scratch_shapes=[
            pltpu.VMEM((T, D), BF16), pltpu.VMEM((T, D), BF16),
            pltpu.VMEM((T + B, KV_EXT), BF16), pltpu.VMEM((T + B, KV_EXT), BF16),
            pltpu.VMEM((T, D), BF16),
        ],
    )
    return pl.pallas_call(
        _attn_kernel,
        grid_spec=grid_spec,
        out_shape=[jax.ShapeDtypeStruct((S * ROWS_PER_TOKEN, LANES), F32), jax.ShapeDtypeStruct((S, LANES), F32)],
        compiler_params=pltpu.CompilerParams(dimension_semantics=("arbitrary",), vmem_limit_bytes=VMEM_LIMIT),
        name="attn_mixer",
    )(sinks, x, pos_f, g.reshape(1, D), freq_lane, sign_lane, w_qkv_ext, b_qkv_ext, w_o, b_o.reshape(1, D),
      g2.reshape(1, D), w_router_pad)


def _moe_kernel(texp_ref, nvt_ref, src0_ref, srcn_ref, dstp_ref, dstl_ref, h_hbm, g_ref, wg_ref, wu_ref, wd_ref,
                y_hbm, x_scr, xn_scr, acc_scr, y_scr, sems):
    i = pl.program_id(0)
    c = pl.program_id(1)
    n_tiles = pl.num_programs(0)
    n_chunks = pl.num_programs(1)
    T = xn_scr.shape[0]
    R = ROWS_PER_TOKEN
    valid = i < nvt_ref[0]
    slot = i % 2
    other = 1 - slot
    SCATTER_SEM = 2

    def row_copy(src, src_row, dst, dst_row, sem):
        return pltpu.make_async_copy(src.at[pl.ds(pl.multiple_of(src_row, R), R), :],
                                     dst.at[pl.ds(pl.multiple_of(dst_row, R), R), :], sem)

    def tile_rows(t):
        return nvt_ref[1 + t]

    def window_offset(t):
        return nvt_ref[1 + 2 * n_tiles + t]

    prev_tile = jnp.maximum(i - 1, 0)
    next_tile = jnp.minimum(i + 1, n_tiles - 1)
    rows_prev = jnp.where(i > 0, tile_rows(prev_tile), 0)
    pad_base = y_hbm.shape[0] - T * R

    def dest_row(window_ref, offset, r, n_rows):
        return jnp.where(r < n_rows, window_ref[0, offset + r], pad_base + r * R)

    def wait_gather(s):
        pltpu.make_async_copy(h_hbm.at[pl.ds(0, T * R), :], x_scr.at[s], sems.at[s]).wait()

    def wait_scatter():
        pltpu.make_async_copy(y_scr.at[0], y_hbm.at[pl.ds(0, T * R), :], sems.at[SCATTER_SEM]).wait()

    @pl.when((i == 0) & (c == 0))
    def _():
        y_scr[...] = jnp.zeros_like(y_scr)

        def issue(r, carry):
            row_copy(h_hbm, src0_ref[0, window_offset(0) + r], x_scr.at[0], r * R, sems.at[0]).start()
            return carry
        lax.fori_loop(0, T, issue, 0)

    @pl.when(c == 0)
    def _():
        wait_gather(slot)

    def row_dmas():
        off_next = window_offset(next_tile)
        off_prev = window_offset(prev_tile)
        for r in range(T):
            row_copy(h_hbm, srcn_ref[0, off_next + r], x_scr.at[other], r * R, sems.at[other]).start()
            row_copy(y_scr.at[other], r * R, y_hbm, dest_row(dstp_ref, off_prev, r, rows_prev),
                     sems.at[SCATTER_SEM]).start()

    def expert_chunk(position, rows):
        xn = xn_scr[0:rows, :]
        gate = jnp.dot(xn, wg_ref[0], preferred_element_type=F32)
        up = jnp.dot(xn, wu_ref[0], preferred_element_type=F32)
        mid = (jax.nn.silu(gate) * up).astype(BF16)
        down = jnp.dot(mid, wd_ref[0], preferred_element_type=F32)
        if position == "first":
            acc_scr[0:rows, :] = down
        elif position == "middle":
            acc_scr[0:rows, :] += down
        else:
            _store_token_major(y_scr.at[slot], acc_scr[0:rows, :] + down)

    few_rows = tile_rows(i) <= T // 2

    for few, rows in ((False, T), (True, T // 2)):
        branch = valid & (few_rows if few else jnp.logical_not(few_rows))

        @pl.when(branch & (c == 0))
        def _():
            xn_scr[0:rows, :] = _rmsnorm(_load_token_major(x_scr.at[slot], rows), g_ref[...]).astype(BF16)
            row_dmas()
            expert_chunk("first", rows)

        if N_MOE_CHUNKS > 2:
            @pl.when(branch & (c > 0) & (c < n_chunks - 1))
            def _():
                expert_chunk("middle", rows)

        @pl.when(branch & (c == n_chunks - 1))
        def _():
            expert_chunk("last", rows)

    @pl.when(jnp.logical_not(valid) & (c == 0))
    def _():
        row_dmas()

    @pl.when(c == n_chunks - 1)
    def _():
        wait_scatter()

    @pl.when((i == n_tiles - 1) & (c == n_chunks - 1))
    def _():
        wait_gather(other)

        def issue(r, carry):
            row_copy(y_scr.at[slot], r * R, y_hbm,
                     dest_row(dstl_ref, window_offset(n_tiles - 1), r, tile_rows(n_tiles - 1)),
                     sems.at[SCATTER_SEM]).start()
            return carry
        lax.fori_loop(0, T, issue, 0)
        wait_scatter()


def _moe_experts(h1_tm, g, tile_expert, n_valid_tiles, src_rows, dst_rows, w_gate, w_up, w_down, n_out_tokens):
    D = D_MODEL
    T = MOE_TILE
    NT = tile_expert.shape[0]
    NC = N_MOE_CHUNKS
    assert NC >= 2
    FC = D_FF // NC

    def chunk(i, c, nvt):
        return jnp.where(i < nvt[0], c, NC - 1)

    def window_spec(tile_of_step):
        return pl.BlockSpec((pl.Element(1), pl.Element(T + LANES)),
                            lambda i, c, te, nvt: (0, nvt[1 + NT + tile_of_step(i)] * LANES),
                            memory_space=pltpu.SMEM)

    grid_spec = pltpu.PrefetchScalarGridSpec(
        num_scalar_prefetch=2,
        grid=(NT, NC),
        in_specs=[
            window_spec(lambda i: 0),
            window_spec(lambda i: jnp.minimum(i + 1, NT - 1)),
            window_spec(lambda i: jnp.maximum(i - 1, 0)),
            window_spec(lambda i: NT - 1),
            pl.BlockSpec(memory_space=pl.ANY),
            pl.BlockSpec((1, D), lambda i, c, te, nvt: (0, 0)),
            pl.BlockSpec((1, D, FC), lambda i, c, te, nvt: (te[i], 0, chunk(i, c, nvt))),
            pl.BlockSpec((1, D, FC), lambda i, c, te, nvt: (te[i], 0, chunk(i, c, nvt))),
            pl.BlockSpec((1, FC, D), lambda i, c, te, nvt: (te[i], chunk(i, c, nvt), 0)),
        ],
        out_specs=pl.BlockSpec(memory_space=pl.ANY),
        scratch_shapes=[pltpu.VMEM((2, T * ROWS_PER_TOKEN, LANES), F32), pltpu.VMEM((T, D), BF16),
                        pltpu.VMEM((T, D), F32), pltpu.VMEM((2, T * ROWS_PER_TOKEN, LANES), F32),
                        pltpu.SemaphoreType.DMA((3,))],
    )
    return pl.pallas_call(
        _moe_kernel,
        grid_spec=grid_spec,
        out_shape=jax.ShapeDtypeStruct((n_out_tokens * ROWS_PER_TOKEN, LANES), F32),
        compiler_params=pltpu.CompilerParams(dimension_semantics=("arbitrary", "arbitrary"),
                                             vmem_limit_bytes=VMEM_LIMIT),
        name="moe_experts",
    )(tile_expert, n_valid_tiles, src_rows, src_rows, dst_rows, dst_rows, h1_tm, g.reshape(1, D),
      w_gate, w_up, w_down)


def _moe_plan(route, n_tokens):
    T = MOE_TILE
    n_assign = TOP_K * n_tokens
    n_tiles = n_assign // T + N_EXPERTS
    e_flat = route[:, 0:TOP_K].astype(jnp.int32).reshape(-1)
    counts = jnp.sum((e_flat[:, None] == jnp.arange(N_EXPERTS, dtype=jnp.int32)[None, :]).astype(jnp.int32), axis=0)
    order = jnp.argsort(e_flat, stable=True).astype(jnp.int32)
    tiles_per = (counts + T - 1) // T
    tile_end = jnp.cumsum(tiles_per)
    n_valid = tile_end[-1]
    tile_ids = jnp.arange(n_tiles, dtype=jnp.int32)
    last_valid = jnp.maximum(n_valid - 1, 0)
    texp = jnp.sum((jnp.minimum(tile_ids, last_valid)[:, None] >= tile_end[None, :]).astype(jnp.int32), axis=1)
    texp = jnp.minimum(texp, N_EXPERTS - 1)
    onehot = (texp[:, None] == jnp.arange(N_EXPERTS, dtype=jnp.int32)[None, :]).astype(jnp.int32)
    pick = lambda v: jnp.sum(onehot * v[None, :], axis=1)
    j = tile_ids - pick(tile_end - tiles_per)
    start = jnp.where(tile_ids < n_valid, pick(jnp.cumsum(counts) - counts) + j * T, 0)
    n_rows = jnp.where(tile_ids < n_valid, jnp.clip(pick(counts) - j * T, 0, T), 0)
    token = order // TOP_K
    slot = order % TOP_K
    spare = jnp.zeros((T + LANES,), jnp.int32)
    src_sorted = jnp.concatenate([token * ROWS_PER_TOKEN, spare]).reshape(1, -1)
    dst_sorted = jnp.concatenate([(slot * n_tokens + token) * ROWS_PER_TOKEN, spare]).reshape(1, -1)
    tile_counts = jnp.concatenate([n_valid.reshape(1), n_rows, start // LANES, start % LANES]).astype(jnp.int32)
    return texp, tile_counts, src_sorted, dst_sorted


def _combine_kernel(h_ref, y0_ref, y1_ref, route_ref, g_ref, o_ref):
    T = o_ref.shape[0]
    route = route_ref[...]
    moe = route[:, 2:3] * _load_token_major(y0_ref, T) + route[:, 3:4] * _load_token_major(y1_ref, T)
    o_ref[...] = _rmsnorm(_load_token_major(h_ref, T) + moe, g_ref[...])


def _combine(h1_tm, y2_tm, route, g):
    D = D_MODEL
    S = route.shape[0]
    T = OUT_TILE
    nb = S // T
    tm_block = (T * ROWS_PER_TOKEN, LANES)
    return pl.pallas_call(
        _combine_kernel,
        grid=(nb,),
        in_specs=[
            pl.BlockSpec(tm_block, lambda i: (i, 0)),
            pl.BlockSpec(tm_block, lambda i: (i, 0)),
            pl.BlockSpec(tm_block, lambda i: (i + nb, 0)),
            pl.BlockSpec((T, LANES), lambda i: (i, 0)),
            pl.BlockSpec((1, D), lambda i: (0, 0)),
        ],
        out_specs=pl.BlockSpec((T, D), lambda i: (i, 0)),
        out_shape=jax.ShapeDtypeStruct((S, D), F32),
        compiler_params=pltpu.CompilerParams(dimension_semantics=("arbitrary",), vmem_limit_bytes=VMEM_LIMIT),
        name="moe_combine",
    )(h1_tm, y2_tm, y2_tm, route, g.reshape(1, D))


def kernel(x, positions, norm_mix, norm_ffn, norm_final, lru_w_in, lru_conv_w, lru_conv_b, lru_w_a, lru_b_a, lru_w_x, lru_b_x, lru_lambda, lru_w_out, attn_w_qkv, attn_b_qkv, attn_sinks, attn_w_o, attn_b_o, ffn_w_gate, ffn_w_up, ffn_w_down, moe_w_router, moe_w_gate, moe_w_up, moe_w_down):
    B, S, D = x.shape
    assert B == 1 and D == D_MODEL and S % FFN_TILE == 0
    h = x.reshape(S, D)

    w_ax = jnp.concatenate([lru_w_a[0], lru_w_x[0]], axis=-1).astype(BF16)
    h, ffn_wg, ffn_wu, ffn_wd = _lru_mixer(h, norm_mix[0], lru_w_in[0].astype(BF16), lru_conv_w[0], lru_conv_b[0],
                                           w_ax, lru_b_a[0], lru_b_x[0], lru_lambda[0], lru_w_out[0].astype(BF16),
                                           ffn_w_gate[0], ffn_w_up[0], ffn_w_down[0])
    h, moe_wg, moe_wu, moe_wd = _dense_ffn(h, norm_ffn[0], ffn_wg, ffn_wu, ffn_wd,
                                           moe_w_gate[0], moe_w_up[0], moe_w_down[0])

    q_dim = N_HEADS * HEAD_DIM
    kv_dim = N_KV_HEADS * HEAD_DIM

    def dup_heads(w):
        w4 = w.reshape(w.shape[:-1] + (N_KV_HEADS, 1, HEAD_DIM))
        return jnp.broadcast_to(w4, w.shape[:-1] + (N_KV_HEADS, 2, HEAD_DIM)).reshape(w.shape[:-1] + (KV_EXT,))

    wqkv, bqkv = attn_w_qkv[0], attn_b_qkv[0]
    w_qkv_ext = jnp.concatenate([wqkv[:, :q_dim], dup_heads(wqkv[:, q_dim:q_dim + kv_dim]),
                                 dup_heads(wqkv[:, q_dim + kv_dim:])], axis=1).astype(BF16)
    b_qkv_ext = jnp.concatenate([bqkv[:q_dim], dup_heads(bqkv[q_dim:q_dim + kv_dim]),
                                 dup_heads(bqkv[q_dim + kv_dim:])]).reshape(1, QKV_EXT)
    inv_freq = ROPE_THETA ** (-jnp.arange(0, ROT_DIM, 2, dtype=F32) / ROT_DIM)
    d_in_head = jnp.arange(LANES) % HEAD_DIM
    freq_lane = jnp.where(d_in_head < ROT_DIM, inv_freq[d_in_head % (ROT_DIM // 2)], 0.0).reshape(1, LANES)
    sign_lane = jnp.where(d_in_head < ROT_DIM // 2, -1.0, jnp.where(d_in_head < ROT_DIM, 1.0, 0.0))
    sign_lane = sign_lane.astype(F32).reshape(1, LANES)
    w_router_pad = jnp.pad(moe_w_router[0], ((0, 0), (0, LANES - N_EXPERTS))).astype(BF16)
    pos_f = positions.reshape(S, 1).astype(F32)
    h1_tm, route = _attn_mixer(h, pos_f, norm_mix[1], freq_lane, sign_lane, w_qkv_ext, b_qkv_ext,
                               attn_sinks[0], attn_w_o[0].astype(BF16), attn_b_o[0], norm_ffn[1], w_router_pad)

    texp, n_valid, src_rows, dst_rows = _moe_plan(route, S)
    y2_tm = _moe_experts(h1_tm, norm_ffn[1], texp, n_valid, src_rows, dst_rows,
                         moe_wg, moe_wu, moe_wd,
                         TOP_K * S + MOE_TILE)
    out = _combine(h1_tm, y2_tm, route, norm_final)
    return out.reshape(B, S, D)
```
